```python
import math
import jax, jax.numpy as jnp
from jax import lax
import numpy as np

D_MODEL = 1024
BATCH = 8
SEQ = 8192
DEPTH = 1

HEAD_DIM = 64
D_MIX = D_MODEL
N_HEADS_DIL = (D_MIX // 2) // HEAD_DIM
N_HEADS_SB = (D_MIX // 2) // HEAD_DIM
D_DIL = N_HEADS_DIL * HEAD_DIM
D_SB = N_HEADS_SB * HEAD_DIM
DIL_PATTERNS = ((128, 1), (512, 4), (2048, 16))
BLK = 128
REL_BUCKETS = 32
REL_MAX_DISTANCE = 2048
N_EXPERTS = 32
TOP_K = 4
D_EXPERT = D_MODEL
SWIGLU_ALPHA = 1.702
SWIGLU_LIMIT = 7.0
MOE_BLK = 256
RMS_EPS = 1e-6
NEG_INF = -1e30

kernel_name = "hybrid_dilated_stickbreaking_moe_layer"


def rmsnorm(x, g):
    x32 = x.astype(jnp.float32)
    y = x32 * lax.rsqrt(jnp.mean(x32 * x32, axis=-1, keepdims=True) + RMS_EPS)
    return (y * g.astype(jnp.float32)).astype(x.dtype)


def t5_bucket(dist):
    max_exact = REL_BUCKETS // 2
    d_f = jnp.maximum(dist, 1).astype(jnp.float32)
    large = max_exact + (jnp.log(d_f / max_exact)
                         / math.log(REL_MAX_DISTANCE / max_exact)
                         * (REL_BUCKETS - max_exact)).astype(jnp.int32)
    large = jnp.minimum(large, REL_BUCKETS - 1)
    return jnp.where(dist < max_exact, dist, large)


def dilated_window_attention(q, k, v, rel_table, window, dilation):
    B, S, H, Dh = q.shape
    d = dilation
    band = window // d
    L = S // d
    nblk = -(-L // BLK)
    Lp = nblk * BLK

    def to_strided(a):
        a = a.astype(jnp.float32).reshape(B, L, d, H, Dh).transpose(0, 2, 1, 3, 4)
        a = jnp.pad(a, ((0, 0), (0, 0), (0, Lp - L), (0, 0), (0, 0)))
        return a.reshape(B, d, nblk, BLK, H, Dh)

    def with_prev(a):
        prev = jnp.pad(a[:, :, :-1], ((0, 0), (0, 0), (1, 0), (0, 0), (0, 0), (0, 0)))
        return jnp.concatenate([prev, a], axis=3)

    qs = to_strided(q)
    kc = with_prev(to_strided(k))
    vc = with_prev(to_strided(v))

    qi = jnp.arange(BLK, dtype=jnp.int32)[:, None]
    kj = jnp.arange(2 * BLK, dtype=jnp.int32)[None, :]
    rel = qi + BLK - kj
    band_ok = (rel >= 0) & (rel <= band)
    key_pos = jnp.arange(nblk, dtype=jnp.int32)[:, None, None] * BLK - BLK + kj[None]
    valid = band_ok[None] & (key_pos >= 0)
    bias = rel_table.astype(jnp.float32)[t5_bucket(jnp.maximum(rel, 0) * d)]
    bias = bias.transpose(2, 0, 1)

    scale = 1.0 / math.sqrt(Dh)
    logits = jnp.einsum('brnqhd,brnkhd->brnhqk', qs, kc) * scale + bias
    vmask = valid[None, None, :, None]
    logits = jnp.where(vmask, logits, NEG_INF)
    m = jnp.max(logits, axis=-1, keepdims=True)
    p = jnp.where(vmask, jnp.exp(logits - m), 0.0)
    s = jnp.sum(p, axis=-1, keepdims=True)
    o = jnp.einsum('brnhqk,brnkhd->brnqhd', p / s, vc)
    lse = (m + jnp.log(s))[..., 0]

    o = o.reshape(B, d, Lp, H, Dh)[:, :, :L].transpose(0, 2, 1, 3, 4).reshape(B, S, H, Dh)
    lse = lse.transpose(0, 1, 2, 4, 3).reshape(B, d, Lp, H)[:, :, :L]
    lse = lse.transpose(0, 2, 1, 3).reshape(B, S, H)
    return o, lse


def dilated_mixture(q, k, v, rel_table):
    outs, lses = [], []
    for window, dilation in DIL_PATTERNS:
        o, l = dilated_window_attention(q, k, v, rel_table, window, dilation)
        outs.append(o)
        lses.append(l)
    w = jax.nn.softmax(jnp.stack(lses, axis=0), axis=0)
    return jnp.sum(w[..., None] * jnp.stack(outs, axis=0), axis=0)


def stick_breaking_attention(q, k, v):
    B, S, H, Dh = q.shape
    nq = S // BLK
    scale = 1.0 / math.sqrt(Dh)
    k32 = k.astype(jnp.float32)
    v32 = v.astype(jnp.float32)
    key_idx = jnp.arange(S, dtype=jnp.int32)

    def block(args):
        qb, t0 = args
        z = jnp.einsum('bqhd,bkhd->bhqk', qb.astype(jnp.float32), k32) * scale
        tq = t0 + jnp.arange(BLK, dtype=jnp.int32)
        mask = key_idx[None, :] < tq[:, None]
        log_beta = jax.nn.log_sigmoid(z)
        log_1m = jnp.where(mask, jax.nn.log_sigmoid(-z), 0.0)
        between = lax.cumsum(log_1m, axis=3, reverse=True) - log_1m
        a = jnp.where(mask, jnp.exp(log_beta + between), 0.0)
        return jnp.einsum('bhqk,bkhd->bqhd', a, v32)

    qb = q.reshape(B, nq, BLK, H, Dh).transpose(1, 0, 2, 3, 4)
    starts = jnp.arange(nq, dtype=jnp.int32) * BLK
    o = lax.map(block, (qb, starts))
    return o.transpose(1, 0, 2, 3, 4).reshape(B, S, H, Dh)


def moe(h, w_router, b_router, w_gate, b_gate, w_up, b_up, w_down, b_down):
    B, S, D = h.shape
    N = B * S
    t = h.reshape(N, D)
    logits = jnp.matmul(t, w_router).astype(jnp.float32) + b_router.astype(jnp.float32)
    top_vals, top_idx = lax.top_k(logits, TOP_K)
    gates = jax.nn.softmax(top_vals, axis=-1)

    n_assign = N * TOP_K
    flat_e = top_idx.reshape(-1).astype(jnp.int32)
    flat_tok = jnp.repeat(jnp.arange(N, dtype=jnp.int32), TOP_K)
    flat_g = gates.reshape(-1)
    order = jnp.argsort(flat_e)
    e_sorted = flat_e[order]
    counts = jnp.bincount(flat_e, length=N_EXPERTS).astype(jnp.int32)
    padded = (counts + MOE_BLK - 1) // MOE_BLK * MOE_BLK
    start = jnp.cumsum(counts) - counts
    pend = jnp.cumsum(padded)
    pstart = pend - padded
    rank = jnp.arange(n_assign, dtype=jnp.int32) - start[e_sorted]
    dest = pstart[e_sorted] + rank

    n_slots = -(-(n_assign + N_EXPERTS * (MOE_BLK - 1)) // MOE_BLK) * MOE_BLK
    n_blocks = n_slots // MOE_BLK
    slot_tok = jnp.full((n_slots,), N, jnp.int32).at[dest].set(flat_tok[order])
    slot_gate = jnp.zeros((n_slots,), jnp.float32).at[dest].set(flat_g[order])
    block_start = jnp.arange(n_blocks, dtype=jnp.int32) * MOE_BLK
    block_expert = jnp.minimum(jnp.searchsorted(pend, block_start, side='right'),
                               N_EXPERTS - 1).astype(jnp.int32)
    t_pad = jnp.concatenate([t, jnp.zeros((1, D), t.dtype)], axis=0)

    def expert_block(args):
        tok, g, e = args
        xb = t_pad[tok]
        x_glu = (jnp.matmul(xb, w_gate[e]) + b_gate[e]).astype(jnp.float32)
        x_lin = (jnp.matmul(xb, w_up[e]) + b_up[e]).astype(jnp.float32)
        x_glu = jnp.minimum(x_glu, SWIGLU_LIMIT)
        x_lin = jnp.clip(x_lin, -SWIGLU_LIMIT, SWIGLU_LIMIT)
        act = (x_glu * jax.nn.sigmoid(SWIGLU_ALPHA * x_glu) * (x_lin + 1.0)).astype(xb.dtype)
        y = (jnp.matmul(act, w_down[e]) + b_down[e]).astype(jnp.float32)
        return (y * g[:, None]).astype(t.dtype)

    ys = lax.map(expert_block, (slot_tok.reshape(n_blocks, MOE_BLK),
                                slot_gate.reshape(n_blocks, MOE_BLK), block_expert))
    out = jax.ops.segment_sum(ys.reshape(n_slots, D), slot_tok, num_segments=N + 1)[:N]
    return out.reshape(B, S, D)


def setup_inputs(seed: int = 0) -> dict:
    key = jax.random.key(seed)
    ks = jax.random.split(key, 20)
    f32 = jnp.float32
    nrm = lambda k, shape, s: jax.random.normal(k, shape, f32) * s
    return {
        "x": nrm(ks[0], (BATCH, SEQ, D_MODEL), 1.0),
        "g_attn": 1.0 + nrm(ks[1], (D_MODEL,), 0.02),
        "w_qkv": nrm(ks[2], (D_MODEL, 3 * D_MIX), D_MODEL ** -0.5),
        "rel_bias": nrm(ks[3], (REL_BUCKETS, N_HEADS_DIL), 0.1),
        "g_out_dil": 1.0 + nrm(ks[4], (D_DIL,), 0.02),
        "g_out_sb": 1.0 + nrm(ks[5], (D_SB,), 0.02),
        "w_o": nrm(ks[6], (D_MIX, D_MODEL), D_MIX ** -0.5),
        "g_moe": 1.0 + nrm(ks[7], (D_MODEL,), 0.02),
        "w_router": nrm(ks[8], (D_MODEL, N_EXPERTS), D_MODEL ** -0.5),
        "b_router": nrm(ks[9], (N_EXPERTS,), 0.01),
        "w_gate": nrm(ks[10], (N_EXPERTS, D_MODEL, D_EXPERT), D_MODEL ** -0.5),
        "b_gate": nrm(ks[11], (N_EXPERTS, D_EXPERT), 0.02),
        "w_up": nrm(ks[12], (N_EXPERTS, D_MODEL, D_EXPERT), D_MODEL ** -0.5),
        "b_up": nrm(ks[13], (N_EXPERTS, D_EXPERT), 0.02),
        "w_down": nrm(ks[14], (N_EXPERTS, D_EXPERT, D_MODEL), D_EXPERT ** -0.5),
        "b_down": nrm(ks[15], (N_EXPERTS, D_MODEL), 0.02),
        "g_final": 1.0 + nrm(ks[16], (D_MODEL,), 0.02),
    }


def reference(x, g_attn, w_qkv, rel_bias, g_out_dil, g_out_sb, w_o, g_moe,
              w_router, b_router, w_gate, b_gate, w_up, b_up, w_down, b_down, g_final):
    B, S, _ = x.shape
    for _layer in range(DEPTH):
        h = rmsnorm(x, g_attn)
        qkv = jnp.matmul(h, w_qkv)
        qa, ka, va, qb, kb, vb = jnp.split(
            qkv, [D_DIL, 2 * D_DIL, 3 * D_DIL, 3 * D_DIL + D_SB, 3 * D_DIL + 2 * D_SB], axis=-1)
        heads_a = lambda a: a.reshape(B, S, N_HEADS_DIL, HEAD_DIM)
        heads_b = lambda a: a.reshape(B, S, N_HEADS_SB, HEAD_DIM)
        oa = dilated_mixture(heads_a(qa), heads_a(ka), heads_a(va), rel_bias)
        oa = rmsnorm(oa.reshape(B, S, D_DIL).astype(x.dtype), g_out_dil)
        ob = stick_breaking_attention(heads_b(qb), heads_b(kb), heads_b(vb))
        ob = rmsnorm(ob.reshape(B, S, D_SB).astype(x.dtype), g_out_sb)
        mix = jnp.matmul(jnp.concatenate([oa, ob], axis=-1), w_o)
        x = x + mix
        x = x + moe(rmsnorm(x, g_moe), w_router, b_router, w_gate, b_gate,
                    w_up, b_up, w_down, b_down)
    return rmsnorm(x, g_final)
```

```python
import functools
import math

import jax
import jax.numpy as jnp
from jax import lax
from jax.experimental import pallas as pl
from jax.experimental.pallas import tpu as pltpu

F32 = jnp.float32
BF16 = jnp.bfloat16

D_MODEL = 1024
HEAD_DIM = 64
D_HALF = 512
N_HEADS = 8
LANES = 128
N_PAIRS = D_HALF // LANES
DIL_PATTERNS = ((128, 1), (512, 4), (2048, 16))
BAND = 128
REL_BUCKETS = 32
REL_MAX_DISTANCE = 2048
N_EXPERTS = 32
TOP_K = 4
SWIGLU_ALPHA = 1.702
SWIGLU_LIMIT = 7.0
RMS_EPS = 1e-6
NEG_INF = -1e30

QKV_TM = 512
SB_TQ = 256
SB_KB = 128
MIX_TM = 256
ROUTER_TM = 512
DISPATCH_R = 256
EXPERT_TM = 256
COMBINE_R = 128
VMEM_LIMIT = 48 * 1024 * 1024


def _cparams(sem):
    return pltpu.CompilerParams(dimension_semantics=sem, vmem_limit_bytes=VMEM_LIMIT)


def _mm(a, b):
    return jnp.dot(a, b, preferred_element_type=F32)


def _mm_nt(a, b):
    return lax.dot_general(a, b, (((1,), (1,)), ((), ())), preferred_element_type=F32)


def _split_bf16(a):
    hi = a.astype(BF16)
    lo = (a - hi.astype(F32)).astype(BF16)
    return hi, lo


def _mm_split(a, b_bf16):
    hi, lo = _split_bf16(a)
    return _mm(hi, b_bf16) + _mm(lo, b_bf16)


def _rms(x):
    return x * lax.rsqrt(jnp.mean(x * x, axis=-1, keepdims=True) + RMS_EPS)


def _half_masks(dtype):
    lane = lax.broadcasted_iota(jnp.int32, (1, LANES), 1)
    lo = jnp.where(lane < HEAD_DIM, 1.0, 0.0).astype(dtype)
    hi = jnp.where(lane >= HEAD_DIM, 1.0, 0.0).astype(dtype)
    return lo, hi


def _qkv_kernel(x_ref, g_ref, w_ref, o_ref):
    h = (_rms(x_ref[...]) * g_ref[...]).astype(BF16)
    for c in range(3 * D_MODEL // D_HALF):
        y = _mm(h, w_ref[:, c * D_HALF:(c + 1) * D_HALF])
        if c in (0, 3):
            y = y * (1.0 / math.sqrt(HEAD_DIM))
        o_ref[:, c * D_HALF:(c + 1) * D_HALF] = y.astype(BF16)


def _qkv(x2, g_attn, w_qkv):
    n = x2.shape[0]
    return pl.pallas_call(
        _qkv_kernel,
        grid=(n // QKV_TM,),
        in_specs=[pl.BlockSpec((QKV_TM, D_MODEL), lambda i: (i, 0)),
                  pl.BlockSpec((1, D_MODEL), lambda i: (0, 0)),
                  pl.BlockSpec((D_MODEL, 3 * D_MODEL), lambda i: (0, 0))],
        out_specs=pl.BlockSpec((QKV_TM, 3 * D_MODEL), lambda i: (i, 0)),
        out_shape=jax.ShapeDtypeStruct((n, 3 * D_MODEL), BF16),
        compiler_params=_cparams(("parallel",)),
        name="qkv",
    )(x2, g_attn.reshape(1, D_MODEL), w_qkv.astype(BF16))


def _t5_bucket(dist):
    max_exact = REL_BUCKETS // 2
    d_f = jnp.maximum(dist, 1).astype(jnp.float32)
    large = max_exact + (jnp.log(d_f / max_exact)
                         / math.log(REL_MAX_DISTANCE / max_exact)
                         * (REL_BUCKETS - max_exact)).astype(jnp.int32)
    large = jnp.minimum(large, REL_BUCKETS - 1)
    return jnp.where(dist < max_exact, dist, large)


def _band_bias(rel_bias, dilation):
    qi = jnp.arange(BAND, dtype=jnp.int32)[:, None]
    kj = jnp.arange(2 * BAND, dtype=jnp.int32)[None, :]
    rel = qi + BAND - kj
    ok = (rel >= 0) & (rel <= BAND)
    bias = rel_bias.astype(F32)[_t5_bucket(jnp.maximum(rel, 0) * dilation)]
    return jnp.where(ok[None], bias.transpose(2, 0, 1), NEG_INF)


def _dilated_kernel(q_ref, kp_ref, kc_ref, vp_ref, vc_ref, bias_ref, o_ref, lse_ref):
    n = pl.program_id(2)
    kj = lax.broadcasted_iota(jnp.int32, (BAND, 2 * BAND), 1)
    has_prev = jnp.logical_or(n > 0, kj >= BAND)
    lane = lax.broadcasted_iota(jnp.int32, (BAND, LANES), 1)
    m_lo, m_hi = _half_masks(BF16)
    lse_tile = jnp.zeros((BAND, LANES), F32)
    for p in range(N_PAIRS):
        cols = slice(p * LANES, (p + 1) * LANES)
        q2 = q_ref[0, :, cols]
        kcat = jnp.concatenate([kp_ref[0, :, cols], kc_ref[0, :, cols]], axis=0)
        vcat = jnp.concatenate([vp_ref[0, :, cols], vc_ref[0, :, cols]], axis=0)
        outs = []
        for s, msk in enumerate((m_lo, m_hi)):
            h = 2 * p + s
            logits = _mm_nt(q2 * msk, kcat) + bias_ref[h]
            logits = jnp.where(has_prev, logits, NEG_INF)
            m = jnp.max(logits, axis=-1, keepdims=True)
            pr = jnp.exp(logits - m)
            den = jnp.sum(pr, axis=-1, keepdims=True)
            outs.append(_mm(pr.astype(BF16), vcat) / den)
            lse_tile = jnp.where(lane == h, m + jnp.log(den), lse_tile)
        o_ref[0, :, cols] = jnp.where(lane < HEAD_DIM, outs[0], outs[1])
    lse_ref[0] = lse_tile


def _dilated(qkv3, rel_bias, dilation):
    b, s, _ = qkv3.shape
    d = dilation
    l = s // d
    nblk = l // BAND
    view = qkv3.reshape(b, l, d * 3 * D_MODEL)
    n_sec = 3 * D_MODEL // D_HALF
    blk = (1, BAND, D_HALF)

    def sec(k, prev):
        if prev:
            return pl.BlockSpec(blk, lambda bi, r, n: (bi, jnp.maximum(n - 1, 0), r * n_sec + k))
        return pl.BlockSpec(blk, lambda bi, r, n: (bi, n, r * n_sec + k))

    o, lse = pl.pallas_call(
        _dilated_kernel,
        grid=(b, d, nblk),
        in_specs=[sec(0, False), sec(1, True), sec(1, False), sec(2, True), sec(2, False),
                  pl.BlockSpec((N_HEADS, BAND, 2 * BAND), lambda bi, r, n: (0, 0, 0))],
        out_specs=[pl.BlockSpec((1, BAND, D_HALF), lambda bi, r, n: (bi, n, r)),
                   pl.BlockSpec((1, BAND, LANES), lambda bi, r, n: (bi, n, r))],
        out_shape=[jax.ShapeDtypeStruct((b, l, d * D_HALF), F32),
                   jax.ShapeDtypeStruct((b, l, d * LANES), F32)],
        compiler_params=_cparams(("parallel", "parallel", "arbitrary")),
        name=f"dilated_d{d}",
    )(view, view, view, view, view, _band_bias(rel_bias, d))
    return o.reshape(b * s, D_HALF), lse.reshape(b * s, LANES)


def _sb_kernel(q_ref, k_ref, v_ref, u_ref, o_ref, acc_ref, carry_ref):
    i = pl.program_id(2)
    nsub = SB_TQ // SB_KB
    row_g = i * SB_TQ + lax.broadcasted_iota(jnp.int32, (SB_TQ, SB_KB), 0)
    col_l = lax.broadcasted_iota(jnp.int32, (SB_TQ, SB_KB), 1)
    m_lo, m_hi = _half_masks(BF16)
    q2 = q_ref[0]

    def tile(qm, j, s, masked):
        start = pl.multiple_of(j * SB_KB, SB_KB)
        kj = k_ref[0, pl.ds(start, SB_KB), :]
        vj = v_ref[0, pl.ds(start, SB_KB), :]
        z = _mm_nt(qm, kj)
        sp = jnp.log(1.0 + jnp.exp(-jnp.abs(z)))
        log_b = jnp.minimum(z, 0.0) - sp
        l1m = log_b - z
        if masked:
            mask = (j * SB_KB + col_l) < row_g
            l1m = jnp.where(mask, l1m, 0.0)
        cum = _mm_split(l1m, u_ref[...])
        a = jnp.exp(log_b + cum[:, :SB_KB] + carry_ref[...])
        if masked:
            a = jnp.where(mask, a, 0.0)
        acc_ref[s] += _mm(a.astype(BF16), vj)
        carry_ref[...] += cum[:, SB_KB:]

    for s, msk in enumerate((m_lo, m_hi)):
        qm = q2 * msk
        acc_ref[s] = jnp.zeros((SB_TQ, LANES), F32)
        carry_ref[...] = jnp.zeros((SB_TQ, SB_KB), F32)
        for jb in range(nsub - 1, -1, -1):
            tile(qm, i * nsub + jb, s, True)

        def body(jj, c, qm=qm, s=s):
            tile(qm, i * nsub - 1 - jj, s, False)
            return c
        lax.fori_loop(0, i * nsub, body, 0)

    lane = lax.broadcasted_iota(jnp.int32, (SB_TQ, LANES), 1)
    o_ref[0] = jnp.where(lane < HEAD_DIM, acc_ref[0], acc_ref[1])


def _suffix_matrix():
    sp = jnp.arange(SB_KB, dtype=jnp.int32)[:, None]
    sc = jnp.arange(2 * SB_KB, dtype=jnp.int32)[None, :]
    return jnp.where((sc >= SB_KB) | (sp > sc), 1.0, 0.0).astype(BF16)


def _stickbreaking(qkv3):
    b, s, _ = qkv3.shape
    sec_q, sec_k, sec_v = 3 * N_PAIRS, 4 * N_PAIRS, 5 * N_PAIRS
    o = pl.pallas_call(
        _sb_kernel,
        grid=(b, N_PAIRS, s // SB_TQ),
        in_specs=[pl.BlockSpec((1, SB_TQ, LANES), lambda bi, p, i: (bi, i, sec_q + p)),
                  pl.BlockSpec((1, s, LANES), lambda bi, p, i: (bi, 0, sec_k + p)),
                  pl.BlockSpec((1, s, LANES), lambda bi, p, i: (bi, 0, sec_v + p)),
                  pl.BlockSpec((SB_KB, 2 * SB_KB), lambda bi, p, i: (0, 0))],
        out_specs=pl.BlockSpec((1, SB_TQ, LANES), lambda bi, p, i: (bi, i, p)),
        out_shape=jax.ShapeDtypeStruct((b, s, D_HALF), F32),
        scratch_shapes=[pltpu.VMEM((2, SB_TQ, LANES), F32), pltpu.VMEM((SB_TQ, SB_KB), F32)],
        compiler_params=_cparams(("parallel", "parallel", "arbitrary")),
        name="stickbreaking",
    )(qkv3, qkv3, qkv3, _suffix_matrix())
    return o.reshape(b * s, D_HALF)


def _mix_kernel(o1_ref, o2_ref, o3_ref, l1_ref, l2_ref, l3_ref, ob_ref, x_ref,
                ga_ref, gb_ref, e_ref, wo_ref, x1_ref):
    l1, l2, l3 = l1_ref[...], l2_ref[...], l3_ref[...]
    m = jnp.maximum(jnp.maximum(l1, l2), l3)
    e1, e2, e3 = jnp.exp(l1 - m), jnp.exp(l2 - m), jnp.exp(l3 - m)
    inv = 1.0 / (e1 + e2 + e3)
    expand = e_ref[...]
    oa = (_mm_split(e1 * inv, expand) * o1_ref[...]
          + _mm_split(e2 * inv, expand) * o2_ref[...]
          + _mm_split(e3 * inv, expand) * o3_ref[...])
    oa = (_rms(oa) * ga_ref[...]).astype(BF16)
    ob = (_rms(ob_ref[...]) * gb_ref[...]).astype(BF16)
    mix = _mm(oa, wo_ref[:D_HALF, :]) + _mm(ob, wo_ref[D_HALF:, :])
    x1_ref[...] = x_ref[...] + mix


def _head_expand():
    lane = jnp.arange(LANES, dtype=jnp.int32)[:, None]
    col = jnp.arange(D_HALF, dtype=jnp.int32)[None, :]
    return jnp.where(col // HEAD_DIM == lane, 1.0, 0.0).astype(BF16)


def _mix(os_, lses, ob, x2, g_out_dil, g_out_sb, w_o):
    n = x2.shape[0]
    row = lambda w: pl.BlockSpec((MIX_TM, w), lambda i: (i, 0))
    const = lambda shape: pl.BlockSpec(shape, lambda i: (0, 0))
    return pl.pallas_call(
        _mix_kernel,
        grid=(n // MIX_TM,),
        in_specs=[row(D_HALF)] * 3 + [row(LANES)] * 3 + [row(D_HALF), row(D_MODEL),
                  const((1, D_HALF)), const((1, D_HALF)), const((LANES, D_HALF)), const((D_MODEL, D_MODEL))],
        out_specs=row(D_MODEL),
        out_shape=jax.ShapeDtypeStruct((n, D_MODEL), F32),
        compiler_params=_cparams(("parallel",)),
        name="mix_wo",
    )(*os_, *lses, ob, x2, g_out_dil.reshape(1, D_HALF), g_out_sb.reshape(1, D_HALF),
      _head_expand(), w_o.astype(BF16))


def _router_kernel(x1_ref, g_ref, wr_ref, br_ref, tri_ref, h2_ref, meta_ref, gate_ref, cnt_ref, carry_ref):
    i = pl.program_id(0)

    @pl.when(i == 0)
    def _():
        carry_ref[...] = jnp.zeros_like(carry_ref)

    h2 = _rms(x1_ref[...]) * g_ref[...]
    h2_ref[...] = h2
    h_hi, h_lo = _split_bf16(h2)
    w_hi, w_lo = wr_ref[0], wr_ref[1]
    logits = _mm(h_hi, w_hi) + (_mm(h_hi, w_lo) + _mm(h_lo, w_hi)) + br_ref[...]
    lane = lax.broadcasted_iota(jnp.int32, (ROUTER_TM, LANES), 1)
    lane_f = lane.astype(F32)
    logits = jnp.where(lane < N_EXPERTS, logits, -jnp.inf)

    onehots, vals, ids = [], [], []
    for _k in range(TOP_K):
        m = jnp.max(logits, axis=-1, keepdims=True)
        idx = jnp.min(jnp.where(logits == m, lane_f, float(LANES)), axis=-1, keepdims=True)
        oh = lane_f == idx
        logits = jnp.where(oh, -jnp.inf, logits)
        onehots.append(oh)
        vals.append(m)
        ids.append(idx)

    es = [jnp.exp(v - vals[0]) for v in vals]
    inv = 1.0 / (es[0] + es[1] + es[2] + es[3])

    sel = jnp.zeros((ROUTER_TM, LANES), F32)
    for oh in onehots:
        sel = jnp.where(oh, 1.0, sel)
    before = _mm(tri_ref[...], sel.astype(BF16)) + carry_ref[0:1, :]
    meta = jnp.zeros((ROUTER_TM, LANES), F32)
    gate = jnp.zeros((ROUTER_TM, LANES), F32)
    for k in range(TOP_K):
        rank = jnp.sum(jnp.where(onehots[k], before, 0.0), axis=-1, keepdims=True)
        meta = jnp.where(lane == k, ids[k], meta)
        meta = jnp.where(lane == TOP_K + k, rank, meta)
        gate = jnp.where(lane == k, es[k] * inv, gate)
    meta_ref[...] = meta.astype(jnp.int32)
    gate_ref[...] = gate
    total = carry_ref[...] + jnp.sum(sel, axis=0, keepdims=True)
    carry_ref[...] = total
    cnt_ref[...] = total


def _router(x1, g_moe, w_router, b_router):
    n = x1.shape[0]
    wr = jnp.zeros((D_MODEL, LANES), F32).at[:, :N_EXPERTS].set(w_router.astype(F32))
    wr_hi = wr.astype(BF16)
    wr_lo = (wr - wr_hi.astype(F32)).astype(BF16)
    br = jnp.zeros((1, LANES), F32).at[0, :N_EXPERTS].set(b_router.astype(F32))
    r = jnp.arange(ROUTER_TM, dtype=jnp.int32)
    tri = jnp.where(r[None, :] < r[:, None], 1.0, 0.0).astype(BF16)
    row = lambda w: pl.BlockSpec((ROUTER_TM, w), lambda i: (i, 0))
    return pl.pallas_call(
        _router_kernel,
        grid=(n // ROUTER_TM,),
        in_specs=[row(D_MODEL),
                  pl.BlockSpec((1, D_MODEL), lambda i: (0, 0)),
                  pl.BlockSpec((2, D_MODEL, LANES), lambda i: (0, 0, 0)),
                  pl.BlockSpec((1, LANES), lambda i: (0, 0)),
                  pl.BlockSpec((ROUTER_TM, ROUTER_TM), lambda i: (0, 0))],
        out_specs=[row(D_MODEL), row(LANES), row(LANES), pl.BlockSpec((8, LANES), lambda i: (0, 0))],
        out_shape=[jax.ShapeDtypeStruct((n, D_MODEL), F32),
                   jax.ShapeDtypeStruct((n, LANES), jnp.int32),
                   jax.ShapeDtypeStruct((n, LANES), F32),
                   jax.ShapeDtypeStruct((8, LANES), F32)],
        scratch_shapes=[pltpu.VMEM((8, LANES), F32)],
        compiler_params=_cparams(("arbitrary",)),
        name="router",
    )(x1, g_moe.reshape(1, D_MODEL), jnp.stack([wr_hi, wr_lo]), br, tri)


def _dispatch_kernel(dest_ref, h2_ref, xs_in_ref, xs_ref, sem):
    del xs_in_ref

    def row_copy(r, k):
        d = dest_ref[0, 0, r * TOP_K + k]
        return pltpu.make_async_copy(h2_ref.at[pl.ds(r, 1)], xs_ref.at[pl.ds(d, 1)], sem)

    def start(r, c):
        for k in range(TOP_K):
            row_copy(r, k).start()
        return c

    def wait(r, c):
        for k in range(TOP_K):
            row_copy(r, k).wait()
        return c

    lax.fori_loop(0, DISPATCH_R, start, 0)
    lax.fori_loop(0, DISPATCH_R, wait, 0)


def _dispatch(h2, dest, n_slots):
    n = h2.shape[0]
    nb = n // DISPATCH_R
    return pl.pallas_call(
        _dispatch_kernel,
        grid=(nb,),
        in_specs=[pl.BlockSpec((1, 1, DISPATCH_R * TOP_K), lambda i: (i, 0, 0), memory_space=pltpu.SMEM),
                  pl.BlockSpec((DISPATCH_R, D_MODEL), lambda i: (i, 0)),
                  pl.BlockSpec(memory_space=pl.ANY)],
        out_specs=pl.BlockSpec(memory_space=pl.ANY),
        out_shape=jax.ShapeDtypeStruct((n_slots, D_MODEL), F32),
        scratch_shapes=[pltpu.SemaphoreType.DMA(())],
        input_output_aliases={2: 0},
        compiler_params=_cparams(("arbitrary",)),
        name="dispatch",
    )(dest.reshape(nb, 1, DISPATCH_R * TOP_K), h2, jnp.zeros((n_slots, D_MODEL), F32))


def _expert_kernel(be_ref, nu_ref, x_ref, wg_ref, bg_ref, wu_ref, bu_ref, wd_ref, bd_ref, y_ref):
    del be_ref

    @pl.when(pl.program_id(0) < nu_ref[0])
    def _():
        xb = x_ref[...].astype(BF16)
        y = jnp.zeros((EXPERT_TM, D_MODEL), F32) + bd_ref[0]
        for c in range(D_MODEL // D_HALF):
            cols = slice(c * D_HALF, (c + 1) * D_HALF)
            glu = _mm(xb, wg_ref[0, :, cols]) + bg_ref[0, :, cols]
            lin = _mm(xb, wu_ref[0, :, cols]) + bu_ref[0, :, cols]
            glu = jnp.minimum(glu, SWIGLU_LIMIT)
            lin = jnp.clip(lin, -SWIGLU_LIMIT, SWIGLU_LIMIT)
            act = glu * (1.0 / (1.0 + jnp.exp(-SWIGLU_ALPHA * glu))) * (lin + 1.0)
            y = y + _mm(act.astype(BF16), wd_ref[0, cols, :])
        y_ref[...] = y


def _experts(xs, block_expert, n_used, w_gate, b_gate, w_up, b_up, w_down, b_down):
    n_slots = xs.shape[0]
    n_blocks = n_slots // EXPERT_TM
    rows = lambda i, be, nu: (jnp.minimum(i, nu[0] - 1), 0)
    wspec = pl.BlockSpec((1, D_MODEL, D_MODEL), lambda i, be, nu: (be[i], 0, 0))
    bspec = pl.BlockSpec((1, 1, D_MODEL), lambda i, be, nu: (be[i], 0, 0))
    grid_spec = pltpu.PrefetchScalarGridSpec(
        num_scalar_prefetch=2,
        grid=(n_blocks,),
        in_specs=[pl.BlockSpec((EXPERT_TM, D_MODEL), rows), wspec, bspec, wspec, bspec, wspec, bspec],
        out_specs=pl.BlockSpec((EXPERT_TM, D_MODEL), rows),
    )
    b3 = lambda a: a.astype(F32).reshape(N_EXPERTS, 1, D_MODEL)
    return pl.pallas_call(
        _expert_kernel,
        grid_spec=grid_spec,
        out_shape=jax.ShapeDtypeStruct((n_slots, D_MODEL), F32),
        compiler_params=_cparams(("arbitrary",)),
        name="experts",
    )(block_expert, n_used, xs, w_gate.astype(BF16), b3(b_gate), w_up.astype(BF16), b3(b_up),
      w_down.astype(BF16), b3(b_down))


def _combine_kernel(dest_ref, ys_ref, gate_ref, x1_ref, g_ref, o_ref, buf_ref, sem):
    def row_copy(r, k):
        d = dest_ref[0, 0, r * TOP_K + k]
        return pltpu.make_async_copy(ys_ref.at[pl.ds(d, 1)], buf_ref.at[k, pl.ds(r, 1)], sem)

    def start(r, c):
        for k in range(TOP_K):
            row_copy(r, k).start()
        return c

    def wait(r, c):
        for k in range(TOP_K):
            row_copy(r, k).wait()
        return c

    lax.fori_loop(0, COMBINE_R, start, 0)
    lax.fori_loop(0, COMBINE_R, wait, 0)
    gate = gate_ref[...]
    x = x1_ref[...]
    for k in range(TOP_K):
        x = x + gate[:, k:k + 1] * buf_ref[k]
    o_ref[...] = _rms(x) * g_ref[...]


def _combine(ys, dest, gates, x1, g_final):
    n = x1.shape[0]
    nb = n // COMBINE_R
    row = lambda w: pl.BlockSpec((COMBINE_R, w), lambda i: (i, 0))
    return pl.pallas_call(
        _combine_kernel,
        grid=(nb,),
        in_specs=[pl.BlockSpec((1, 1, COMBINE_R * TOP_K), lambda i: (i, 0, 0), memory_space=pltpu.SMEM),
                  pl.BlockSpec(memory_space=pl.ANY),
                  row(LANES), row(D_MODEL),
                  pl.BlockSpec((1, D_MODEL), lambda i: (0, 0))],
        out_specs=row(D_MODEL),
        out_shape=jax.ShapeDtypeStruct((n, D_MODEL), F32),
        scratch_shapes=[pltpu.VMEM((TOP_K, COMBINE_R, D_MODEL), F32), pltpu.SemaphoreType.DMA(())],
        compiler_params=_cparams(("arbitrary",)),
        name="combine",
    )(dest.reshape(nb, 1, COMBINE_R * TOP_K), ys, gates, x1, g_final.reshape(1, D_MODEL))


def _slot_layout(meta, counts_f, n):
    eid = meta[:, :TOP_K]
    rank = meta[:, TOP_K:2 * TOP_K]
    counts = counts_f[0, :N_EXPERTS].astype(jnp.int32)
    padded = (counts + EXPERT_TM - 1) // EXPERT_TM * EXPERT_TM
    pend = jnp.cumsum(padded)
    pstart = pend - padded
    dest = (pstart[eid] + rank).astype(jnp.int32)
    n_slots = -(-(n * TOP_K + N_EXPERTS * (EXPERT_TM - 1)) // EXPERT_TM) * EXPERT_TM
    block_start = jnp.arange(n_slots // EXPERT_TM, dtype=jnp.int32) * EXPERT_TM
    block_expert = jnp.minimum(jnp.searchsorted(pend, block_start, side='right'),
                               N_EXPERTS - 1).astype(jnp.int32)
    n_used = (pend[-1:] // EXPERT_TM).astype(jnp.int32)
    return dest, block_expert, n_used, n_slots


def kernel(x, g_attn, w_qkv, rel_bias, g_out_dil, g_out_sb, w_o, g_moe, w_router, b_router,
           w_gate, b_gate, w_up, b_up, w_down, b_down, g_final):
    b, s, d = x.shape
    n = b * s
    x2 = x.reshape(n, d)
    qkv3 = _qkv(x2, g_attn, w_qkv).reshape(b, s, 3 * d)
    dil = [_dilated(qkv3, rel_bias, dilation) for _window, dilation in DIL_PATTERNS]
    ob = _stickbreaking(qkv3)
    x1 = _mix([o for o, _ in dil], [l for _, l in dil], ob, x2, g_out_dil, g_out_sb, w_o)
    h2, meta, gates, counts = _router(x1, g_moe, w_router, b_router)
    dest, block_expert, n_used, n_slots = _slot_layout(meta, counts, n)
    xs = _dispatch(h2, dest, n_slots)
    ys = _experts(xs, block_expert, n_used, w_gate, b_gate, w_up, b_up, w_down, b_down)
    out = _combine(ys, dest, gates, x1, g_final)
    return out.reshape(b, s, d)
```

```python
import functools
import math

import jax
import jax.numpy as jnp
from jax import lax
from jax.experimental import pallas as pl
from jax.experimental.pallas import tpu as pltpu

F32 = jnp.float32
BF16 = jnp.bfloat16

D_MODEL = 1024
HEAD_DIM = 64
D_HALF = 512
N_HEADS = 8
LANES = 128
N_PAIRS = D_HALF // LANES
DIL_PATTERNS = ((128, 1), (512, 4), (2048, 16))
BAND = 128
REL_BUCKETS = 32
REL_MAX_DISTANCE = 2048
N_EXPERTS = 32
TOP_K = 4
SWIGLU_ALPHA = 1.702
SWIGLU_LIMIT = 7.0
RMS_EPS = 1e-6
NEG_INF = -1e30

QKV_TM = 512
SB_T = 256
SB_EXP_ZERO = 104.0
MIX_TM = 256
ROUTER_TM = 512
DISPATCH_R = 256
EXPERT_TM = 256
COMBINE_R = 128
VMEM_LIMIT = 48 * 1024 * 1024


def _cparams(sem):
    return pltpu.CompilerParams(dimension_semantics=sem, vmem_limit_bytes=VMEM_LIMIT)


def _mm(a, b):
    return jnp.dot(a, b, preferred_element_type=F32)


def _mm_nt(a, b):
    return lax.dot_general(a, b, (((1,), (1,)), ((), ())), preferred_element_type=F32)


def _split_bf16(a):
    hi = a.astype(BF16)
    lo = (a - hi.astype(F32)).astype(BF16)
    return hi, lo


def _mm_split(a, b_bf16):
    hi, lo = _split_bf16(a)
    return _mm(hi, b_bf16) + _mm(lo, b_bf16)


def _rms(x):
    return x * lax.rsqrt(jnp.mean(x * x, axis=-1, keepdims=True) + RMS_EPS)


def _half_masks(dtype):
    lane = lax.broadcasted_iota(jnp.int32, (1, LANES), 1)
    lo = jnp.where(lane < HEAD_DIM, 1.0, 0.0).astype(dtype)
    hi = jnp.where(lane >= HEAD_DIM, 1.0, 0.0).astype(dtype)
    return lo, hi


def _qkv_kernel(x_ref, g_ref, w_ref, o_ref):
    h = (_rms(x_ref[...]) * g_ref[...]).astype(BF16)
    for c in range(3 * D_MODEL // D_HALF):
        y = _mm(h, w_ref[:, c * D_HALF:(c + 1) * D_HALF])
        if c in (0, 3):
            y = y * (1.0 / math.sqrt(HEAD_DIM))
        o_ref[:, c * D_HALF:(c + 1) * D_HALF] = y.astype(BF16)


def _qkv(x2, g_attn, w_qkv):
    n = x2.shape[0]
    return pl.pallas_call(
        _qkv_kernel,
        grid=(n // QKV_TM,),
        in_specs=[pl.BlockSpec((QKV_TM, D_MODEL), lambda i: (i, 0)),
                  pl.BlockSpec((1, D_MODEL), lambda i: (0, 0)),
                  pl.BlockSpec((D_MODEL, 3 * D_MODEL), lambda i: (0, 0))],
        out_specs=pl.BlockSpec((QKV_TM, 3 * D_MODEL), lambda i: (i, 0)),
        out_shape=jax.ShapeDtypeStruct((n, 3 * D_MODEL), BF16),
        compiler_params=_cparams(("parallel",)),
        name="qkv",
    )(x2, g_attn.reshape(1, D_MODEL), w_qkv.astype(BF16))


def _t5_bucket(dist):
    max_exact = REL_BUCKETS // 2
    d_f = jnp.maximum(dist, 1).astype(jnp.float32)
    large = max_exact + (jnp.log(d_f / max_exact)
                         / math.log(REL_MAX_DISTANCE / max_exact)
                         * (REL_BUCKETS - max_exact)).astype(jnp.int32)
    large = jnp.minimum(large, REL_BUCKETS - 1)
    return jnp.where(dist < max_exact, dist, large)


def _band_bias(rel_bias, dilation):
    qi = jnp.arange(BAND, dtype=jnp.int32)[:, None]
    kj = jnp.arange(2 * BAND, dtype=jnp.int32)[None, :]
    rel = qi + BAND - kj
    ok = (rel >= 0) & (rel <= BAND)
    bias = rel_bias.astype(F32)[_t5_bucket(jnp.maximum(rel, 0) * dilation)]
    return jnp.where(ok[None], bias.transpose(2, 0, 1), NEG_INF)


def _dilated_kernel(q_ref, kp_ref, kc_ref, vp_ref, vc_ref, bias_ref, o_ref, lse_ref):
    n = pl.program_id(2)
    kj = lax.broadcasted_iota(jnp.int32, (BAND, 2 * BAND), 1)
    has_prev = jnp.logical_or(n > 0, kj >= BAND)
    lane = lax.broadcasted_iota(jnp.int32, (BAND, LANES), 1)
    m_lo, m_hi = _half_masks(BF16)
    lse_tile = jnp.zeros((BAND, LANES), F32)
    for p in range(N_PAIRS):
        cols = slice(p * LANES, (p + 1) * LANES)
        q2 = q_ref[0, :, cols]
        kcat = jnp.concatenate([kp_ref[0, :, cols], kc_ref[0, :, cols]], axis=0)
        vcat = jnp.concatenate([vp_ref[0, :, cols], vc_ref[0, :, cols]], axis=0)
        outs = []
        for s, msk in enumerate((m_lo, m_hi)):
            h = 2 * p + s
            logits = _mm_nt(q2 * msk, kcat) + bias_ref[h]
            logits = jnp.where(has_prev, logits, NEG_INF)
            m = jnp.max(logits, axis=-1, keepdims=True)
            pr = jnp.exp(logits - m)
            den = jnp.sum(pr, axis=-1, keepdims=True)
            outs.append(_mm(pr.astype(BF16), vcat) / den)
            lse_tile = jnp.where(lane == h, m + jnp.log(den), lse_tile)
        o_ref[0, :, cols] = jnp.where(lane < HEAD_DIM, outs[0], outs[1])
    lse_ref[0] = lse_tile


def _dilated(qkv3, rel_bias, dilation):
    b, s, _ = qkv3.shape
    d = dilation
    l = s // d
    nblk = l // BAND
    view = qkv3.reshape(b, l, d * 3 * D_MODEL)
    n_sec = 3 * D_MODEL // D_HALF
    blk = (1, BAND, D_HALF)

    def sec(k, prev):
        if prev:
            return pl.BlockSpec(blk, lambda bi, r, n: (bi, jnp.maximum(n - 1, 0), r * n_sec + k))
        return pl.BlockSpec(blk, lambda bi, r, n: (bi, n, r * n_sec + k))

    o, lse = pl.pallas_call(
        _dilated_kernel,
        grid=(b, d, nblk),
        in_specs=[sec(0, False), sec(1, True), sec(1, False), sec(2, True), sec(2, False),
                  pl.BlockSpec((N_HEADS, BAND, 2 * BAND), lambda bi, r, n: (0, 0, 0))],
        out_specs=[pl.BlockSpec((1, BAND, D_HALF), lambda bi, r, n: (bi, n, r)),
                   pl.BlockSpec((1, BAND, LANES), lambda bi, r, n: (bi, n, r))],
        out_shape=[jax.ShapeDtypeStruct((b, l, d * D_HALF), F32),
                   jax.ShapeDtypeStruct((b, l, d * LANES), F32)],
        compiler_params=_cparams(("parallel", "parallel", "arbitrary")),
        name=f"dilated_d{d}",
    )(view, view, view, view, view, _band_bias(rel_bias, d))
    return o.reshape(b * s, D_HALF), lse.reshape(b * s, LANES)


def _sb_kernel(q_ref, k_ref, v_ref, u_ref, o_ref, acc_ref, carry_ref):
    i = pl.program_id(2)
    m_lo, m_hi = _half_masks(BF16)
    q2 = q_ref[0]
    qs = jnp.concatenate([q2 * m_lo, q2 * m_hi], axis=0)
    row = lax.broadcasted_iota(jnp.int32, (2 * SB_T, SB_T), 0) & (SB_T - 1)
    col = lax.broadcasted_iota(jnp.int32, (2 * SB_T, SB_T), 1)
    causal = col < row
    u = u_ref[...]

    def tile(j, mask, carry):
        start = pl.multiple_of(j * SB_T, SB_T)
        kj = k_ref[0, pl.ds(start, SB_T), :]
        vj = v_ref[0, pl.ds(start, SB_T), :]
        z = _mm_nt(qs, kj)
        sp = jnp.log(1.0 + jnp.exp(-jnp.abs(z)))
        log_b = jnp.minimum(z, 0.0) - sp
        l1m = log_b - z
        if mask is not None:
            l1m = jnp.where(mask, l1m, 0.0)
        cum = _mm_split(l1m, u)
        total = cum[:, 0:1] + l1m[:, 0:1]
        a = jnp.exp(log_b + cum + carry)
        if mask is not None:
            a = jnp.where(mask, a, 0.0)
        return _mm(a.astype(BF16), vj), total

    pv_d, tot_d = tile(i, causal, 0.0)
    pv_p, tot_p = tile(jnp.maximum(i - 1, 0), i > 0, tot_d)
    acc_ref[...] = pv_d + pv_p
    carry0 = tot_d + tot_p
    carry_ref[...] = carry0

    def more(state):
        j, top = state
        return jnp.logical_and(j >= 0, top > -SB_EXP_ZERO)

    def body(state):
        j, _ = state
        pv, tot = tile(j, None, carry_ref[...])
        acc_ref[...] += pv
        carry = carry_ref[...] + tot
        carry_ref[...] = carry
        return j - 1, jnp.max(carry)

    lax.while_loop(more, body, (i - 2, jnp.max(carry0)))
    lane = lax.broadcasted_iota(jnp.int32, (SB_T, LANES), 1)
    o_ref[0] = jnp.where(lane < HEAD_DIM, acc_ref[:SB_T], acc_ref[SB_T:])


def _suffix_matrix():
    sp = jnp.arange(SB_T, dtype=jnp.int32)[:, None]
    sc = jnp.arange(SB_T, dtype=jnp.int32)[None, :]
    return jnp.where(sp > sc, 1.0, 0.0).astype(BF16)


def _stickbreaking(qkv3):
    b, s, _ = qkv3.shape
    sec_q, sec_k, sec_v = 3 * N_PAIRS, 4 * N_PAIRS, 5 * N_PAIRS
    o = pl.pallas_call(
        _sb_kernel,
        grid=(b, N_PAIRS, s // SB_T),
        in_specs=[pl.BlockSpec((1, SB_T, LANES), lambda bi, p, i: (bi, i, sec_q + p)),
                  pl.BlockSpec((1, s, LANES), lambda bi, p, i: (bi, 0, sec_k + p)),
                  pl.BlockSpec((1, s, LANES), lambda bi, p, i: (bi, 0, sec_v + p)),
                  pl.BlockSpec((SB_T, SB_T), lambda bi, p, i: (0, 0))],
        out_specs=pl.BlockSpec((1, SB_T, LANES), lambda bi, p, i: (bi, i, p)),
        out_shape=jax.ShapeDtypeStruct((b, s, D_HALF), F32),
        scratch_shapes=[pltpu.VMEM((2 * SB_T, LANES), F32), pltpu.VMEM((2 * SB_T, 1), F32)],
        compiler_params=_cparams(("parallel", "parallel", "arbitrary")),
        name="stickbreaking",
    )(qkv3, qkv3, qkv3, _suffix_matrix())
    return o.reshape(b * s, D_HALF)


def _mix_kernel(o1_ref, o2_ref, o3_ref, l1_ref, l2_ref, l3_ref, ob_ref, x_ref,
                ga_ref, gb_ref, e_ref, wo_ref, x1_ref):
    l1, l2, l3 = l1_ref[...], l2_ref[...], l3_ref[...]
    m = jnp.maximum(jnp.maximum(l1, l2), l3)
    e1, e2, e3 = jnp.exp(l1 - m), jnp.exp(l2 - m), jnp.exp(l3 - m)
    inv = 1.0 / (e1 + e2 + e3)
    expand = e_ref[...]
    oa = (_mm_split(e1 * inv, expand) * o1_ref[...]
          + _mm_split(e2 * inv, expand) * o2_ref[...]
          + _mm_split(e3 * inv, expand) * o3_ref[...])
    oa = (_rms(oa) * ga_ref[...]).astype(BF16)
    ob = (_rms(ob_ref[...]) * gb_ref[...]).astype(BF16)
    mix = _mm(oa, wo_ref[:D_HALF, :]) + _mm(ob, wo_ref[D_HALF:, :])
    x1_ref[...] = x_ref[...] + mix


def _head_expand():
    lane = jnp.arange(LANES, dtype=jnp.int32)[:, None]
    col = jnp.arange(D_HALF, dtype=jnp.int32)[None, :]
    return jnp.where(col // HEAD_DIM == lane, 1.0, 0.0).astype(BF16)


def _mix(os_, lses, ob, x2, g_out_dil, g_out_sb, w_o):
    n = x2.shape[0]
    row = lambda w: pl.BlockSpec((MIX_TM, w), lambda i: (i, 0))
    const = lambda shape: pl.BlockSpec(shape, lambda i: (0, 0))
    return pl.pallas_call(
        _mix_kernel,
        grid=(n // MIX_TM,),
        in_specs=[row(D_HALF)] * 3 + [row(LANES)] * 3 + [row(D_HALF), row(D_MODEL),
                  const((1, D_HALF)), const((1, D_HALF)), const((LANES, D_HALF)), const((D_MODEL, D_MODEL))],
        out_specs=row(D_MODEL),
        out_shape=jax.ShapeDtypeStruct((n, D_MODEL), F32),
        compiler_params=_cparams(("parallel",)),
        name="mix_wo",
    )(*os_, *lses, ob, x2, g_out_dil.reshape(1, D_HALF), g_out_sb.reshape(1, D_HALF),
      _head_expand(), w_o.astype(BF16))


def _router_kernel(x1_ref, g_ref, wr_ref, br_ref, tri_ref, h2_ref, meta_ref, gate_ref, cnt_ref, carry_ref):
    i = pl.program_id(0)

    @pl.when(i == 0)
    def _():
        carry_ref[...] = jnp.zeros_like(carry_ref)

    h2 = _rms(x1_ref[...]) * g_ref[...]
    h2_ref[...] = h2
    h_hi, h_lo = _split_bf16(h2)
    w_hi, w_lo = wr_ref[0], wr_ref[1]
    logits = _mm(h_hi, w_hi) + (_mm(h_hi, w_lo) + _mm(h_lo, w_hi)) + br_ref[...]
    lane = lax.broadcasted_iota(jnp.int32, (ROUTER_TM, LANES), 1)
    lane_f = lane.astype(F32)
    logits = jnp.where(lane < N_EXPERTS, logits, -jnp.inf)

    onehots, vals, ids = [], [], []
    for _k in range(TOP_K):
        m = jnp.max(logits, axis=-1, keepdims=True)
        idx = jnp.min(jnp.where(logits == m, lane_f, float(LANES)), axis=-1, keepdims=True)
        oh = lane_f == idx
        logits = jnp.where(oh, -jnp.inf, logits)
        onehots.append(oh)
        vals.append(m)
        ids.append(idx)

    es = [jnp.exp(v - vals[0]) for v in vals]
    inv = 1.0 / (es[0] + es[1] + es[2] + es[3])

    sel = jnp.zeros((ROUTER_TM, LANES), F32)
    for oh in onehots:
        sel = jnp.where(oh, 1.0, sel)
    before = _mm(tri_ref[...], sel.astype(BF16)) + carry_ref[0:1, :]
    meta = jnp.zeros((ROUTER_TM, LANES), F32)
    gate = jnp.zeros((ROUTER_TM, LANES), F32)
    for k in range(TOP_K):
        rank = jnp.sum(jnp.where(onehots[k], before, 0.0), axis=-1, keepdims=True)
        meta = jnp.where(lane == k, ids[k], meta)
        meta = jnp.where(lane == TOP_K + k, rank, meta)
        gate = jnp.where(lane == k, es[k] * inv, gate)
    meta_ref[...] = meta.astype(jnp.int32)
    gate_ref[...] = gate
    total = carry_ref[...] + jnp.sum(sel, axis=0, keepdims=True)
    carry_ref[...] = total
    cnt_ref[...] = total


def _router(x1, g_moe, w_router, b_router):
    n = x1.shape[0]
    wr = jnp.zeros((D_MODEL, LANES), F32).at[:, :N_EXPERTS].set(w_router.astype(F32))
    wr_hi = wr.astype(BF16)
    wr_lo = (wr - wr_hi.astype(F32)).astype(BF16)
    br = jnp.zeros((1, LANES), F32).at[0, :N_EXPERTS].set(b_router.astype(F32))
    r = jnp.arange(ROUTER_TM, dtype=jnp.int32)
    tri = jnp.where(r[None, :] < r[:, None], 1.0, 0.0).astype(BF16)
    row = lambda w: pl.BlockSpec((ROUTER_TM, w), lambda i: (i, 0))
    return pl.pallas_call(
        _router_kernel,
        grid=(n // ROUTER_TM,),
        in_specs=[row(D_MODEL),
                  pl.BlockSpec((1, D_MODEL), lambda i: (0, 0)),
                  pl.BlockSpec((2, D_MODEL, LANES), lambda i: (0, 0, 0)),
                  pl.BlockSpec((1, LANES), lambda i: (0, 0)),
                  pl.BlockSpec((ROUTER_TM, ROUTER_TM), lambda i: (0, 0))],
        out_specs=[row(D_MODEL), row(LANES), row(LANES), pl.BlockSpec((8, LANES), lambda i: (0, 0))],
        out_shape=[jax.ShapeDtypeStruct((n, D_MODEL), F32),
                   jax.ShapeDtypeStruct((n, LANES), jnp.int32),
                   jax.ShapeDtypeStruct((n, LANES), F32),
                   jax.ShapeDtypeStruct((8, LANES), F32)],
        scratch_shapes=[pltpu.VMEM((8, LANES), F32)],
        compiler_params=_cparams(("arbitrary",)),
        name="router",
    )(x1, g_moe.reshape(1, D_MODEL), jnp.stack([wr_hi, wr_lo]), br, tri)


def _dispatch_kernel(dest_ref, h2_ref, xs_in_ref, xs_ref, sem):
    del xs_in_ref

    def row_copy(r, k):
        d = dest_ref[0, 0, r * TOP_K + k]
        return pltpu.make_async_copy(h2_ref.at[pl.ds(r, 1)], xs_ref.at[pl.ds(d, 1)], sem)

    def start(r, c):
        for k in range(TOP_K):
            row_copy(r, k).start()
        return c

    def wait(r, c):
        for k in range(TOP_K):
            row_copy(r, k).wait()
        return c

    lax.fori_loop(0, DISPATCH_R, start, 0)
    lax.fori_loop(0, DISPATCH_R, wait, 0)


def _dispatch(h2, dest, n_slots):
    n = h2.shape[0]
    nb = n // DISPATCH_R
    return pl.pallas_call(
        _dispatch_kernel,
        grid=(nb,),
        in_specs=[pl.BlockSpec((1, 1, DISPATCH_R * TOP_K), lambda i: (i, 0, 0), memory_space=pltpu.SMEM),
                  pl.BlockSpec((DISPATCH_R, D_MODEL), lambda i: (i, 0)),
                  pl.BlockSpec(memory_space=pl.ANY)],
        out_specs=pl.BlockSpec(memory_space=pl.ANY),
        out_shape=jax.ShapeDtypeStruct((n_slots, D_MODEL), F32),
        scratch_shapes=[pltpu.SemaphoreType.DMA(())],
        input_output_aliases={2: 0},
        compiler_params=_cparams(("arbitrary",)),
        name="dispatch",
    )(dest.reshape(nb, 1, DISPATCH_R * TOP_K), h2, jnp.zeros((n_slots, D_MODEL), F32))


def _expert_kernel(be_ref, nu_ref, x_ref, wg_ref, bg_ref, wu_ref, bu_ref, wd_ref, bd_ref, y_ref):
    del be_ref

    @pl.when(pl.program_id(0) < nu_ref[0])
    def _():
        xb = x_ref[...].astype(BF16)
        y = jnp.zeros((EXPERT_TM, D_MODEL), F32) + bd_ref[0]
        for c in range(D_MODEL // D_HALF):
            cols = slice(c * D_HALF, (c + 1) * D_HALF)
            glu = _mm(xb, wg_ref[0, :, cols]) + bg_ref[0, :, cols]
            lin = _mm(xb, wu_ref[0, :, cols]) + bu_ref[0, :, cols]
            glu = jnp.minimum(glu, SWIGLU_LIMIT)
            lin = jnp.clip(lin, -SWIGLU_LIMIT, SWIGLU_LIMIT)
            act = glu * (1.0 / (1.0 + jnp.exp(-SWIGLU_ALPHA * glu))) * (lin + 1.0)
            y = y + _mm(act.astype(BF16), wd_ref[0, cols, :])
        y_ref[...] = y


def _experts(xs, block_expert, n_used, w_gate, b_gate, w_up, b_up, w_down, b_down):
    n_slots = xs.shape[0]
    n_blocks = n_slots // EXPERT_TM
    rows = lambda i, be, nu: (jnp.minimum(i, nu[0] - 1), 0)
    wspec = pl.BlockSpec((1, D_MODEL, D_MODEL), lambda i, be, nu: (be[i], 0, 0))
    bspec = pl.BlockSpec((1, 1, D_MODEL), lambda i, be, nu: (be[i], 0, 0))
    grid_spec = pltpu.PrefetchScalarGridSpec(
        num_scalar_prefetch=2,
        grid=(n_blocks,),
        in_specs=[pl.BlockSpec((EXPERT_TM, D_MODEL), rows), wspec, bspec, wspec, bspec, wspec, bspec],
        out_specs=pl.BlockSpec((EXPERT_TM, D_MODEL), rows),
    )
    b3 = lambda a: a.astype(F32).reshape(N_EXPERTS, 1, D_MODEL)
    return pl.pallas_call(
        _expert_kernel,
        grid_spec=grid_spec,
        out_shape=jax.ShapeDtypeStruct((n_slots, D_MODEL), F32),
        compiler_params=_cparams(("arbitrary",)),
        name="experts",
    )(block_expert, n_used, xs, w_gate.astype(BF16), b3(b_gate), w_up.astype(BF16), b3(b_up),
      w_down.astype(BF16), b3(b_down))


def _combine_kernel(dest_ref, ys_ref, gate_ref, x1_ref, g_ref, o_ref, buf_ref, sem):
    def row_copy(r, k):
        d = dest_ref[0, 0, r * TOP_K + k]
        return pltpu.make_async_copy(ys_ref.at[pl.ds(d, 1)], buf_ref.at[k, pl.ds(r, 1)], sem)

    def start(r, c):
        for k in range(TOP_K):
            row_copy(r, k).start()
        return c

    def wait(r, c):
        for k in range(TOP_K):
            row_copy(r, k).wait()
        return c

    lax.fori_loop(0, COMBINE_R, start, 0)
    lax.fori_loop(0, COMBINE_R, wait, 0)
    gate = gate_ref[...]
    x = x1_ref[...]
    for k in range(TOP_K):
        x = x + gate[:, k:k + 1] * buf_ref[k]
    o_ref[...] = _rms(x) * g_ref[...]


def _combine(ys, dest, gates, x1, g_final):
    n = x1.shape[0]
    nb = n // COMBINE_R
    row = lambda w: pl.BlockSpec((COMBINE_R, w), lambda i: (i, 0))
    return pl.pallas_call(
        _combine_kernel,
        grid=(nb,),
        in_specs=[pl.BlockSpec((1, 1, COMBINE_R * TOP_K), lambda i: (i, 0, 0), memory_space=pltpu.SMEM),
                  pl.BlockSpec(memory_space=pl.ANY),
                  row(LANES), row(D_MODEL),
                  pl.BlockSpec((1, D_MODEL), lambda i: (0, 0))],
        out_specs=row(D_MODEL),
        out_shape=jax.ShapeDtypeStruct((n, D_MODEL), F32),
        scratch_shapes=[pltpu.VMEM((TOP_K, COMBINE_R, D_MODEL), F32), pltpu.SemaphoreType.DMA(())],
        compiler_params=_cparams(("arbitrary",)),
        name="combine",
    )(dest.reshape(nb, 1, COMBINE_R * TOP_K), ys, gates, x1, g_final.reshape(1, D_MODEL))


def _slot_layout(meta, counts_f, n):
    eid = meta[:, :TOP_K]
    rank = meta[:, TOP_K:2 * TOP_K]
    counts = counts_f[0, :N_EXPERTS].astype(jnp.int32)
    padded = (counts + EXPERT_TM - 1) // EXPERT_TM * EXPERT_TM
    pend = jnp.cumsum(padded)
    pstart = pend - padded
    dest = (pstart[eid] + rank).astype(jnp.int32)
    n_slots = -(-(n * TOP_K + N_EXPERTS * (EXPERT_TM - 1)) // EXPERT_TM) * EXPERT_TM
    block_start = jnp.arange(n_slots // EXPERT_TM, dtype=jnp.int32) * EXPERT_TM
    block_expert = jnp.minimum(jnp.searchsorted(pend, block_start, side='right'),
                               N_EXPERTS - 1).astype(jnp.int32)
    n_used = (pend[-1:] // EXPERT_TM).astype(jnp.int32)
    return dest, block_expert, n_used, n_slots


def kernel(x, g_attn, w_qkv, rel_bias, g_out_dil, g_out_sb, w_o, g_moe, w_router, b_router,
           w_gate, b_gate, w_up, b_up, w_down, b_down, g_final):
    b, s, d = x.shape
    n = b * s
    x2 = x.reshape(n, d)
    qkv3 = _qkv(x2, g_attn, w_qkv).reshape(b, s, 3 * d)
    dil = [_dilated(qkv3, rel_bias, dilation) for _window, dilation in DIL_PATTERNS]
    ob = _stickbreaking(qkv3)
    x1 = _mix([o for o, _ in dil], [l for _, l in dil], ob, x2, g_out_dil, g_out_sb, w_o)
    h2, meta, gates, counts = _router(x1, g_moe, w_router, b_router)
    dest, block_expert, n_used, n_slots = _slot_layout(meta, counts, n)
    xs = _dispatch(h2, dest, n_slots)
    ys = _experts(xs, block_expert, n_used, w_gate, b_gate, w_up, b_up, w_down, b_down)
    out = _combine(ys, dest, gates, x1, g_final)
    return out.reshape(b, s, d)
```

```python
import functools
import math

import jax
import jax.numpy as jnp
from jax import lax
from jax.experimental import pallas as pl
from jax.experimental.pallas import tpu as pltpu

F32 = jnp.float32
BF16 = jnp.bfloat16

D_MODEL = 1024
HEAD_DIM = 64
D_HALF = 512
N_HEADS = 8
LANES = 128
N_PAIRS = D_HALF // LANES
DIL_PATTERNS = ((128, 1), (512, 4), (2048, 16))
BAND = 128
REL_BUCKETS = 32
REL_MAX_DISTANCE = 2048
N_EXPERTS = 32
TOP_K = 4
SWIGLU_ALPHA = 1.702
SWIGLU_LIMIT = 7.0
RMS_EPS = 1e-6
NEG_INF = -1e30

QKV_TM = 512
A_COLS = 3 * D_HALF
STRIDED_DILATIONS = tuple(d for _w, d in DIL_PATTERNS if d > 1)
LOG2E = math.log2(math.e)
SB_T = 256
SB_G = 4
SB_EXP_ZERO = 104.0
MIX_TM = 256
ROUTER_TM = 512
DISPATCH_R = 256
EXPERT_TM = 512
COMBINE_R = 128
VMEM_LIMIT = 48 * 1024 * 1024


def _cparams(sem):
    return pltpu.CompilerParams(dimension_semantics=sem, vmem_limit_bytes=VMEM_LIMIT)


def _mm(a, b):
    return jnp.dot(a, b, preferred_element_type=F32)


def _mm_nt(a, b):
    return lax.dot_general(a, b, (((1,), (1,)), ((), ())), preferred_element_type=F32)


def _split_bf16(a):
    hi = a.astype(BF16)
    lo = (a - hi.astype(F32)).astype(BF16)
    return hi, lo


def _mm_split(a, b_bf16):
    hi, lo = _split_bf16(a)
    return _mm(hi, b_bf16) + _mm(lo, b_bf16)


def _rms(x):
    return x * lax.rsqrt(jnp.mean(x * x, axis=-1, keepdims=True) + RMS_EPS)


def _half_masks(dtype):
    lane = lax.broadcasted_iota(jnp.int32, (1, LANES), 1)
    lo = jnp.where(lane < HEAD_DIM, 1.0, 0.0).astype(dtype)
    hi = jnp.where(lane >= HEAD_DIM, 1.0, 0.0).astype(dtype)
    return lo, hi


def _qkv_kernel(x_ref, g_ref, w_ref, o_ref, *rest):
    stage_ref = rest[-1]
    h = (_rms(x_ref[...]) * g_ref[...]).astype(BF16)
    for c in range(3 * D_MODEL // D_HALF):
        y = _mm(h, w_ref[:, c * D_HALF:(c + 1) * D_HALF])
        if c == 0:
            y = y * (1.0 / math.sqrt(HEAD_DIM))
        if c == 3:
            y = y * (LOG2E / math.sqrt(HEAD_DIM))
        o_ref[:, c * D_HALF:(c + 1) * D_HALF] = y.astype(BF16)
        if c < 3:
            for p in range(N_PAIRS):
                stage_ref[c * N_PAIRS + p] = y[:, p * LANES:(p + 1) * LANES]
    for od_ref, d in zip(rest[:-1], STRIDED_DILATIONS):
        for r in range(d):
            for ch in range(A_COLS // LANES):
                col = r * A_COLS + ch * LANES
                od_ref[:, col:col + LANES] = stage_ref[ch, pl.ds(r, QKV_TM // d, stride=d), :].astype(BF16)


def _qkv(x2, g_attn, w_qkv):
    n = x2.shape[0]
    strided = [(n // d, d * A_COLS) for d in STRIDED_DILATIONS]
    return pl.pallas_call(
        _qkv_kernel,
        grid=(n // QKV_TM,),
        in_specs=[pl.BlockSpec((QKV_TM, D_MODEL), lambda i: (i, 0)),
                  pl.BlockSpec((1, D_MODEL), lambda i: (0, 0)),
                  pl.BlockSpec((D_MODEL, 3 * D_MODEL), lambda i: (0, 0))],
        out_specs=[pl.BlockSpec((QKV_TM, 3 * D_MODEL), lambda i: (i, 0))]
                  + [pl.BlockSpec((QKV_TM // d, d * A_COLS), lambda i: (i, 0)) for d in STRIDED_DILATIONS],
        out_shape=[jax.ShapeDtypeStruct((n, 3 * D_MODEL), BF16)]
                  + [jax.ShapeDtypeStruct(shape, BF16) for shape in strided],
        scratch_shapes=[pltpu.VMEM((A_COLS // LANES, QKV_TM, LANES), F32)],
        compiler_params=_cparams(("parallel",)),
        name="qkv",
    )(x2, g_attn.reshape(1, D_MODEL), w_qkv.astype(BF16))


def _t5_bucket(dist):
    max_exact = REL_BUCKETS // 2
    d_f = jnp.maximum(dist, 1).astype(jnp.float32)
    large = max_exact + (jnp.log(d_f / max_exact)
                         / math.log(REL_MAX_DISTANCE / max_exact)
                         * (REL_BUCKETS - max_exact)).astype(jnp.int32)
    large = jnp.minimum(large, REL_BUCKETS - 1)
    return jnp.where(dist < max_exact, dist, large)


def _band_bias(rel_bias, dilation):
    qi = jnp.arange(BAND, dtype=jnp.int32)[:, None]
    kj = jnp.arange(2 * BAND, dtype=jnp.int32)[None, :]
    rel = qi + BAND - kj
    ok = (rel >= 0) & (rel <= BAND)
    bias = rel_bias.astype(F32)[_t5_bucket(jnp.maximum(rel, 0) * dilation)]
    return jnp.where(ok[None], bias.transpose(2, 0, 1), NEG_INF)


def _dilated_kernel(q_ref, kp_ref, kc_ref, vp_ref, vc_ref, bias_ref, o_ref, lse_ref):
    n = pl.program_id(2)
    kj = lax.broadcasted_iota(jnp.int32, (BAND, 2 * BAND), 1)
    has_prev = jnp.logical_or(n > 0, kj >= BAND)
    lane = lax.broadcasted_iota(jnp.int32, (BAND, LANES), 1)
    m_lo, m_hi = _half_masks(BF16)
    lse_tile = jnp.zeros((BAND, LANES), F32)
    for p in range(N_PAIRS):
        cols = slice(p * LANES, (p + 1) * LANES)
        q2 = q_ref[0, :, cols]
        kcat = jnp.concatenate([kp_ref[0, :, cols], kc_ref[0, :, cols]], axis=0)
        vcat = jnp.concatenate([vp_ref[0, :, cols], vc_ref[0, :, cols]], axis=0)
        outs = []
        for s, msk in enumerate((m_lo, m_hi)):
            h = 2 * p + s
            logits = _mm_nt(q2 * msk, kcat) + bias_ref[h]
            logits = jnp.where(has_prev, logits, NEG_INF)
            m = jnp.max(logits, axis=-1, keepdims=True)
            pr = jnp.exp(logits - m)
            den = jnp.sum(pr, axis=-1, keepdims=True)
            outs.append(_mm(pr.astype(BF16), vcat) / den)
            lse_tile = jnp.where(lane == h, m + jnp.log(den), lse_tile)
        o_ref[0, :, cols] = jnp.where(lane < HEAD_DIM, outs[0], outs[1])
    lse_ref[0] = lse_tile


def _dilated(src, b, s, rel_bias, dilation):
    d = dilation
    l = s // d
    nblk = l // BAND
    view = src.reshape(b, l, src.shape[1])
    n_sec = src.shape[1] // d // D_HALF
    blk = (1, BAND, D_HALF)

    def sec(k, prev):
        if prev:
            return pl.BlockSpec(blk, lambda bi, r, n: (bi, jnp.maximum(n - 1, 0), r * n_sec + k))
        return pl.BlockSpec(blk, lambda bi, r, n: (bi, n, r * n_sec + k))

    o, lse = pl.pallas_call(
        _dilated_kernel,
        grid=(b, d, nblk),
        in_specs=[sec(0, False), sec(1, True), sec(1, False), sec(2, True), sec(2, False),
                  pl.BlockSpec((N_HEADS, BAND, 2 * BAND), lambda bi, r, n: (0, 0, 0))],
        out_specs=[pl.BlockSpec((1, BAND, D_HALF), lambda bi, r, n: (bi, n, r)),
                   pl.BlockSpec((1, BAND, LANES), lambda bi, r, n: (bi, n, r))],
        out_shape=[jax.ShapeDtypeStruct((b, l, d * D_HALF), F32),
                   jax.ShapeDtypeStruct((b, l, d * LANES), F32)],
        compiler_params=_cparams(("parallel", "parallel", "arbitrary")),
        name=f"dilated_d{d}",
    )(view, view, view, view, view, _band_bias(rel_bias, d))
    return o.reshape(b * s, D_HALF), lse.reshape(b * s, LANES)


def _sb_kernel(q_ref, k_ref, v_ref, u_ref, o_ref, acc_ref, carry_ref):
    i = pl.program_id(2)
    m_lo, m_hi = _half_masks(BF16)
    row = lax.broadcasted_iota(jnp.int32, (2 * SB_T, SB_T), 0) & (SB_T - 1)
    col = lax.broadcasted_iota(jnp.int32, (2 * SB_T, SB_T), 1)
    causal = col < row
    lane = lax.broadcasted_iota(jnp.int32, (SB_T, LANES), 1)
    u = u_ref[...]

    def tile(qs, j, mask, carry):
        start = pl.multiple_of(j * SB_T, SB_T)
        kj = k_ref[0, pl.ds(start, SB_T), :]
        vj = v_ref[0, pl.ds(start, SB_T), :]
        z = _mm_nt(qs, kj)
        neg_abs = lax.bitcast_convert_type(
            lax.bitcast_convert_type(z, jnp.uint32) | jnp.uint32(0x80000000), F32)
        sp = jnp.log2(1.0 + jnp.exp2(neg_abs))
        log_b = jnp.minimum(z, 0.0) - sp
        l1m = log_b - z
        if mask is not None:
            l1m = jnp.where(mask, l1m, 0.0)
        cum = _mm_split(l1m, u)
        total = cum[:, 0:1] + l1m[:, 0:1]
        a = jnp.exp2(log_b + cum + carry)
        if mask is not None:
            a = jnp.where(mask, a, 0.0)
        return _mm(a.astype(BF16), vj), total

    def more(state):
        j, top = state
        return jnp.logical_and(j >= 0, top > -SB_EXP_ZERO * LOG2E)

    tops = []
    for g in range(SB_G):
        blk = i * SB_G + g
        q2 = q_ref[0, g * SB_T:(g + 1) * SB_T, :]
        qs = jnp.concatenate([q2 * m_lo, q2 * m_hi], axis=0)
        pv_d, tot_d = tile(qs, blk, causal, 0.0)
        if g == 0:
            pv_p, tot_p = tile(qs, jnp.maximum(blk - 1, 0), blk > 0, tot_d)
        else:
            pv_p, tot_p = tile(qs, blk - 1, None, tot_d)
        acc_ref[g] = pv_d + pv_p
        carry0 = tot_d + tot_p
        carry_ref[g] = carry0
        tops.append(jnp.max(carry0))

    for g in range(SB_G):
        def body(state, g=g):
            j, _ = state
            q2 = q_ref[0, g * SB_T:(g + 1) * SB_T, :]
            qs = jnp.concatenate([q2 * m_lo, q2 * m_hi], axis=0)
            pv, tot = tile(qs, j, None, carry_ref[g])
            acc_ref[g] += pv
            carry = carry_ref[g] + tot
            carry_ref[g] = carry
            return j - 1, jnp.max(carry)

        lax.while_loop(more, body, (i * SB_G + g - 2, tops[g]))
        o_ref[0, g * SB_T:(g + 1) * SB_T, :] = jnp.where(lane < HEAD_DIM, acc_ref[g, :SB_T], acc_ref[g, SB_T:])


def _suffix_matrix():
    sp = jnp.arange(SB_T, dtype=jnp.int32)[:, None]
    sc = jnp.arange(SB_T, dtype=jnp.int32)[None, :]
    return jnp.where(sp > sc, 1.0, 0.0).astype(BF16)


def _stickbreaking(qkv3):
    b, s, _ = qkv3.shape
    sec_q, sec_k, sec_v = 3 * N_PAIRS, 4 * N_PAIRS, 5 * N_PAIRS
    o = pl.pallas_call(
        _sb_kernel,
        grid=(b, N_PAIRS, s // (SB_G * SB_T)),
        in_specs=[pl.BlockSpec((1, SB_G * SB_T, LANES), lambda bi, p, i: (bi, i, sec_q + p)),
                  pl.BlockSpec((1, s, LANES), lambda bi, p, i: (bi, 0, sec_k + p)),
                  pl.BlockSpec((1, s, LANES), lambda bi, p, i: (bi, 0, sec_v + p)),
                  pl.BlockSpec((SB_T, SB_T), lambda bi, p, i: (0, 0))],
        out_specs=pl.BlockSpec((1, SB_G * SB_T, LANES), lambda bi, p, i: (bi, i, p)),
        out_shape=jax.ShapeDtypeStruct((b, s, D_HALF), F32),
        scratch_shapes=[pltpu.VMEM((SB_G, 2 * SB_T, LANES), F32), pltpu.VMEM((SB_G, 2 * SB_T, 1), F32)],
        compiler_params=_cparams(("parallel", "parallel", "arbitrary")),
        name="stickbreaking",
    )(qkv3, qkv3, qkv3, _suffix_matrix())
    return o.reshape(b * s, D_HALF)


def _mix_kernel(o1_ref, o2_ref, o3_ref, l1_ref, l2_ref, l3_ref, ob_ref, x_ref,
                ga_ref, gb_ref, e_ref, wo_ref, x1_ref):
    l1, l2, l3 = l1_ref[...], l2_ref[...], l3_ref[...]
    m = jnp.maximum(jnp.maximum(l1, l2), l3)
    e1, e2, e3 = jnp.exp(l1 - m), jnp.exp(l2 - m), jnp.exp(l3 - m)
    inv = 1.0 / (e1 + e2 + e3)
    expand = e_ref[...]
    oa = (_mm_split(e1 * inv, expand) * o1_ref[...]
          + _mm_split(e2 * inv, expand) * o2_ref[...]
          + _mm_split(e3 * inv, expand) * o3_ref[...])
    oa = (_rms(oa) * ga_ref[...]).astype(BF16)
    ob = (_rms(ob_ref[...]) * gb_ref[...]).astype(BF16)
    mix = _mm(oa, wo_ref[:D_HALF, :]) + _mm(ob, wo_ref[D_HALF:, :])
    x1_ref[...] = x_ref[...] + mix


def _head_expand():
    lane = jnp.arange(LANES, dtype=jnp.int32)[:, None]
    col = jnp.arange(D_HALF, dtype=jnp.int32)[None, :]
    return jnp.where(col // HEAD_DIM == lane, 1.0, 0.0).astype(BF16)


def _mix(os_, lses, ob, x2, g_out_dil, g_out_sb, w_o):
    n = x2.shape[0]
    row = lambda w: pl.BlockSpec((MIX_TM, w), lambda i: (i, 0))
    const = lambda shape: pl.BlockSpec(shape, lambda i: (0, 0))
    return pl.pallas_call(
        _mix_kernel,
        grid=(n // MIX_TM,),
        in_specs=[row(D_HALF)] * 3 + [row(LANES)] * 3 + [row(D_HALF), row(D_MODEL),
                  const((1, D_HALF)), const((1, D_HALF)), const((LANES, D_HALF)), const((D_MODEL, D_MODEL))],
        out_specs=row(D_MODEL),
        out_shape=jax.ShapeDtypeStruct((n, D_MODEL), F32),
        compiler_params=_cparams(("parallel",)),
        name="mix_wo",
    )(*os_, *lses, ob, x2, g_out_dil.reshape(1, D_HALF), g_out_sb.reshape(1, D_HALF),
      _head_expand(), w_o.astype(BF16))


def _router_kernel(x1_ref, g_ref, wr_ref, br_ref, tri_ref, h2_ref, meta_ref, gate_ref, cnt_ref, carry_ref):
    i = pl.program_id(0)

    @pl.when(i == 0)
    def _():
        carry_ref[...] = jnp.zeros_like(carry_ref)

    h2 = _rms(x1_ref[...]) * g_ref[...]
    h2_ref[...] = h2
    h_hi, h_lo = _split_bf16(h2)
    w_hi, w_lo = wr_ref[0], wr_ref[1]
    logits = _mm(h_hi, w_hi) + (_mm(h_hi, w_lo) + _mm(h_lo, w_hi)) + br_ref[...]
    lane = lax.broadcasted_iota(jnp.int32, (ROUTER_TM, LANES), 1)
    lane_f = lane.astype(F32)
    logits = jnp.where(lane < N_EXPERTS, logits, -jnp.inf)

    onehots, vals, ids = [], [], []
    for _k in range(TOP_K):
        m = jnp.max(logits, axis=-1, keepdims=True)
        idx = jnp.min(jnp.where(logits == m, lane_f, float(LANES)), axis=-1, keepdims=True)
        oh = lane_f == idx
        logits = jnp.where(oh, -jnp.inf, logits)
        onehots.append(oh)
        vals.append(m)
        ids.append(idx)

    es = [jnp.exp(v - vals[0]) for v in vals]
    inv = 1.0 / (es[0] + es[1] + es[2] + es[3])

    sel = jnp.zeros((ROUTER_TM, LANES), F32)
    for oh in onehots:
        sel = jnp.where(oh, 1.0, sel)
    before = _mm(tri_ref[...], sel.astype(BF16)) + carry_ref[0:1, :]
    meta = jnp.zeros((ROUTER_TM, LANES), F32)
    gate = jnp.zeros((ROUTER_TM, LANES), F32)
    for k in range(TOP_K):
        rank = jnp.sum(jnp.where(onehots[k], before, 0.0), axis=-1, keepdims=True)
        meta = jnp.where(lane == k, ids[k], meta)
        meta = jnp.where(lane == TOP_K + k, rank, meta)
        gate = jnp.where(lane == k, es[k] * inv, gate)
    meta_ref[...] = meta.astype(jnp.int32)
    gate_ref[...] = gate
    total = carry_ref[...] + jnp.sum(sel, axis=0, keepdims=True)
    carry_ref[...] = total
    cnt_ref[...] = total


def _router(x1, g_moe, w_router, b_router):
    n = x1.shape[0]
    wr = jnp.zeros((D_MODEL, LANES), F32).at[:, :N_EXPERTS].set(w_router.astype(F32))
    wr_hi = wr.astype(BF16)
    wr_lo = (wr - wr_hi.astype(F32)).astype(BF16)
    br = jnp.zeros((1, LANES), F32).at[0, :N_EXPERTS].set(b_router.astype(F32))
    r = jnp.arange(ROUTER_TM, dtype=jnp.int32)
    tri = jnp.where(r[None, :] < r[:, None], 1.0, 0.0).astype(BF16)
    row = lambda w: pl.BlockSpec((ROUTER_TM, w), lambda i: (i, 0))
    return pl.pallas_call(
        _router_kernel,
        grid=(n // ROUTER_TM,),
        in_specs=[row(D_MODEL),
                  pl.BlockSpec((1, D_MODEL), lambda i: (0, 0)),
                  pl.BlockSpec((2, D_MODEL, LANES), lambda i: (0, 0, 0)),
                  pl.BlockSpec((1, LANES), lambda i: (0, 0)),
                  pl.BlockSpec((ROUTER_TM, ROUTER_TM), lambda i: (0, 0))],
        out_specs=[row(D_MODEL), row(LANES), row(LANES), pl.BlockSpec((8, LANES), lambda i: (0, 0))],
        out_shape=[jax.ShapeDtypeStruct((n, D_MODEL), F32),
                   jax.ShapeDtypeStruct((n, LANES), jnp.int32),
                   jax.ShapeDtypeStruct((n, LANES), F32),
                   jax.ShapeDtypeStruct((8, LANES), F32)],
        scratch_shapes=[pltpu.VMEM((8, LANES), F32)],
        compiler_params=_cparams(("arbitrary",)),
        name="router",
    )(x1, g_moe.reshape(1, D_MODEL), jnp.stack([wr_hi, wr_lo]), br, tri)


def _dispatch_kernel(dest_ref, h2_ref, xs_in_ref, xs_ref, sem):
    del xs_in_ref

    def row_copy(r, k):
        d = dest_ref[0, 0, r * TOP_K + k]
        return pltpu.make_async_copy(h2_ref.at[pl.ds(r, 1)], xs_ref.at[pl.ds(d, 1)], sem)

    def start(r, c):
        for k in range(TOP_K):
            row_copy(r, k).start()
        return c

    def wait(r, c):
        for k in range(TOP_K):
            row_copy(r, k).wait()
        return c

    lax.fori_loop(0, DISPATCH_R, start, 0)
    lax.fori_loop(0, DISPATCH_R, wait, 0)


def _dispatch(h2, dest, n_slots):
    n = h2.shape[0]
    nb = n // DISPATCH_R
    return pl.pallas_call(
        _dispatch_kernel,
        grid=(nb,),
        in_specs=[pl.BlockSpec((1, 1, DISPATCH_R * TOP_K), lambda i: (i, 0, 0), memory_space=pltpu.SMEM),
                  pl.BlockSpec((DISPATCH_R, D_MODEL), lambda i: (i, 0)),
                  pl.BlockSpec(memory_space=pl.ANY)],
        out_specs=pl.BlockSpec(memory_space=pl.ANY),
        out_shape=jax.ShapeDtypeStruct((n_slots, D_MODEL), F32),
        scratch_shapes=[pltpu.SemaphoreType.DMA(())],
        input_output_aliases={2: 0},
        compiler_params=_cparams(("arbitrary",)),
        name="dispatch",
    )(dest.reshape(nb, 1, DISPATCH_R * TOP_K), h2, jnp.zeros((n_slots, D_MODEL), F32))


def _expert_kernel(be_ref, nu_ref, x_ref, wg_ref, bg_ref, wu_ref, bu_ref, wd_ref, bd_ref, y_ref):
    del be_ref

    @pl.when(pl.program_id(0) < nu_ref[0])
    def _():
        xb = x_ref[...].astype(BF16)
        y = jnp.zeros((EXPERT_TM, D_MODEL), F32) + bd_ref[0]
        for c in range(D_MODEL // D_HALF):
            cols = slice(c * D_HALF, (c + 1) * D_HALF)
            glu = _mm(xb, wg_ref[0, :, cols]) + bg_ref[0, :, cols]
            lin = _mm(xb, wu_ref[0, :, cols]) + bu_ref[0, :, cols]
            glu = jnp.minimum(glu, SWIGLU_LIMIT)
            lin = jnp.clip(lin, -SWIGLU_LIMIT, SWIGLU_LIMIT)
            act = glu * (1.0 / (1.0 + jnp.exp(-SWIGLU_ALPHA * glu))) * (lin + 1.0)
            y = y + _mm(act.astype(BF16), wd_ref[0, cols, :])
        y_ref[...] = y


def _experts(xs, block_expert, n_used, w_gate, b_gate, w_up, b_up, w_down, b_down):
    n_slots = xs.shape[0]
    n_blocks = n_slots // EXPERT_TM
    rows = lambda i, be, nu: (jnp.minimum(i, nu[0] - 1), 0)
    wspec = pl.BlockSpec((1, D_MODEL, D_MODEL), lambda i, be, nu: (be[i], 0, 0))
    bspec = pl.BlockSpec((1, 1, D_MODEL), lambda i, be, nu: (be[i], 0, 0))
    grid_spec = pltpu.PrefetchScalarGridSpec(
        num_scalar_prefetch=2,
        grid=(n_blocks,),
        in_specs=[pl.BlockSpec((EXPERT_TM, D_MODEL), rows), wspec, bspec, wspec, bspec, wspec, bspec],
        out_specs=pl.BlockSpec((EXPERT_TM, D_MODEL), rows),
    )
    b3 = lambda a: a.astype(F32).reshape(N_EXPERTS, 1, D_MODEL)
    return pl.pallas_call(
        _expert_kernel,
        grid_spec=grid_spec,
        out_shape=jax.ShapeDtypeStruct((n_slots, D_MODEL), F32),
        compiler_params=_cparams(("arbitrary",)),
        name="experts",
    )(block_expert, n_used, xs, w_gate.astype(BF16), b3(b_gate), w_up.astype(BF16), b3(b_up),
      w_down.astype(BF16), b3(b_down))


def _combine_kernel(dest_ref, ys_ref, gate_ref, x1_ref, g_ref, o_ref, buf_ref, sem):
    def row_copy(r, k):
        d = dest_ref[0, 0, r * TOP_K + k]
        return pltpu.make_async_copy(ys_ref.at[pl.ds(d, 1)], buf_ref.at[k, pl.ds(r, 1)], sem)

    def start(r, c):
        for k in range(TOP_K):
            row_copy(r, k).start()
        return c

    def wait(r, c):
        for k in range(TOP_K):
            row_copy(r, k).wait()
        return c

    lax.fori_loop(0, COMBINE_R, start, 0)
    lax.fori_loop(0, COMBINE_R, wait, 0)
    gate = gate_ref[...]
    x = x1_ref[...]
    for k in range(TOP_K):
        x = x + gate[:, k:k + 1] * buf_ref[k]
    o_ref[...] = _rms(x) * g_ref[...]


def _combine(ys, dest, gates, x1, g_final):
    n = x1.shape[0]
    nb = n // COMBINE_R
    row = lambda w: pl.BlockSpec((COMBINE_R, w), lambda i: (i, 0))
    return pl.pallas_call(
        _combine_kernel,
        grid=(nb,),
        in_specs=[pl.BlockSpec((1, 1, COMBINE_R * TOP_K), lambda i: (i, 0, 0), memory_space=pltpu.SMEM),
                  pl.BlockSpec(memory_space=pl.ANY),
                  row(LANES), row(D_MODEL),
                  pl.BlockSpec((1, D_MODEL), lambda i: (0, 0))],
        out_specs=row(D_MODEL),
        out_shape=jax.ShapeDtypeStruct((n, D_MODEL), F32),
        scratch_shapes=[pltpu.VMEM((TOP_K, COMBINE_R, D_MODEL), F32), pltpu.SemaphoreType.DMA(())],
        compiler_params=_cparams(("arbitrary",)),
        name="combine",
    )(dest.reshape(nb, 1, COMBINE_R * TOP_K), ys, gates, x1, g_final.reshape(1, D_MODEL))


def _slot_layout(meta, counts_f, n):
    eid = meta[:, :TOP_K]
    rank = meta[:, TOP_K:2 * TOP_K]
    counts = counts_f[0, :N_EXPERTS].astype(jnp.int32)
    padded = (counts + EXPERT_TM - 1) // EXPERT_TM * EXPERT_TM
    pend = jnp.cumsum(padded)
    pstart = pend - padded
    dest = (pstart[eid] + rank).astype(jnp.int32)
    n_slots = -(-(n * TOP_K + N_EXPERTS * (EXPERT_TM - 1)) // EXPERT_TM) * EXPERT_TM
    block_start = jnp.arange(n_slots // EXPERT_TM, dtype=jnp.int32) * EXPERT_TM
    block_expert = jnp.minimum(jnp.searchsorted(pend, block_start, side='right'),
                               N_EXPERTS - 1).astype(jnp.int32)
    n_used = (pend[-1:] // EXPERT_TM).astype(jnp.int32)
    return dest, block_expert, n_used, n_slots


def kernel(x, g_attn, w_qkv, rel_bias, g_out_dil, g_out_sb, w_o, g_moe, w_router, b_router,
           w_gate, b_gate, w_up, b_up, w_down, b_down, g_final):
    b, s, d = x.shape
    n = b * s
    x2 = x.reshape(n, d)
    qkv, *strided = _qkv(x2, g_attn, w_qkv)
    sources = dict(zip(STRIDED_DILATIONS, strided))
    dil = [_dilated(sources.get(dilation, qkv), b, s, rel_bias, dilation) for _window, dilation in DIL_PATTERNS]
    ob = _stickbreaking(qkv.reshape(b, s, 3 * d))
    x1 = _mix([o for o, _ in dil], [l for _, l in dil], ob, x2, g_out_dil, g_out_sb, w_o)
    h2, meta, gates, counts = _router(x1, g_moe, w_router, b_router)
    dest, block_expert, n_used, n_slots = _slot_layout(meta, counts, n)
    xs = _dispatch(h2, dest, n_slots)
    ys = _experts(xs, block_expert, n_used, w_gate, b_gate, w_up, b_up, w_down, b_down)
    out = _combine(ys, dest, gates, x1, g_final)
    return out.reshape(b, s, d)
```

```python
import functools
import math

import jax
import jax.numpy as jnp
from jax import lax
from jax.experimental import pallas as pl
from jax.experimental.pallas import tpu as pltpu

F32 = jnp.float32
BF16 = jnp.bfloat16

D_MODEL = 1024
HEAD_DIM = 64
D_HALF = 512
N_HEADS = 8
LANES = 128
N_PAIRS = D_HALF // LANES
DIL_PATTERNS = ((128, 1), (512, 4), (2048, 16))
BAND = 128
REL_BUCKETS = 32
REL_MAX_DISTANCE = 2048
N_EXPERTS = 32
TOP_K = 4
SWIGLU_ALPHA = 1.702
SWIGLU_LIMIT = 7.0
RMS_EPS = 1e-6
NEG_INF = -1e30

QKV_TM = 512
A_COLS = 3 * D_HALF
STRIDED_DILATIONS = tuple(d for _w, d in DIL_PATTERNS if d > 1)
LOG2E = math.log2(math.e)
SB_T = 256
SB_G = 4
SB_EXP_ZERO = 104.0
MIX_TM = 256
ROUTER_TM = 512
DISPATCH_R = 256
EXPERT_TM = 512
COMBINE_R = 128
DMA_UNROLL = 8
VMEM_LIMIT = 48 * 1024 * 1024


def _cparams(sem):
    return pltpu.CompilerParams(dimension_semantics=sem, vmem_limit_bytes=VMEM_LIMIT)


def _mm(a, b):
    return jnp.dot(a, b, preferred_element_type=F32)


def _mm_nt(a, b):
    return lax.dot_general(a, b, (((1,), (1,)), ((), ())), preferred_element_type=F32)


def _split_bf16(a):
    hi = a.astype(BF16)
    lo = (a - hi.astype(F32)).astype(BF16)
    return hi, lo


def _mm_split(a, b_bf16):
    hi, lo = _split_bf16(a)
    return _mm(hi, b_bf16) + _mm(lo, b_bf16)


def _rms(x):
    return x * lax.rsqrt(jnp.mean(x * x, axis=-1, keepdims=True) + RMS_EPS)


def _half_masks(dtype):
    lane = lax.broadcasted_iota(jnp.int32, (1, LANES), 1)
    lo = jnp.where(lane < HEAD_DIM, 1.0, 0.0).astype(dtype)
    hi = jnp.where(lane >= HEAD_DIM, 1.0, 0.0).astype(dtype)
    return lo, hi


def _qkv_kernel(x_ref, g_ref, w_ref, o_ref, *rest):
    stage_ref = rest[-1]
    h = (_rms(x_ref[...]) * g_ref[...]).astype(BF16)
    for c in range(3 * D_MODEL // D_HALF):
        y = _mm(h, w_ref[:, c * D_HALF:(c + 1) * D_HALF])
        if c == 0:
            y = y * (1.0 / math.sqrt(HEAD_DIM))
        if c == 3:
            y = y * (LOG2E / math.sqrt(HEAD_DIM))
        o_ref[:, c * D_HALF:(c + 1) * D_HALF] = y.astype(BF16)
        if c < 3:
            for p in range(N_PAIRS):
                stage_ref[c * N_PAIRS + p] = y[:, p * LANES:(p + 1) * LANES]
    for od_ref, d in zip(rest[:-1], STRIDED_DILATIONS):
        for r in range(d):
            for ch in range(A_COLS // LANES):
                col = r * A_COLS + ch * LANES
                od_ref[:, col:col + LANES] = stage_ref[ch, pl.ds(r, QKV_TM // d, stride=d), :].astype(BF16)


def _qkv(x2, g_attn, w_qkv):
    n = x2.shape[0]
    strided = [(n // d, d * A_COLS) for d in STRIDED_DILATIONS]
    return pl.pallas_call(
        _qkv_kernel,
        grid=(n // QKV_TM,),
        in_specs=[pl.BlockSpec((QKV_TM, D_MODEL), lambda i: (i, 0)),
                  pl.BlockSpec((1, D_MODEL), lambda i: (0, 0)),
                  pl.BlockSpec((D_MODEL, 3 * D_MODEL), lambda i: (0, 0))],
        out_specs=[pl.BlockSpec((QKV_TM, 3 * D_MODEL), lambda i: (i, 0))]
                  + [pl.BlockSpec((QKV_TM // d, d * A_COLS), lambda i: (i, 0)) for d in STRIDED_DILATIONS],
        out_shape=[jax.ShapeDtypeStruct((n, 3 * D_MODEL), BF16)]
                  + [jax.ShapeDtypeStruct(shape, BF16) for shape in strided],
        scratch_shapes=[pltpu.VMEM((A_COLS // LANES, QKV_TM, LANES), F32)],
        compiler_params=_cparams(("parallel",)),
        name="qkv",
    )(x2, g_attn.reshape(1, D_MODEL), w_qkv.astype(BF16))


def _t5_bucket(dist):
    max_exact = REL_BUCKETS // 2
    d_f = jnp.maximum(dist, 1).astype(jnp.float32)
    large = max_exact + (jnp.log(d_f / max_exact)
                         / math.log(REL_MAX_DISTANCE / max_exact)
                         * (REL_BUCKETS - max_exact)).astype(jnp.int32)
    large = jnp.minimum(large, REL_BUCKETS - 1)
    return jnp.where(dist < max_exact, dist, large)


def _band_bias(rel_bias, dilation):
    qi = jnp.arange(BAND, dtype=jnp.int32)[:, None]
    kj = jnp.arange(2 * BAND, dtype=jnp.int32)[None, :]
    rel = qi + BAND - kj
    ok = (rel >= 0) & (rel <= BAND)
    bucket = _t5_bucket(jnp.maximum(rel, 0) * dilation)
    table = rel_bias.astype(F32)
    bias = jnp.zeros((N_HEADS, BAND, 2 * BAND), F32)
    for bk in range(REL_BUCKETS):
        bias = jnp.where(bucket[None] == bk, table[bk][:, None, None], bias)
    return jnp.where(ok[None], bias, NEG_INF)


def _dilated_kernel(q_ref, kp_ref, kc_ref, vp_ref, vc_ref, bias_ref, o_ref, lse_ref):
    n = pl.program_id(2)
    kj = lax.broadcasted_iota(jnp.int32, (BAND, 2 * BAND), 1)
    has_prev = jnp.logical_or(n > 0, kj >= BAND)
    lane = lax.broadcasted_iota(jnp.int32, (BAND, LANES), 1)
    m_lo, m_hi = _half_masks(BF16)
    lse_tile = jnp.zeros((BAND, LANES), F32)
    for p in range(N_PAIRS):
        cols = slice(p * LANES, (p + 1) * LANES)
        q2 = q_ref[0, :, cols]
        kcat = jnp.concatenate([kp_ref[0, :, cols], kc_ref[0, :, cols]], axis=0)
        vcat = jnp.concatenate([vp_ref[0, :, cols], vc_ref[0, :, cols]], axis=0)
        outs = []
        for s, msk in enumerate((m_lo, m_hi)):
            h = 2 * p + s
            logits = _mm_nt(q2 * msk, kcat) + bias_ref[h]
            logits = jnp.where(has_prev, logits, NEG_INF)
            m = jnp.max(logits, axis=-1, keepdims=True)
            pr = jnp.exp(logits - m)
            den = jnp.sum(pr, axis=-1, keepdims=True)
            outs.append(_mm(pr.astype(BF16), vcat) / den)
            lse_tile = jnp.where(lane == h, m + jnp.log(den), lse_tile)
        o_ref[0, :, cols] = jnp.where(lane < HEAD_DIM, outs[0], outs[1])
    lse_ref[0] = lse_tile


def _dilated(src, b, s, rel_bias, dilation):
    d = dilation
    l = s // d
    nblk = l // BAND
    view = src.reshape(b, l, src.shape[1])
    n_sec = src.shape[1] // d // D_HALF
    blk = (1, BAND, D_HALF)

    def sec(k, prev):
        if prev:
            return pl.BlockSpec(blk, lambda bi, r, n: (bi, jnp.maximum(n - 1, 0), r * n_sec + k))
        return pl.BlockSpec(blk, lambda bi, r, n: (bi, n, r * n_sec + k))

    o, lse = pl.pallas_call(
        _dilated_kernel,
        grid=(b, d, nblk),
        in_specs=[sec(0, False), sec(1, True), sec(1, False), sec(2, True), sec(2, False),
                  pl.BlockSpec((N_HEADS, BAND, 2 * BAND), lambda bi, r, n: (0, 0, 0))],
        out_specs=[pl.BlockSpec((1, BAND, D_HALF), lambda bi, r, n: (bi, n, r)),
                   pl.BlockSpec((1, BAND, LANES), lambda bi, r, n: (bi, n, r))],
        out_shape=[jax.ShapeDtypeStruct((b, l, d * D_HALF), F32),
                   jax.ShapeDtypeStruct((b, l, d * LANES), F32)],
        compiler_params=_cparams(("parallel", "parallel", "arbitrary")),
        name=f"dilated_d{d}",
    )(view, view, view, view, view, _band_bias(rel_bias, d))
    return o.reshape(b * s, D_HALF), lse.reshape(b * s, LANES)


def _sb_kernel(q_ref, k_ref, v_ref, u_ref, o_ref, acc_ref, carry_ref):
    i = pl.program_id(2)
    m_lo, m_hi = _half_masks(BF16)
    row = lax.broadcasted_iota(jnp.int32, (2 * SB_T, SB_T), 0) & (SB_T - 1)
    col = lax.broadcasted_iota(jnp.int32, (2 * SB_T, SB_T), 1)
    causal = col < row
    lane = lax.broadcasted_iota(jnp.int32, (SB_T, LANES), 1)
    u = u_ref[...]

    def tile(qs, j, mask, carry):
        start = pl.multiple_of(j * SB_T, SB_T)
        kj = k_ref[0, pl.ds(start, SB_T), :]
        vj = v_ref[0, pl.ds(start, SB_T), :]
        z = _mm_nt(qs, kj)
        neg_abs = lax.bitcast_convert_type(
            lax.bitcast_convert_type(z, jnp.uint32) | jnp.uint32(0x80000000), F32)
        sp = jnp.log2(1.0 + jnp.exp2(neg_abs))
        log_b = jnp.minimum(z, 0.0) - sp
        l1m = log_b - z
        if mask is not None:
            l1m = jnp.where(mask, l1m, 0.0)
        cum = _mm_split(l1m, u)
        total = cum[:, 0:1] + l1m[:, 0:1]
        a = jnp.exp2(log_b + cum + carry)
        if mask is not None:
            a = jnp.where(mask, a, 0.0)
        return _mm(a.astype(BF16), vj), total

    def more(state):
        j, top = state
        return jnp.logical_and(j >= 0, top > -SB_EXP_ZERO * LOG2E)

    tops = []
    for g in range(SB_G):
        blk = i * SB_G + g
        q2 = q_ref[0, g * SB_T:(g + 1) * SB_T, :]
        qs = jnp.concatenate([q2 * m_lo, q2 * m_hi], axis=0)
        pv_d, tot_d = tile(qs, blk, causal, 0.0)
        if g == 0:
            pv_p, tot_p = tile(qs, jnp.maximum(blk - 1, 0), blk > 0, tot_d)
        else:
            pv_p, tot_p = tile(qs, blk - 1, None, tot_d)
        acc_ref[g] = pv_d + pv_p
        carry0 = tot_d + tot_p
        carry_ref[g] = carry0
        tops.append(jnp.max(carry0))

    for g in range(SB_G):
        def body(state, g=g):
            j, _ = state
            q2 = q_ref[0, g * SB_T:(g + 1) * SB_T, :]
            qs = jnp.concatenate([q2 * m_lo, q2 * m_hi], axis=0)
            pv, tot = tile(qs, j, None, carry_ref[g])
            acc_ref[g] += pv
            carry = carry_ref[g] + tot
            carry_ref[g] = carry
            return j - 1, jnp.max(carry)

        lax.while_loop(more, body, (i * SB_G + g - 2, tops[g]))
        o_ref[0, g * SB_T:(g + 1) * SB_T, :] = jnp.where(lane < HEAD_DIM, acc_ref[g, :SB_T], acc_ref[g, SB_T:])


def _suffix_matrix():
    sp = jnp.arange(SB_T, dtype=jnp.int32)[:, None]
    sc = jnp.arange(SB_T, dtype=jnp.int32)[None, :]
    return jnp.where(sp > sc, 1.0, 0.0).astype(BF16)


def _stickbreaking(qkv3):
    b, s, _ = qkv3.shape
    sec_q, sec_k, sec_v = 3 * N_PAIRS, 4 * N_PAIRS, 5 * N_PAIRS
    o = pl.pallas_call(
        _sb_kernel,
        grid=(b, N_PAIRS, s // (SB_G * SB_T)),
        in_specs=[pl.BlockSpec((1, SB_G * SB_T, LANES), lambda bi, p, i: (bi, i, sec_q + p)),
                  pl.BlockSpec((1, s, LANES), lambda bi, p, i: (bi, 0, sec_k + p)),
                  pl.BlockSpec((1, s, LANES), lambda bi, p, i: (bi, 0, sec_v + p)),
                  pl.BlockSpec((SB_T, SB_T), lambda bi, p, i: (0, 0))],
        out_specs=pl.BlockSpec((1, SB_G * SB_T, LANES), lambda bi, p, i: (bi, i, p)),
        out_shape=jax.ShapeDtypeStruct((b, s, D_HALF), F32),
        scratch_shapes=[pltpu.VMEM((SB_G, 2 * SB_T, LANES), F32), pltpu.VMEM((SB_G, 2 * SB_T, 1), F32)],
        compiler_params=_cparams(("parallel", "parallel", "arbitrary")),
        name="stickbreaking",
    )(qkv3, qkv3, qkv3, _suffix_matrix())
    return o.reshape(b * s, D_HALF)


def _mix_kernel(o1_ref, o2_ref, o3_ref, l1_ref, l2_ref, l3_ref, ob_ref, x_ref,
                ga_ref, gb_ref, e_ref, wo_ref, x1_ref):
    l1, l2, l3 = l1_ref[...], l2_ref[...], l3_ref[...]
    m = jnp.maximum(jnp.maximum(l1, l2), l3)
    e1, e2, e3 = jnp.exp(l1 - m), jnp.exp(l2 - m), jnp.exp(l3 - m)
    inv = 1.0 / (e1 + e2 + e3)
    expand = e_ref[...]
    oa = (_mm_split(e1 * inv, expand) * o1_ref[...]
          + _mm_split(e2 * inv, expand) * o2_ref[...]
          + _mm_split(e3 * inv, expand) * o3_ref[...])
    oa = (_rms(oa) * ga_ref[...]).astype(BF16)
    ob = (_rms(ob_ref[...]) * gb_ref[...]).astype(BF16)
    mix = _mm(oa, wo_ref[:D_HALF, :]) + _mm(ob, wo_ref[D_HALF:, :])
    x1_ref[...] = x_ref[...] + mix


def _head_expand():
    lane = jnp.arange(LANES, dtype=jnp.int32)[:, None]
    col = jnp.arange(D_HALF, dtype=jnp.int32)[None, :]
    return jnp.where(col // HEAD_DIM == lane, 1.0, 0.0).astype(BF16)


def _mix(os_, lses, ob, x2, g_out_dil, g_out_sb, w_o):
    n = x2.shape[0]
    row = lambda w: pl.BlockSpec((MIX_TM, w), lambda i: (i, 0))
    const = lambda shape: pl.BlockSpec(shape, lambda i: (0, 0))
    return pl.pallas_call(
        _mix_kernel,
        grid=(n // MIX_TM,),
        in_specs=[row(D_HALF)] * 3 + [row(LANES)] * 3 + [row(D_HALF), row(D_MODEL),
                  const((1, D_HALF)), const((1, D_HALF)), const((LANES, D_HALF)), const((D_MODEL, D_MODEL))],
        out_specs=row(D_MODEL),
        out_shape=jax.ShapeDtypeStruct((n, D_MODEL), F32),
        compiler_params=_cparams(("parallel",)),
        name="mix_wo",
    )(*os_, *lses, ob, x2, g_out_dil.reshape(1, D_HALF), g_out_sb.reshape(1, D_HALF),
      _head_expand(), w_o.astype(BF16))


def _router_kernel(x1_ref, g_ref, wr_ref, br_ref, tri_ref, h2_ref, meta_ref, gate_ref, cnt_ref, carry_ref):
    i = pl.program_id(0)

    @pl.when(i == 0)
    def _():
        carry_ref[...] = jnp.zeros_like(carry_ref)

    h2 = _rms(x1_ref[...]) * g_ref[...]
    h2_ref[...] = h2
    h_hi, h_lo = _split_bf16(h2)
    w_hi, w_lo = wr_ref[0], wr_ref[1]
    logits = _mm(h_hi, w_hi) + (_mm(h_hi, w_lo) + _mm(h_lo, w_hi)) + br_ref[...]
    lane = lax.broadcasted_iota(jnp.int32, (ROUTER_TM, LANES), 1)
    lane_f = lane.astype(F32)
    logits = jnp.where(lane < N_EXPERTS, logits, -jnp.inf)

    onehots, vals, ids = [], [], []
    for _k in range(TOP_K):
        m = jnp.max(logits, axis=-1, keepdims=True)
        idx = jnp.min(jnp.where(logits == m, lane_f, float(LANES)), axis=-1, keepdims=True)
        oh = lane_f == idx
        logits = jnp.where(oh, -jnp.inf, logits)
        onehots.append(oh)
        vals.append(m)
        ids.append(idx)

    es = [jnp.exp(v - vals[0]) for v in vals]
    inv = 1.0 / (es[0] + es[1] + es[2] + es[3])

    sel = jnp.zeros((ROUTER_TM, LANES), F32)
    for oh in onehots:
        sel = jnp.where(oh, 1.0, sel)
    before = _mm(tri_ref[...], sel.astype(BF16)) + carry_ref[0:1, :]
    meta = jnp.zeros((ROUTER_TM, LANES), F32)
    gate = jnp.zeros((ROUTER_TM, LANES), F32)
    for k in range(TOP_K):
        rank = jnp.sum(jnp.where(onehots[k], before, 0.0), axis=-1, keepdims=True)
        meta = jnp.where(lane == k, ids[k], meta)
        meta = jnp.where(lane == TOP_K + k, rank, meta)
        gate = jnp.where(lane == k, es[k] * inv, gate)
    meta_ref[...] = meta.astype(jnp.int32)
    gate_ref[...] = gate
    total = carry_ref[...] + jnp.sum(sel, axis=0, keepdims=True)
    carry_ref[...] = total
    cnt_ref[...] = total


def _router(x1, g_moe, w_router, b_router):
    n = x1.shape[0]
    wr = jnp.zeros((D_MODEL, LANES), F32).at[:, :N_EXPERTS].set(w_router.astype(F32))
    wr_hi = wr.astype(BF16)
    wr_lo = (wr - wr_hi.astype(F32)).astype(BF16)
    br = jnp.zeros((1, LANES), F32).at[0, :N_EXPERTS].set(b_router.astype(F32))
    r = jnp.arange(ROUTER_TM, dtype=jnp.int32)
    tri = jnp.where(r[None, :] < r[:, None], 1.0, 0.0).astype(BF16)
    row = lambda w: pl.BlockSpec((ROUTER_TM, w), lambda i: (i, 0))
    return pl.pallas_call(
        _router_kernel,
        grid=(n // ROUTER_TM,),
        in_specs=[row(D_MODEL),
                  pl.BlockSpec((1, D_MODEL), lambda i: (0, 0)),
                  pl.BlockSpec((2, D_MODEL, LANES), lambda i: (0, 0, 0)),
                  pl.BlockSpec((1, LANES), lambda i: (0, 0)),
                  pl.BlockSpec((ROUTER_TM, ROUTER_TM), lambda i: (0, 0))],
        out_specs=[row(D_MODEL), row(LANES), row(LANES), pl.BlockSpec((8, LANES), lambda i: (0, 0))],
        out_shape=[jax.ShapeDtypeStruct((n, D_MODEL), F32),
                   jax.ShapeDtypeStruct((n, LANES), jnp.int32),
                   jax.ShapeDtypeStruct((n, LANES), F32),
                   jax.ShapeDtypeStruct((8, LANES), F32)],
        scratch_shapes=[pltpu.VMEM((8, LANES), F32)],
        compiler_params=_cparams(("arbitrary",)),
        name="router",
    )(x1, g_moe.reshape(1, D_MODEL), jnp.stack([wr_hi, wr_lo]), br, tri)


def _dispatch_kernel(dest_ref, h2_ref, xs_ref, sem):
    def row_copy(r, k):
        d = dest_ref[0, 0, r * TOP_K + k]
        return pltpu.make_async_copy(h2_ref.at[pl.ds(r, 1)], xs_ref.at[pl.ds(d, 1)], sem)

    def start(r, c):
        for k in range(TOP_K):
            row_copy(r, k).start()
        return c

    def wait(r, c):
        for k in range(TOP_K):
            row_copy(r, k).wait()
        return c

    lax.fori_loop(0, DISPATCH_R, start, 0, unroll=DMA_UNROLL)
    lax.fori_loop(0, DISPATCH_R, wait, 0, unroll=DMA_UNROLL)


def _dispatch(h2, dest, n_slots):
    n = h2.shape[0]
    nb = n // DISPATCH_R
    return pl.pallas_call(
        _dispatch_kernel,
        grid=(nb,),
        in_specs=[pl.BlockSpec((1, 1, DISPATCH_R * TOP_K), lambda i: (i, 0, 0), memory_space=pltpu.SMEM),
                  pl.BlockSpec((DISPATCH_R, D_MODEL), lambda i: (i, 0))],
        out_specs=pl.BlockSpec(memory_space=pl.ANY),
        out_shape=jax.ShapeDtypeStruct((n_slots, D_MODEL), F32),
        scratch_shapes=[pltpu.SemaphoreType.DMA(())],
        compiler_params=_cparams(("arbitrary",)),
        name="dispatch",
    )(dest.reshape(nb, 1, DISPATCH_R * TOP_K), h2)


def _expert_kernel(be_ref, nu_ref, nv_ref, x_ref, wg_ref, bg_ref, wu_ref, bu_ref, wd_ref, bd_ref, y_ref):
    del be_ref
    i = pl.program_id(0)

    @pl.when(i < nu_ref[0])
    def _():
        row = lax.broadcasted_iota(jnp.int32, (EXPERT_TM, 1), 0)
        xb = jnp.where(row < nv_ref[i], x_ref[...], 0.0).astype(BF16)
        y = jnp.zeros((EXPERT_TM, D_MODEL), F32) + bd_ref[0]
        for c in range(D_MODEL // D_HALF):
            cols = slice(c * D_HALF, (c + 1) * D_HALF)
            glu = _mm(xb, wg_ref[0, :, cols]) + bg_ref[0, :, cols]
            lin = _mm(xb, wu_ref[0, :, cols]) + bu_ref[0, :, cols]
            glu = jnp.minimum(glu, SWIGLU_LIMIT)
            lin = jnp.clip(lin, -SWIGLU_LIMIT, SWIGLU_LIMIT)
            act = glu * (1.0 / (1.0 + jnp.exp(-SWIGLU_ALPHA * glu))) * (lin + 1.0)
            y = y + _mm(act.astype(BF16), wd_ref[0, cols, :])
        y_ref[...] = y


def _experts(xs, block_expert, n_used, n_valid, w_gate, b_gate, w_up, b_up, w_down, b_down):
    n_slots = xs.shape[0]
    n_blocks = n_slots // EXPERT_TM
    rows = lambda i, be, nu, nv: (jnp.minimum(i, nu[0] - 1), 0)
    wspec = pl.BlockSpec((1, D_MODEL, D_MODEL), lambda i, be, nu, nv: (be[i], 0, 0))
    bspec = pl.BlockSpec((1, 1, D_MODEL), lambda i, be, nu, nv: (be[i], 0, 0))
    grid_spec = pltpu.PrefetchScalarGridSpec(
        num_scalar_prefetch=3,
        grid=(n_blocks,),
        in_specs=[pl.BlockSpec((EXPERT_TM, D_MODEL), rows), wspec, bspec, wspec, bspec, wspec, bspec],
        out_specs=pl.BlockSpec((EXPERT_TM, D_MODEL), rows),
    )
    b3 = lambda a: a.astype(F32).reshape(N_EXPERTS, 1, D_MODEL)
    return pl.pallas_call(
        _expert_kernel,
        grid_spec=grid_spec,
        out_shape=jax.ShapeDtypeStruct((n_slots, D_MODEL), F32),
        compiler_params=_cparams(("arbitrary",)),
        name="experts",
    )(block_expert, n_used, n_valid, xs, w_gate.astype(BF16), b3(b_gate), w_up.astype(BF16), b3(b_up),
      w_down.astype(BF16), b3(b_down))


def _combine_kernel(dest_ref, ys_ref, gate_ref, x1_ref, g_ref, o_ref, buf_ref, sem):
    def row_copy(r, k):
        d = dest_ref[0, 0, r * TOP_K + k]
        return pltpu.make_async_copy(ys_ref.at[pl.ds(d, 1)], buf_ref.at[k, pl.ds(r, 1)], sem)

    def start(r, c):
        for k in range(TOP_K):
            row_copy(r, k).start()
        return c

    def wait(r, c):
        for k in range(TOP_K):
            row_copy(r, k).wait()
        return c

    lax.fori_loop(0, COMBINE_R, start, 0, unroll=DMA_UNROLL)
    lax.fori_loop(0, COMBINE_R, wait, 0, unroll=DMA_UNROLL)
    gate = gate_ref[...]
    x = x1_ref[...]
    for k in range(TOP_K):
        x = x + gate[:, k:k + 1] * buf_ref[k]
    o_ref[...] = _rms(x) * g_ref[...]


def _combine(ys, dest, gates, x1, g_final):
    n = x1.shape[0]
    nb = n // COMBINE_R
    row = lambda w: pl.BlockSpec((COMBINE_R, w), lambda i: (i, 0))
    return pl.pallas_call(
        _combine_kernel,
        grid=(nb,),
        in_specs=[pl.BlockSpec((1, 1, COMBINE_R * TOP_K), lambda i: (i, 0, 0), memory_space=pltpu.SMEM),
                  pl.BlockSpec(memory_space=pl.ANY),
                  row(LANES), row(D_MODEL),
                  pl.BlockSpec((1, D_MODEL), lambda i: (0, 0))],
        out_specs=row(D_MODEL),
        out_shape=jax.ShapeDtypeStruct((n, D_MODEL), F32),
        scratch_shapes=[pltpu.VMEM((TOP_K, COMBINE_R, D_MODEL), F32), pltpu.SemaphoreType.DMA(())],
        compiler_params=_cparams(("arbitrary",)),
        name="combine",
    )(dest.reshape(nb, 1, COMBINE_R * TOP_K), ys, gates, x1, g_final.reshape(1, D_MODEL))


def _slot_layout(meta, counts_f, n):
    eid = meta[:, :TOP_K]
    rank = meta[:, TOP_K:2 * TOP_K]
    counts = counts_f[0, :N_EXPERTS].astype(jnp.int32)
    padded = (counts + EXPERT_TM - 1) // EXPERT_TM * EXPERT_TM
    pend = jnp.cumsum(padded)
    pstart = pend - padded
    experts = jnp.arange(N_EXPERTS, dtype=jnp.int32)
    dest = rank + jnp.sum(jnp.where(eid[..., None] == experts, pstart, 0), axis=-1)
    n_slots = -(-(n * TOP_K + N_EXPERTS * (EXPERT_TM - 1)) // EXPERT_TM) * EXPERT_TM
    block_start = jnp.arange(n_slots // EXPERT_TM, dtype=jnp.int32) * EXPERT_TM
    block_expert = jnp.minimum(jnp.sum(pend[None, :] <= block_start[:, None], axis=-1), N_EXPERTS - 1)
    last_valid = jnp.sum(jnp.where(block_expert[:, None] == experts, pstart + counts, 0), axis=-1)
    n_valid = jnp.clip(last_valid - block_start, 0, EXPERT_TM).astype(jnp.int32)
    n_used = (pend[-1:] // EXPERT_TM).astype(jnp.int32)
    return dest.astype(jnp.int32), block_expert.astype(jnp.int32), n_used, n_valid, n_slots


def kernel(x, g_attn, w_qkv, rel_bias, g_out_dil, g_out_sb, w_o, g_moe, w_router, b_router,
           w_gate, b_gate, w_up, b_up, w_down, b_down, g_final):
    b, s, d = x.shape
    n = b * s
    x2 = x.reshape(n, d)
    qkv, *strided = _qkv(x2, g_attn, w_qkv)
    sources = dict(zip(STRIDED_DILATIONS, strided))
    dil = [_dilated(sources.get(dilation, qkv), b, s, rel_bias, dilation) for _window, dilation in DIL_PATTERNS]
    ob = _stickbreaking(qkv.reshape(b, s, 3 * d))
    x1 = _mix([o for o, _ in dil], [l for _, l in dil], ob, x2, g_out_dil, g_out_sb, w_o)
    h2, meta, gates, counts = _router(x1, g_moe, w_router, b_router)
    dest, block_expert, n_used, n_valid, n_slots = _slot_layout(meta, counts, n)
    xs = _dispatch(h2, dest, n_slots)
    ys = _experts(xs, block_expert, n_used, n_valid, w_gate, b_gate, w_up, b_up, w_down, b_down)
    out = _combine(ys, dest, gates, x1, g_final)
    return out.reshape(b, s, d)
```

```python
import functools
import math

import jax
import jax.numpy as jnp
from jax import lax
from jax.experimental import pallas as pl
from jax.experimental.pallas import tpu as pltpu
from jax.experimental.pallas import tpu_sc as plsc

F32 = jnp.float32
BF16 = jnp.bfloat16

D_MODEL = 1024
HEAD_DIM = 64
D_HALF = 512
N_HEADS = 8
LANES = 128
N_PAIRS = D_HALF // LANES
DIL_PATTERNS = ((128, 1), (512, 4), (2048, 16))
BAND = 128
REL_BUCKETS = 32
REL_MAX_DISTANCE = 2048
N_EXPERTS = 32
TOP_K = 4
SWIGLU_ALPHA = 1.702
SWIGLU_LIMIT = 7.0
RMS_EPS = 1e-6
NEG_INF = -1e30

QKV_TM = 512
A_COLS = 3 * D_HALF
STRIDED_DILATIONS = tuple(d for _w, d in DIL_PATTERNS if d > 1)
LOG2E = math.log2(math.e)
SB_T = 256
SB_G = 4
SB_EXP_ZERO = 104.0
MIX_TM = 256
ROUTER_TM = 512
DISPATCH_R = 256
EXPERT_TM = 512
COMBINE_R = 256
SC_CORES = 2
SC_WORKERS = SC_CORES * 16
SC_CHUNK = 32
DMA_UNROLL = 8
VMEM_LIMIT = 48 * 1024 * 1024


def _cparams(sem):
    return pltpu.CompilerParams(dimension_semantics=sem, vmem_limit_bytes=VMEM_LIMIT)


def _mm(a, b):
    return jnp.dot(a, b, preferred_element_type=F32)


def _mm_nt(a, b):
    return lax.dot_general(a, b, (((1,), (1,)), ((), ())), preferred_element_type=F32)


def _split_bf16(a):
    hi = a.astype(BF16)
    lo = (a - hi.astype(F32)).astype(BF16)
    return hi, lo


def _mm_split(a, b_bf16):
    hi, lo = _split_bf16(a)
    return _mm(hi, b_bf16) + _mm(lo, b_bf16)


def _rms(x):
    return x * lax.rsqrt(jnp.mean(x * x, axis=-1, keepdims=True) + RMS_EPS)


def _half_masks(dtype):
    lane = lax.broadcasted_iota(jnp.int32, (1, LANES), 1)
    lo = jnp.where(lane < HEAD_DIM, 1.0, 0.0).astype(dtype)
    hi = jnp.where(lane >= HEAD_DIM, 1.0, 0.0).astype(dtype)
    return lo, hi


def _qkv_kernel(x_ref, g_ref, w_ref, o_ref, *rest):
    stage_ref = rest[-1]
    h = (_rms(x_ref[...]) * g_ref[...]).astype(BF16)
    for c in range(3 * D_MODEL // D_HALF):
        y = _mm(h, w_ref[:, c * D_HALF:(c + 1) * D_HALF])
        if c == 0:
            y = y * (1.0 / math.sqrt(HEAD_DIM))
        if c == 3:
            y = y * (LOG2E / math.sqrt(HEAD_DIM))
        o_ref[:, c * D_HALF:(c + 1) * D_HALF] = y.astype(BF16)
        if c < 3:
            for p in range(N_PAIRS):
                stage_ref[c * N_PAIRS + p] = y[:, p * LANES:(p + 1) * LANES]
    for od_ref, d in zip(rest[:-1], STRIDED_DILATIONS):
        for r in range(d):
            for ch in range(A_COLS // LANES):
                col = r * A_COLS + ch * LANES
                od_ref[:, col:col + LANES] = stage_ref[ch, pl.ds(r, QKV_TM // d, stride=d), :].astype(BF16)


def _qkv(x2, g_attn, w_qkv):
    n = x2.shape[0]
    strided = [(n // d, d * A_COLS) for d in STRIDED_DILATIONS]
    return pl.pallas_call(
        _qkv_kernel,
        grid=(n // QKV_TM,),
        in_specs=[pl.BlockSpec((QKV_TM, D_MODEL), lambda i: (i, 0)),
                  pl.BlockSpec((1, D_MODEL), lambda i: (0, 0)),
                  pl.BlockSpec((D_MODEL, 3 * D_MODEL), lambda i: (0, 0))],
        out_specs=[pl.BlockSpec((QKV_TM, 3 * D_MODEL), lambda i: (i, 0))]
                  + [pl.BlockSpec((QKV_TM // d, d * A_COLS), lambda i: (i, 0)) for d in STRIDED_DILATIONS],
        out_shape=[jax.ShapeDtypeStruct((n, 3 * D_MODEL), BF16)]
                  + [jax.ShapeDtypeStruct(shape, BF16) for shape in strided],
        scratch_shapes=[pltpu.VMEM((A_COLS // LANES, QKV_TM, LANES), F32)],
        compiler_params=_cparams(("parallel",)),
        name="qkv",
    )(x2, g_attn.reshape(1, D_MODEL), w_qkv.astype(BF16))


def _t5_bucket(dist):
    max_exact = REL_BUCKETS // 2
    d_f = jnp.maximum(dist, 1).astype(jnp.float32)
    large = max_exact + (jnp.log(d_f / max_exact)
                         / math.log(REL_MAX_DISTANCE / max_exact)
                         * (REL_BUCKETS - max_exact)).astype(jnp.int32)
    large = jnp.minimum(large, REL_BUCKETS - 1)
    return jnp.where(dist < max_exact, dist, large)


def _band_bias(rel_bias, dilation):
    qi = jnp.arange(BAND, dtype=jnp.int32)[:, None]
    kj = jnp.arange(2 * BAND, dtype=jnp.int32)[None, :]
    rel = qi + BAND - kj
    ok = (rel >= 0) & (rel <= BAND)
    bucket = _t5_bucket(jnp.maximum(rel, 0) * dilation)
    table = rel_bias.astype(F32)
    bias = jnp.zeros((N_HEADS, BAND, 2 * BAND), F32)
    for bk in range(REL_BUCKETS):
        bias = jnp.where(bucket[None] == bk, table[bk][:, None, None], bias)
    return jnp.where(ok[None], bias, NEG_INF)


def _dilated_kernel(q_ref, kp_ref, kc_ref, vp_ref, vc_ref, bias_ref, o_ref, lse_ref):
    n = pl.program_id(2)
    kj = lax.broadcasted_iota(jnp.int32, (BAND, 2 * BAND), 1)
    has_prev = jnp.logical_or(n > 0, kj >= BAND)
    lane = lax.broadcasted_iota(jnp.int32, (BAND, LANES), 1)
    m_lo, m_hi = _half_masks(BF16)
    lse_tile = jnp.zeros((BAND, LANES), F32)
    for p in range(N_PAIRS):
        cols = slice(p * LANES, (p + 1) * LANES)
        q2 = q_ref[0, :, cols]
        kcat = jnp.concatenate([kp_ref[0, :, cols], kc_ref[0, :, cols]], axis=0)
        vcat = jnp.concatenate([vp_ref[0, :, cols], vc_ref[0, :, cols]], axis=0)
        outs = []
        for s, msk in enumerate((m_lo, m_hi)):
            h = 2 * p + s
            logits = _mm_nt(q2 * msk, kcat) + bias_ref[h]
            logits = jnp.where(has_prev, logits, NEG_INF)
            m = jnp.max(logits, axis=-1, keepdims=True)
            pr = jnp.exp(logits - m)
            den = jnp.sum(pr, axis=-1, keepdims=True)
            outs.append(_mm(pr.astype(BF16), vcat) / den)
            lse_tile = jnp.where(lane == h, m + jnp.log(den), lse_tile)
        o_ref[0, :, cols] = jnp.where(lane < HEAD_DIM, outs[0], outs[1])
    lse_ref[0] = lse_tile


def _dilated(src, b, s, rel_bias, dilation):
    d = dilation
    l = s // d
    nblk = l // BAND
    view = src.reshape(b, l, src.shape[1])
    n_sec = src.shape[1] // d // D_HALF
    blk = (1, BAND, D_HALF)

    def sec(k, prev):
        if prev:
            return pl.BlockSpec(blk, lambda bi, r, n: (bi, jnp.maximum(n - 1, 0), r * n_sec + k))
        return pl.BlockSpec(blk, lambda bi, r, n: (bi, n, r * n_sec + k))

    o, lse = pl.pallas_call(
        _dilated_kernel,
        grid=(b, d, nblk),
        in_specs=[sec(0, False), sec(1, True), sec(1, False), sec(2, True), sec(2, False),
                  pl.BlockSpec((N_HEADS, BAND, 2 * BAND), lambda bi, r, n: (0, 0, 0))],
        out_specs=[pl.BlockSpec((1, BAND, D_HALF), lambda bi, r, n: (bi, n, r)),
                   pl.BlockSpec((1, BAND, LANES), lambda bi, r, n: (bi, n, r))],
        out_shape=[jax.ShapeDtypeStruct((b, l, d * D_HALF), F32),
                   jax.ShapeDtypeStruct((b, l, d * LANES), F32)],
        compiler_params=_cparams(("parallel", "parallel", "arbitrary")),
        name=f"dilated_d{d}",
    )(view, view, view, view, view, _band_bias(rel_bias, d))
    return o.reshape(b * s, D_HALF), lse.reshape(b * s, LANES)


def _sb_kernel(q_ref, k_ref, v_ref, u_ref, o_ref, acc_ref, carry_ref):
    i = pl.program_id(2)
    m_lo, m_hi = _half_masks(BF16)
    row = lax.broadcasted_iota(jnp.int32, (2 * SB_T, SB_T), 0) & (SB_T - 1)
    col = lax.broadcasted_iota(jnp.int32, (2 * SB_T, SB_T), 1)
    causal = col < row
    lane = lax.broadcasted_iota(jnp.int32, (SB_T, LANES), 1)
    u = u_ref[...]

    def tile(qs, j, mask, carry):
        start = pl.multiple_of(j * SB_T, SB_T)
        kj = k_ref[0, pl.ds(start, SB_T), :]
        vj = v_ref[0, pl.ds(start, SB_T), :]
        z = _mm_nt(qs, kj)
        neg_abs = lax.bitcast_convert_type(
            lax.bitcast_convert_type(z, jnp.uint32) | jnp.uint32(0x80000000), F32)
        sp = jnp.log2(1.0 + jnp.exp2(neg_abs))
        log_b = jnp.minimum(z, 0.0) - sp
        l1m = log_b - z
        if mask is not None:
            l1m = jnp.where(mask, l1m, 0.0)
        cum = _mm_split(l1m, u)
        total = cum[:, 0:1] + l1m[:, 0:1]
        a = jnp.exp2(log_b + cum + carry)
        if mask is not None:
            a = jnp.where(mask, a, 0.0)
        return _mm(a.astype(BF16), vj), total

    def more(state):
        j, top = state
        return jnp.logical_and(j >= 0, top > -SB_EXP_ZERO * LOG2E)

    tops = []
    for g in range(SB_G):
        blk = i * SB_G + g
        q2 = q_ref[0, g * SB_T:(g + 1) * SB_T, :]
        qs = jnp.concatenate([q2 * m_lo, q2 * m_hi], axis=0)
        pv_d, tot_d = tile(qs, blk, causal, 0.0)
        if g == 0:
            pv_p, tot_p = tile(qs, jnp.maximum(blk - 1, 0), blk > 0, tot_d)
        else:
            pv_p, tot_p = tile(qs, blk - 1, None, tot_d)
        acc_ref[g] = pv_d + pv_p
        carry0 = tot_d + tot_p
        carry_ref[g] = carry0
        tops.append(jnp.max(carry0))

    for g in range(SB_G):
        def body(state, g=g):
            j, _ = state
            q2 = q_ref[0, g * SB_T:(g + 1) * SB_T, :]
            qs = jnp.concatenate([q2 * m_lo, q2 * m_hi], axis=0)
            pv, tot = tile(qs, j, None, carry_ref[g])
            acc_ref[g] += pv
            carry = carry_ref[g] + tot
            carry_ref[g] = carry
            return j - 1, jnp.max(carry)

        lax.while_loop(more, body, (i * SB_G + g - 2, tops[g]))
        o_ref[0, g * SB_T:(g + 1) * SB_T, :] = jnp.where(lane < HEAD_DIM, acc_ref[g, :SB_T], acc_ref[g, SB_T:])


def _suffix_matrix():
    sp = jnp.arange(SB_T, dtype=jnp.int32)[:, None]
    sc = jnp.arange(SB_T, dtype=jnp.int32)[None, :]
    return jnp.where(sp > sc, 1.0, 0.0).astype(BF16)


def _stickbreaking(qkv3):
    b, s, _ = qkv3.shape
    sec_q, sec_k, sec_v = 3 * N_PAIRS, 4 * N_PAIRS, 5 * N_PAIRS
    o = pl.pallas_call(
        _sb_kernel,
        grid=(b, N_PAIRS, s // (SB_G * SB_T)),
        in_specs=[pl.BlockSpec((1, SB_G * SB_T, LANES), lambda bi, p, i: (bi, i, sec_q + p)),
                  pl.BlockSpec((1, s, LANES), lambda bi, p, i: (bi, 0, sec_k + p)),
                  pl.BlockSpec((1, s, LANES), lambda bi, p, i: (bi, 0, sec_v + p)),
                  pl.BlockSpec((SB_T, SB_T), lambda bi, p, i: (0, 0))],
        out_specs=pl.BlockSpec((1, SB_G * SB_T, LANES), lambda bi, p, i: (bi, i, p)),
        out_shape=jax.ShapeDtypeStruct((b, s, D_HALF), F32),
        scratch_shapes=[pltpu.VMEM((SB_G, 2 * SB_T, LANES), F32), pltpu.VMEM((SB_G, 2 * SB_T, 1), F32)],
        compiler_params=_cparams(("parallel", "parallel", "arbitrary")),
        name="stickbreaking",
    )(qkv3, qkv3, qkv3, _suffix_matrix())
    return o.reshape(b * s, D_HALF)


def _mix_kernel(o1_ref, o2_ref, o3_ref, l1_ref, l2_ref, l3_ref, ob_ref, x_ref,
                ga_ref, gb_ref, e_ref, wo_ref, x1_ref):
    l1, l2, l3 = l1_ref[...], l2_ref[...], l3_ref[...]
    m = jnp.maximum(jnp.maximum(l1, l2), l3)
    e1, e2, e3 = jnp.exp(l1 - m), jnp.exp(l2 - m), jnp.exp(l3 - m)
    inv = 1.0 / (e1 + e2 + e3)
    expand = e_ref[...]
    oa = (_mm_split(e1 * inv, expand) * o1_ref[...]
          + _mm_split(e2 * inv, expand) * o2_ref[...]
          + _mm_split(e3 * inv, expand) * o3_ref[...])
    oa = (_rms(oa) * ga_ref[...]).astype(BF16)
    ob = (_rms(ob_ref[...]) * gb_ref[...]).astype(BF16)
    mix = _mm(oa, wo_ref[:D_HALF, :]) + _mm(ob, wo_ref[D_HALF:, :])
    x1_ref[...] = x_ref[...] + mix


def _head_expand():
    lane = jnp.arange(LANES, dtype=jnp.int32)[:, None]
    col = jnp.arange(D_HALF, dtype=jnp.int32)[None, :]
    return jnp.where(col // HEAD_DIM == lane, 1.0, 0.0).astype(BF16)


def _mix(os_, lses, ob, x2, g_out_dil, g_out_sb, w_o):
    n = x2.shape[0]
    row = lambda w: pl.BlockSpec((MIX_TM, w), lambda i: (i, 0))
    const = lambda shape: pl.BlockSpec(shape, lambda i: (0, 0))
    return pl.pallas_call(
        _mix_kernel,
        grid=(n // MIX_TM,),
        in_specs=[row(D_HALF)] * 3 + [row(LANES)] * 3 + [row(D_HALF), row(D_MODEL),
                  const((1, D_HALF)), const((1, D_HALF)), const((LANES, D_HALF)), const((D_MODEL, D_MODEL))],
        out_specs=row(D_MODEL),
        out_shape=jax.ShapeDtypeStruct((n, D_MODEL), F32),
        compiler_params=_cparams(("parallel",)),
        name="mix_wo",
    )(*os_, *lses, ob, x2, g_out_dil.reshape(1, D_HALF), g_out_sb.reshape(1, D_HALF),
      _head_expand(), w_o.astype(BF16))


def _router_kernel(x1_ref, g_ref, wr_ref, br_ref, tri_ref, h2_ref, meta_ref, gate_ref, cnt_ref, carry_ref):
    i = pl.program_id(0)

    @pl.when(i == 0)
    def _():
        carry_ref[...] = jnp.zeros_like(carry_ref)

    h2 = _rms(x1_ref[...]) * g_ref[...]
    h2_ref[...] = h2
    h_hi, h_lo = _split_bf16(h2)
    w_hi, w_lo = wr_ref[0], wr_ref[1]
    logits = _mm(h_hi, w_hi) + (_mm(h_hi, w_lo) + _mm(h_lo, w_hi)) + br_ref[...]
    lane = lax.broadcasted_iota(jnp.int32, (ROUTER_TM, LANES), 1)
    lane_f = lane.astype(F32)
    logits = jnp.where(lane < N_EXPERTS, logits, -jnp.inf)

    onehots, vals, ids = [], [], []
    for _k in range(TOP_K):
        m = jnp.max(logits, axis=-1, keepdims=True)
        idx = jnp.min(jnp.where(logits == m, lane_f, float(LANES)), axis=-1, keepdims=True)
        oh = lane_f == idx
        logits = jnp.where(oh, -jnp.inf, logits)
        onehots.append(oh)
        vals.append(m)
        ids.append(idx)

    es = [jnp.exp(v - vals[0]) for v in vals]
    inv = 1.0 / (es[0] + es[1] + es[2] + es[3])

    sel = jnp.zeros((ROUTER_TM, LANES), F32)
    for oh in onehots:
        sel = jnp.where(oh, 1.0, sel)
    before = _mm(tri_ref[...], sel.astype(BF16)) + carry_ref[0:1, :]
    meta = jnp.zeros((ROUTER_TM, LANES), F32)
    gate = jnp.zeros((ROUTER_TM, LANES), F32)
    for k in range(TOP_K):
        rank = jnp.sum(jnp.where(onehots[k], before, 0.0), axis=-1, keepdims=True)
        meta = jnp.where(lane == k, ids[k], meta)
        meta = jnp.where(lane == TOP_K + k, rank, meta)
        gate = jnp.where(lane == k, es[k] * inv, gate)
    meta_ref[...] = meta.astype(jnp.int32)
    gate_ref[...] = gate
    total = carry_ref[...] + jnp.sum(sel, axis=0, keepdims=True)
    carry_ref[...] = total
    cnt_ref[...] = total


def _router(x1, g_moe, w_router, b_router):
    n = x1.shape[0]
    wr = jnp.zeros((D_MODEL, LANES), F32).at[:, :N_EXPERTS].set(w_router.astype(F32))
    wr_hi = wr.astype(BF16)
    wr_lo = (wr - wr_hi.astype(F32)).astype(BF16)
    br = jnp.zeros((1, LANES), F32).at[0, :N_EXPERTS].set(b_router.astype(F32))
    r = jnp.arange(ROUTER_TM, dtype=jnp.int32)
    tri = jnp.where(r[None, :] < r[:, None], 1.0, 0.0).astype(BF16)
    row = lambda w: pl.BlockSpec((ROUTER_TM, w), lambda i: (i, 0))
    return pl.pallas_call(
        _router_kernel,
        grid=(n // ROUTER_TM,),
        in_specs=[row(D_MODEL),
                  pl.BlockSpec((1, D_MODEL), lambda i: (0, 0)),
                  pl.BlockSpec((2, D_MODEL, LANES), lambda i: (0, 0, 0)),
                  pl.BlockSpec((1, LANES), lambda i: (0, 0)),
                  pl.BlockSpec((ROUTER_TM, ROUTER_TM), lambda i: (0, 0))],
        out_specs=[row(D_MODEL), row(LANES), row(LANES), pl.BlockSpec((8, LANES), lambda i: (0, 0))],
        out_shape=[jax.ShapeDtypeStruct((n, D_MODEL), F32),
                   jax.ShapeDtypeStruct((n, LANES), jnp.int32),
                   jax.ShapeDtypeStruct((n, LANES), F32),
                   jax.ShapeDtypeStruct((8, LANES), F32)],
        scratch_shapes=[pltpu.VMEM((8, LANES), F32)],
        compiler_params=_cparams(("arbitrary",)),
        name="router",
    )(x1, g_moe.reshape(1, D_MODEL), jnp.stack([wr_hi, wr_lo]), br, tri)


def _dispatch_kernel(dest_ref, h2_ref, xs_ref, sem):
    def row_copy(r, k):
        d = dest_ref[0, 0, r * TOP_K + k]
        return pltpu.make_async_copy(h2_ref.at[pl.ds(r, 1)], xs_ref.at[pl.ds(d, 1)], sem)

    def start(r, c):
        for k in range(TOP_K):
            row_copy(r, k).start()
        return c

    def wait(r, c):
        for k in range(TOP_K):
            row_copy(r, k).wait()
        return c

    lax.fori_loop(0, DISPATCH_R, start, 0, unroll=DMA_UNROLL)
    lax.fori_loop(0, DISPATCH_R, wait, 0, unroll=DMA_UNROLL)


def _dispatch(h2, dest, n_slots):
    n = h2.shape[0]
    nb = n // DISPATCH_R
    return pl.pallas_call(
        _dispatch_kernel,
        grid=(nb,),
        in_specs=[pl.BlockSpec((1, 1, DISPATCH_R * TOP_K), lambda i: (i, 0, 0), memory_space=pltpu.SMEM),
                  pl.BlockSpec((DISPATCH_R, D_MODEL), lambda i: (i, 0))],
        out_specs=pl.BlockSpec(memory_space=pl.ANY),
        out_shape=jax.ShapeDtypeStruct((n_slots, D_MODEL), F32),
        scratch_shapes=[pltpu.SemaphoreType.DMA(())],
        compiler_params=_cparams(("arbitrary",)),
        name="dispatch",
    )(dest.reshape(nb, 1, DISPATCH_R * TOP_K), h2)


def _expert_kernel(be_ref, nu_ref, nv_ref, x_ref, wg_ref, bg_ref, wu_ref, bu_ref, wd_ref, bd_ref, y_ref):
    del be_ref
    i = pl.program_id(0)

    @pl.when(i < nu_ref[0])
    def _():
        row = lax.broadcasted_iota(jnp.int32, (EXPERT_TM, 1), 0)
        xb = jnp.where(row < nv_ref[i], x_ref[...], 0.0).astype(BF16)
        y = jnp.zeros((EXPERT_TM, D_MODEL), F32) + bd_ref[0]
        for c in range(D_MODEL // D_HALF):
            cols = slice(c * D_HALF, (c + 1) * D_HALF)
            glu = _mm(xb, wg_ref[0, :, cols]) + bg_ref[0, :, cols]
            lin = _mm(xb, wu_ref[0, :, cols]) + bu_ref[0, :, cols]
            glu = jnp.minimum(glu, SWIGLU_LIMIT)
            lin = jnp.clip(lin, -SWIGLU_LIMIT, SWIGLU_LIMIT)
            act = glu * (1.0 / (1.0 + jnp.exp(-SWIGLU_ALPHA * glu))) * (lin + 1.0)
            y = y + _mm(act.astype(BF16), wd_ref[0, cols, :])
        y_ref[...] = y


def _experts(xs, block_expert, n_used, n_valid, w_gate, b_gate, w_up, b_up, w_down, b_down):
    n_slots = xs.shape[0]
    n_blocks = n_slots // EXPERT_TM
    rows = lambda i, be, nu, nv: (jnp.minimum(i, nu[0] - 1), 0)
    wspec = pl.BlockSpec((1, D_MODEL, D_MODEL), lambda i, be, nu, nv: (be[i], 0, 0))
    bspec = pl.BlockSpec((1, 1, D_MODEL), lambda i, be, nu, nv: (be[i], 0, 0))
    grid_spec = pltpu.PrefetchScalarGridSpec(
        num_scalar_prefetch=3,
        grid=(n_blocks,),
        in_specs=[pl.BlockSpec((EXPERT_TM, D_MODEL), rows), wspec, bspec, wspec, bspec, wspec, bspec],
        out_specs=pl.BlockSpec((EXPERT_TM, D_MODEL), rows),
    )
    b3 = lambda a: a.astype(F32).reshape(N_EXPERTS, 1, D_MODEL)
    return pl.pallas_call(
        _expert_kernel,
        grid_spec=grid_spec,
        out_shape=jax.ShapeDtypeStruct((n_slots, D_MODEL), F32),
        compiler_params=_cparams(("arbitrary",)),
        name="experts",
    )(block_expert, n_used, n_valid, xs, w_gate.astype(BF16), b3(b_gate), w_up.astype(BF16), b3(b_up),
      w_down.astype(BF16), b3(b_down))


def _sc_row_gather(table, idx):
    n_idx = idx.shape[0]
    width = table.shape[1]
    per_w = n_idx // SC_WORKERS
    nch = per_w // SC_CHUNK
    mesh = plsc.VectorSubcoreMesh(core_axis_name="core", subcore_axis_name="subcore")

    @functools.partial(
        pl.kernel, mesh=mesh,
        out_type=jax.ShapeDtypeStruct((n_idx, width), table.dtype),
        scratch_types=[pltpu.VMEM((per_w,), jnp.int32), pltpu.VMEM((2, SC_CHUNK, width), table.dtype),
                       pltpu.SemaphoreType.DMA((2,)), pltpu.SemaphoreType.DMA((2,))])
    def gather_kernel(table_hbm, idx_hbm, out_hbm, idx_v, rows_v, sem_g, sem_p):
        wid = lax.axis_index("subcore") * SC_CORES + lax.axis_index("core")
        base = wid * per_w
        pltpu.sync_copy(idx_hbm.at[pl.ds(base, per_w)], idx_v)

        def gather(c, b):
            return pltpu.make_async_copy(table_hbm.at[idx_v.at[pl.ds(c * SC_CHUNK, SC_CHUNK)]], rows_v.at[b], sem_g.at[b])

        def put(c, b):
            return pltpu.make_async_copy(rows_v.at[b], out_hbm.at[pl.ds(base + c * SC_CHUNK, SC_CHUNK)], sem_p.at[b])

        gather(0, 0).start()

        @pl.loop(0, nch, step=2)
        def _(c):
            for b in (0, 1):
                cc = c + b

                @pl.when(cc >= 1)
                def _():
                    put(cc - 1, 1 - b).wait()

                @pl.when(cc + 1 < nch)
                def _():
                    gather(cc + 1, 1 - b).start()

                gather(cc, b).wait()
                put(cc, b).start()

        put(nch - 1, 1).wait()

    return gather_kernel(table, idx)


def _combine_kernel(yt_ref, gate_ref, x1_ref, g_ref, o_ref):
    gate = gate_ref[...]
    x = x1_ref[...]
    for k in range(TOP_K):
        x = x + gate[:, k:k + 1] * yt_ref[k]
    o_ref[...] = _rms(x) * g_ref[...]


def _combine(ys, dest, gates, x1, g_final):
    n = x1.shape[0]
    yt = _sc_row_gather(ys, dest.T.reshape(-1)).reshape(TOP_K, n, D_MODEL)
    row = lambda w: pl.BlockSpec((COMBINE_R, w), lambda i: (i, 0))
    return pl.pallas_call(
        _combine_kernel,
        grid=(n // COMBINE_R,),
        in_specs=[pl.BlockSpec((TOP_K, COMBINE_R, D_MODEL), lambda i: (0, i, 0)),
                  row(LANES), row(D_MODEL),
                  pl.BlockSpec((1, D_MODEL), lambda i: (0, 0))],
        out_specs=row(D_MODEL),
        out_shape=jax.ShapeDtypeStruct((n, D_MODEL), F32),
        compiler_params=_cparams(("parallel",)),
        name="combine",
    )(yt, gates, x1, g_final.reshape(1, D_MODEL))


def _slot_layout(meta, counts_f, n):
    eid = meta[:, :TOP_K]
    rank = meta[:, TOP_K:2 * TOP_K]
    counts = counts_f[0, :N_EXPERTS].astype(jnp.int32)
    padded = (counts + EXPERT_TM - 1) // EXPERT_TM * EXPERT_TM
    pend = jnp.cumsum(padded)
    pstart = pend - padded
    experts = jnp.arange(N_EXPERTS, dtype=jnp.int32)
    dest = rank + jnp.sum(jnp.where(eid[..., None] == experts, pstart, 0), axis=-1)
    n_slots = -(-(n * TOP_K + N_EXPERTS * (EXPERT_TM - 1)) // EXPERT_TM) * EXPERT_TM
    block_start = jnp.arange(n_slots // EXPERT_TM, dtype=jnp.int32) * EXPERT_TM
    block_expert = jnp.minimum(jnp.sum(pend[None, :] <= block_start[:, None], axis=-1), N_EXPERTS - 1)
    last_valid = jnp.sum(jnp.where(block_expert[:, None] == experts, pstart + counts, 0), axis=-1)
    n_valid = jnp.clip(last_valid - block_start, 0, EXPERT_TM).astype(jnp.int32)
    n_used = (pend[-1:] // EXPERT_TM).astype(jnp.int32)
    return dest.astype(jnp.int32), block_expert.astype(jnp.int32), n_used, n_valid, n_slots


def kernel(x, g_attn, w_qkv, rel_bias, g_out_dil, g_out_sb, w_o, g_moe, w_router, b_router,
           w_gate, b_gate, w_up, b_up, w_down, b_down, g_final):
    b, s, d = x.shape
    n = b * s
    x2 = x.reshape(n, d)
    qkv, *strided = _qkv(x2, g_attn, w_qkv)
    sources = dict(zip(STRIDED_DILATIONS, strided))
    dil = [_dilated(sources.get(dilation, qkv), b, s, rel_bias, dilation) for _window, dilation in DIL_PATTERNS]
    ob = _stickbreaking(qkv.reshape(b, s, 3 * d))
    x1 = _mix([o for o, _ in dil], [l for _, l in dil], ob, x2, g_out_dil, g_out_sb, w_o)
    h2, meta, gates, counts = _router(x1, g_moe, w_router, b_router)
    dest, block_expert, n_used, n_valid, n_slots = _slot_layout(meta, counts, n)
    xs = _dispatch(h2, dest, n_slots)
    ys = _experts(xs, block_expert, n_used, n_valid, w_gate, b_gate, w_up, b_up, w_down, b_down)
    out = _combine(ys, dest, gates, x1, g_final)
    return out.reshape(b, s, d)
```

```python
import functools
import math

import jax
import jax.numpy as jnp
from jax import lax
from jax.experimental import pallas as pl
from jax.experimental.pallas import tpu as pltpu
from jax.experimental.pallas import tpu_sc as plsc

F32 = jnp.float32
BF16 = jnp.bfloat16

D_MODEL = 1024
HEAD_DIM = 64
D_HALF = 512
N_HEADS = 8
LANES = 128
N_PAIRS = D_HALF // LANES
DIL_PATTERNS = ((128, 1), (512, 4), (2048, 16))
BAND = 128
REL_BUCKETS = 32
REL_MAX_DISTANCE = 2048
N_EXPERTS = 32
TOP_K = 4
SWIGLU_ALPHA = 1.702
SWIGLU_LIMIT = 7.0
RMS_EPS = 1e-6
NEG_INF = -1e30

QKV_TM = 512
A_COLS = 3 * D_HALF
STRIDED_DILATIONS = tuple(d for _w, d in DIL_PATTERNS if d > 1)
LOG2E = math.log2(math.e)
SB_T = 256
SB_G = 4
SB_EXP_ZERO = 104.0
MIX_TM = 256
ROUTER_TM = 512
DISPATCH_R = 256
EXPERT_TM = 512
COMBINE_R = 256
SC_CORES = 2
SC_WORKERS = SC_CORES * 16
SC_CHUNK = 32
SC_SCATTER_ROWS = 128
SC_SCATTER_COLS = 256
DMA_UNROLL = 8
VMEM_LIMIT = 48 * 1024 * 1024


def _cparams(sem):
    return pltpu.CompilerParams(dimension_semantics=sem, vmem_limit_bytes=VMEM_LIMIT)


def _mm(a, b):
    return jnp.dot(a, b, preferred_element_type=F32)


def _mm_nt(a, b):
    return lax.dot_general(a, b, (((1,), (1,)), ((), ())), preferred_element_type=F32)


def _split_bf16(a):
    hi = a.astype(BF16)
    lo = (a - hi.astype(F32)).astype(BF16)
    return hi, lo


def _mm_split(a, b_bf16):
    hi, lo = _split_bf16(a)
    return _mm(hi, b_bf16) + _mm(lo, b_bf16)


def _rms(x):
    return x * lax.rsqrt(jnp.mean(x * x, axis=-1, keepdims=True) + RMS_EPS)


def _half_masks(dtype):
    lane = lax.broadcasted_iota(jnp.int32, (1, LANES), 1)
    lo = jnp.where(lane < HEAD_DIM, 1.0, 0.0).astype(dtype)
    hi = jnp.where(lane >= HEAD_DIM, 1.0, 0.0).astype(dtype)
    return lo, hi


def _qkv_kernel(x_ref, g_ref, w_ref, o_ref, *rest):
    stage_ref = rest[-1]
    h = (_rms(x_ref[...]) * g_ref[...]).astype(BF16)
    for c in range(3 * D_MODEL // D_HALF):
        y = _mm(h, w_ref[:, c * D_HALF:(c + 1) * D_HALF])
        if c == 0:
            y = y * (1.0 / math.sqrt(HEAD_DIM))
        if c == 3:
            y = y * (LOG2E / math.sqrt(HEAD_DIM))
        o_ref[:, c * D_HALF:(c + 1) * D_HALF] = y.astype(BF16)
        if c < 3:
            for p in range(N_PAIRS):
                stage_ref[c * N_PAIRS + p] = y[:, p * LANES:(p + 1) * LANES]
    for od_ref, d in zip(rest[:-1], STRIDED_DILATIONS):
        for r in range(d):
            for ch in range(A_COLS // LANES):
                col = r * A_COLS + ch * LANES
                od_ref[:, col:col + LANES] = stage_ref[ch, pl.ds(r, QKV_TM // d, stride=d), :].astype(BF16)


def _qkv(x2, g_attn, w_qkv):
    n = x2.shape[0]
    strided = [(n // d, d * A_COLS) for d in STRIDED_DILATIONS]
    return pl.pallas_call(
        _qkv_kernel,
        grid=(n // QKV_TM,),
        in_specs=[pl.BlockSpec((QKV_TM, D_MODEL), lambda i: (i, 0)),
                  pl.BlockSpec((1, D_MODEL), lambda i: (0, 0)),
                  pl.BlockSpec((D_MODEL, 3 * D_MODEL), lambda i: (0, 0))],
        out_specs=[pl.BlockSpec((QKV_TM, 3 * D_MODEL), lambda i: (i, 0))]
                  + [pl.BlockSpec((QKV_TM // d, d * A_COLS), lambda i: (i, 0)) for d in STRIDED_DILATIONS],
        out_shape=[jax.ShapeDtypeStruct((n, 3 * D_MODEL), BF16)]
                  + [jax.ShapeDtypeStruct(shape, BF16) for shape in strided],
        scratch_shapes=[pltpu.VMEM((A_COLS // LANES, QKV_TM, LANES), F32)],
        compiler_params=_cparams(("parallel",)),
        name="qkv",
    )(x2, g_attn.reshape(1, D_MODEL), w_qkv.astype(BF16))


def _t5_bucket(dist):
    max_exact = REL_BUCKETS // 2
    d_f = jnp.maximum(dist, 1).astype(jnp.float32)
    large = max_exact + (jnp.log(d_f / max_exact)
                         / math.log(REL_MAX_DISTANCE / max_exact)
                         * (REL_BUCKETS - max_exact)).astype(jnp.int32)
    large = jnp.minimum(large, REL_BUCKETS - 1)
    return jnp.where(dist < max_exact, dist, large)


def _band_bias(rel_bias, dilation):
    qi = jnp.arange(BAND, dtype=jnp.int32)[:, None]
    kj = jnp.arange(2 * BAND, dtype=jnp.int32)[None, :]
    rel = qi + BAND - kj
    ok = (rel >= 0) & (rel <= BAND)
    bucket = _t5_bucket(jnp.maximum(rel, 0) * dilation)
    table = rel_bias.astype(F32)
    bias = jnp.zeros((N_HEADS, BAND, 2 * BAND), F32)
    for bk in range(REL_BUCKETS):
        bias = jnp.where(bucket[None] == bk, table[bk][:, None, None], bias)
    return jnp.where(ok[None], bias, NEG_INF)


def _dilated_kernel(q_ref, kp_ref, kc_ref, vp_ref, vc_ref, bias_ref, o_ref, lse_ref):
    n = pl.program_id(2)
    kj = lax.broadcasted_iota(jnp.int32, (BAND, 2 * BAND), 1)
    has_prev = jnp.logical_or(n > 0, kj >= BAND)
    lane = lax.broadcasted_iota(jnp.int32, (BAND, LANES), 1)
    m_lo, m_hi = _half_masks(BF16)
    lse_tile = jnp.zeros((BAND, LANES), F32)
    for p in range(N_PAIRS):
        cols = slice(p * LANES, (p + 1) * LANES)
        q2 = q_ref[0, :, cols]
        kcat = jnp.concatenate([kp_ref[0, :, cols], kc_ref[0, :, cols]], axis=0)
        vcat = jnp.concatenate([vp_ref[0, :, cols], vc_ref[0, :, cols]], axis=0)
        outs = []
        for s, msk in enumerate((m_lo, m_hi)):
            h = 2 * p + s
            logits = _mm_nt(q2 * msk, kcat) + bias_ref[h]
            logits = jnp.where(has_prev, logits, NEG_INF)
            m = jnp.max(logits, axis=-1, keepdims=True)
            pr = jnp.exp(logits - m)
            den = jnp.sum(pr, axis=-1, keepdims=True)
            outs.append(_mm(pr.astype(BF16), vcat) / den)
            lse_tile = jnp.where(lane == h, m + jnp.log(den), lse_tile)
        o_ref[0, :, cols] = jnp.where(lane < HEAD_DIM, outs[0], outs[1])
    lse_ref[0] = lse_tile


def _dilated(src, b, s, rel_bias, dilation):
    d = dilation
    l = s // d
    nblk = l // BAND
    view = src.reshape(b, l, src.shape[1])
    n_sec = src.shape[1] // d // D_HALF
    blk = (1, BAND, D_HALF)

    def sec(k, prev):
        if prev:
            return pl.BlockSpec(blk, lambda bi, r, n: (bi, jnp.maximum(n - 1, 0), r * n_sec + k))
        return pl.BlockSpec(blk, lambda bi, r, n: (bi, n, r * n_sec + k))

    o, lse = pl.pallas_call(
        _dilated_kernel,
        grid=(b, d, nblk),
        in_specs=[sec(0, False), sec(1, True), sec(1, False), sec(2, True), sec(2, False),
                  pl.BlockSpec((N_HEADS, BAND, 2 * BAND), lambda bi, r, n: (0, 0, 0))],
        out_specs=[pl.BlockSpec((1, BAND, D_HALF), lambda bi, r, n: (bi, n, r)),
                   pl.BlockSpec((1, BAND, LANES), lambda bi, r, n: (bi, n, r))],
        out_shape=[jax.ShapeDtypeStruct((b, l, d * D_HALF), F32),
                   jax.ShapeDtypeStruct((b, l, d * LANES), F32)],
        compiler_params=_cparams(("parallel", "parallel", "arbitrary")),
        name=f"dilated_d{d}",
    )(view, view, view, view, view, _band_bias(rel_bias, d))
    return o.reshape(b * s, D_HALF), lse.reshape(b * s, LANES)


def _sb_kernel(q_ref, k_ref, v_ref, u_ref, o_ref, acc_ref, carry_ref):
    i = pl.program_id(2)
    m_lo, m_hi = _half_masks(BF16)
    row = lax.broadcasted_iota(jnp.int32, (2 * SB_T, SB_T), 0) & (SB_T - 1)
    col = lax.broadcasted_iota(jnp.int32, (2 * SB_T, SB_T), 1)
    causal = col < row
    lane = lax.broadcasted_iota(jnp.int32, (SB_T, LANES), 1)
    u = u_ref[...]

    def tile(qs, j, mask, carry):
        start = pl.multiple_of(j * SB_T, SB_T)
        kj = k_ref[0, pl.ds(start, SB_T), :]
        vj = v_ref[0, pl.ds(start, SB_T), :]
        z = _mm_nt(qs, kj)
        neg_abs = lax.bitcast_convert_type(
            lax.bitcast_convert_type(z, jnp.uint32) | jnp.uint32(0x80000000), F32)
        sp = jnp.log2(1.0 + jnp.exp2(neg_abs))
        log_b = jnp.minimum(z, 0.0) - sp
        l1m = log_b - z
        if mask is not None:
            l1m = jnp.where(mask, l1m, 0.0)
        cum = _mm_split(l1m, u)
        total = cum[:, 0:1] + l1m[:, 0:1]
        a = jnp.exp2(log_b + cum + carry)
        if mask is not None:
            a = jnp.where(mask, a, 0.0)
        return _mm(a.astype(BF16), vj), total

    def more(state):
        j, top = state
        return jnp.logical_and(j >= 0, top > -SB_EXP_ZERO * LOG2E)

    tops = []
    for g in range(SB_G):
        blk = i * SB_G + g
        q2 = q_ref[0, g * SB_T:(g + 1) * SB_T, :]
        qs = jnp.concatenate([q2 * m_lo, q2 * m_hi], axis=0)
        pv_d, tot_d = tile(qs, blk, causal, 0.0)
        if g == 0:
            pv_p, tot_p = tile(qs, jnp.maximum(blk - 1, 0), blk > 0, tot_d)
        else:
            pv_p, tot_p = tile(qs, blk - 1, None, tot_d)
        acc_ref[g] = pv_d + pv_p
        carry0 = tot_d + tot_p
        carry_ref[g] = carry0
        tops.append(jnp.max(carry0))

    for g in range(SB_G):
        def body(state, g=g):
            j, _ = state
            q2 = q_ref[0, g * SB_T:(g + 1) * SB_T, :]
            qs = jnp.concatenate([q2 * m_lo, q2 * m_hi], axis=0)
            pv, tot = tile(qs, j, None, carry_ref[g])
            acc_ref[g] += pv
            carry = carry_ref[g] + tot
            carry_ref[g] = carry
            return j - 1, jnp.max(carry)

        lax.while_loop(more, body, (i * SB_G + g - 2, tops[g]))
        o_ref[0, g * SB_T:(g + 1) * SB_T, :] = jnp.where(lane < HEAD_DIM, acc_ref[g, :SB_T], acc_ref[g, SB_T:])


def _suffix_matrix():
    sp = jnp.arange(SB_T, dtype=jnp.int32)[:, None]
    sc = jnp.arange(SB_T, dtype=jnp.int32)[None, :]
    return jnp.where(sp > sc, 1.0, 0.0).astype(BF16)


def _stickbreaking(qkv3):
    b, s, _ = qkv3.shape
    sec_q, sec_k, sec_v = 3 * N_PAIRS, 4 * N_PAIRS, 5 * N_PAIRS
    o = pl.pallas_call(
        _sb_kernel,
        grid=(b, N_PAIRS, s // (SB_G * SB_T)),
        in_specs=[pl.BlockSpec((1, SB_G * SB_T, LANES), lambda bi, p, i: (bi, i, sec_q + p)),
                  pl.BlockSpec((1, s, LANES), lambda bi, p, i: (bi, 0, sec_k + p)),
                  pl.BlockSpec((1, s, LANES), lambda bi, p, i: (bi, 0, sec_v + p)),
                  pl.BlockSpec((SB_T, SB_T), lambda bi, p, i: (0, 0))],
        out_specs=pl.BlockSpec((1, SB_G * SB_T, LANES), lambda bi, p, i: (bi, i, p)),
        out_shape=jax.ShapeDtypeStruct((b, s, D_HALF), F32),
        scratch_shapes=[pltpu.VMEM((SB_G, 2 * SB_T, LANES), F32), pltpu.VMEM((SB_G, 2 * SB_T, 1), F32)],
        compiler_params=_cparams(("parallel", "parallel", "arbitrary")),
        name="stickbreaking",
    )(qkv3, qkv3, qkv3, _suffix_matrix())
    return o.reshape(b * s, D_HALF)


def _mix_kernel(o1_ref, o2_ref, o3_ref, l1_ref, l2_ref, l3_ref, ob_ref, x_ref,
                ga_ref, gb_ref, e_ref, wo_ref, x1_ref):
    l1, l2, l3 = l1_ref[...], l2_ref[...], l3_ref[...]
    m = jnp.maximum(jnp.maximum(l1, l2), l3)
    e1, e2, e3 = jnp.exp(l1 - m), jnp.exp(l2 - m), jnp.exp(l3 - m)
    inv = 1.0 / (e1 + e2 + e3)
    expand = e_ref[...]
    oa = (_mm_split(e1 * inv, expand) * o1_ref[...]
          + _mm_split(e2 * inv, expand) * o2_ref[...]
          + _mm_split(e3 * inv, expand) * o3_ref[...])
    oa = (_rms(oa) * ga_ref[...]).astype(BF16)
    ob = (_rms(ob_ref[...]) * gb_ref[...]).astype(BF16)
    mix = _mm(oa, wo_ref[:D_HALF, :]) + _mm(ob, wo_ref[D_HALF:, :])
    x1_ref[...] = x_ref[...] + mix


def _head_expand():
    lane = jnp.arange(LANES, dtype=jnp.int32)[:, None]
    col = jnp.arange(D_HALF, dtype=jnp.int32)[None, :]
    return jnp.where(col // HEAD_DIM == lane, 1.0, 0.0).astype(BF16)


def _mix(os_, lses, ob, x2, g_out_dil, g_out_sb, w_o):
    n = x2.shape[0]
    row = lambda w: pl.BlockSpec((MIX_TM, w), lambda i: (i, 0))
    const = lambda shape: pl.BlockSpec(shape, lambda i: (0, 0))
    return pl.pallas_call(
        _mix_kernel,
        grid=(n // MIX_TM,),
        in_specs=[row(D_HALF)] * 3 + [row(LANES)] * 3 + [row(D_HALF), row(D_MODEL),
                  const((1, D_HALF)), const((1, D_HALF)), const((LANES, D_HALF)), const((D_MODEL, D_MODEL))],
        out_specs=row(D_MODEL),
        out_shape=jax.ShapeDtypeStruct((n, D_MODEL), F32),
        compiler_params=_cparams(("parallel",)),
        name="mix_wo",
    )(*os_, *lses, ob, x2, g_out_dil.reshape(1, D_HALF), g_out_sb.reshape(1, D_HALF),
      _head_expand(), w_o.astype(BF16))


def _router_kernel(x1_ref, g_ref, wr_ref, br_ref, tri_ref, h2_ref, meta_ref, gate_ref, cnt_ref, carry_ref):
    i = pl.program_id(0)

    @pl.when(i == 0)
    def _():
        carry_ref[...] = jnp.zeros_like(carry_ref)

    h2 = _rms(x1_ref[...]) * g_ref[...]
    h2_ref[...] = h2
    h_hi, h_lo = _split_bf16(h2)
    w_hi, w_lo = wr_ref[0], wr_ref[1]
    logits = _mm(h_hi, w_hi) + (_mm(h_hi, w_lo) + _mm(h_lo, w_hi)) + br_ref[...]
    lane = lax.broadcasted_iota(jnp.int32, (ROUTER_TM, LANES), 1)
    lane_f = lane.astype(F32)
    logits = jnp.where(lane < N_EXPERTS, logits, -jnp.inf)

    onehots, vals, ids = [], [], []
    for _k in range(TOP_K):
        m = jnp.max(logits, axis=-1, keepdims=True)
        idx = jnp.min(jnp.where(logits == m, lane_f, float(LANES)), axis=-1, keepdims=True)
        oh = lane_f == idx
        logits = jnp.where(oh, -jnp.inf, logits)
        onehots.append(oh)
        vals.append(m)
        ids.append(idx)

    es = [jnp.exp(v - vals[0]) for v in vals]
    inv = 1.0 / (es[0] + es[1] + es[2] + es[3])

    sel = jnp.zeros((ROUTER_TM, LANES), F32)
    for oh in onehots:
        sel = jnp.where(oh, 1.0, sel)
    before = _mm(tri_ref[...], sel.astype(BF16)) + carry_ref[0:1, :]
    meta = jnp.zeros((ROUTER_TM, LANES), F32)
    gate = jnp.zeros((ROUTER_TM, LANES), F32)
    for k in range(TOP_K):
        rank = jnp.sum(jnp.where(onehots[k], before, 0.0), axis=-1, keepdims=True)
        meta = jnp.where(lane == k, ids[k], meta)
        meta = jnp.where(lane == TOP_K + k, rank, meta)
        gate = jnp.where(lane == k, es[k] * inv, gate)
    meta_ref[...] = meta.astype(jnp.int32)
    gate_ref[...] = gate
    total = carry_ref[...] + jnp.sum(sel, axis=0, keepdims=True)
    carry_ref[...] = total
    cnt_ref[...] = total


def _router(x1, g_moe, w_router, b_router):
    n = x1.shape[0]
    wr = jnp.zeros((D_MODEL, LANES), F32).at[:, :N_EXPERTS].set(w_router.astype(F32))
    wr_hi = wr.astype(BF16)
    wr_lo = (wr - wr_hi.astype(F32)).astype(BF16)
    br = jnp.zeros((1, LANES), F32).at[0, :N_EXPERTS].set(b_router.astype(F32))
    r = jnp.arange(ROUTER_TM, dtype=jnp.int32)
    tri = jnp.where(r[None, :] < r[:, None], 1.0, 0.0).astype(BF16)
    row = lambda w: pl.BlockSpec((ROUTER_TM, w), lambda i: (i, 0))
    return pl.pallas_call(
        _router_kernel,
        grid=(n // ROUTER_TM,),
        in_specs=[row(D_MODEL),
                  pl.BlockSpec((1, D_MODEL), lambda i: (0, 0)),
                  pl.BlockSpec((2, D_MODEL, LANES), lambda i: (0, 0, 0)),
                  pl.BlockSpec((1, LANES), lambda i: (0, 0)),
                  pl.BlockSpec((ROUTER_TM, ROUTER_TM), lambda i: (0, 0))],
        out_specs=[row(D_MODEL), row(LANES), row(LANES), pl.BlockSpec((8, LANES), lambda i: (0, 0))],
        out_shape=[jax.ShapeDtypeStruct((n, D_MODEL), F32),
                   jax.ShapeDtypeStruct((n, LANES), jnp.int32),
                   jax.ShapeDtypeStruct((n, LANES), F32),
                   jax.ShapeDtypeStruct((8, LANES), F32)],
        scratch_shapes=[pltpu.VMEM((8, LANES), F32)],
        compiler_params=_cparams(("arbitrary",)),
        name="router",
    )(x1, g_moe.reshape(1, D_MODEL), jnp.stack([wr_hi, wr_lo]), br, tri)


def _dispatch(h2, dest, n_slots):
    n = h2.shape[0]
    per_w = n // SC_WORKERS
    nch = per_w // SC_SCATTER_ROWS
    nslab = D_MODEL // SC_SCATTER_COLS
    nj = nch * nslab
    idx = dest.T.reshape(TOP_K, SC_WORKERS, nch, SC_SCATTER_ROWS).transpose(1, 2, 0, 3)
    idx = idx.reshape(SC_WORKERS * nch * TOP_K, SC_SCATTER_ROWS)
    mesh = plsc.VectorSubcoreMesh(core_axis_name="core", subcore_axis_name="subcore")

    @functools.partial(
        pl.kernel, mesh=mesh,
        out_type=jax.ShapeDtypeStruct((n_slots, D_MODEL), h2.dtype),
        scratch_types=[pltpu.VMEM((nch * TOP_K, SC_SCATTER_ROWS), jnp.int32),
                       pltpu.VMEM((2, SC_SCATTER_ROWS, SC_SCATTER_COLS), h2.dtype),
                       pltpu.SemaphoreType.DMA((2,)), pltpu.SemaphoreType.DMA((2,))])
    def scatter_kernel(h2_hbm, idx_hbm, xs_hbm, idx_v, rows_v, sem_l, sem_s):
        wid = lax.axis_index("subcore") * SC_CORES + lax.axis_index("core")
        t0 = wid * per_w
        pltpu.sync_copy(idx_hbm.at[pl.ds(wid * (nch * TOP_K), nch * TOP_K)], idx_v)

        def load(j, b):
            rows = pl.ds(t0 + (j // nslab) * SC_SCATTER_ROWS, SC_SCATTER_ROWS)
            cols = pl.ds((j % nslab) * SC_SCATTER_COLS, SC_SCATTER_COLS)
            return pltpu.make_async_copy(h2_hbm.at[rows, cols], rows_v.at[b], sem_l.at[b])

        def scatter(j, k, b):
            cols = pl.ds((j % nslab) * SC_SCATTER_COLS, SC_SCATTER_COLS)
            return pltpu.make_async_copy(rows_v.at[b], xs_hbm.at[idx_v.at[(j // nslab) * TOP_K + k], cols], sem_s.at[b])

        load(0, 0).start()

        @pl.loop(0, nj, step=2)
        def _(j0):
            for b in (0, 1):
                j = j0 + b

                @pl.when(j >= 1)
                def _():
                    for k in range(TOP_K):
                        scatter(j - 1, k, 1 - b).wait()

                @pl.when(j + 1 < nj)
                def _():
                    load(j + 1, 1 - b).start()

                load(j, b).wait()
                for k in range(TOP_K):
                    scatter(j, k, b).start()

        for k in range(TOP_K):
            scatter(nj - 1, k, 1).wait()

    return scatter_kernel(h2, idx)


def _expert_kernel(be_ref, nu_ref, nv_ref, x_ref, wg_ref, bg_ref, wu_ref, bu_ref, wd_ref, bd_ref, y_ref):
    del be_ref
    i = pl.program_id(0)

    @pl.when(i < nu_ref[0])
    def _():
        row = lax.broadcasted_iota(jnp.int32, (EXPERT_TM, 1), 0)
        xb = jnp.where(row < nv_ref[i], x_ref[...], 0.0).astype(BF16)
        y = jnp.zeros((EXPERT_TM, D_MODEL), F32) + bd_ref[0]
        for c in range(D_MODEL // D_HALF):
            cols = slice(c * D_HALF, (c + 1) * D_HALF)
            glu = _mm(xb, wg_ref[0, :, cols]) + bg_ref[0, :, cols]
            lin = _mm(xb, wu_ref[0, :, cols]) + bu_ref[0, :, cols]
            glu = jnp.minimum(glu, SWIGLU_LIMIT)
            lin = jnp.clip(lin, -SWIGLU_LIMIT, SWIGLU_LIMIT)
            act = glu * (1.0 / (1.0 + jnp.exp(-SWIGLU_ALPHA * glu))) * (lin + 1.0)
            y = y + _mm(act.astype(BF16), wd_ref[0, cols, :])
        y_ref[...] = y


def _experts(xs, block_expert, n_used, n_valid, w_gate, b_gate, w_up, b_up, w_down, b_down):
    n_slots = xs.shape[0]
    n_blocks = n_slots // EXPERT_TM
    rows = lambda i, be, nu, nv: (jnp.minimum(i, nu[0] - 1), 0)
    wspec = pl.BlockSpec((1, D_MODEL, D_MODEL), lambda i, be, nu, nv: (be[i], 0, 0))
    bspec = pl.BlockSpec((1, 1, D_MODEL), lambda i, be, nu, nv: (be[i], 0, 0))
    grid_spec = pltpu.PrefetchScalarGridSpec(
        num_scalar_prefetch=3,
        grid=(n_blocks,),
        in_specs=[pl.BlockSpec((EXPERT_TM, D_MODEL), rows), wspec, bspec, wspec, bspec, wspec, bspec],
        out_specs=pl.BlockSpec((EXPERT_TM, D_MODEL), rows),
    )
    b3 = lambda a: a.astype(F32).reshape(N_EXPERTS, 1, D_MODEL)
    return pl.pallas_call(
        _expert_kernel,
        grid_spec=grid_spec,
        out_shape=jax.ShapeDtypeStruct((n_slots, D_MODEL), F32),
        compiler_params=_cparams(("arbitrary",)),
        name="experts",
    )(block_expert, n_used, n_valid, xs, w_gate.astype(BF16), b3(b_gate), w_up.astype(BF16), b3(b_up),
      w_down.astype(BF16), b3(b_down))


def _sc_row_gather(table, idx):
    n_idx = idx.shape[0]
    width = table.shape[1]
    per_w = n_idx // SC_WORKERS
    nch = per_w // SC_CHUNK
    mesh = plsc.VectorSubcoreMesh(core_axis_name="core", subcore_axis_name="subcore")

    @functools.partial(
        pl.kernel, mesh=mesh,
        out_type=jax.ShapeDtypeStruct((n_idx, width), table.dtype),
        scratch_types=[pltpu.VMEM((per_w,), jnp.int32), pltpu.VMEM((2, SC_CHUNK, width), table.dtype),
                       pltpu.SemaphoreType.DMA((2,)), pltpu.SemaphoreType.DMA((2,))])
    def gather_kernel(table_hbm, idx_hbm, out_hbm, idx_v, rows_v, sem_g, sem_p):
        wid = lax.axis_index("subcore") * SC_CORES + lax.axis_index("core")
        base = wid * per_w
        pltpu.sync_copy(idx_hbm.at[pl.ds(base, per_w)], idx_v)

        def gather(c, b):
            return pltpu.make_async_copy(table_hbm.at[idx_v.at[pl.ds(c * SC_CHUNK, SC_CHUNK)]], rows_v.at[b], sem_g.at[b])

        def put(c, b):
            return pltpu.make_async_copy(rows_v.at[b], out_hbm.at[pl.ds(base + c * SC_CHUNK, SC_CHUNK)], sem_p.at[b])

        gather(0, 0).start()

        @pl.loop(0, nch, step=2)
        def _(c):
            for b in (0, 1):
                cc = c + b

                @pl.when(cc >= 1)
                def _():
                    put(cc - 1, 1 - b).wait()

                @pl.when(cc + 1 < nch)
                def _():
                    gather(cc + 1, 1 - b).start()

                gather(cc, b).wait()
                put(cc, b).start()

        put(nch - 1, 1).wait()

    return gather_kernel(table, idx)


def _combine_kernel(yt_ref, gate_ref, x1_ref, g_ref, o_ref):
    gate = gate_ref[...]
    x = x1_ref[...]
    for k in range(TOP_K):
        x = x + gate[:, k:k + 1] * yt_ref[k]
    o_ref[...] = _rms(x) * g_ref[...]


def _combine(ys, dest, gates, x1, g_final):
    n = x1.shape[0]
    yt = _sc_row_gather(ys, dest.T.reshape(-1)).reshape(TOP_K, n, D_MODEL)
    row = lambda w: pl.BlockSpec((COMBINE_R, w), lambda i: (i, 0))
    return pl.pallas_call(
        _combine_kernel,
        grid=(n // COMBINE_R,),
        in_specs=[pl.BlockSpec((TOP_K, COMBINE_R, D_MODEL), lambda i: (0, i, 0)),
                  row(LANES), row(D_MODEL),
                  pl.BlockSpec((1, D_MODEL), lambda i: (0, 0))],
        out_specs=row(D_MODEL),
        out_shape=jax.ShapeDtypeStruct((n, D_MODEL), F32),
        compiler_params=_cparams(("parallel",)),
        name="combine",
    )(yt, gates, x1, g_final.reshape(1, D_MODEL))


def _slot_layout(meta, counts_f, n):
    eid = meta[:, :TOP_K]
    rank = meta[:, TOP_K:2 * TOP_K]
    counts = counts_f[0, :N_EXPERTS].astype(jnp.int32)
    padded = (counts + EXPERT_TM - 1) // EXPERT_TM * EXPERT_TM
    pend = jnp.cumsum(padded)
    pstart = pend - padded
    experts = jnp.arange(N_EXPERTS, dtype=jnp.int32)
    dest = rank + jnp.sum(jnp.where(eid[..., None] == experts, pstart, 0), axis=-1)
    n_slots = -(-(n * TOP_K + N_EXPERTS * (EXPERT_TM - 1)) // EXPERT_TM) * EXPERT_TM
    block_start = jnp.arange(n_slots // EXPERT_TM, dtype=jnp.int32) * EXPERT_TM
    block_expert = jnp.minimum(jnp.sum(pend[None, :] <= block_start[:, None], axis=-1), N_EXPERTS - 1)
    last_valid = jnp.sum(jnp.where(block_expert[:, None] == experts, pstart + counts, 0), axis=-1)
    n_valid = jnp.clip(last_valid - block_start, 0, EXPERT_TM).astype(jnp.int32)
    n_used = (pend[-1:] // EXPERT_TM).astype(jnp.int32)
    return dest.astype(jnp.int32), block_expert.astype(jnp.int32), n_used, n_valid, n_slots


def kernel(x, g_attn, w_qkv, rel_bias, g_out_dil, g_out_sb, w_o, g_moe, w_router, b_router,
           w_gate, b_gate, w_up, b_up, w_down, b_down, g_final):
    b, s, d = x.shape
    n = b * s
    x2 = x.reshape(n, d)
    qkv, *strided = _qkv(x2, g_attn, w_qkv)
    sources = dict(zip(STRIDED_DILATIONS, strided))
    dil = [_dilated(sources.get(dilation, qkv), b, s, rel_bias, dilation) for _window, dilation in DIL_PATTERNS]
    ob = _stickbreaking(qkv.reshape(b, s, 3 * d))
    x1 = _mix([o for o, _ in dil], [l for _, l in dil], ob, x2, g_out_dil, g_out_sb, w_o)
    h2, meta, gates, counts = _router(x1, g_moe, w_router, b_router)
    dest, block_expert, n_used, n_valid, n_slots = _slot_layout(meta, counts, n)
    xs = _dispatch(h2, dest, n_slots)
    ys = _experts(xs, block_expert, n_used, n_valid, w_gate, b_gate, w_up, b_up, w_down, b_down)
    out = _combine(ys, dest, gates, x1, g_final)
    return out.reshape(b, s, d)
```

```python
import functools
import math

import jax
import jax.numpy as jnp
from jax import lax
from jax.experimental import pallas as pl
from jax.experimental.pallas import tpu as pltpu
from jax.experimental.pallas import tpu_sc as plsc

F32 = jnp.float32
BF16 = jnp.bfloat16

D_MODEL = 1024
HEAD_DIM = 64
D_HALF = 512
N_HEADS = 8
LANES = 128
N_PAIRS = D_HALF // LANES
DIL_PATTERNS = ((128, 1), (512, 4), (2048, 16))
BAND = 128
REL_BUCKETS = 32
REL_MAX_DISTANCE = 2048
N_EXPERTS = 32
TOP_K = 4
SWIGLU_ALPHA = 1.702
SWIGLU_LIMIT = 7.0
RMS_EPS = 1e-6
NEG_INF = -1e30

QKV_TM = 512
DIL_G = 4
A_COLS = 3 * D_HALF
STRIDED_DILATIONS = tuple(d for _w, d in DIL_PATTERNS if d > 1)
LOG2E = math.log2(math.e)
SB_T = 256
SB_G = 4
SB_EXP_ZERO = 104.0
MIX_TM = 256
ROUTER_TM = 512
DISPATCH_R = 256
EXPERT_TM = 512
COMBINE_R = 256
SC_CORES = 2
SC_WORKERS = SC_CORES * 16
SC_CHUNK = 32
SC_SCATTER_ROWS = 128
SC_SCATTER_COLS = 256
DMA_UNROLL = 8
VMEM_LIMIT = 56 * 1024 * 1024


def _cparams(sem):
    return pltpu.CompilerParams(dimension_semantics=sem, vmem_limit_bytes=VMEM_LIMIT)


def _mm(a, b):
    return jnp.dot(a, b, preferred_element_type=F32)


def _mm_nt(a, b):
    return lax.dot_general(a, b, (((1,), (1,)), ((), ())), preferred_element_type=F32)


def _split_bf16(a):
    hi = a.astype(BF16)
    lo = (a - hi.astype(F32)).astype(BF16)
    return hi, lo


def _mm_split(a, b_bf16):
    hi, lo = _split_bf16(a)
    return _mm(hi, b_bf16) + _mm(lo, b_bf16)


def _rms(x):
    return x * lax.rsqrt(jnp.mean(x * x, axis=-1, keepdims=True) + RMS_EPS)


def _half_masks(dtype):
    lane = lax.broadcasted_iota(jnp.int32, (1, LANES), 1)
    lo = jnp.where(lane < HEAD_DIM, 1.0, 0.0).astype(dtype)
    hi = jnp.where(lane >= HEAD_DIM, 1.0, 0.0).astype(dtype)
    return lo, hi


def _qkv_kernel(x_ref, g_ref, w_ref, o_ref, *rest):
    stage_ref = rest[-1]
    h = (_rms(x_ref[...]) * g_ref[...]).astype(BF16)
    for c in range(3 * D_MODEL // D_HALF):
        y = _mm(h, w_ref[:, c * D_HALF:(c + 1) * D_HALF])
        if c == 0:
            y = y * (1.0 / math.sqrt(HEAD_DIM))
        if c == 3:
            y = y * (LOG2E / math.sqrt(HEAD_DIM))
        o_ref[:, c * D_HALF:(c + 1) * D_HALF] = y.astype(BF16)
        if c < 3:
            for p in range(N_PAIRS):
                stage_ref[c * N_PAIRS + p] = y[:, p * LANES:(p + 1) * LANES]
    for od_ref, d in zip(rest[:-1], STRIDED_DILATIONS):
        for r in range(d):
            for ch in range(A_COLS // LANES):
                col = r * A_COLS + ch * LANES
                od_ref[:, col:col + LANES] = stage_ref[ch, pl.ds(r, QKV_TM // d, stride=d), :].astype(BF16)


def _qkv(x2, g_attn, w_qkv):
    n = x2.shape[0]
    strided = [(n // d, d * A_COLS) for d in STRIDED_DILATIONS]
    return pl.pallas_call(
        _qkv_kernel,
        grid=(n // QKV_TM,),
        in_specs=[pl.BlockSpec((QKV_TM, D_MODEL), lambda i: (i, 0)),
                  pl.BlockSpec((1, D_MODEL), lambda i: (0, 0)),
                  pl.BlockSpec((D_MODEL, 3 * D_MODEL), lambda i: (0, 0))],
        out_specs=[pl.BlockSpec((QKV_TM, 3 * D_MODEL), lambda i: (i, 0))]
                  + [pl.BlockSpec((QKV_TM // d, d * A_COLS), lambda i: (i, 0)) for d in STRIDED_DILATIONS],
        out_shape=[jax.ShapeDtypeStruct((n, 3 * D_MODEL), BF16)]
                  + [jax.ShapeDtypeStruct(shape, BF16) for shape in strided],
        scratch_shapes=[pltpu.VMEM((A_COLS // LANES, QKV_TM, LANES), F32)],
        compiler_params=_cparams(("parallel",)),
        name="qkv",
    )(x2, g_attn.reshape(1, D_MODEL), w_qkv.astype(BF16))


def _t5_bucket(dist):
    max_exact = REL_BUCKETS // 2
    d_f = jnp.maximum(dist, 1).astype(jnp.float32)
    large = max_exact + (jnp.log(d_f / max_exact)
                         / math.log(REL_MAX_DISTANCE / max_exact)
                         * (REL_BUCKETS - max_exact)).astype(jnp.int32)
    large = jnp.minimum(large, REL_BUCKETS - 1)
    return jnp.where(dist < max_exact, dist, large)


def _band_bias(rel_bias, dilation):
    qi = jnp.arange(BAND, dtype=jnp.int32)[:, None]
    kj = jnp.arange(2 * BAND, dtype=jnp.int32)[None, :]
    rel = qi + BAND - kj
    ok = (rel >= 0) & (rel <= BAND)
    bucket = _t5_bucket(jnp.maximum(rel, 0) * dilation)
    table = rel_bias.astype(F32)
    bias = jnp.zeros((N_HEADS, BAND, 2 * BAND), F32)
    for bk in range(REL_BUCKETS):
        bias = jnp.where(bucket[None] == bk, table[bk][:, None, None], bias)
    return jnp.where(ok[None], bias, NEG_INF)


def _dilated_kernel(q_ref, kp_ref, kc_ref, vp_ref, vc_ref, bias_ref, o_ref, lse_ref):
    n = pl.program_id(2)
    kj = lax.broadcasted_iota(jnp.int32, (BAND, 2 * BAND), 1)
    has_prev = jnp.logical_or(n > 0, kj >= BAND)
    lane = lax.broadcasted_iota(jnp.int32, (BAND, LANES), 1)
    m_lo, m_hi = _half_masks(BF16)
    for g in range(DIL_G):
        rows = slice(g * BAND, (g + 1) * BAND)
        prev = slice((g - 1) * BAND, g * BAND)
        lse_tile = jnp.zeros((BAND, LANES), F32)
        for p in range(N_PAIRS):
            cols = slice(p * LANES, (p + 1) * LANES)
            q2 = q_ref[0, rows, cols]
            k_prev = kp_ref[0, :, cols] if g == 0 else kc_ref[0, prev, cols]
            v_prev = vp_ref[0, :, cols] if g == 0 else vc_ref[0, prev, cols]
            kcat = jnp.concatenate([k_prev, kc_ref[0, rows, cols]], axis=0)
            vcat = jnp.concatenate([v_prev, vc_ref[0, rows, cols]], axis=0)
            outs = []
            for s, msk in enumerate((m_lo, m_hi)):
                h = 2 * p + s
                logits = _mm_nt(q2 * msk, kcat) + bias_ref[h]
                if g == 0:
                    logits = jnp.where(has_prev, logits, NEG_INF)
                m = jnp.max(logits, axis=-1, keepdims=True)
                pr = jnp.exp(logits - m)
                den = jnp.sum(pr, axis=-1, keepdims=True)
                outs.append(_mm(pr.astype(BF16), vcat) / den)
                lse_tile = jnp.where(lane == h, m + jnp.log(den), lse_tile)
            o_ref[0, rows, cols] = jnp.where(lane < HEAD_DIM, outs[0], outs[1])
        lse_ref[0, rows, :] = lse_tile


def _dilated(src, b, s, rel_bias, dilation):
    d = dilation
    l = s // d
    nstep = l // (DIL_G * BAND)
    view = src.reshape(b, l, src.shape[1])
    n_sec = src.shape[1] // d // D_HALF

    def sec(k, prev):
        if prev:
            return pl.BlockSpec((1, BAND, D_HALF),
                                lambda bi, r, n: (bi, jnp.maximum(n * DIL_G - 1, 0), r * n_sec + k))
        return pl.BlockSpec((1, DIL_G * BAND, D_HALF), lambda bi, r, n: (bi, n, r * n_sec + k))

    o, lse = pl.pallas_call(
        _dilated_kernel,
        grid=(b, d, nstep),
        in_specs=[sec(0, False), sec(1, True), sec(1, False), sec(2, True), sec(2, False),
                  pl.BlockSpec((N_HEADS, BAND, 2 * BAND), lambda bi, r, n: (0, 0, 0))],
        out_specs=[pl.BlockSpec((1, DIL_G * BAND, D_HALF), lambda bi, r, n: (bi, n, r)),
                   pl.BlockSpec((1, DIL_G * BAND, LANES), lambda bi, r, n: (bi, n, r))],
        out_shape=[jax.ShapeDtypeStruct((b, l, d * D_HALF), F32),
                   jax.ShapeDtypeStruct((b, l, d * LANES), F32)],
        compiler_params=_cparams(("parallel", "parallel", "arbitrary")),
        name=f"dilated_d{d}",
    )(view, view, view, view, view, _band_bias(rel_bias, d))
    return o.reshape(b * l, d * D_HALF), lse.reshape(b * l, d * LANES)


def _sb_kernel(q_ref, k_ref, v_ref, u_ref, o_ref, acc_ref, carry_ref):
    i = pl.program_id(2)
    m_lo, m_hi = _half_masks(BF16)
    row = lax.broadcasted_iota(jnp.int32, (2 * SB_T, SB_T), 0) & (SB_T - 1)
    col = lax.broadcasted_iota(jnp.int32, (2 * SB_T, SB_T), 1)
    causal = col < row
    lane = lax.broadcasted_iota(jnp.int32, (SB_T, LANES), 1)
    u = u_ref[...]

    def tile(qs, j, mask, carry):
        start = pl.multiple_of(j * SB_T, SB_T)
        kj = k_ref[0, pl.ds(start, SB_T), :]
        vj = v_ref[0, pl.ds(start, SB_T), :]
        z = _mm_nt(qs, kj)
        neg_abs = lax.bitcast_convert_type(
            lax.bitcast_convert_type(z, jnp.uint32) | jnp.uint32(0x80000000), F32)
        sp = jnp.log2(1.0 + jnp.exp2(neg_abs))
        log_b = jnp.minimum(z, 0.0) - sp
        l1m = log_b - z
        if mask is not None:
            l1m = jnp.where(mask, l1m, 0.0)
        cum = _mm_split(l1m, u)
        total = cum[:, 0:1] + l1m[:, 0:1]
        a = jnp.exp2(log_b + cum + carry)
        if mask is not None:
            a = jnp.where(mask, a, 0.0)
        return _mm(a.astype(BF16), vj), total

    def more(state):
        j, top = state
        return jnp.logical_and(j >= 0, top > -SB_EXP_ZERO * LOG2E)

    tops = []
    for g in range(SB_G):
        blk = i * SB_G + g
        q2 = q_ref[0, g * SB_T:(g + 1) * SB_T, :]
        qs = jnp.concatenate([q2 * m_lo, q2 * m_hi], axis=0)
        pv_d, tot_d = tile(qs, blk, causal, 0.0)
        if g == 0:
            pv_p, tot_p = tile(qs, jnp.maximum(blk - 1, 0), blk > 0, tot_d)
        else:
            pv_p, tot_p = tile(qs, blk - 1, None, tot_d)
        acc_ref[g] = pv_d + pv_p
        carry0 = tot_d + tot_p
        carry_ref[g] = carry0
        tops.append(jnp.max(carry0))

    for g in range(SB_G):
        def body(state, g=g):
            j, _ = state
            q2 = q_ref[0, g * SB_T:(g + 1) * SB_T, :]
            qs = jnp.concatenate([q2 * m_lo, q2 * m_hi], axis=0)
            pv, tot = tile(qs, j, None, carry_ref[g])
            acc_ref[g] += pv
            carry = carry_ref[g] + tot
            carry_ref[g] = carry
            return j - 1, jnp.max(carry)

        lax.while_loop(more, body, (i * SB_G + g - 2, tops[g]))
        o_ref[0, g * SB_T:(g + 1) * SB_T, :] = jnp.where(lane < HEAD_DIM, acc_ref[g, :SB_T], acc_ref[g, SB_T:])


def _suffix_matrix():
    sp = jnp.arange(SB_T, dtype=jnp.int32)[:, None]
    sc = jnp.arange(SB_T, dtype=jnp.int32)[None, :]
    return jnp.where(sp > sc, 1.0, 0.0).astype(BF16)


def _stickbreaking(qkv3):
    b, s, _ = qkv3.shape
    sec_q, sec_k, sec_v = 3 * N_PAIRS, 4 * N_PAIRS, 5 * N_PAIRS
    o = pl.pallas_call(
        _sb_kernel,
        grid=(b, N_PAIRS, s // (SB_G * SB_T)),
        in_specs=[pl.BlockSpec((1, SB_G * SB_T, LANES), lambda bi, p, i: (bi, i, sec_q + p)),
                  pl.BlockSpec((1, s, LANES), lambda bi, p, i: (bi, 0, sec_k + p)),
                  pl.BlockSpec((1, s, LANES), lambda bi, p, i: (bi, 0, sec_v + p)),
                  pl.BlockSpec((SB_T, SB_T), lambda bi, p, i: (0, 0))],
        out_specs=pl.BlockSpec((1, SB_G * SB_T, LANES), lambda bi, p, i: (bi, i, p)),
        out_shape=jax.ShapeDtypeStruct((b, s, D_HALF), F32),
        scratch_shapes=[pltpu.VMEM((SB_G, 2 * SB_T, LANES), F32), pltpu.VMEM((SB_G, 2 * SB_T, 1), F32)],
        compiler_params=_cparams(("parallel", "parallel", "arbitrary")),
        name="stickbreaking",
    )(qkv3, qkv3, qkv3, _suffix_matrix())
    return o.reshape(b * s, D_HALF)


def _token_major(ref, d, st_ref):
    if d == 1:
        return ref[...]
    width = ref.shape[1] // d
    for r in range(d):
        for ch in range(width // LANES):
            col = r * width + ch * LANES
            st_ref[ch, pl.ds(r, MIX_TM // d, stride=d), :] = ref[:, col:col + LANES]
    return jnp.concatenate([st_ref[ch] for ch in range(width // LANES)], axis=1)


def _mix_kernel(o1_ref, o2_ref, o3_ref, l1_ref, l2_ref, l3_ref, ob_ref, x_ref,
                ga_ref, gb_ref, e_ref, wo_ref, x1_ref, *scratch):
    dils = [d for _w, d in DIL_PATTERNS]
    o_st = dict(zip(STRIDED_DILATIONS, scratch[:len(STRIDED_DILATIONS)]))
    l_st = dict(zip(STRIDED_DILATIONS, scratch[len(STRIDED_DILATIONS):]))
    o1, o2, o3 = [_token_major(r, d, o_st.get(d)) for r, d in zip((o1_ref, o2_ref, o3_ref), dils)]
    l1, l2, l3 = [_token_major(r, d, l_st.get(d)) for r, d in zip((l1_ref, l2_ref, l3_ref), dils)]
    m = jnp.maximum(jnp.maximum(l1, l2), l3)
    e1, e2, e3 = jnp.exp(l1 - m), jnp.exp(l2 - m), jnp.exp(l3 - m)
    inv = 1.0 / (e1 + e2 + e3)
    expand = e_ref[...]
    oa = (_mm_split(e1 * inv, expand) * o1
          + _mm_split(e2 * inv, expand) * o2
          + _mm_split(e3 * inv, expand) * o3)
    oa = (_rms(oa) * ga_ref[...]).astype(BF16)
    ob = (_rms(ob_ref[...]) * gb_ref[...]).astype(BF16)
    mix = _mm(oa, wo_ref[:D_HALF, :]) + _mm(ob, wo_ref[D_HALF:, :])
    x1_ref[...] = x_ref[...] + mix


def _head_expand():
    lane = jnp.arange(LANES, dtype=jnp.int32)[:, None]
    col = jnp.arange(D_HALF, dtype=jnp.int32)[None, :]
    return jnp.where(col // HEAD_DIM == lane, 1.0, 0.0).astype(BF16)


def _mix(os_, lses, ob, x2, g_out_dil, g_out_sb, w_o):
    n = x2.shape[0]
    row = lambda w: pl.BlockSpec((MIX_TM, w), lambda i: (i, 0))
    srow = lambda a: pl.BlockSpec((MIX_TM * a.shape[0] // n, a.shape[1]), lambda i: (i, 0))
    const = lambda shape: pl.BlockSpec(shape, lambda i: (0, 0))
    return pl.pallas_call(
        _mix_kernel,
        grid=(n // MIX_TM,),
        in_specs=[srow(a) for a in os_] + [srow(a) for a in lses] + [row(D_HALF), row(D_MODEL),
                  const((1, D_HALF)), const((1, D_HALF)), const((LANES, D_HALF)), const((D_MODEL, D_MODEL))],
        out_specs=row(D_MODEL),
        out_shape=jax.ShapeDtypeStruct((n, D_MODEL), F32),
        scratch_shapes=[pltpu.VMEM((D_HALF // LANES, MIX_TM, LANES), F32) for _ in STRIDED_DILATIONS]
                       + [pltpu.VMEM((1, MIX_TM, LANES), F32) for _ in STRIDED_DILATIONS],
        compiler_params=_cparams(("parallel",)),
        name="mix_wo",
    )(*os_, *lses, ob, x2, g_out_dil.reshape(1, D_HALF), g_out_sb.reshape(1, D_HALF),
      _head_expand(), w_o.astype(BF16))


def _router_kernel(x1_ref, g_ref, wr_ref, br_ref, tri_ref, h2_ref, meta_ref, gate_ref, cnt_ref, carry_ref):
    i = pl.program_id(0)

    @pl.when(i == 0)
    def _():
        carry_ref[...] = jnp.zeros_like(carry_ref)

    h2 = _rms(x1_ref[...]) * g_ref[...]
    h2_ref[...] = h2
    h_hi, h_lo = _split_bf16(h2)
    w_hi, w_lo = wr_ref[0], wr_ref[1]
    logits = _mm(h_hi, w_hi) + (_mm(h_hi, w_lo) + _mm(h_lo, w_hi)) + br_ref[...]
    lane = lax.broadcasted_iota(jnp.int32, (ROUTER_TM, LANES), 1)
    lane_f = lane.astype(F32)
    logits = jnp.where(lane < N_EXPERTS, logits, -jnp.inf)

    onehots, vals, ids = [], [], []
    for _k in range(TOP_K):
        m = jnp.max(logits, axis=-1, keepdims=True)
        idx = jnp.min(jnp.where(logits == m, lane_f, float(LANES)), axis=-1, keepdims=True)
        oh = lane_f == idx
        logits = jnp.where(oh, -jnp.inf, logits)
        onehots.append(oh)
        vals.append(m)
        ids.append(idx)

    es = [jnp.exp(v - vals[0]) for v in vals]
    inv = 1.0 / (es[0] + es[1] + es[2] + es[3])

    sel = jnp.zeros((ROUTER_TM, LANES), F32)
    for oh in onehots:
        sel = jnp.where(oh, 1.0, sel)
    before = _mm(tri_ref[...], sel.astype(BF16)) + carry_ref[0:1, :]
    meta = jnp.zeros((ROUTER_TM, LANES), F32)
    gate = jnp.zeros((ROUTER_TM, LANES), F32)
    for k in range(TOP_K):
        rank = jnp.sum(jnp.where(onehots[k], before, 0.0), axis=-1, keepdims=True)
        meta = jnp.where(lane == k, ids[k], meta)
        meta = jnp.where(lane == TOP_K + k, rank, meta)
        gate = jnp.where(lane == k, es[k] * inv, gate)
    meta_ref[...] = meta.astype(jnp.int32)
    gate_ref[...] = gate
    total = carry_ref[...] + jnp.sum(sel, axis=0, keepdims=True)
    carry_ref[...] = total
    cnt_ref[...] = total


def _router(x1, g_moe, w_router, b_router):
    n = x1.shape[0]
    wr = jnp.zeros((D_MODEL, LANES), F32).at[:, :N_EXPERTS].set(w_router.astype(F32))
    wr_hi = wr.astype(BF16)
    wr_lo = (wr - wr_hi.astype(F32)).astype(BF16)
    br = jnp.zeros((1, LANES), F32).at[0, :N_EXPERTS].set(b_router.astype(F32))
    r = jnp.arange(ROUTER_TM, dtype=jnp.int32)
    tri = jnp.where(r[None, :] < r[:, None], 1.0, 0.0).astype(BF16)
    row = lambda w: pl.BlockSpec((ROUTER_TM, w), lambda i: (i, 0))
    return pl.pallas_call(
        _router_kernel,
        grid=(n // ROUTER_TM,),
        in_specs=[row(D_MODEL),
                  pl.BlockSpec((1, D_MODEL), lambda i: (0, 0)),
                  pl.BlockSpec((2, D_MODEL, LANES), lambda i: (0, 0, 0)),
                  pl.BlockSpec((1, LANES), lambda i: (0, 0)),
                  pl.BlockSpec((ROUTER_TM, ROUTER_TM), lambda i: (0, 0))],
        out_specs=[row(D_MODEL), row(LANES), row(LANES), pl.BlockSpec((8, LANES), lambda i: (0, 0))],
        out_shape=[jax.ShapeDtypeStruct((n, D_MODEL), F32),
                   jax.ShapeDtypeStruct((n, LANES), jnp.int32),
                   jax.ShapeDtypeStruct((n, LANES), F32),
                   jax.ShapeDtypeStruct((8, LANES), F32)],
        scratch_shapes=[pltpu.VMEM((8, LANES), F32)],
        compiler_params=_cparams(("arbitrary",)),
        name="router",
    )(x1, g_moe.reshape(1, D_MODEL), jnp.stack([wr_hi, wr_lo]), br, tri)


def _dispatch(h2, dest, n_slots):
    n = h2.shape[0]
    per_w = n // SC_WORKERS
    nch = per_w // SC_SCATTER_ROWS
    nslab = D_MODEL // SC_SCATTER_COLS
    nj = nch * nslab
    idx = dest.T.reshape(TOP_K, SC_WORKERS, nch, SC_SCATTER_ROWS).transpose(1, 2, 0, 3)
    idx = idx.reshape(SC_WORKERS * nch * TOP_K, SC_SCATTER_ROWS)
    mesh = plsc.VectorSubcoreMesh(core_axis_name="core", subcore_axis_name="subcore")

    @functools.partial(
        pl.kernel, mesh=mesh,
        out_type=jax.ShapeDtypeStruct((n_slots, D_MODEL), h2.dtype),
        scratch_types=[pltpu.VMEM((nch * TOP_K, SC_SCATTER_ROWS), jnp.int32),
                       pltpu.VMEM((2, SC_SCATTER_ROWS, SC_SCATTER_COLS), h2.dtype),
                       pltpu.SemaphoreType.DMA((2,)), pltpu.SemaphoreType.DMA((2,))])
    def scatter_kernel(h2_hbm, idx_hbm, xs_hbm, idx_v, rows_v, sem_l, sem_s):
        wid = lax.axis_index("subcore") * SC_CORES + lax.axis_index("core")
        t0 = wid * per_w
        pltpu.sync_copy(idx_hbm.at[pl.ds(wid * (nch * TOP_K), nch * TOP_K)], idx_v)

        def load(j, b):
            rows = pl.ds(t0 + (j // nslab) * SC_SCATTER_ROWS, SC_SCATTER_ROWS)
            cols = pl.ds((j % nslab) * SC_SCATTER_COLS, SC_SCATTER_COLS)
            return pltpu.make_async_copy(h2_hbm.at[rows, cols], rows_v.at[b], sem_l.at[b])

        def scatter(j, k, b):
            cols = pl.ds((j % nslab) * SC_SCATTER_COLS, SC_SCATTER_COLS)
            return pltpu.make_async_copy(rows_v.at[b], xs_hbm.at[idx_v.at[(j // nslab) * TOP_K + k], cols], sem_s.at[b])

        load(0, 0).start()

        @pl.loop(0, nj, step=2)
        def _(j0):
            for b in (0, 1):
                j = j0 + b

                @pl.when(j >= 1)
                def _():
                    for k in range(TOP_K):
                        scatter(j - 1, k, 1 - b).wait()

                @pl.when(j + 1 < nj)
                def _():
                    load(j + 1, 1 - b).start()

                load(j, b).wait()
                for k in range(TOP_K):
                    scatter(j, k, b).start()

        for k in range(TOP_K):
            scatter(nj - 1, k, 1).wait()

    return scatter_kernel(h2, idx)


def _expert_kernel(be_ref, nu_ref, nv_ref, x_ref, wg_ref, bg_ref, wu_ref, bu_ref, wd_ref, bd_ref, y_ref):
    del be_ref
    i = pl.program_id(0)

    @pl.when(i < nu_ref[0])
    def _():
        row = lax.broadcasted_iota(jnp.int32, (EXPERT_TM, 1), 0)
        xb = jnp.where(row < nv_ref[i], x_ref[...], 0.0).astype(BF16)
        y = jnp.zeros((EXPERT_TM, D_MODEL), F32) + bd_ref[0]
        for c in range(D_MODEL // D_HALF):
            cols = slice(c * D_HALF, (c + 1) * D_HALF)
            glu = _mm(xb, wg_ref[0, :, cols].astype(BF16)) + bg_ref[0, :, cols]
            lin = _mm(xb, wu_ref[0, :, cols].astype(BF16)) + bu_ref[0, :, cols]
            glu = jnp.minimum(glu, SWIGLU_LIMIT)
            lin = jnp.clip(lin, -SWIGLU_LIMIT, SWIGLU_LIMIT)
            act = glu * (1.0 / (1.0 + jnp.exp(-SWIGLU_ALPHA * glu))) * (lin + 1.0)
            y = y + _mm(act.astype(BF16), wd_ref[0, cols, :].astype(BF16))
        y_ref[...] = y


def _experts(xs, block_expert, n_used, n_valid, w_gate, b_gate, w_up, b_up, w_down, b_down):
    n_slots = xs.shape[0]
    n_blocks = n_slots // EXPERT_TM
    rows = lambda i, be, nu, nv: (jnp.minimum(i, nu[0] - 1), 0)
    wspec = pl.BlockSpec((1, D_MODEL, D_MODEL), lambda i, be, nu, nv: (be[i], 0, 0))
    bspec = pl.BlockSpec((1, 1, D_MODEL), lambda i, be, nu, nv: (be[i], 0, 0))
    grid_spec = pltpu.PrefetchScalarGridSpec(
        num_scalar_prefetch=3,
        grid=(n_blocks,),
        in_specs=[pl.BlockSpec((EXPERT_TM, D_MODEL), rows), wspec, bspec, wspec, bspec, wspec, bspec],
        out_specs=pl.BlockSpec((EXPERT_TM, D_MODEL), rows),
    )
    b3 = lambda a: a.astype(F32).reshape(N_EXPERTS, 1, D_MODEL)
    return pl.pallas_call(
        _expert_kernel,
        grid_spec=grid_spec,
        out_shape=jax.ShapeDtypeStruct((n_slots, D_MODEL), F32),
        compiler_params=_cparams(("arbitrary",)),
        name="experts",
    )(block_expert, n_used, n_valid, xs, w_gate, b3(b_gate), w_up, b3(b_up), w_down, b3(b_down))


def _sc_row_gather(table, idx):
    n_idx = idx.shape[0]
    width = table.shape[1]
    per_w = n_idx // SC_WORKERS
    nch = per_w // SC_CHUNK
    mesh = plsc.VectorSubcoreMesh(core_axis_name="core", subcore_axis_name="subcore")

    @functools.partial(
        pl.kernel, mesh=mesh,
        out_type=jax.ShapeDtypeStruct((n_idx, width), table.dtype),
        scratch_types=[pltpu.VMEM((per_w,), jnp.int32), pltpu.VMEM((2, SC_CHUNK, width), table.dtype),
                       pltpu.SemaphoreType.DMA((2,)), pltpu.SemaphoreType.DMA((2,))])
    def gather_kernel(table_hbm, idx_hbm, out_hbm, idx_v, rows_v, sem_g, sem_p):
        wid = lax.axis_index("subcore") * SC_CORES + lax.axis_index("core")
        base = wid * per_w
        pltpu.sync_copy(idx_hbm.at[pl.ds(base, per_w)], idx_v)

        def gather(c, b):
            return pltpu.make_async_copy(table_hbm.at[idx_v.at[pl.ds(c * SC_CHUNK, SC_CHUNK)]], rows_v.at[b], sem_g.at[b])

        def put(c, b):
            return pltpu.make_async_copy(rows_v.at[b], out_hbm.at[pl.ds(base + c * SC_CHUNK, SC_CHUNK)], sem_p.at[b])

        gather(0, 0).start()

        @pl.loop(0, nch, step=2)
        def _(c):
            for b in (0, 1):
                cc = c + b

                @pl.when(cc >= 1)
                def _():
                    put(cc - 1, 1 - b).wait()

                @pl.when(cc + 1 < nch)
                def _():
                    gather(cc + 1, 1 - b).start()

                gather(cc, b).wait()
                put(cc, b).start()

        put(nch - 1, 1).wait()

    return gather_kernel(table, idx)


def _combine_kernel(yt_ref, gate_ref, x1_ref, g_ref, o_ref):
    gate = gate_ref[...]
    x = x1_ref[...]
    for k in range(TOP_K):
        x = x + gate[:, k:k + 1] * yt_ref[k]
    o_ref[...] = _rms(x) * g_ref[...]


def _combine(ys, dest, gates, x1, g_final):
    n = x1.shape[0]
    yt = _sc_row_gather(ys, dest.T.reshape(-1)).reshape(TOP_K, n, D_MODEL)
    row = lambda w: pl.BlockSpec((COMBINE_R, w), lambda i: (i, 0))
    return pl.pallas_call(
        _combine_kernel,
        grid=(n // COMBINE_R,),
        in_specs=[pl.BlockSpec((TOP_K, COMBINE_R, D_MODEL), lambda i: (0, i, 0)),
                  row(LANES), row(D_MODEL),
                  pl.BlockSpec((1, D_MODEL), lambda i: (0, 0))],
        out_specs=row(D_MODEL),
        out_shape=jax.ShapeDtypeStruct((n, D_MODEL), F32),
        compiler_params=_cparams(("parallel",)),
        name="combine",
    )(yt, gates, x1, g_final.reshape(1, D_MODEL))


def _slot_layout(meta, counts_f, n):
    eid = meta[:, :TOP_K]
    rank = meta[:, TOP_K:2 * TOP_K]
    counts = counts_f[0, :N_EXPERTS].astype(jnp.int32)
    padded = (counts + EXPERT_TM - 1) // EXPERT_TM * EXPERT_TM
    pend = jnp.cumsum(padded)
    pstart = pend - padded
    experts = jnp.arange(N_EXPERTS, dtype=jnp.int32)
    dest = rank + jnp.sum(jnp.where(eid[..., None] == experts, pstart, 0), axis=-1)
    n_slots = -(-(n * TOP_K + N_EXPERTS * (EXPERT_TM - 1)) // EXPERT_TM) * EXPERT_TM
    block_start = jnp.arange(n_slots // EXPERT_TM, dtype=jnp.int32) * EXPERT_TM
    block_expert = jnp.minimum(jnp.sum(pend[None, :] <= block_start[:, None], axis=-1), N_EXPERTS - 1)
    last_valid = jnp.sum(jnp.where(block_expert[:, None] == experts, pstart + counts, 0), axis=-1)
    n_valid = jnp.clip(last_valid - block_start, 0, EXPERT_TM).astype(jnp.int32)
    n_used = (pend[-1:] // EXPERT_TM).astype(jnp.int32)
    return dest.astype(jnp.int32), block_expert.astype(jnp.int32), n_used, n_valid, n_slots


def kernel(x, g_attn, w_qkv, rel_bias, g_out_dil, g_out_sb, w_o, g_moe, w_router, b_router,
           w_gate, b_gate, w_up, b_up, w_down, b_down, g_final):
    b, s, d = x.shape
    n = b * s
    x2 = x.reshape(n, d)
    qkv, *strided = _qkv(x2, g_attn, w_qkv)
    sources = dict(zip(STRIDED_DILATIONS, strided))
    dil = [_dilated(sources.get(dilation, qkv), b, s, rel_bias, dilation) for _window, dilation in DIL_PATTERNS]
    ob = _stickbreaking(qkv.reshape(b, s, 3 * d))
    x1 = _mix([o for o, _ in dil], [l for _, l in dil], ob, x2, g_out_dil, g_out_sb, w_o)
    h2, meta, gates, counts = _router(x1, g_moe, w_router, b_router)
    dest, block_expert, n_used, n_valid, n_slots = _slot_layout(meta, counts, n)
    xs = _dispatch(h2, dest, n_slots)
    ys = _experts(xs, block_expert, n_used, n_valid, w_gate, b_gate, w_up, b_up, w_down, b_down)
    out = _combine(ys, dest, gates, x1, g_final)
    return out.reshape(b, s, d)
```

```python
import functools
import math

import jax
import jax.numpy as jnp
from jax import lax
from jax.experimental import pallas as pl
from jax.experimental.pallas import tpu as pltpu
from jax.experimental.pallas import tpu_sc as plsc

F32 = jnp.float32
BF16 = jnp.bfloat16

D_MODEL = 1024
HEAD_DIM = 64
D_HALF = 512
N_HEADS = 8
LANES = 128
N_PAIRS = D_HALF // LANES
DIL_PATTERNS = ((128, 1), (512, 4), (2048, 16))
BAND = 128
REL_BUCKETS = 32
REL_MAX_DISTANCE = 2048
N_EXPERTS = 32
TOP_K = 4
SWIGLU_ALPHA = 1.702
SWIGLU_LIMIT = 7.0
RMS_EPS = 1e-6
NEG_INF = -1e30

QKV_TM = 512
N_GROUPS = 2
DIL_G = 4
A_COLS = 3 * D_HALF
STRIDED_DILATIONS = tuple(d for _w, d in DIL_PATTERNS if d > 1)
LOG2E = math.log2(math.e)
SB_T = 256
SB_G = 4
SB_EXP_ZERO = 104.0
MIX_TM = 256
ROUTER_TM = 512
DISPATCH_R = 256
EXPERT_TM = 512
COMBINE_R = 256
SC_CORES = 2
SC_WORKERS = SC_CORES * 16
SC_CHUNK = 32
SC_SCATTER_ROWS = 128
SC_SCATTER_COLS = 256
DMA_UNROLL = 8
VMEM_LIMIT = 56 * 1024 * 1024


def _cparams(sem):
    return pltpu.CompilerParams(dimension_semantics=sem, vmem_limit_bytes=VMEM_LIMIT)


def _mm(a, b):
    return jnp.dot(a, b, preferred_element_type=F32)


def _mm_nt(a, b):
    return lax.dot_general(a, b, (((1,), (1,)), ((), ())), preferred_element_type=F32)


def _split_bf16(a):
    hi = a.astype(BF16)
    lo = (a - hi.astype(F32)).astype(BF16)
    return hi, lo


def _mm_split(a, b_bf16):
    hi, lo = _split_bf16(a)
    return _mm(hi, b_bf16) + _mm(lo, b_bf16)


def _rms(x):
    return x * lax.rsqrt(jnp.mean(x * x, axis=-1, keepdims=True) + RMS_EPS)


def _half_masks(dtype):
    lane = lax.broadcasted_iota(jnp.int32, (1, LANES), 1)
    lo = jnp.where(lane < HEAD_DIM, 1.0, 0.0).astype(dtype)
    hi = jnp.where(lane >= HEAD_DIM, 1.0, 0.0).astype(dtype)
    return lo, hi


def _qkv_kernel(x_ref, g_ref, w_ref, o_ref, *rest):
    stage_ref = rest[-1]
    h = (_rms(x_ref[...]) * g_ref[...]).astype(BF16)
    for c in range(3 * D_MODEL // D_HALF):
        y = _mm(h, w_ref[:, c * D_HALF:(c + 1) * D_HALF])
        if c == 0:
            y = y * (1.0 / math.sqrt(HEAD_DIM))
        if c == 3:
            y = y * (LOG2E / math.sqrt(HEAD_DIM))
        o_ref[:, c * D_HALF:(c + 1) * D_HALF] = y.astype(BF16)
        if c < 3:
            for p in range(N_PAIRS):
                stage_ref[c * N_PAIRS + p] = y[:, p * LANES:(p + 1) * LANES]
    for od_ref, d in zip(rest[:-1], STRIDED_DILATIONS):
        for r in range(d):
            for ch in range(A_COLS // LANES):
                col = r * A_COLS + ch * LANES
                od_ref[:, col:col + LANES] = stage_ref[ch, pl.ds(r, QKV_TM // d, stride=d), :].astype(BF16)


def _qkv(x2, g_attn, w_qkv):
    n = x2.shape[0]
    strided = [(n // d, d * A_COLS) for d in STRIDED_DILATIONS]
    return pl.pallas_call(
        _qkv_kernel,
        grid=(n // QKV_TM,),
        in_specs=[pl.BlockSpec((QKV_TM, D_MODEL), lambda i: (i, 0)),
                  pl.BlockSpec((1, D_MODEL), lambda i: (0, 0)),
                  pl.BlockSpec((D_MODEL, 3 * D_MODEL), lambda i: (0, 0))],
        out_specs=[pl.BlockSpec((QKV_TM, 3 * D_MODEL), lambda i: (i, 0))]
                  + [pl.BlockSpec((QKV_TM // d, d * A_COLS), lambda i: (i, 0)) for d in STRIDED_DILATIONS],
        out_shape=[jax.ShapeDtypeStruct((n, 3 * D_MODEL), BF16)]
                  + [jax.ShapeDtypeStruct(shape, BF16) for shape in strided],
        scratch_shapes=[pltpu.VMEM((A_COLS // LANES, QKV_TM, LANES), F32)],
        compiler_params=_cparams(("parallel",)),
        name="qkv",
    )(x2, g_attn.reshape(1, D_MODEL), w_qkv.astype(BF16))


def _t5_bucket(dist):
    max_exact = REL_BUCKETS // 2
    d_f = jnp.maximum(dist, 1).astype(jnp.float32)
    large = max_exact + (jnp.log(d_f / max_exact)
                         / math.log(REL_MAX_DISTANCE / max_exact)
                         * (REL_BUCKETS - max_exact)).astype(jnp.int32)
    large = jnp.minimum(large, REL_BUCKETS - 1)
    return jnp.where(dist < max_exact, dist, large)


def _band_bias(rel_bias, dilation):
    qi = jnp.arange(BAND, dtype=jnp.int32)[:, None]
    kj = jnp.arange(2 * BAND, dtype=jnp.int32)[None, :]
    rel = qi + BAND - kj
    ok = (rel >= 0) & (rel <= BAND)
    bucket = _t5_bucket(jnp.maximum(rel, 0) * dilation)
    table = rel_bias.astype(F32)
    bias = jnp.zeros((N_HEADS, BAND, 2 * BAND), F32)
    for bk in range(REL_BUCKETS):
        bias = jnp.where(bucket[None] == bk, table[bk][:, None, None], bias)
    return jnp.where(ok[None], bias, NEG_INF)


def _dilated_kernel(q_ref, kp_ref, kc_ref, vp_ref, vc_ref, bias_ref, o_ref, lse_ref):
    n = pl.program_id(2)
    kj = lax.broadcasted_iota(jnp.int32, (BAND, 2 * BAND), 1)
    has_prev = jnp.logical_or(n > 0, kj >= BAND)
    lane = lax.broadcasted_iota(jnp.int32, (BAND, LANES), 1)
    m_lo, m_hi = _half_masks(BF16)
    for g in range(DIL_G):
        rows = slice(g * BAND, (g + 1) * BAND)
        prev = slice((g - 1) * BAND, g * BAND)
        lse_tile = jnp.zeros((BAND, LANES), F32)
        for p in range(N_PAIRS):
            cols = slice(p * LANES, (p + 1) * LANES)
            q2 = q_ref[0, rows, cols]
            k_prev = kp_ref[0, :, cols] if g == 0 else kc_ref[0, prev, cols]
            v_prev = vp_ref[0, :, cols] if g == 0 else vc_ref[0, prev, cols]
            kcat = jnp.concatenate([k_prev, kc_ref[0, rows, cols]], axis=0)
            vcat = jnp.concatenate([v_prev, vc_ref[0, rows, cols]], axis=0)
            outs = []
            for s, msk in enumerate((m_lo, m_hi)):
                h = 2 * p + s
                logits = _mm_nt(q2 * msk, kcat) + bias_ref[h]
                if g == 0:
                    logits = jnp.where(has_prev, logits, NEG_INF)
                m = jnp.max(logits, axis=-1, keepdims=True)
                pr = jnp.exp(logits - m)
                den = jnp.sum(pr, axis=-1, keepdims=True)
                outs.append(_mm(pr.astype(BF16), vcat) / den)
                lse_tile = jnp.where(lane == h, m + jnp.log(den), lse_tile)
            o_ref[0, rows, cols] = jnp.where(lane < HEAD_DIM, outs[0], outs[1])
        lse_ref[0, rows, :] = lse_tile


def _dilated(src, b, s, rel_bias, dilation):
    d = dilation
    l = s // d
    nstep = l // (DIL_G * BAND)
    view = src.reshape(b, l, src.shape[1])
    n_sec = src.shape[1] // d // D_HALF

    def sec(k, prev):
        if prev:
            return pl.BlockSpec((1, BAND, D_HALF),
                                lambda bi, r, n: (bi, jnp.maximum(n * DIL_G - 1, 0), r * n_sec + k))
        return pl.BlockSpec((1, DIL_G * BAND, D_HALF), lambda bi, r, n: (bi, n, r * n_sec + k))

    o, lse = pl.pallas_call(
        _dilated_kernel,
        grid=(b, d, nstep),
        in_specs=[sec(0, False), sec(1, True), sec(1, False), sec(2, True), sec(2, False),
                  pl.BlockSpec((N_HEADS, BAND, 2 * BAND), lambda bi, r, n: (0, 0, 0))],
        out_specs=[pl.BlockSpec((1, DIL_G * BAND, D_HALF), lambda bi, r, n: (bi, n, r)),
                   pl.BlockSpec((1, DIL_G * BAND, LANES), lambda bi, r, n: (bi, n, r))],
        out_shape=[jax.ShapeDtypeStruct((b, l, d * D_HALF), F32),
                   jax.ShapeDtypeStruct((b, l, d * LANES), F32)],
        compiler_params=_cparams(("parallel", "parallel", "arbitrary")),
        name=f"dilated_d{d}",
    )(view, view, view, view, view, _band_bias(rel_bias, d))
    return o.reshape(b * l, d * D_HALF), lse.reshape(b * l, d * LANES)


def _sb_kernel(q_ref, k_ref, v_ref, u_ref, o_ref, acc_ref, carry_ref):
    i = pl.program_id(2)
    m_lo, m_hi = _half_masks(BF16)
    row = lax.broadcasted_iota(jnp.int32, (2 * SB_T, SB_T), 0) & (SB_T - 1)
    col = lax.broadcasted_iota(jnp.int32, (2 * SB_T, SB_T), 1)
    causal = col < row
    lane = lax.broadcasted_iota(jnp.int32, (SB_T, LANES), 1)
    u = u_ref[...]

    def tile(qs, j, mask, carry):
        start = pl.multiple_of(j * SB_T, SB_T)
        kj = k_ref[0, pl.ds(start, SB_T), :]
        vj = v_ref[0, pl.ds(start, SB_T), :]
        z = _mm_nt(qs, kj)
        neg_abs = lax.bitcast_convert_type(
            lax.bitcast_convert_type(z, jnp.uint32) | jnp.uint32(0x80000000), F32)
        sp = jnp.log2(1.0 + jnp.exp2(neg_abs))
        log_b = jnp.minimum(z, 0.0) - sp
        l1m = log_b - z
        if mask is not None:
            l1m = jnp.where(mask, l1m, 0.0)
        cum = _mm_split(l1m, u)
        total = cum[:, 0:1] + l1m[:, 0:1]
        a = jnp.exp2(log_b + cum + carry)
        if mask is not None:
            a = jnp.where(mask, a, 0.0)
        return _mm(a.astype(BF16), vj), total

    def more(state):
        j, top = state
        return jnp.logical_and(j >= 0, top > -SB_EXP_ZERO * LOG2E)

    tops = []
    for g in range(SB_G):
        blk = i * SB_G + g
        q2 = q_ref[0, g * SB_T:(g + 1) * SB_T, :]
        qs = jnp.concatenate([q2 * m_lo, q2 * m_hi], axis=0)
        pv_d, tot_d = tile(qs, blk, causal, 0.0)
        if g == 0:
            pv_p, tot_p = tile(qs, jnp.maximum(blk - 1, 0), blk > 0, tot_d)
        else:
            pv_p, tot_p = tile(qs, blk - 1, None, tot_d)
        acc_ref[g] = pv_d + pv_p
        carry0 = tot_d + tot_p
        carry_ref[g] = carry0
        tops.append(jnp.max(carry0))

    for g in range(SB_G):
        def body(state, g=g):
            j, _ = state
            q2 = q_ref[0, g * SB_T:(g + 1) * SB_T, :]
            qs = jnp.concatenate([q2 * m_lo, q2 * m_hi], axis=0)
            pv, tot = tile(qs, j, None, carry_ref[g])
            acc_ref[g] += pv
            carry = carry_ref[g] + tot
            carry_ref[g] = carry
            return j - 1, jnp.max(carry)

        lax.while_loop(more, body, (i * SB_G + g - 2, tops[g]))
        o_ref[0, g * SB_T:(g + 1) * SB_T, :] = jnp.where(lane < HEAD_DIM, acc_ref[g, :SB_T], acc_ref[g, SB_T:])


def _suffix_matrix():
    sp = jnp.arange(SB_T, dtype=jnp.int32)[:, None]
    sc = jnp.arange(SB_T, dtype=jnp.int32)[None, :]
    return jnp.where(sp > sc, 1.0, 0.0).astype(BF16)


def _stickbreaking(qkv3):
    b, s, _ = qkv3.shape
    sec_q, sec_k, sec_v = 3 * N_PAIRS, 4 * N_PAIRS, 5 * N_PAIRS
    o = pl.pallas_call(
        _sb_kernel,
        grid=(b, N_PAIRS, s // (SB_G * SB_T)),
        in_specs=[pl.BlockSpec((1, SB_G * SB_T, LANES), lambda bi, p, i: (bi, i, sec_q + p)),
                  pl.BlockSpec((1, s, LANES), lambda bi, p, i: (bi, 0, sec_k + p)),
                  pl.BlockSpec((1, s, LANES), lambda bi, p, i: (bi, 0, sec_v + p)),
                  pl.BlockSpec((SB_T, SB_T), lambda bi, p, i: (0, 0))],
        out_specs=pl.BlockSpec((1, SB_G * SB_T, LANES), lambda bi, p, i: (bi, i, p)),
        out_shape=jax.ShapeDtypeStruct((b, s, D_HALF), F32),
        scratch_shapes=[pltpu.VMEM((SB_G, 2 * SB_T, LANES), F32), pltpu.VMEM((SB_G, 2 * SB_T, 1), F32)],
        compiler_params=_cparams(("parallel", "parallel", "arbitrary")),
        name="stickbreaking",
    )(qkv3, qkv3, qkv3, _suffix_matrix())
    return o.reshape(b * s, D_HALF)


def _token_major(ref, d, st_ref):
    if d == 1:
        return ref[...]
    width = ref.shape[1] // d
    for r in range(d):
        for ch in range(width // LANES):
            col = r * width + ch * LANES
            st_ref[ch, pl.ds(r, MIX_TM // d, stride=d), :] = ref[:, col:col + LANES]
    return jnp.concatenate([st_ref[ch] for ch in range(width // LANES)], axis=1)


def _mix_kernel(o1_ref, o2_ref, o3_ref, l1_ref, l2_ref, l3_ref, ob_ref, x_ref,
                ga_ref, gb_ref, e_ref, wo_ref, x1_ref, *scratch):
    dils = [d for _w, d in DIL_PATTERNS]
    o_st = dict(zip(STRIDED_DILATIONS, scratch[:len(STRIDED_DILATIONS)]))
    l_st = dict(zip(STRIDED_DILATIONS, scratch[len(STRIDED_DILATIONS):]))
    o1, o2, o3 = [_token_major(r, d, o_st.get(d)) for r, d in zip((o1_ref, o2_ref, o3_ref), dils)]
    l1, l2, l3 = [_token_major(r, d, l_st.get(d)) for r, d in zip((l1_ref, l2_ref, l3_ref), dils)]
    m = jnp.maximum(jnp.maximum(l1, l2), l3)
    e1, e2, e3 = jnp.exp(l1 - m), jnp.exp(l2 - m), jnp.exp(l3 - m)
    inv = 1.0 / (e1 + e2 + e3)
    expand = e_ref[...]
    oa = (_mm_split(e1 * inv, expand) * o1
          + _mm_split(e2 * inv, expand) * o2
          + _mm_split(e3 * inv, expand) * o3)
    oa = (_rms(oa) * ga_ref[...]).astype(BF16)
    ob = (_rms(ob_ref[...]) * gb_ref[...]).astype(BF16)
    mix = _mm(oa, wo_ref[:D_HALF, :]) + _mm(ob, wo_ref[D_HALF:, :])
    x1_ref[...] = x_ref[...] + mix


def _head_expand():
    lane = jnp.arange(LANES, dtype=jnp.int32)[:, None]
    col = jnp.arange(D_HALF, dtype=jnp.int32)[None, :]
    return jnp.where(col // HEAD_DIM == lane, 1.0, 0.0).astype(BF16)


def _mix(os_, lses, ob, x2, g_out_dil, g_out_sb, w_o):
    n = x2.shape[0]
    row = lambda w: pl.BlockSpec((MIX_TM, w), lambda i: (i, 0))
    srow = lambda a: pl.BlockSpec((MIX_TM * a.shape[0] // n, a.shape[1]), lambda i: (i, 0))
    const = lambda shape: pl.BlockSpec(shape, lambda i: (0, 0))
    return pl.pallas_call(
        _mix_kernel,
        grid=(n // MIX_TM,),
        in_specs=[srow(a) for a in os_] + [srow(a) for a in lses] + [row(D_HALF), row(D_MODEL),
                  const((1, D_HALF)), const((1, D_HALF)), const((LANES, D_HALF)), const((D_MODEL, D_MODEL))],
        out_specs=row(D_MODEL),
        out_shape=jax.ShapeDtypeStruct((n, D_MODEL), F32),
        scratch_shapes=[pltpu.VMEM((D_HALF // LANES, MIX_TM, LANES), F32) for _ in STRIDED_DILATIONS]
                       + [pltpu.VMEM((1, MIX_TM, LANES), F32) for _ in STRIDED_DILATIONS],
        compiler_params=_cparams(("parallel",)),
        name="mix_wo",
    )(*os_, *lses, ob, x2, g_out_dil.reshape(1, D_HALF), g_out_sb.reshape(1, D_HALF),
      _head_expand(), w_o.astype(BF16))


def _router_kernel(x1_ref, g_ref, wr_ref, br_ref, tri_ref, h2_ref, meta_ref, gate_ref, cnt_ref, carry_ref):
    i = pl.program_id(0)

    @pl.when(i == 0)
    def _():
        carry_ref[...] = jnp.zeros_like(carry_ref)

    h2 = _rms(x1_ref[...]) * g_ref[...]
    h2_ref[...] = h2
    h_hi, h_lo = _split_bf16(h2)
    w_hi, w_lo = wr_ref[0], wr_ref[1]
    logits = _mm(h_hi, w_hi) + (_mm(h_hi, w_lo) + _mm(h_lo, w_hi)) + br_ref[...]
    lane = lax.broadcasted_iota(jnp.int32, (ROUTER_TM, LANES), 1)
    lane_f = lane.astype(F32)
    logits = jnp.where(lane < N_EXPERTS, logits, -jnp.inf)

    onehots, vals, ids = [], [], []
    for _k in range(TOP_K):
        m = jnp.max(logits, axis=-1, keepdims=True)
        idx = jnp.min(jnp.where(logits == m, lane_f, float(LANES)), axis=-1, keepdims=True)
        oh = lane_f == idx
        logits = jnp.where(oh, -jnp.inf, logits)
        onehots.append(oh)
        vals.append(m)
        ids.append(idx)

    es = [jnp.exp(v - vals[0]) for v in vals]
    inv = 1.0 / (es[0] + es[1] + es[2] + es[3])

    sel = jnp.zeros((ROUTER_TM, LANES), F32)
    for oh in onehots:
        sel = jnp.where(oh, 1.0, sel)
    before = _mm(tri_ref[...], sel.astype(BF16)) + carry_ref[0:1, :]
    meta = jnp.zeros((ROUTER_TM, LANES), F32)
    gate = jnp.zeros((ROUTER_TM, LANES), F32)
    for k in range(TOP_K):
        rank = jnp.sum(jnp.where(onehots[k], before, 0.0), axis=-1, keepdims=True)
        meta = jnp.where(lane == k, ids[k], meta)
        meta = jnp.where(lane == TOP_K + k, rank, meta)
        gate = jnp.where(lane == k, es[k] * inv, gate)
    meta_ref[...] = meta.astype(jnp.int32)
    gate_ref[...] = gate
    total = carry_ref[...] + jnp.sum(sel, axis=0, keepdims=True)
    carry_ref[...] = total
    cnt_ref[...] = total


def _router(x1, g_moe, w_router, b_router):
    n = x1.shape[0]
    wr = jnp.zeros((D_MODEL, LANES), F32).at[:, :N_EXPERTS].set(w_router.astype(F32))
    wr_hi = wr.astype(BF16)
    wr_lo = (wr - wr_hi.astype(F32)).astype(BF16)
    br = jnp.zeros((1, LANES), F32).at[0, :N_EXPERTS].set(b_router.astype(F32))
    r = jnp.arange(ROUTER_TM, dtype=jnp.int32)
    tri = jnp.where(r[None, :] < r[:, None], 1.0, 0.0).astype(BF16)
    row = lambda w: pl.BlockSpec((ROUTER_TM, w), lambda i: (i, 0))
    return pl.pallas_call(
        _router_kernel,
        grid=(n // ROUTER_TM,),
        in_specs=[row(D_MODEL),
                  pl.BlockSpec((1, D_MODEL), lambda i: (0, 0)),
                  pl.BlockSpec((2, D_MODEL, LANES), lambda i: (0, 0, 0)),
                  pl.BlockSpec((1, LANES), lambda i: (0, 0)),
                  pl.BlockSpec((ROUTER_TM, ROUTER_TM), lambda i: (0, 0))],
        out_specs=[row(D_MODEL), row(LANES), row(LANES), pl.BlockSpec((8, LANES), lambda i: (0, 0))],
        out_shape=[jax.ShapeDtypeStruct((n, D_MODEL), F32),
                   jax.ShapeDtypeStruct((n, LANES), jnp.int32),
                   jax.ShapeDtypeStruct((n, LANES), F32),
                   jax.ShapeDtypeStruct((8, LANES), F32)],
        scratch_shapes=[pltpu.VMEM((8, LANES), F32)],
        compiler_params=_cparams(("arbitrary",)),
        name="router",
    )(x1, g_moe.reshape(1, D_MODEL), jnp.stack([wr_hi, wr_lo]), br, tri)


def _dispatch(h2, dest, n_slots):
    n = h2.shape[0]
    per_w = n // SC_WORKERS
    nch = per_w // SC_SCATTER_ROWS
    nslab = D_MODEL // SC_SCATTER_COLS
    nj = nch * nslab
    idx = dest.T.reshape(TOP_K, SC_WORKERS, nch, SC_SCATTER_ROWS).transpose(1, 2, 0, 3)
    idx = idx.reshape(SC_WORKERS * nch * TOP_K, SC_SCATTER_ROWS)
    mesh = plsc.VectorSubcoreMesh(core_axis_name="core", subcore_axis_name="subcore")

    @functools.partial(
        pl.kernel, mesh=mesh,
        out_type=jax.ShapeDtypeStruct((n_slots, D_MODEL), h2.dtype),
        scratch_types=[pltpu.VMEM((nch * TOP_K, SC_SCATTER_ROWS), jnp.int32),
                       pltpu.VMEM((2, SC_SCATTER_ROWS, SC_SCATTER_COLS), h2.dtype),
                       pltpu.SemaphoreType.DMA((2,)), pltpu.SemaphoreType.DMA((2,))])
    def scatter_kernel(h2_hbm, idx_hbm, xs_hbm, idx_v, rows_v, sem_l, sem_s):
        wid = lax.axis_index("subcore") * SC_CORES + lax.axis_index("core")
        t0 = wid * per_w
        pltpu.sync_copy(idx_hbm.at[pl.ds(wid * (nch * TOP_K), nch * TOP_K)], idx_v)

        def load(j, b):
            rows = pl.ds(t0 + (j // nslab) * SC_SCATTER_ROWS, SC_SCATTER_ROWS)
            cols = pl.ds((j % nslab) * SC_SCATTER_COLS, SC_SCATTER_COLS)
            return pltpu.make_async_copy(h2_hbm.at[rows, cols], rows_v.at[b], sem_l.at[b])

        def scatter(j, k, b):
            cols = pl.ds((j % nslab) * SC_SCATTER_COLS, SC_SCATTER_COLS)
            return pltpu.make_async_copy(rows_v.at[b], xs_hbm.at[idx_v.at[(j // nslab) * TOP_K + k], cols], sem_s.at[b])

        load(0, 0).start()

        @pl.loop(0, nj, step=2)
        def _(j0):
            for b in (0, 1):
                j = j0 + b

                @pl.when(j >= 1)
                def _():
                    for k in range(TOP_K):
                        scatter(j - 1, k, 1 - b).wait()

                @pl.when(j + 1 < nj)
                def _():
                    load(j + 1, 1 - b).start()

                load(j, b).wait()
                for k in range(TOP_K):
                    scatter(j, k, b).start()

        for k in range(TOP_K):
            scatter(nj - 1, k, 1).wait()

    return scatter_kernel(h2, idx)


def _expert_kernel(be_ref, nu_ref, nv_ref, x_ref, wg_ref, bg_ref, wu_ref, bu_ref, wd_ref, bd_ref, y_ref):
    del be_ref
    i = pl.program_id(0)

    @pl.when(i < nu_ref[0])
    def _():
        row = lax.broadcasted_iota(jnp.int32, (EXPERT_TM, 1), 0)
        xb = jnp.where(row < nv_ref[i], x_ref[...], 0.0).astype(BF16)
        y = jnp.zeros((EXPERT_TM, D_MODEL), F32) + bd_ref[0]
        for c in range(D_MODEL // D_HALF):
            cols = slice(c * D_HALF, (c + 1) * D_HALF)
            glu = _mm(xb, wg_ref[0, :, cols].astype(BF16)) + bg_ref[0, :, cols]
            lin = _mm(xb, wu_ref[0, :, cols].astype(BF16)) + bu_ref[0, :, cols]
            glu = jnp.minimum(glu, SWIGLU_LIMIT)
            lin = jnp.clip(lin, -SWIGLU_LIMIT, SWIGLU_LIMIT)
            act = glu * (1.0 / (1.0 + jnp.exp(-SWIGLU_ALPHA * glu))) * (lin + 1.0)
            y = y + _mm(act.astype(BF16), wd_ref[0, cols, :].astype(BF16))
        y_ref[...] = y


def _experts(xs, block_expert, n_used, n_valid, w_gate, b_gate, w_up, b_up, w_down, b_down):
    n_slots = xs.shape[0]
    n_blocks = n_slots // EXPERT_TM
    rows = lambda i, be, nu, nv: (jnp.minimum(i, nu[0] - 1), 0)
    wspec = pl.BlockSpec((1, D_MODEL, D_MODEL), lambda i, be, nu, nv: (be[i], 0, 0))
    bspec = pl.BlockSpec((1, 1, D_MODEL), lambda i, be, nu, nv: (be[i], 0, 0))
    grid_spec = pltpu.PrefetchScalarGridSpec(
        num_scalar_prefetch=3,
        grid=(n_blocks,),
        in_specs=[pl.BlockSpec((EXPERT_TM, D_MODEL), rows), wspec, bspec, wspec, bspec, wspec, bspec],
        out_specs=pl.BlockSpec((EXPERT_TM, D_MODEL), rows),
    )
    b3 = lambda a: a.astype(F32).reshape(N_EXPERTS, 1, D_MODEL)
    return pl.pallas_call(
        _expert_kernel,
        grid_spec=grid_spec,
        out_shape=jax.ShapeDtypeStruct((n_slots, D_MODEL), F32),
        compiler_params=_cparams(("arbitrary",)),
        name="experts",
    )(block_expert, n_used, n_valid, xs, w_gate, b3(b_gate), w_up, b3(b_up), w_down, b3(b_down))


def _sc_row_gather(table, idx):
    n_idx = idx.shape[0]
    width = table.shape[1]
    per_w = n_idx // SC_WORKERS
    nch = per_w // SC_CHUNK
    mesh = plsc.VectorSubcoreMesh(core_axis_name="core", subcore_axis_name="subcore")

    @functools.partial(
        pl.kernel, mesh=mesh,
        out_type=jax.ShapeDtypeStruct((n_idx, width), table.dtype),
        scratch_types=[pltpu.VMEM((per_w,), jnp.int32), pltpu.VMEM((2, SC_CHUNK, width), table.dtype),
                       pltpu.SemaphoreType.DMA((2,)), pltpu.SemaphoreType.DMA((2,))])
    def gather_kernel(table_hbm, idx_hbm, out_hbm, idx_v, rows_v, sem_g, sem_p):
        wid = lax.axis_index("subcore") * SC_CORES + lax.axis_index("core")
        base = wid * per_w
        pltpu.sync_copy(idx_hbm.at[pl.ds(base, per_w)], idx_v)

        def gather(c, b):
            return pltpu.make_async_copy(table_hbm.at[idx_v.at[pl.ds(c * SC_CHUNK, SC_CHUNK)]], rows_v.at[b], sem_g.at[b])

        def put(c, b):
            return pltpu.make_async_copy(rows_v.at[b], out_hbm.at[pl.ds(base + c * SC_CHUNK, SC_CHUNK)], sem_p.at[b])

        gather(0, 0).start()

        @pl.loop(0, nch, step=2)
        def _(c):
            for b in (0, 1):
                cc = c + b

                @pl.when(cc >= 1)
                def _():
                    put(cc - 1, 1 - b).wait()

                @pl.when(cc + 1 < nch)
                def _():
                    gather(cc + 1, 1 - b).start()

                gather(cc, b).wait()
                put(cc, b).start()

        put(nch - 1, 1).wait()

    return gather_kernel(table, idx)


def _combine_kernel(yt_ref, gate_ref, x1_ref, g_ref, o_ref):
    gate = gate_ref[...]
    x = x1_ref[...]
    for k in range(TOP_K):
        x = x + gate[:, k:k + 1] * yt_ref[k]
    o_ref[...] = _rms(x) * g_ref[...]


def _combine(yt, gates, x1, g_final):
    n = x1.shape[0]
    yt = yt.reshape(TOP_K, n, D_MODEL)
    row = lambda w: pl.BlockSpec((COMBINE_R, w), lambda i: (i, 0))
    return pl.pallas_call(
        _combine_kernel,
        grid=(n // COMBINE_R,),
        in_specs=[pl.BlockSpec((TOP_K, COMBINE_R, D_MODEL), lambda i: (0, i, 0)),
                  row(LANES), row(D_MODEL),
                  pl.BlockSpec((1, D_MODEL), lambda i: (0, 0))],
        out_specs=row(D_MODEL),
        out_shape=jax.ShapeDtypeStruct((n, D_MODEL), F32),
        compiler_params=_cparams(("parallel",)),
        name="combine",
    )(yt, gates, x1, g_final.reshape(1, D_MODEL))


def _slot_layout(meta, counts_f, n):
    eid = meta[:, :TOP_K]
    rank = meta[:, TOP_K:2 * TOP_K]
    counts = counts_f[0, :N_EXPERTS].astype(jnp.int32)
    padded = (counts + EXPERT_TM - 1) // EXPERT_TM * EXPERT_TM
    pend = jnp.cumsum(padded)
    pstart = pend - padded
    experts = jnp.arange(N_EXPERTS, dtype=jnp.int32)
    dest = rank + jnp.sum(jnp.where(eid[..., None] == experts, pstart, 0), axis=-1)
    n_slots = -(-(n * TOP_K + N_EXPERTS * (EXPERT_TM - 1)) // EXPERT_TM) * EXPERT_TM
    block_start = jnp.arange(n_slots // EXPERT_TM, dtype=jnp.int32) * EXPERT_TM
    block_expert = jnp.minimum(jnp.sum(pend[None, :] <= block_start[:, None], axis=-1), N_EXPERTS - 1)
    last_valid = jnp.sum(jnp.where(block_expert[:, None] == experts, pstart + counts, 0), axis=-1)
    n_valid = jnp.clip(last_valid - block_start, 0, EXPERT_TM).astype(jnp.int32)
    n_used = (pend[-1:] // EXPERT_TM).astype(jnp.int32)
    return dest.astype(jnp.int32), block_expert.astype(jnp.int32), n_used, n_valid, n_slots


def kernel(x, g_attn, w_qkv, rel_bias, g_out_dil, g_out_sb, w_o, g_moe, w_router, b_router,
           w_gate, b_gate, w_up, b_up, w_down, b_down, g_final):
    b, s, d = x.shape

    def mixer_and_route(xg):
        bg = xg.shape[0]
        n = bg * s
        x2 = xg.reshape(n, d)
        qkv, *strided = _qkv(x2, g_attn, w_qkv)
        sources = dict(zip(STRIDED_DILATIONS, strided))
        dil = [_dilated(sources.get(dilation, qkv), bg, s, rel_bias, dilation) for _window, dilation in DIL_PATTERNS]
        ob = _stickbreaking(qkv.reshape(bg, s, 3 * d))
        x1 = _mix([o for o, _ in dil], [l for _, l in dil], ob, x2, g_out_dil, g_out_sb, w_o)
        h2, meta, gates, counts = _router(x1, g_moe, w_router, b_router)
        return x1, h2, gates, _slot_layout(meta, counts, n)

    groups = [x[g * (b // N_GROUPS):(g + 1) * (b // N_GROUPS)] for g in range(N_GROUPS)]
    routed, slots = [], []
    for xg in groups:
        x1, h2, gates, (dest, block_expert, n_used, n_valid, n_slots) = mixer_and_route(xg)
        routed.append((x1, gates, dest, block_expert, n_used, n_valid))
        slots.append(_dispatch(h2, dest, n_slots))
    ys = [_experts(xs, be, nu, nv, w_gate, b_gate, w_up, b_up, w_down, b_down)
          for xs, (_x1, _g, _d, be, nu, nv) in zip(slots, routed)]
    yts = [_sc_row_gather(y, dest.T.reshape(-1)) for y, (_x1, _g, dest, *_r) in zip(ys, routed)]
    outs = [_combine(yt, gates, x1, g_final) for yt, (x1, gates, *_r) in zip(yts, routed)]
    return jnp.concatenate(outs, axis=0).reshape(b, s, d)
```

```python
import functools
import math

import jax
import jax.numpy as jnp
from jax import lax
from jax.experimental import pallas as pl
from jax.experimental.pallas import tpu as pltpu
from jax.experimental.pallas import tpu_sc as plsc

F32 = jnp.float32
BF16 = jnp.bfloat16

D_MODEL = 1024
HEAD_DIM = 64
D_HALF = 512
N_HEADS = 8
LANES = 128
N_PAIRS = D_HALF // LANES
DIL_PATTERNS = ((128, 1), (512, 4), (2048, 16))
BAND = 128
REL_BUCKETS = 32
REL_MAX_DISTANCE = 2048
N_EXPERTS = 32
TOP_K = 4
SWIGLU_ALPHA = 1.702
SWIGLU_LIMIT = 7.0
RMS_EPS = 1e-6
NEG_INF = -1e30

QKV_TM = 512
DIL_G = 4
A_COLS = 3 * D_HALF
STRIDED_DILATIONS = tuple(d for _w, d in DIL_PATTERNS if d > 1)
LOG2E = math.log2(math.e)
SB_T = 256
SB_G = 4
SB_EXP_ZERO = 104.0
MIX_TM = 256
ROUTER_TM = 512
DISPATCH_R = 256
EXPERT_TM = 512
COMBINE_R = 256
SC_CORES = 2
SC_WORKERS = SC_CORES * 16
SC_CHUNK = 64
SC_SCATTER_ROWS = 128
SC_SCATTER_COLS = 256
DMA_UNROLL = 8
VMEM_LIMIT = 56 * 1024 * 1024


def _cparams(sem):
    return pltpu.CompilerParams(dimension_semantics=sem, vmem_limit_bytes=VMEM_LIMIT)


def _mm(a, b):
    return jnp.dot(a, b, preferred_element_type=F32)


def _mm_nt(a, b):
    return lax.dot_general(a, b, (((1,), (1,)), ((), ())), preferred_element_type=F32)


def _split_bf16(a):
    hi = a.astype(BF16)
    lo = (a - hi.astype(F32)).astype(BF16)
    return hi, lo


def _mm_split(a, b_bf16):
    hi, lo = _split_bf16(a)
    return _mm(hi, b_bf16) + _mm(lo, b_bf16)


def _pack_bf16_halves(x):
    c = x.shape[1] // 2
    return pltpu.pack_elementwise([x[:, :c], x[:, c:]], packed_dtype=BF16)


def _unpack_bf16_halves(p):
    return jnp.concatenate([pltpu.unpack_elementwise(p, index=i, packed_dtype=BF16, unpacked_dtype=F32)
                            for i in (0, 1)], axis=1)


def _rms(x):
    return x * lax.rsqrt(jnp.mean(x * x, axis=-1, keepdims=True) + RMS_EPS)


def _half_masks(dtype):
    lane = lax.broadcasted_iota(jnp.int32, (1, LANES), 1)
    lo = jnp.where(lane < HEAD_DIM, 1.0, 0.0).astype(dtype)
    hi = jnp.where(lane >= HEAD_DIM, 1.0, 0.0).astype(dtype)
    return lo, hi


def _qkv_kernel(x_ref, g_ref, w_ref, o_ref, *rest):
    stage_ref = rest[-1]
    h = (_rms(x_ref[...]) * g_ref[...]).astype(BF16)
    for c in range(3 * D_MODEL // D_HALF):
        y = _mm(h, w_ref[:, c * D_HALF:(c + 1) * D_HALF])
        if c == 0:
            y = y * (1.0 / math.sqrt(HEAD_DIM))
        if c == 3:
            y = y * (LOG2E / math.sqrt(HEAD_DIM))
        o_ref[:, c * D_HALF:(c + 1) * D_HALF] = y.astype(BF16)
        if c < 3:
            for p in range(N_PAIRS):
                stage_ref[c * N_PAIRS + p] = y[:, p * LANES:(p + 1) * LANES]
    for od_ref, d in zip(rest[:-1], STRIDED_DILATIONS):
        for r in range(d):
            for ch in range(A_COLS // LANES):
                col = r * A_COLS + ch * LANES
                od_ref[:, col:col + LANES] = stage_ref[ch, pl.ds(r, QKV_TM // d, stride=d), :].astype(BF16)


def _qkv(x2, g_attn, w_qkv):
    n = x2.shape[0]
    strided = [(n // d, d * A_COLS) for d in STRIDED_DILATIONS]
    return pl.pallas_call(
        _qkv_kernel,
        grid=(n // QKV_TM,),
        in_specs=[pl.BlockSpec((QKV_TM, D_MODEL), lambda i: (i, 0)),
                  pl.BlockSpec((1, D_MODEL), lambda i: (0, 0)),
                  pl.BlockSpec((D_MODEL, 3 * D_MODEL), lambda i: (0, 0))],
        out_specs=[pl.BlockSpec((QKV_TM, 3 * D_MODEL), lambda i: (i, 0))]
                  + [pl.BlockSpec((QKV_TM // d, d * A_COLS), lambda i: (i, 0)) for d in STRIDED_DILATIONS],
        out_shape=[jax.ShapeDtypeStruct((n, 3 * D_MODEL), BF16)]
                  + [jax.ShapeDtypeStruct(shape, BF16) for shape in strided],
        scratch_shapes=[pltpu.VMEM((A_COLS // LANES, QKV_TM, LANES), F32)],
        compiler_params=_cparams(("parallel",)),
        name="qkv",
    )(x2, g_attn.reshape(1, D_MODEL), w_qkv.astype(BF16))


def _t5_bucket(dist):
    max_exact = REL_BUCKETS // 2
    d_f = jnp.maximum(dist, 1).astype(jnp.float32)
    large = max_exact + (jnp.log(d_f / max_exact)
                         / math.log(REL_MAX_DISTANCE / max_exact)
                         * (REL_BUCKETS - max_exact)).astype(jnp.int32)
    large = jnp.minimum(large, REL_BUCKETS - 1)
    return jnp.where(dist < max_exact, dist, large)


def _band_bias(rel_bias, dilation):
    qi = jnp.arange(BAND, dtype=jnp.int32)[:, None]
    kj = jnp.arange(2 * BAND, dtype=jnp.int32)[None, :]
    rel = qi + BAND - kj
    ok = (rel >= 0) & (rel <= BAND)
    bucket = _t5_bucket(jnp.maximum(rel, 0) * dilation)
    table = rel_bias.astype(F32)
    bias = jnp.zeros((N_HEADS, BAND, 2 * BAND), F32)
    for bk in range(REL_BUCKETS):
        bias = jnp.where(bucket[None] == bk, table[bk][:, None, None], bias)
    return jnp.where(ok[None], bias, NEG_INF)


def _dilated_kernel(q_ref, kp_ref, kc_ref, vp_ref, vc_ref, bias_ref, o_ref, lse_ref):
    n = pl.program_id(2)
    kj = lax.broadcasted_iota(jnp.int32, (BAND, 2 * BAND), 1)
    has_prev = jnp.logical_or(n > 0, kj >= BAND)
    lane = lax.broadcasted_iota(jnp.int32, (BAND, LANES), 1)
    m_lo, m_hi = _half_masks(BF16)
    for g in range(DIL_G):
        rows = slice(g * BAND, (g + 1) * BAND)
        prev = slice((g - 1) * BAND, g * BAND)
        lse_tile = jnp.zeros((BAND, LANES), F32)
        for p in range(N_PAIRS):
            cols = slice(p * LANES, (p + 1) * LANES)
            q2 = q_ref[0, rows, cols]
            k_prev = kp_ref[0, :, cols] if g == 0 else kc_ref[0, prev, cols]
            v_prev = vp_ref[0, :, cols] if g == 0 else vc_ref[0, prev, cols]
            kcat = jnp.concatenate([k_prev, kc_ref[0, rows, cols]], axis=0)
            vcat = jnp.concatenate([v_prev, vc_ref[0, rows, cols]], axis=0)
            outs = []
            for s, msk in enumerate((m_lo, m_hi)):
                h = 2 * p + s
                logits = _mm_nt(q2 * msk, kcat) + bias_ref[h]
                if g == 0:
                    logits = jnp.where(has_prev, logits, NEG_INF)
                m = jnp.max(logits, axis=-1, keepdims=True)
                pr = jnp.exp(logits - m)
                den = jnp.sum(pr, axis=-1, keepdims=True)
                outs.append(_mm(pr.astype(BF16), vcat) / den)
                lse_tile = jnp.where(lane == h, m + jnp.log(den), lse_tile)
            o_ref[0, rows, cols] = jnp.where(lane < HEAD_DIM, outs[0], outs[1])
        lse_ref[0, rows, :] = lse_tile


def _dilated(src, b, s, rel_bias, dilation):
    d = dilation
    l = s // d
    nstep = l // (DIL_G * BAND)
    view = src.reshape(b, l, src.shape[1])
    n_sec = src.shape[1] // d // D_HALF

    def sec(k, prev):
        if prev:
            return pl.BlockSpec((1, BAND, D_HALF),
                                lambda bi, r, n: (bi, jnp.maximum(n * DIL_G - 1, 0), r * n_sec + k))
        return pl.BlockSpec((1, DIL_G * BAND, D_HALF), lambda bi, r, n: (bi, n, r * n_sec + k))

    o, lse = pl.pallas_call(
        _dilated_kernel,
        grid=(b, d, nstep),
        in_specs=[sec(0, False), sec(1, True), sec(1, False), sec(2, True), sec(2, False),
                  pl.BlockSpec((N_HEADS, BAND, 2 * BAND), lambda bi, r, n: (0, 0, 0))],
        out_specs=[pl.BlockSpec((1, DIL_G * BAND, D_HALF), lambda bi, r, n: (bi, n, r)),
                   pl.BlockSpec((1, DIL_G * BAND, LANES), lambda bi, r, n: (bi, n, r))],
        out_shape=[jax.ShapeDtypeStruct((b, l, d * D_HALF), F32),
                   jax.ShapeDtypeStruct((b, l, d * LANES), F32)],
        compiler_params=_cparams(("parallel", "parallel", "arbitrary")),
        name=f"dilated_d{d}",
    )(view, view, view, view, view, _band_bias(rel_bias, d))
    return o.reshape(b * l, d * D_HALF), lse.reshape(b * l, d * LANES)


def _sb_kernel(q_ref, k_ref, v_ref, u_ref, o_ref, acc_ref, carry_ref):
    i = pl.program_id(2)
    m_lo, m_hi = _half_masks(BF16)
    row = lax.broadcasted_iota(jnp.int32, (2 * SB_T, SB_T), 0) & (SB_T - 1)
    col = lax.broadcasted_iota(jnp.int32, (2 * SB_T, SB_T), 1)
    causal = col < row
    lane = lax.broadcasted_iota(jnp.int32, (SB_T, LANES), 1)
    u = u_ref[...]

    def tile(qs, j, mask, carry):
        start = pl.multiple_of(j * SB_T, SB_T)
        kj = k_ref[0, pl.ds(start, SB_T), :]
        vj = v_ref[0, pl.ds(start, SB_T), :]
        z = _mm_nt(qs, kj)
        neg_abs = lax.bitcast_convert_type(
            lax.bitcast_convert_type(z, jnp.uint32) | jnp.uint32(0x80000000), F32)
        sp = jnp.log2(1.0 + jnp.exp2(neg_abs))
        log_b = jnp.minimum(z, 0.0) - sp
        l1m = log_b - z
        if mask is not None:
            l1m = jnp.where(mask, l1m, 0.0)
        cum = _mm_split(l1m, u)
        total = cum[:, 0:1] + l1m[:, 0:1]
        a = jnp.exp2(log_b + cum + carry)
        if mask is not None:
            a = jnp.where(mask, a, 0.0)
        return _mm(a.astype(BF16), vj), total

    def more(state):
        j, top = state
        return jnp.logical_and(j >= 0, top > -SB_EXP_ZERO * LOG2E)

    tops = []
    for g in range(SB_G):
        blk = i * SB_G + g
        q2 = q_ref[0, g * SB_T:(g + 1) * SB_T, :]
        qs = jnp.concatenate([q2 * m_lo, q2 * m_hi], axis=0)
        pv_d, tot_d = tile(qs, blk, causal, 0.0)
        if g == 0:
            pv_p, tot_p = tile(qs, jnp.maximum(blk - 1, 0), blk > 0, tot_d)
        else:
            pv_p, tot_p = tile(qs, blk - 1, None, tot_d)
        acc_ref[g] = pv_d + pv_p
        carry0 = tot_d + tot_p
        carry_ref[g] = carry0
        tops.append(jnp.max(carry0))

    for g in range(SB_G):
        def body(state, g=g):
            j, _ = state
            q2 = q_ref[0, g * SB_T:(g + 1) * SB_T, :]
            qs = jnp.concatenate([q2 * m_lo, q2 * m_hi], axis=0)
            pv, tot = tile(qs, j, None, carry_ref[g])
            acc_ref[g] += pv
            carry = carry_ref[g] + tot
            carry_ref[g] = carry
            return j - 1, jnp.max(carry)

        lax.while_loop(more, body, (i * SB_G + g - 2, tops[g]))
        o_ref[0, g * SB_T:(g + 1) * SB_T, :] = jnp.where(lane < HEAD_DIM, acc_ref[g, :SB_T], acc_ref[g, SB_T:])


def _suffix_matrix():
    sp = jnp.arange(SB_T, dtype=jnp.int32)[:, None]
    sc = jnp.arange(SB_T, dtype=jnp.int32)[None, :]
    return jnp.where(sp > sc, 1.0, 0.0).astype(BF16)


def _stickbreaking(qkv3):
    b, s, _ = qkv3.shape
    sec_q, sec_k, sec_v = 3 * N_PAIRS, 4 * N_PAIRS, 5 * N_PAIRS
    o = pl.pallas_call(
        _sb_kernel,
        grid=(b, N_PAIRS, s // (SB_G * SB_T)),
        in_specs=[pl.BlockSpec((1, SB_G * SB_T, LANES), lambda bi, p, i: (bi, i, sec_q + p)),
                  pl.BlockSpec((1, s, LANES), lambda bi, p, i: (bi, 0, sec_k + p)),
                  pl.BlockSpec((1, s, LANES), lambda bi, p, i: (bi, 0, sec_v + p)),
                  pl.BlockSpec((SB_T, SB_T), lambda bi, p, i: (0, 0))],
        out_specs=pl.BlockSpec((1, SB_G * SB_T, LANES), lambda bi, p, i: (bi, i, p)),
        out_shape=jax.ShapeDtypeStruct((b, s, D_HALF), F32),
        scratch_shapes=[pltpu.VMEM((SB_G, 2 * SB_T, LANES), F32), pltpu.VMEM((SB_G, 2 * SB_T, 1), F32)],
        compiler_params=_cparams(("parallel", "parallel", "arbitrary")),
        name="stickbreaking",
    )(qkv3, qkv3, qkv3, _suffix_matrix())
    return o.reshape(b * s, D_HALF)


def _token_major(ref, d, st_ref):
    if d == 1:
        return ref[...]
    width = ref.shape[1] // d
    for r in range(d):
        for ch in range(width // LANES):
            col = r * width + ch * LANES
            st_ref[ch, pl.ds(r, MIX_TM // d, stride=d), :] = ref[:, col:col + LANES]
    return jnp.concatenate([st_ref[ch] for ch in range(width // LANES)], axis=1)


def _mix_kernel(o1_ref, o2_ref, o3_ref, l1_ref, l2_ref, l3_ref, ob_ref, x_ref,
                ga_ref, gb_ref, e_ref, wo_ref, x1_ref, *scratch):
    dils = [d for _w, d in DIL_PATTERNS]
    o_st = dict(zip(STRIDED_DILATIONS, scratch[:len(STRIDED_DILATIONS)]))
    l_st = dict(zip(STRIDED_DILATIONS, scratch[len(STRIDED_DILATIONS):]))
    o1, o2, o3 = [_token_major(r, d, o_st.get(d)) for r, d in zip((o1_ref, o2_ref, o3_ref), dils)]
    l1, l2, l3 = [_token_major(r, d, l_st.get(d)) for r, d in zip((l1_ref, l2_ref, l3_ref), dils)]
    m = jnp.maximum(jnp.maximum(l1, l2), l3)
    e1, e2, e3 = jnp.exp(l1 - m), jnp.exp(l2 - m), jnp.exp(l3 - m)
    inv = 1.0 / (e1 + e2 + e3)
    expand = e_ref[...]
    oa = (_mm_split(e1 * inv, expand) * o1
          + _mm_split(e2 * inv, expand) * o2
          + _mm_split(e3 * inv, expand) * o3)
    oa = (_rms(oa) * ga_ref[...]).astype(BF16)
    ob = (_rms(ob_ref[...]) * gb_ref[...]).astype(BF16)
    mix = _mm(oa, wo_ref[:D_HALF, :]) + _mm(ob, wo_ref[D_HALF:, :])
    x1_ref[...] = x_ref[...] + mix


def _head_expand():
    lane = jnp.arange(LANES, dtype=jnp.int32)[:, None]
    col = jnp.arange(D_HALF, dtype=jnp.int32)[None, :]
    return jnp.where(col // HEAD_DIM == lane, 1.0, 0.0).astype(BF16)


def _mix(os_, lses, ob, x2, g_out_dil, g_out_sb, w_o):
    n = x2.shape[0]
    row = lambda w: pl.BlockSpec((MIX_TM, w), lambda i: (i, 0))
    srow = lambda a: pl.BlockSpec((MIX_TM * a.shape[0] // n, a.shape[1]), lambda i: (i, 0))
    const = lambda shape: pl.BlockSpec(shape, lambda i: (0, 0))
    return pl.pallas_call(
        _mix_kernel,
        grid=(n // MIX_TM,),
        in_specs=[srow(a) for a in os_] + [srow(a) for a in lses] + [row(D_HALF), row(D_MODEL),
                  const((1, D_HALF)), const((1, D_HALF)), const((LANES, D_HALF)), const((D_MODEL, D_MODEL))],
        out_specs=row(D_MODEL),
        out_shape=jax.ShapeDtypeStruct((n, D_MODEL), F32),
        scratch_shapes=[pltpu.VMEM((D_HALF // LANES, MIX_TM, LANES), F32) for _ in STRIDED_DILATIONS]
                       + [pltpu.VMEM((1, MIX_TM, LANES), F32) for _ in STRIDED_DILATIONS],
        compiler_params=_cparams(("parallel",)),
        name="mix_wo",
    )(*os_, *lses, ob, x2, g_out_dil.reshape(1, D_HALF), g_out_sb.reshape(1, D_HALF),
      _head_expand(), w_o.astype(BF16))


def _router_kernel(x1_ref, g_ref, wr_ref, br_ref, tri_ref, h2_ref, meta_ref, gate_ref, cnt_ref, carry_ref):
    i = pl.program_id(0)

    @pl.when(i == 0)
    def _():
        carry_ref[...] = jnp.zeros_like(carry_ref)

    h2 = _rms(x1_ref[...]) * g_ref[...]
    h2_ref[...] = _pack_bf16_halves(h2)
    h_hi, h_lo = _split_bf16(h2)
    w_hi, w_lo = wr_ref[0], wr_ref[1]
    logits = _mm(h_hi, w_hi) + (_mm(h_hi, w_lo) + _mm(h_lo, w_hi)) + br_ref[...]
    lane = lax.broadcasted_iota(jnp.int32, (ROUTER_TM, LANES), 1)
    lane_f = lane.astype(F32)
    logits = jnp.where(lane < N_EXPERTS, logits, -jnp.inf)

    onehots, vals, ids = [], [], []
    for _k in range(TOP_K):
        m = jnp.max(logits, axis=-1, keepdims=True)
        idx = jnp.min(jnp.where(logits == m, lane_f, float(LANES)), axis=-1, keepdims=True)
        oh = lane_f == idx
        logits = jnp.where(oh, -jnp.inf, logits)
        onehots.append(oh)
        vals.append(m)
        ids.append(idx)

    es = [jnp.exp(v - vals[0]) for v in vals]
    inv = 1.0 / (es[0] + es[1] + es[2] + es[3])

    sel = jnp.zeros((ROUTER_TM, LANES), F32)
    for oh in onehots:
        sel = jnp.where(oh, 1.0, sel)
    before = _mm(tri_ref[...], sel.astype(BF16)) + carry_ref[0:1, :]
    meta = jnp.zeros((ROUTER_TM, LANES), F32)
    gate = jnp.zeros((ROUTER_TM, LANES), F32)
    for k in range(TOP_K):
        rank = jnp.sum(jnp.where(onehots[k], before, 0.0), axis=-1, keepdims=True)
        meta = jnp.where(lane == k, ids[k], meta)
        meta = jnp.where(lane == TOP_K + k, rank, meta)
        gate = jnp.where(lane == k, es[k] * inv, gate)
    meta_ref[...] = meta.astype(jnp.int32)
    gate_ref[...] = gate
    total = carry_ref[...] + jnp.sum(sel, axis=0, keepdims=True)
    carry_ref[...] = total
    cnt_ref[...] = total


def _router(x1, g_moe, w_router, b_router):
    n = x1.shape[0]
    wr = jnp.zeros((D_MODEL, LANES), F32).at[:, :N_EXPERTS].set(w_router.astype(F32))
    wr_hi = wr.astype(BF16)
    wr_lo = (wr - wr_hi.astype(F32)).astype(BF16)
    br = jnp.zeros((1, LANES), F32).at[0, :N_EXPERTS].set(b_router.astype(F32))
    r = jnp.arange(ROUTER_TM, dtype=jnp.int32)
    tri = jnp.where(r[None, :] < r[:, None], 1.0, 0.0).astype(BF16)
    row = lambda w: pl.BlockSpec((ROUTER_TM, w), lambda i: (i, 0))
    return pl.pallas_call(
        _router_kernel,
        grid=(n // ROUTER_TM,),
        in_specs=[row(D_MODEL),
                  pl.BlockSpec((1, D_MODEL), lambda i: (0, 0)),
                  pl.BlockSpec((2, D_MODEL, LANES), lambda i: (0, 0, 0)),
                  pl.BlockSpec((1, LANES), lambda i: (0, 0)),
                  pl.BlockSpec((ROUTER_TM, ROUTER_TM), lambda i: (0, 0))],
        out_specs=[row(D_HALF), row(LANES), row(LANES), pl.BlockSpec((8, LANES), lambda i: (0, 0))],
        out_shape=[jax.ShapeDtypeStruct((n, D_HALF), jnp.int32),
                   jax.ShapeDtypeStruct((n, LANES), jnp.int32),
                   jax.ShapeDtypeStruct((n, LANES), F32),
                   jax.ShapeDtypeStruct((8, LANES), F32)],
        scratch_shapes=[pltpu.VMEM((8, LANES), F32)],
        compiler_params=_cparams(("arbitrary",)),
        name="router",
    )(x1, g_moe.reshape(1, D_MODEL), jnp.stack([wr_hi, wr_lo]), br, tri)


def _dispatch(h2, dest, n_slots):
    n = h2.shape[0]
    per_w = n // SC_WORKERS
    nch = per_w // SC_SCATTER_ROWS
    nslab = h2.shape[1] // SC_SCATTER_COLS
    nj = nch * nslab
    idx = dest.T.reshape(TOP_K, SC_WORKERS, nch, SC_SCATTER_ROWS).transpose(1, 2, 0, 3)
    idx = idx.reshape(SC_WORKERS * nch * TOP_K, SC_SCATTER_ROWS)
    mesh = plsc.VectorSubcoreMesh(core_axis_name="core", subcore_axis_name="subcore")

    @functools.partial(
        pl.kernel, mesh=mesh,
        out_type=jax.ShapeDtypeStruct((n_slots, h2.shape[1]), h2.dtype),
        scratch_types=[pltpu.VMEM((nch * TOP_K, SC_SCATTER_ROWS), jnp.int32),
                       pltpu.VMEM((2, SC_SCATTER_ROWS, SC_SCATTER_COLS), h2.dtype),
                       pltpu.SemaphoreType.DMA((2,)), pltpu.SemaphoreType.DMA((2,))])
    def scatter_kernel(h2_hbm, idx_hbm, xs_hbm, idx_v, rows_v, sem_l, sem_s):
        wid = lax.axis_index("subcore") * SC_CORES + lax.axis_index("core")
        t0 = wid * per_w
        pltpu.sync_copy(idx_hbm.at[pl.ds(wid * (nch * TOP_K), nch * TOP_K)], idx_v)

        def load(j, b):
            rows = pl.ds(t0 + (j // nslab) * SC_SCATTER_ROWS, SC_SCATTER_ROWS)
            cols = pl.ds((j % nslab) * SC_SCATTER_COLS, SC_SCATTER_COLS)
            return pltpu.make_async_copy(h2_hbm.at[rows, cols], rows_v.at[b], sem_l.at[b])

        def scatter(j, k, b):
            cols = pl.ds((j % nslab) * SC_SCATTER_COLS, SC_SCATTER_COLS)
            return pltpu.make_async_copy(rows_v.at[b], xs_hbm.at[idx_v.at[(j // nslab) * TOP_K + k], cols], sem_s.at[b])

        load(0, 0).start()

        @pl.loop(0, nj, step=2)
        def _(j0):
            for b in (0, 1):
                j = j0 + b

                @pl.when(j >= 1)
                def _():
                    for k in range(TOP_K):
                        scatter(j - 1, k, 1 - b).wait()

                @pl.when(j + 1 < nj)
                def _():
                    load(j + 1, 1 - b).start()

                load(j, b).wait()
                for k in range(TOP_K):
                    scatter(j, k, b).start()

        for k in range(TOP_K):
            scatter(nj - 1, k, 1).wait()

    return scatter_kernel(h2, idx)


def _expert_kernel(be_ref, nu_ref, nv_ref, x_ref, wg_ref, bg_ref, wu_ref, bu_ref, wd_ref, bd_ref, y_ref):
    del be_ref
    i = pl.program_id(0)

    @pl.when(i < nu_ref[0])
    def _():
        row = lax.broadcasted_iota(jnp.int32, (EXPERT_TM, 1), 0)
        xb = _unpack_bf16_halves(jnp.where(row < nv_ref[i], x_ref[...], jnp.int32(0))).astype(BF16)
        y = jnp.zeros((EXPERT_TM, D_MODEL), F32) + bd_ref[0]
        for c in range(D_MODEL // D_HALF):
            cols = slice(c * D_HALF, (c + 1) * D_HALF)
            glu = _mm(xb, wg_ref[0, :, cols].astype(BF16)) + bg_ref[0, :, cols]
            lin = _mm(xb, wu_ref[0, :, cols].astype(BF16)) + bu_ref[0, :, cols]
            glu = jnp.minimum(glu, SWIGLU_LIMIT)
            lin = jnp.clip(lin, -SWIGLU_LIMIT, SWIGLU_LIMIT)
            act = glu * (1.0 / (1.0 + jnp.exp(-SWIGLU_ALPHA * glu))) * (lin + 1.0)
            y = y + _mm(act.astype(BF16), wd_ref[0, cols, :].astype(BF16))
        y_ref[...] = _pack_bf16_halves(y)


def _experts(xs, block_expert, n_used, n_valid, w_gate, b_gate, w_up, b_up, w_down, b_down):
    n_slots = xs.shape[0]
    n_blocks = n_slots // EXPERT_TM
    rows = lambda i, be, nu, nv: (jnp.minimum(i, nu[0] - 1), 0)
    wspec = pl.BlockSpec((1, D_MODEL, D_MODEL), lambda i, be, nu, nv: (be[i], 0, 0))
    bspec = pl.BlockSpec((1, 1, D_MODEL), lambda i, be, nu, nv: (be[i], 0, 0))
    grid_spec = pltpu.PrefetchScalarGridSpec(
        num_scalar_prefetch=3,
        grid=(n_blocks,),
        in_specs=[pl.BlockSpec((EXPERT_TM, D_HALF), rows), wspec, bspec, wspec, bspec, wspec, bspec],
        out_specs=pl.BlockSpec((EXPERT_TM, D_HALF), rows),
    )
    b3 = lambda a: a.astype(F32).reshape(N_EXPERTS, 1, D_MODEL)
    return pl.pallas_call(
        _expert_kernel,
        grid_spec=grid_spec,
        out_shape=jax.ShapeDtypeStruct((n_slots, D_HALF), jnp.int32),
        compiler_params=_cparams(("arbitrary",)),
        name="experts",
    )(block_expert, n_used, n_valid, xs, w_gate, b3(b_gate), w_up, b3(b_up), w_down, b3(b_down))


def _sc_row_gather(table, idx):
    n_idx = idx.shape[0]
    width = table.shape[1]
    per_w = n_idx // SC_WORKERS
    nch = per_w // SC_CHUNK
    mesh = plsc.VectorSubcoreMesh(core_axis_name="core", subcore_axis_name="subcore")

    @functools.partial(
        pl.kernel, mesh=mesh,
        out_type=jax.ShapeDtypeStruct((n_idx, width), table.dtype),
        scratch_types=[pltpu.VMEM((per_w,), jnp.int32), pltpu.VMEM((2, SC_CHUNK, width), table.dtype),
                       pltpu.SemaphoreType.DMA((2,)), pltpu.SemaphoreType.DMA((2,))])
    def gather_kernel(table_hbm, idx_hbm, out_hbm, idx_v, rows_v, sem_g, sem_p):
        wid = lax.axis_index("subcore") * SC_CORES + lax.axis_index("core")
        base = wid * per_w
        pltpu.sync_copy(idx_hbm.at[pl.ds(base, per_w)], idx_v)

        def gather(c, b):
            return pltpu.make_async_copy(table_hbm.at[idx_v.at[pl.ds(c * SC_CHUNK, SC_CHUNK)]], rows_v.at[b], sem_g.at[b])

        def put(c, b):
            return pltpu.make_async_copy(rows_v.at[b], out_hbm.at[pl.ds(base + c * SC_CHUNK, SC_CHUNK)], sem_p.at[b])

        gather(0, 0).start()

        @pl.loop(0, nch, step=2)
        def _(c):
            for b in (0, 1):
                cc = c + b

                @pl.when(cc >= 1)
                def _():
                    put(cc - 1, 1 - b).wait()

                @pl.when(cc + 1 < nch)
                def _():
                    gather(cc + 1, 1 - b).start()

                gather(cc, b).wait()
                put(cc, b).start()

        put(nch - 1, 1).wait()

    return gather_kernel(table, idx)


def _combine_kernel(yt_ref, gate_ref, x1_ref, g_ref, o_ref):
    gate = gate_ref[...]
    x = x1_ref[...]
    for k in range(TOP_K):
        x = x + gate[:, k:k + 1] * _unpack_bf16_halves(yt_ref[k])
    o_ref[...] = _rms(x) * g_ref[...]


def _combine(yt, gates, x1, g_final):
    n = x1.shape[0]
    yt = yt.reshape(TOP_K, n, D_HALF)
    row = lambda w: pl.BlockSpec((COMBINE_R, w), lambda i: (i, 0))
    return pl.pallas_call(
        _combine_kernel,
        grid=(n // COMBINE_R,),
        in_specs=[pl.BlockSpec((TOP_K, COMBINE_R, D_HALF), lambda i: (0, i, 0)),
                  row(LANES), row(D_MODEL),
                  pl.BlockSpec((1, D_MODEL), lambda i: (0, 0))],
        out_specs=row(D_MODEL),
        out_shape=jax.ShapeDtypeStruct((n, D_MODEL), F32),
        compiler_params=_cparams(("parallel",)),
        name="combine",
    )(yt, gates, x1, g_final.reshape(1, D_MODEL))


def _slot_layout(meta, counts_f, n):
    eid = meta[:, :TOP_K]
    rank = meta[:, TOP_K:2 * TOP_K]
    counts = counts_f[0, :N_EXPERTS].astype(jnp.int32)
    padded = (counts + EXPERT_TM - 1) // EXPERT_TM * EXPERT_TM
    pend = jnp.cumsum(padded)
    pstart = pend - padded
    experts = jnp.arange(N_EXPERTS, dtype=jnp.int32)
    dest = rank + jnp.sum(jnp.where(eid[..., None] == experts, pstart, 0), axis=-1)
    n_slots = -(-(n * TOP_K + N_EXPERTS * (EXPERT_TM - 1)) // EXPERT_TM) * EXPERT_TM
    block_start = jnp.arange(n_slots // EXPERT_TM, dtype=jnp.int32) * EXPERT_TM
    block_expert = jnp.minimum(jnp.sum(pend[None, :] <= block_start[:, None], axis=-1), N_EXPERTS - 1)
    last_valid = jnp.sum(jnp.where(block_expert[:, None] == experts, pstart + counts, 0), axis=-1)
    n_valid = jnp.clip(last_valid - block_start, 0, EXPERT_TM).astype(jnp.int32)
    n_used = (pend[-1:] // EXPERT_TM).astype(jnp.int32)
    return dest.astype(jnp.int32), block_expert.astype(jnp.int32), n_used, n_valid, n_slots


def kernel(x, g_attn, w_qkv, rel_bias, g_out_dil, g_out_sb, w_o, g_moe, w_router, b_router,
           w_gate, b_gate, w_up, b_up, w_down, b_down, g_final):
    b, s, d = x.shape
    n = b * s
    x2 = x.reshape(n, d)
    qkv, *strided = _qkv(x2, g_attn, w_qkv)
    sources = dict(zip(STRIDED_DILATIONS, strided))
    dil = [_dilated(sources.get(dilation, qkv), b, s, rel_bias, dilation) for _window, dilation in DIL_PATTERNS]
    ob = _stickbreaking(qkv.reshape(b, s, 3 * d))
    x1 = _mix([o for o, _ in dil], [l for _, l in dil], ob, x2, g_out_dil, g_out_sb, w_o)
    h2, meta, gates, counts = _router(x1, g_moe, w_router, b_router)
    dest, block_expert, n_used, n_valid, n_slots = _slot_layout(meta, counts, n)
    xs = _dispatch(h2, dest, n_slots)
    ys = _experts(xs, block_expert, n_used, n_valid, w_gate, b_gate, w_up, b_up, w_down, b_down)
    yt = _sc_row_gather(ys, dest.T.reshape(-1))
    return _combine(yt, gates, x1, g_final).reshape(b, s, d)
```

```python
import functools
import math

import jax
import jax.numpy as jnp
from jax import lax
from jax.experimental import pallas as pl
from jax.experimental.pallas import tpu as pltpu
from jax.experimental.pallas import tpu_sc as plsc

F32 = jnp.float32
BF16 = jnp.bfloat16

D_MODEL = 1024
HEAD_DIM = 64
D_HALF = 512
N_HEADS = 8
LANES = 128
N_PAIRS = D_HALF // LANES
DIL_PATTERNS = ((128, 1), (512, 4), (2048, 16))
BAND = 128
REL_BUCKETS = 32
REL_MAX_DISTANCE = 2048
N_EXPERTS = 32
TOP_K = 4
SWIGLU_ALPHA = 1.702
SWIGLU_LIMIT = 7.0
RMS_EPS = 1e-6
NEG_INF = -1e30

QKV_TM = 512
DIL_G = 4
A_COLS = 3 * D_HALF
STRIDED_DILATIONS = tuple(d for _w, d in DIL_PATTERNS if d > 1)
LOG2E = math.log2(math.e)
SB_T = 256
SB_G = 4
SB_EXP_ZERO = 104.0
MIX_TM = 256
ROUTER_TM = 512
DISPATCH_R = 256
EXPERT_TM = 512
COMBINE_R = 256
SC_CORES = 2
SC_WORKERS = SC_CORES * 16
SC_CHUNK = 64
SC_SCATTER_ROWS = 128
SC_SCATTER_COLS = 256
DMA_UNROLL = 8
VMEM_LIMIT = 56 * 1024 * 1024


def _cparams(sem):
    return pltpu.CompilerParams(dimension_semantics=sem, vmem_limit_bytes=VMEM_LIMIT)


def _mm(a, b):
    return jnp.dot(a, b, preferred_element_type=F32)


def _mm_nt(a, b):
    return lax.dot_general(a, b, (((1,), (1,)), ((), ())), preferred_element_type=F32)


def _split_bf16(a):
    hi = a.astype(BF16)
    lo = (a - hi.astype(F32)).astype(BF16)
    return hi, lo


def _mm_split(a, b_bf16):
    hi, lo = _split_bf16(a)
    return _mm(hi, b_bf16) + _mm(lo, b_bf16)


def _pack_bf16_halves(x):
    c = x.shape[1] // 2
    return pltpu.pack_elementwise([x[:, :c], x[:, c:]], packed_dtype=BF16)


def _unpack_bf16_halves(p):
    return jnp.concatenate([pltpu.unpack_elementwise(p, index=i, packed_dtype=BF16, unpacked_dtype=F32)
                            for i in (0, 1)], axis=1)


def _rms(x):
    return x * lax.rsqrt(jnp.mean(x * x, axis=-1, keepdims=True) + RMS_EPS)


def _half_masks(dtype):
    lane = lax.broadcasted_iota(jnp.int32, (1, LANES), 1)
    lo = jnp.where(lane < HEAD_DIM, 1.0, 0.0).astype(dtype)
    hi = jnp.where(lane >= HEAD_DIM, 1.0, 0.0).astype(dtype)
    return lo, hi


def _qkv_kernel(x_ref, g_ref, w_ref, o_ref, *rest):
    stage_ref = rest[-1]
    h = (_rms(x_ref[...]) * g_ref[...]).astype(BF16)
    for c in range(3 * D_MODEL // D_HALF):
        y = _mm(h, w_ref[:, c * D_HALF:(c + 1) * D_HALF])
        if c == 0:
            y = y * (1.0 / math.sqrt(HEAD_DIM))
        if c == 3:
            y = y * (LOG2E / math.sqrt(HEAD_DIM))
        o_ref[:, c * D_HALF:(c + 1) * D_HALF] = y.astype(BF16)
        if c < 3:
            for p in range(N_PAIRS):
                stage_ref[c * N_PAIRS + p] = y[:, p * LANES:(p + 1) * LANES]
    for od_ref, d in zip(rest[:-1], STRIDED_DILATIONS):
        for r in range(d):
            for ch in range(A_COLS // LANES):
                col = r * A_COLS + ch * LANES
                od_ref[:, col:col + LANES] = stage_ref[ch, pl.ds(r, QKV_TM // d, stride=d), :].astype(BF16)


def _qkv(x2, g_attn, w_qkv):
    n = x2.shape[0]
    strided = [(n // d, d * A_COLS) for d in STRIDED_DILATIONS]
    return pl.pallas_call(
        _qkv_kernel,
        grid=(n // QKV_TM,),
        in_specs=[pl.BlockSpec((QKV_TM, D_MODEL), lambda i: (i, 0)),
                  pl.BlockSpec((1, D_MODEL), lambda i: (0, 0)),
                  pl.BlockSpec((D_MODEL, 3 * D_MODEL), lambda i: (0, 0))],
        out_specs=[pl.BlockSpec((QKV_TM, 3 * D_MODEL), lambda i: (i, 0))]
                  + [pl.BlockSpec((QKV_TM // d, d * A_COLS), lambda i: (i, 0)) for d in STRIDED_DILATIONS],
        out_shape=[jax.ShapeDtypeStruct((n, 3 * D_MODEL), BF16)]
                  + [jax.ShapeDtypeStruct(shape, BF16) for shape in strided],
        scratch_shapes=[pltpu.VMEM((A_COLS // LANES, QKV_TM, LANES), F32)],
        compiler_params=_cparams(("parallel",)),
        name="qkv",
    )(x2, g_attn.reshape(1, D_MODEL), w_qkv.astype(BF16))


def _t5_bucket(dist):
    max_exact = REL_BUCKETS // 2
    d_f = jnp.maximum(dist, 1).astype(jnp.float32)
    large = max_exact + (jnp.log(d_f / max_exact)
                         / math.log(REL_MAX_DISTANCE / max_exact)
                         * (REL_BUCKETS - max_exact)).astype(jnp.int32)
    large = jnp.minimum(large, REL_BUCKETS - 1)
    return jnp.where(dist < max_exact, dist, large)


def _band_bias(rel_bias, dilation):
    qi = jnp.arange(BAND, dtype=jnp.int32)[:, None]
    kj = jnp.arange(2 * BAND, dtype=jnp.int32)[None, :]
    rel = qi + BAND - kj
    ok = (rel >= 0) & (rel <= BAND)
    bucket = _t5_bucket(jnp.maximum(rel, 0) * dilation)
    table = rel_bias.astype(F32)
    bias = jnp.zeros((N_HEADS, BAND, 2 * BAND), F32)
    for bk in range(REL_BUCKETS):
        bias = jnp.where(bucket[None] == bk, table[bk][:, None, None], bias)
    return jnp.where(ok[None], bias, NEG_INF)


def _dilated_kernel(q_ref, kp_ref, kc_ref, vp_ref, vc_ref, bias_ref, o_ref, lse_ref):
    n = pl.program_id(2)
    kj = lax.broadcasted_iota(jnp.int32, (BAND, 2 * BAND), 1)
    has_prev = jnp.logical_or(n > 0, kj >= BAND)
    lane = lax.broadcasted_iota(jnp.int32, (BAND, LANES), 1)
    m_lo, m_hi = _half_masks(BF16)
    for g in range(DIL_G):
        rows = slice(g * BAND, (g + 1) * BAND)
        prev = slice((g - 1) * BAND, g * BAND)
        lse_tile = jnp.zeros((BAND, LANES), F32)
        for p in range(N_PAIRS):
            cols = slice(p * LANES, (p + 1) * LANES)
            q2 = q_ref[0, rows, cols]
            k_prev = kp_ref[0, :, cols] if g == 0 else kc_ref[0, prev, cols]
            v_prev = vp_ref[0, :, cols] if g == 0 else vc_ref[0, prev, cols]
            kcat = jnp.concatenate([k_prev, kc_ref[0, rows, cols]], axis=0)
            vcat = jnp.concatenate([v_prev, vc_ref[0, rows, cols]], axis=0)
            outs = []
            for s, msk in enumerate((m_lo, m_hi)):
                h = 2 * p + s
                logits = _mm_nt(q2 * msk, kcat) + bias_ref[h]
                if g == 0:
                    logits = jnp.where(has_prev, logits, NEG_INF)
                m = jnp.max(logits, axis=-1, keepdims=True)
                pr = jnp.exp(logits - m)
                den = jnp.sum(pr, axis=-1, keepdims=True)
                outs.append(_mm(pr.astype(BF16), vcat) / den)
                lse_tile = jnp.where(lane == h, m + jnp.log(den), lse_tile)
            o_ref[0, rows, cols] = jnp.where(lane < HEAD_DIM, outs[0], outs[1])
        lse_ref[0, rows, :] = lse_tile


def _dilated(src, b, s, rel_bias, dilation):
    d = dilation
    l = s // d
    nstep = l // (DIL_G * BAND)
    view = src.reshape(b, l, src.shape[1])
    n_sec = src.shape[1] // d // D_HALF

    def sec(k, prev):
        if prev:
            return pl.BlockSpec((1, BAND, D_HALF),
                                lambda bi, r, n: (bi, jnp.maximum(n * DIL_G - 1, 0), r * n_sec + k))
        return pl.BlockSpec((1, DIL_G * BAND, D_HALF), lambda bi, r, n: (bi, n, r * n_sec + k))

    o, lse = pl.pallas_call(
        _dilated_kernel,
        grid=(b, d, nstep),
        in_specs=[sec(0, False), sec(1, True), sec(1, False), sec(2, True), sec(2, False),
                  pl.BlockSpec((N_HEADS, BAND, 2 * BAND), lambda bi, r, n: (0, 0, 0))],
        out_specs=[pl.BlockSpec((1, DIL_G * BAND, D_HALF), lambda bi, r, n: (bi, n, r)),
                   pl.BlockSpec((1, DIL_G * BAND, LANES), lambda bi, r, n: (bi, n, r))],
        out_shape=[jax.ShapeDtypeStruct((b, l, d * D_HALF), F32),
                   jax.ShapeDtypeStruct((b, l, d * LANES), F32)],
        compiler_params=_cparams(("parallel", "parallel", "arbitrary")),
        name=f"dilated_d{d}",
    )(view, view, view, view, view, _band_bias(rel_bias, d))
    return o.reshape(b * l, d * D_HALF), lse.reshape(b * l, d * LANES)


def _sb_kernel(q_ref, k_ref, v_ref, u_ref, o_ref, acc_ref, carry_ref):
    i = pl.program_id(2)
    m_lo, m_hi = _half_masks(BF16)
    row = lax.broadcasted_iota(jnp.int32, (2 * SB_T, SB_T), 0) & (SB_T - 1)
    col = lax.broadcasted_iota(jnp.int32, (2 * SB_T, SB_T), 1)
    causal = col < row
    lane = lax.broadcasted_iota(jnp.int32, (SB_T, LANES), 1)
    u = u_ref[...]

    def tile(qs, j, mask, carry):
        start = pl.multiple_of(j * SB_T, SB_T)
        kj = k_ref[0, pl.ds(start, SB_T), :]
        vj = v_ref[0, pl.ds(start, SB_T), :]
        z = _mm_nt(qs, kj)
        neg_abs = lax.bitcast_convert_type(
            lax.bitcast_convert_type(z, jnp.uint32) | jnp.uint32(0x80000000), F32)
        sp = jnp.log2(1.0 + jnp.exp2(neg_abs))
        log_b = jnp.minimum(z, 0.0) - sp
        l1m = log_b - z
        if mask is not None:
            l1m = jnp.where(mask, l1m, 0.0)
        cum = _mm(l1m.astype(BF16), u)
        total = cum[:, 0:1] + l1m[:, 0:1]
        a = jnp.exp2(log_b + cum + carry)
        if mask is not None:
            a = jnp.where(mask, a, 0.0)
        return _mm(a.astype(BF16), vj), total

    def more(state):
        j, top = state
        return jnp.logical_and(j >= 0, top > -SB_EXP_ZERO * LOG2E)

    tops = []
    for g in range(SB_G):
        blk = i * SB_G + g
        q2 = q_ref[0, g * SB_T:(g + 1) * SB_T, :]
        qs = jnp.concatenate([q2 * m_lo, q2 * m_hi], axis=0)
        pv_d, tot_d = tile(qs, blk, causal, 0.0)
        if g == 0:
            pv_p, tot_p = tile(qs, jnp.maximum(blk - 1, 0), blk > 0, tot_d)
        else:
            pv_p, tot_p = tile(qs, blk - 1, None, tot_d)
        acc_ref[g] = pv_d + pv_p
        carry0 = tot_d + tot_p
        carry_ref[g] = carry0
        tops.append(jnp.max(carry0))

    for g in range(SB_G):
        def body(state, g=g):
            j, _ = state
            q2 = q_ref[0, g * SB_T:(g + 1) * SB_T, :]
            qs = jnp.concatenate([q2 * m_lo, q2 * m_hi], axis=0)
            pv, tot = tile(qs, j, None, carry_ref[g])
            acc_ref[g] += pv
            carry = carry_ref[g] + tot
            carry_ref[g] = carry
            return j - 1, jnp.max(carry)

        lax.while_loop(more, body, (i * SB_G + g - 2, tops[g]))
        o_ref[0, g * SB_T:(g + 1) * SB_T, :] = jnp.where(lane < HEAD_DIM, acc_ref[g, :SB_T], acc_ref[g, SB_T:])


def _suffix_matrix():
    sp = jnp.arange(SB_T, dtype=jnp.int32)[:, None]
    sc = jnp.arange(SB_T, dtype=jnp.int32)[None, :]
    return jnp.where(sp > sc, 1.0, 0.0).astype(BF16)


def _stickbreaking(qkv3):
    b, s, _ = qkv3.shape
    sec_q, sec_k, sec_v = 3 * N_PAIRS, 4 * N_PAIRS, 5 * N_PAIRS
    o = pl.pallas_call(
        _sb_kernel,
        grid=(b, N_PAIRS, s // (SB_G * SB_T)),
        in_specs=[pl.BlockSpec((1, SB_G * SB_T, LANES), lambda bi, p, i: (bi, i, sec_q + p)),
                  pl.BlockSpec((1, s, LANES), lambda bi, p, i: (bi, 0, sec_k + p)),
                  pl.BlockSpec((1, s, LANES), lambda bi, p, i: (bi, 0, sec_v + p)),
                  pl.BlockSpec((SB_T, SB_T), lambda bi, p, i: (0, 0))],
        out_specs=pl.BlockSpec((1, SB_G * SB_T, LANES), lambda bi, p, i: (bi, i, p)),
        out_shape=jax.ShapeDtypeStruct((b, s, D_HALF), F32),
        scratch_shapes=[pltpu.VMEM((SB_G, 2 * SB_T, LANES), F32), pltpu.VMEM((SB_G, 2 * SB_T, 1), F32)],
        compiler_params=_cparams(("parallel", "parallel", "arbitrary")),
        name="stickbreaking",
    )(qkv3, qkv3, qkv3, _suffix_matrix())
    return o.reshape(b * s, D_HALF)


def _token_major(ref, d, st_ref):
    if d == 1:
        return ref[...]
    width = ref.shape[1] // d
    for r in range(d):
        for ch in range(width // LANES):
            col = r * width + ch * LANES
            st_ref[ch, pl.ds(r, MIX_TM // d, stride=d), :] = ref[:, col:col + LANES]
    return jnp.concatenate([st_ref[ch] for ch in range(width // LANES)], axis=1)


def _mix_kernel(o1_ref, o2_ref, o3_ref, l1_ref, l2_ref, l3_ref, ob_ref, x_ref,
                ga_ref, gb_ref, e_ref, wo_ref, x1_ref, *scratch):
    dils = [d for _w, d in DIL_PATTERNS]
    o_st = dict(zip(STRIDED_DILATIONS, scratch[:len(STRIDED_DILATIONS)]))
    l_st = dict(zip(STRIDED_DILATIONS, scratch[len(STRIDED_DILATIONS):]))
    o1, o2, o3 = [_token_major(r, d, o_st.get(d)) for r, d in zip((o1_ref, o2_ref, o3_ref), dils)]
    l1, l2, l3 = [_token_major(r, d, l_st.get(d)) for r, d in zip((l1_ref, l2_ref, l3_ref), dils)]
    m = jnp.maximum(jnp.maximum(l1, l2), l3)
    e1, e2, e3 = jnp.exp(l1 - m), jnp.exp(l2 - m), jnp.exp(l3 - m)
    inv = 1.0 / (e1 + e2 + e3)
    expand = e_ref[...]
    oa = (_mm_split(e1 * inv, expand) * o1
          + _mm_split(e2 * inv, expand) * o2
          + _mm_split(e3 * inv, expand) * o3)
    oa = (_rms(oa) * ga_ref[...]).astype(BF16)
    ob = (_rms(ob_ref[...]) * gb_ref[...]).astype(BF16)
    mix = _mm(oa, wo_ref[:D_HALF, :]) + _mm(ob, wo_ref[D_HALF:, :])
    x1_ref[...] = x_ref[...] + mix


def _head_expand():
    lane = jnp.arange(LANES, dtype=jnp.int32)[:, None]
    col = jnp.arange(D_HALF, dtype=jnp.int32)[None, :]
    return jnp.where(col // HEAD_DIM == lane, 1.0, 0.0).astype(BF16)


def _mix(os_, lses, ob, x2, g_out_dil, g_out_sb, w_o):
    n = x2.shape[0]
    row = lambda w: pl.BlockSpec((MIX_TM, w), lambda i: (i, 0))
    srow = lambda a: pl.BlockSpec((MIX_TM * a.shape[0] // n, a.shape[1]), lambda i: (i, 0))
    const = lambda shape: pl.BlockSpec(shape, lambda i: (0, 0))
    return pl.pallas_call(
        _mix_kernel,
        grid=(n // MIX_TM,),
        in_specs=[srow(a) for a in os_] + [srow(a) for a in lses] + [row(D_HALF), row(D_MODEL),
                  const((1, D_HALF)), const((1, D_HALF)), const((LANES, D_HALF)), const((D_MODEL, D_MODEL))],
        out_specs=row(D_MODEL),
        out_shape=jax.ShapeDtypeStruct((n, D_MODEL), F32),
        scratch_shapes=[pltpu.VMEM((D_HALF // LANES, MIX_TM, LANES), F32) for _ in STRIDED_DILATIONS]
                       + [pltpu.VMEM((1, MIX_TM, LANES), F32) for _ in STRIDED_DILATIONS],
        compiler_params=_cparams(("parallel",)),
        name="mix_wo",
    )(*os_, *lses, ob, x2, g_out_dil.reshape(1, D_HALF), g_out_sb.reshape(1, D_HALF),
      _head_expand(), w_o.astype(BF16))


def _router_kernel(x1_ref, g_ref, wr_ref, br_ref, tri_ref, h2_ref, meta_ref, gate_ref, cnt_ref, carry_ref):
    i = pl.program_id(0)

    @pl.when(i == 0)
    def _():
        carry_ref[...] = jnp.zeros_like(carry_ref)

    h2 = _rms(x1_ref[...]) * g_ref[...]
    h2_ref[...] = _pack_bf16_halves(h2)
    h_hi, h_lo = _split_bf16(h2)
    w_hi, w_lo = wr_ref[0], wr_ref[1]
    logits = _mm(h_hi, w_hi) + (_mm(h_hi, w_lo) + _mm(h_lo, w_hi)) + br_ref[...]
    lane = lax.broadcasted_iota(jnp.int32, (ROUTER_TM, LANES), 1)
    lane_f = lane.astype(F32)
    logits = jnp.where(lane < N_EXPERTS, logits, -jnp.inf)

    onehots, vals, ids = [], [], []
    for _k in range(TOP_K):
        m = jnp.max(logits, axis=-1, keepdims=True)
        idx = jnp.min(jnp.where(logits == m, lane_f, float(LANES)), axis=-1, keepdims=True)
        oh = lane_f == idx
        logits = jnp.where(oh, -jnp.inf, logits)
        onehots.append(oh)
        vals.append(m)
        ids.append(idx)

    es = [jnp.exp(v - vals[0]) for v in vals]
    inv = 1.0 / (es[0] + es[1] + es[2] + es[3])

    sel = jnp.zeros((ROUTER_TM, LANES), F32)
    for oh in onehots:
        sel = jnp.where(oh, 1.0, sel)
    before = _mm(tri_ref[...], sel.astype(BF16)) + carry_ref[0:1, :]
    meta = jnp.zeros((ROUTER_TM, LANES), F32)
    gate = jnp.zeros((ROUTER_TM, LANES), F32)
    for k in range(TOP_K):
        rank = jnp.sum(jnp.where(onehots[k], before, 0.0), axis=-1, keepdims=True)
        meta = jnp.where(lane == k, ids[k], meta)
        meta = jnp.where(lane == TOP_K + k, rank, meta)
        gate = jnp.where(lane == k, es[k] * inv, gate)
    meta_ref[...] = meta.astype(jnp.int32)
    gate_ref[...] = gate
    total = carry_ref[...] + jnp.sum(sel, axis=0, keepdims=True)
    carry_ref[...] = total
    cnt_ref[...] = total


def _router(x1, g_moe, w_router, b_router):
    n = x1.shape[0]
    wr = jnp.zeros((D_MODEL, LANES), F32).at[:, :N_EXPERTS].set(w_router.astype(F32))
    wr_hi = wr.astype(BF16)
    wr_lo = (wr - wr_hi.astype(F32)).astype(BF16)
    br = jnp.zeros((1, LANES), F32).at[0, :N_EXPERTS].set(b_router.astype(F32))
    r = jnp.arange(ROUTER_TM, dtype=jnp.int32)
    tri = jnp.where(r[None, :] < r[:, None], 1.0, 0.0).astype(BF16)
    row = lambda w: pl.BlockSpec((ROUTER_TM, w), lambda i: (i, 0))
    return pl.pallas_call(
        _router_kernel,
        grid=(n // ROUTER_TM,),
        in_specs=[row(D_MODEL),
                  pl.BlockSpec((1, D_MODEL), lambda i: (0, 0)),
                  pl.BlockSpec((2, D_MODEL, LANES), lambda i: (0, 0, 0)),
                  pl.BlockSpec((1, LANES), lambda i: (0, 0)),
                  pl.BlockSpec((ROUTER_TM, ROUTER_TM), lambda i: (0, 0))],
        out_specs=[row(D_HALF), row(LANES), row(LANES), pl.BlockSpec((8, LANES), lambda i: (0, 0))],
        out_shape=[jax.ShapeDtypeStruct((n, D_HALF), jnp.int32),
                   jax.ShapeDtypeStruct((n, LANES), jnp.int32),
                   jax.ShapeDtypeStruct((n, LANES), F32),
                   jax.ShapeDtypeStruct((8, LANES), F32)],
        scratch_shapes=[pltpu.VMEM((8, LANES), F32)],
        compiler_params=_cparams(("arbitrary",)),
        name="router",
    )(x1, g_moe.reshape(1, D_MODEL), jnp.stack([wr_hi, wr_lo]), br, tri)


def _dispatch(h2, dest, n_slots):
    n = h2.shape[0]
    per_w = n // SC_WORKERS
    nch = per_w // SC_SCATTER_ROWS
    nslab = h2.shape[1] // SC_SCATTER_COLS
    nj = nch * nslab
    idx = dest.T.reshape(TOP_K, SC_WORKERS, nch, SC_SCATTER_ROWS).transpose(1, 2, 0, 3)
    idx = idx.reshape(SC_WORKERS * nch * TOP_K, SC_SCATTER_ROWS)
    mesh = plsc.VectorSubcoreMesh(core_axis_name="core", subcore_axis_name="subcore")

    @functools.partial(
        pl.kernel, mesh=mesh,
        out_type=jax.ShapeDtypeStruct((n_slots, h2.shape[1]), h2.dtype),
        scratch_types=[pltpu.VMEM((nch * TOP_K, SC_SCATTER_ROWS), jnp.int32),
                       pltpu.VMEM((2, SC_SCATTER_ROWS, SC_SCATTER_COLS), h2.dtype),
                       pltpu.SemaphoreType.DMA((2,)), pltpu.SemaphoreType.DMA((2,))])
    def scatter_kernel(h2_hbm, idx_hbm, xs_hbm, idx_v, rows_v, sem_l, sem_s):
        wid = lax.axis_index("subcore") * SC_CORES + lax.axis_index("core")
        t0 = wid * per_w
        pltpu.sync_copy(idx_hbm.at[pl.ds(wid * (nch * TOP_K), nch * TOP_K)], idx_v)

        def load(j, b):
            rows = pl.ds(t0 + (j // nslab) * SC_SCATTER_ROWS, SC_SCATTER_ROWS)
            cols = pl.ds((j % nslab) * SC_SCATTER_COLS, SC_SCATTER_COLS)
            return pltpu.make_async_copy(h2_hbm.at[rows, cols], rows_v.at[b], sem_l.at[b])

        def scatter(j, k, b):
            cols = pl.ds((j % nslab) * SC_SCATTER_COLS, SC_SCATTER_COLS)
            return pltpu.make_async_copy(rows_v.at[b], xs_hbm.at[idx_v.at[(j // nslab) * TOP_K + k], cols], sem_s.at[b])

        load(0, 0).start()

        @pl.loop(0, nj, step=2)
        def _(j0):
            for b in (0, 1):
                j = j0 + b

                @pl.when(j >= 1)
                def _():
                    for k in range(TOP_K):
                        scatter(j - 1, k, 1 - b).wait()

                @pl.when(j + 1 < nj)
                def _():
                    load(j + 1, 1 - b).start()

                load(j, b).wait()
                for k in range(TOP_K):
                    scatter(j, k, b).start()

        for k in range(TOP_K):
            scatter(nj - 1, k, 1).wait()

    return scatter_kernel(h2, idx)


def _expert_kernel(be_ref, nu_ref, nv_ref, x_ref, wg_ref, bg_ref, wu_ref, bu_ref, wd_ref, bd_ref, y_ref):
    del be_ref
    i = pl.program_id(0)

    @pl.when(i < nu_ref[0])
    def _():
        row = lax.broadcasted_iota(jnp.int32, (EXPERT_TM, 1), 0)
        xb = _unpack_bf16_halves(jnp.where(row < nv_ref[i], x_ref[...], jnp.int32(0))).astype(BF16)
        glu = _mm(xb, wg_ref[0].astype(BF16)) + bg_ref[0]
        lin = _mm(xb, wu_ref[0].astype(BF16)) + bu_ref[0]
        glu = jnp.minimum(glu, SWIGLU_LIMIT)
        lin = jnp.clip(lin, -SWIGLU_LIMIT, SWIGLU_LIMIT)
        act = (0.5 * glu) * (1.0 + jnp.tanh((0.5 * SWIGLU_ALPHA) * glu)) * (lin + 1.0)
        y = _mm(act.astype(BF16), wd_ref[0].astype(BF16)) + bd_ref[0]
        y_ref[...] = _pack_bf16_halves(y)


def _experts(xs, block_expert, n_used, n_valid, w_gate, b_gate, w_up, b_up, w_down, b_down):
    n_slots = xs.shape[0]
    n_blocks = n_slots // EXPERT_TM
    rows = lambda i, be, nu, nv: (jnp.minimum(i, nu[0] - 1), 0)
    wspec = pl.BlockSpec((1, D_MODEL, D_MODEL), lambda i, be, nu, nv: (be[i], 0, 0))
    bspec = pl.BlockSpec((1, 1, D_MODEL), lambda i, be, nu, nv: (be[i], 0, 0))
    grid_spec = pltpu.PrefetchScalarGridSpec(
        num_scalar_prefetch=3,
        grid=(n_blocks,),
        in_specs=[pl.BlockSpec((EXPERT_TM, D_HALF), rows), wspec, bspec, wspec, bspec, wspec, bspec],
        out_specs=pl.BlockSpec((EXPERT_TM, D_HALF), rows),
    )
    b3 = lambda a: a.astype(F32).reshape(N_EXPERTS, 1, D_MODEL)
    return pl.pallas_call(
        _expert_kernel,
        grid_spec=grid_spec,
        out_shape=jax.ShapeDtypeStruct((n_slots, D_HALF), jnp.int32),
        compiler_params=_cparams(("arbitrary",)),
        name="experts",
    )(block_expert, n_used, n_valid, xs, w_gate, b3(b_gate), w_up, b3(b_up), w_down, b3(b_down))


def _sc_row_gather(table, idx):
    n_idx = idx.shape[0]
    width = table.shape[1]
    per_w = n_idx // SC_WORKERS
    nch = per_w // SC_CHUNK
    mesh = plsc.VectorSubcoreMesh(core_axis_name="core", subcore_axis_name="subcore")

    @functools.partial(
        pl.kernel, mesh=mesh,
        out_type=jax.ShapeDtypeStruct((n_idx, width), table.dtype),
        scratch_types=[pltpu.VMEM((per_w,), jnp.int32), pltpu.VMEM((2, SC_CHUNK, width), table.dtype),
                       pltpu.SemaphoreType.DMA((2,)), pltpu.SemaphoreType.DMA((2,))])
    def gather_kernel(table_hbm, idx_hbm, out_hbm, idx_v, rows_v, sem_g, sem_p):
        wid = lax.axis_index("subcore") * SC_CORES + lax.axis_index("core")
        base = wid * per_w
        pltpu.sync_copy(idx_hbm.at[pl.ds(base, per_w)], idx_v)

        def gather(c, b):
            return pltpu.make_async_copy(table_hbm.at[idx_v.at[pl.ds(c * SC_CHUNK, SC_CHUNK)]], rows_v.at[b], sem_g.at[b])

        def put(c, b):
            return pltpu.make_async_copy(rows_v.at[b], out_hbm.at[pl.ds(base + c * SC_CHUNK, SC_CHUNK)], sem_p.at[b])

        gather(0, 0).start()

        @pl.loop(0, nch, step=2)
        def _(c):
            for b in (0, 1):
                cc = c + b

                @pl.when(cc >= 1)
                def _():
                    put(cc - 1, 1 - b).wait()

                @pl.when(cc + 1 < nch)
                def _():
                    gather(cc + 1, 1 - b).start()

                gather(cc, b).wait()
                put(cc, b).start()

        put(nch - 1, 1).wait()

    return gather_kernel(table, idx)


def _combine_kernel(yt_ref, gate_ref, x1_ref, g_ref, o_ref):
    gate = gate_ref[...]
    x = x1_ref[...]
    for k in range(TOP_K):
        x = x + gate[:, k:k + 1] * _unpack_bf16_halves(yt_ref[k])
    o_ref[...] = _rms(x) * g_ref[...]


def _combine(yt, gates, x1, g_final):
    n = x1.shape[0]
    yt = yt.reshape(TOP_K, n, D_HALF)
    row = lambda w: pl.BlockSpec((COMBINE_R, w), lambda i: (i, 0))
    return pl.pallas_call(
        _combine_kernel,
        grid=(n // COMBINE_R,),
        in_specs=[pl.BlockSpec((TOP_K, COMBINE_R, D_HALF), lambda i: (0, i, 0)),
                  row(LANES), row(D_MODEL),
                  pl.BlockSpec((1, D_MODEL), lambda i: (0, 0))],
        out_specs=row(D_MODEL),
        out_shape=jax.ShapeDtypeStruct((n, D_MODEL), F32),
        compiler_params=_cparams(("parallel",)),
        name="combine",
    )(yt, gates, x1, g_final.reshape(1, D_MODEL))


def _slot_layout(meta, counts_f, n):
    eid = meta[:, :TOP_K]
    rank = meta[:, TOP_K:2 * TOP_K]
    counts = counts_f[0, :N_EXPERTS].astype(jnp.int32)
    padded = (counts + EXPERT_TM - 1) // EXPERT_TM * EXPERT_TM
    pend = jnp.cumsum(padded)
    pstart = pend - padded
    experts = jnp.arange(N_EXPERTS, dtype=jnp.int32)
    dest = rank + jnp.sum(jnp.where(eid[..., None] == experts, pstart, 0), axis=-1)
    n_slots = -(-(n * TOP_K + N_EXPERTS * (EXPERT_TM - 1)) // EXPERT_TM) * EXPERT_TM
    block_start = jnp.arange(n_slots // EXPERT_TM, dtype=jnp.int32) * EXPERT_TM
    block_expert = jnp.minimum(jnp.sum(pend[None, :] <= block_start[:, None], axis=-1), N_EXPERTS - 1)
    last_valid = jnp.sum(jnp.where(block_expert[:, None] == experts, pstart + counts, 0), axis=-1)
    n_valid = jnp.clip(last_valid - block_start, 0, EXPERT_TM).astype(jnp.int32)
    n_used = (pend[-1:] // EXPERT_TM).astype(jnp.int32)
    return dest.astype(jnp.int32), block_expert.astype(jnp.int32), n_used, n_valid, n_slots


def kernel(x, g_attn, w_qkv, rel_bias, g_out_dil, g_out_sb, w_o, g_moe, w_router, b_router,
           w_gate, b_gate, w_up, b_up, w_down, b_down, g_final):
    b, s, d = x.shape
    n = b * s
    x2 = x.reshape(n, d)
    qkv, *strided = _qkv(x2, g_attn, w_qkv)
    sources = dict(zip(STRIDED_DILATIONS, strided))
    dil = [_dilated(sources.get(dilation, qkv), b, s, rel_bias, dilation) for _window, dilation in DIL_PATTERNS]
    ob = _stickbreaking(qkv.reshape(b, s, 3 * d))
    x1 = _mix([o for o, _ in dil], [l for _, l in dil], ob, x2, g_out_dil, g_out_sb, w_o)
    h2, meta, gates, counts = _router(x1, g_moe, w_router, b_router)
    dest, block_expert, n_used, n_valid, n_slots = _slot_layout(meta, counts, n)
    xs = _dispatch(h2, dest, n_slots)
    ys = _experts(xs, block_expert, n_used, n_valid, w_gate, b_gate, w_up, b_up, w_down, b_down)
    yt = _sc_row_gather(ys, dest.T.reshape(-1))
    return _combine(yt, gates, x1, g_final).reshape(b, s, d)
```

```python
import functools
import math

import jax
import jax.numpy as jnp
from jax import lax
from jax.experimental import pallas as pl
from jax.experimental.pallas import tpu as pltpu
from jax.experimental.pallas import tpu_sc as plsc

F32 = jnp.float32
BF16 = jnp.bfloat16

D_MODEL = 1024
HEAD_DIM = 64
D_HALF = 512
N_HEADS = 8
LANES = 128
N_PAIRS = D_HALF // LANES
DIL_PATTERNS = ((128, 1), (512, 4), (2048, 16))
BAND = 128
REL_BUCKETS = 32
REL_MAX_DISTANCE = 2048
N_EXPERTS = 32
TOP_K = 4
SWIGLU_ALPHA = 1.702
SWIGLU_LIMIT = 7.0
RMS_EPS = 1e-6
NEG_INF = -1e30

QKV_TM = 512
DIL_G = 4
A_COLS = 3 * D_HALF
STRIDED_DILATIONS = tuple(d for _w, d in DIL_PATTERNS if d > 1)
LOG2E = math.log2(math.e)
SB_T = 256
SB_G = 4
SB_EXP_ZERO = 104.0
MIX_TM = 512
EXPERT_TM = 512
COMBINE_R = 256
SC_CORES = 2
SC_WORKERS = SC_CORES * 16
SC_CHUNK = 64
SC_SCATTER_ROWS = 128
SC_SCATTER_COLS = 256
VMEM_LIMIT = 56 * 1024 * 1024


def _cparams(sem):
    return pltpu.CompilerParams(dimension_semantics=sem, vmem_limit_bytes=VMEM_LIMIT)


def _mm(a, b):
    return jnp.dot(a, b, preferred_element_type=F32)


def _mm_nt(a, b):
    return lax.dot_general(a, b, (((1,), (1,)), ((), ())), preferred_element_type=F32)


def _split_bf16(a):
    hi = a.astype(BF16)
    lo = (a - hi.astype(F32)).astype(BF16)
    return hi, lo


def _mm_split(a, b_bf16):
    hi, lo = _split_bf16(a)
    return _mm(hi, b_bf16) + _mm(lo, b_bf16)


def _pack_bf16_halves(x):
    c = x.shape[1] // 2
    return pltpu.pack_elementwise([x[:, :c], x[:, c:]], packed_dtype=BF16)


def _unpack_bf16_halves(p):
    return jnp.concatenate([pltpu.unpack_elementwise(p, index=i, packed_dtype=BF16, unpacked_dtype=F32)
                            for i in (0, 1)], axis=1)


def _rms(x):
    return x * lax.rsqrt(jnp.mean(x * x, axis=-1, keepdims=True) + RMS_EPS)


def _half_masks(dtype):
    lane = lax.broadcasted_iota(jnp.int32, (1, LANES), 1)
    lo = jnp.where(lane < HEAD_DIM, 1.0, 0.0).astype(dtype)
    hi = jnp.where(lane >= HEAD_DIM, 1.0, 0.0).astype(dtype)
    return lo, hi


def _qkv_kernel(x_ref, g_ref, w_ref, o_ref, *rest):
    stage_ref = rest[-1]
    h = (_rms(x_ref[...]) * g_ref[...]).astype(BF16)
    for c in range(3 * D_MODEL // D_HALF):
        y = _mm(h, w_ref[:, c * D_HALF:(c + 1) * D_HALF])
        if c == 0:
            y = y * (1.0 / math.sqrt(HEAD_DIM))
        if c == 3:
            y = y * (LOG2E / math.sqrt(HEAD_DIM))
        o_ref[:, c * D_HALF:(c + 1) * D_HALF] = y.astype(BF16)
        if c < 3:
            for p in range(N_PAIRS):
                stage_ref[c * N_PAIRS + p] = y[:, p * LANES:(p + 1) * LANES]
    for od_ref, d in zip(rest[:-1], STRIDED_DILATIONS):
        for r in range(d):
            for ch in range(A_COLS // LANES):
                col = r * A_COLS + ch * LANES
                od_ref[:, col:col + LANES] = stage_ref[ch, pl.ds(r, QKV_TM // d, stride=d), :].astype(BF16)


def _qkv(x2, g_attn, w_qkv):
    n = x2.shape[0]
    strided = [(n // d, d * A_COLS) for d in STRIDED_DILATIONS]
    return pl.pallas_call(
        _qkv_kernel,
        grid=(n // QKV_TM,),
        in_specs=[pl.BlockSpec((QKV_TM, D_MODEL), lambda i: (i, 0)),
                  pl.BlockSpec((1, D_MODEL), lambda i: (0, 0)),
                  pl.BlockSpec((D_MODEL, 3 * D_MODEL), lambda i: (0, 0))],
        out_specs=[pl.BlockSpec((QKV_TM, 3 * D_MODEL), lambda i: (i, 0))]
                  + [pl.BlockSpec((QKV_TM // d, d * A_COLS), lambda i: (i, 0)) for d in STRIDED_DILATIONS],
        out_shape=[jax.ShapeDtypeStruct((n, 3 * D_MODEL), BF16)]
                  + [jax.ShapeDtypeStruct(shape, BF16) for shape in strided],
        scratch_shapes=[pltpu.VMEM((A_COLS // LANES, QKV_TM, LANES), F32)],
        compiler_params=_cparams(("parallel",)),
        name="qkv",
    )(x2, g_attn.reshape(1, D_MODEL), w_qkv.astype(BF16))


def _t5_bucket(dist):
    max_exact = REL_BUCKETS // 2
    d_f = jnp.maximum(dist, 1).astype(jnp.float32)
    large = max_exact + (jnp.log(d_f / max_exact)
                         / math.log(REL_MAX_DISTANCE / max_exact)
                         * (REL_BUCKETS - max_exact)).astype(jnp.int32)
    large = jnp.minimum(large, REL_BUCKETS - 1)
    return jnp.where(dist < max_exact, dist, large)


def _band_bias(rel_bias, dilation):
    qi = jnp.arange(BAND, dtype=jnp.int32)[:, None]
    kj = jnp.arange(2 * BAND, dtype=jnp.int32)[None, :]
    rel = qi + BAND - kj
    ok = (rel >= 0) & (rel <= BAND)
    bucket = _t5_bucket(jnp.maximum(rel, 0) * dilation)
    table = rel_bias.astype(F32)
    bias = jnp.zeros((N_HEADS, BAND, 2 * BAND), F32)
    for bk in range(REL_BUCKETS):
        bias = jnp.where(bucket[None] == bk, table[bk][:, None, None], bias)
    return jnp.where(ok[None], bias, NEG_INF)


def _dilated_kernel(q_ref, kp_ref, kc_ref, vp_ref, vc_ref, bias_ref, o_ref, lse_ref):
    n = pl.program_id(2)
    kj = lax.broadcasted_iota(jnp.int32, (BAND, 2 * BAND), 1)
    has_prev = jnp.logical_or(n > 0, kj >= BAND)
    lane = lax.broadcasted_iota(jnp.int32, (BAND, LANES), 1)
    m_lo, m_hi = _half_masks(BF16)
    for g in range(DIL_G):
        rows = slice(g * BAND, (g + 1) * BAND)
        prev = slice((g - 1) * BAND, g * BAND)
        lse_tile = jnp.zeros((BAND, LANES), F32)
        for p in range(N_PAIRS):
            cols = slice(p * LANES, (p + 1) * LANES)
            q2 = q_ref[0, rows, cols]
            k_prev = kp_ref[0, :, cols] if g == 0 else kc_ref[0, prev, cols]
            v_prev = vp_ref[0, :, cols] if g == 0 else vc_ref[0, prev, cols]
            kcat = jnp.concatenate([k_prev, kc_ref[0, rows, cols]], axis=0)
            vcat = jnp.concatenate([v_prev, vc_ref[0, rows, cols]], axis=0)
            outs = []
            for s, msk in enumerate((m_lo, m_hi)):
                h = 2 * p + s
                logits = _mm_nt(q2 * msk, kcat) + bias_ref[h]
                if g == 0:
                    logits = jnp.where(has_prev, logits, NEG_INF)
                m = jnp.max(logits, axis=-1, keepdims=True)
                pr = jnp.exp(logits - m)
                den = jnp.sum(pr, axis=-1, keepdims=True)
                outs.append(_mm(pr.astype(BF16), vcat) / den)
                lse_tile = jnp.where(lane == h, m + jnp.log(den), lse_tile)
            o_ref[0, rows, cols] = jnp.where(lane < HEAD_DIM, outs[0], outs[1])
        lse_ref[0, rows, :] = lse_tile


def _dilated(src, b, s, rel_bias, dilation):
    d = dilation
    l = s // d
    nstep = l // (DIL_G * BAND)
    view = src.reshape(b, l, src.shape[1])
    n_sec = src.shape[1] // d // D_HALF

    def sec(k, prev):
        if prev:
            return pl.BlockSpec((1, BAND, D_HALF),
                                lambda bi, r, n: (bi, jnp.maximum(n * DIL_G - 1, 0), r * n_sec + k))
        return pl.BlockSpec((1, DIL_G * BAND, D_HALF), lambda bi, r, n: (bi, n, r * n_sec + k))

    o, lse = pl.pallas_call(
        _dilated_kernel,
        grid=(b, d, nstep),
        in_specs=[sec(0, False), sec(1, True), sec(1, False), sec(2, True), sec(2, False),
                  pl.BlockSpec((N_HEADS, BAND, 2 * BAND), lambda bi, r, n: (0, 0, 0))],
        out_specs=[pl.BlockSpec((1, DIL_G * BAND, D_HALF), lambda bi, r, n: (bi, n, r)),
                   pl.BlockSpec((1, DIL_G * BAND, LANES), lambda bi, r, n: (bi, n, r))],
        out_shape=[jax.ShapeDtypeStruct((b, l, d * D_HALF), F32),
                   jax.ShapeDtypeStruct((b, l, d * LANES), F32)],
        compiler_params=_cparams(("parallel", "parallel", "arbitrary")),
        name=f"dilated_d{d}",
    )(view, view, view, view, view, _band_bias(rel_bias, d))
    return o.reshape(b * l, d * D_HALF), lse.reshape(b * l, d * LANES)


def _sb_kernel(q_ref, k_ref, v_ref, u_ref, o_ref, acc_ref, carry_ref):
    i = pl.program_id(2)
    m_lo, m_hi = _half_masks(BF16)
    row = lax.broadcasted_iota(jnp.int32, (2 * SB_T, SB_T), 0) & (SB_T - 1)
    col = lax.broadcasted_iota(jnp.int32, (2 * SB_T, SB_T), 1)
    causal = col < row
    lane = lax.broadcasted_iota(jnp.int32, (SB_T, LANES), 1)
    u = u_ref[...]

    def tile(qs, j, mask, carry):
        start = pl.multiple_of(j * SB_T, SB_T)
        kj = k_ref[0, pl.ds(start, SB_T), :]
        vj = v_ref[0, pl.ds(start, SB_T), :]
        z = _mm_nt(qs, kj)
        neg_abs = lax.bitcast_convert_type(
            lax.bitcast_convert_type(z, jnp.uint32) | jnp.uint32(0x80000000), F32)
        sp = jnp.log2(1.0 + jnp.exp2(neg_abs))
        log_b = jnp.minimum(z, 0.0) - sp
        l1m = log_b - z
        if mask is not None:
            l1m = jnp.where(mask, l1m, 0.0)
        cum = _mm(l1m.astype(BF16), u)
        total = cum[:, 0:1] + l1m[:, 0:1]
        a = jnp.exp2(log_b + cum + carry)
        if mask is not None:
            a = jnp.where(mask, a, 0.0)
        return _mm(a.astype(BF16), vj), total

    def more(state):
        j, top = state
        return jnp.logical_and(j >= 0, top > -SB_EXP_ZERO * LOG2E)

    tops = []
    for g in range(SB_G):
        blk = i * SB_G + g
        q2 = q_ref[0, g * SB_T:(g + 1) * SB_T, :]
        qs = jnp.concatenate([q2 * m_lo, q2 * m_hi], axis=0)
        pv_d, tot_d = tile(qs, blk, causal, 0.0)
        if g == 0:
            pv_p, tot_p = tile(qs, jnp.maximum(blk - 1, 0), blk > 0, tot_d)
        else:
            pv_p, tot_p = tile(qs, blk - 1, None, tot_d)
        acc_ref[g] = pv_d + pv_p
        carry0 = tot_d + tot_p
        carry_ref[g] = carry0
        tops.append(jnp.max(carry0))

    for g in range(SB_G):
        def body(state, g=g):
            j, _ = state
            q2 = q_ref[0, g * SB_T:(g + 1) * SB_T, :]
            qs = jnp.concatenate([q2 * m_lo, q2 * m_hi], axis=0)
            pv, tot = tile(qs, j, None, carry_ref[g])
            acc_ref[g] += pv
            carry = carry_ref[g] + tot
            carry_ref[g] = carry
            return j - 1, jnp.max(carry)

        lax.while_loop(more, body, (i * SB_G + g - 2, tops[g]))
        o_ref[0, g * SB_T:(g + 1) * SB_T, :] = jnp.where(lane < HEAD_DIM, acc_ref[g, :SB_T], acc_ref[g, SB_T:])


def _suffix_matrix():
    sp = jnp.arange(SB_T, dtype=jnp.int32)[:, None]
    sc = jnp.arange(SB_T, dtype=jnp.int32)[None, :]
    return jnp.where(sp > sc, 1.0, 0.0).astype(BF16)


def _stickbreaking(qkv3):
    b, s, _ = qkv3.shape
    sec_q, sec_k, sec_v = 3 * N_PAIRS, 4 * N_PAIRS, 5 * N_PAIRS
    o = pl.pallas_call(
        _sb_kernel,
        grid=(b, N_PAIRS, s // (SB_G * SB_T)),
        in_specs=[pl.BlockSpec((1, SB_G * SB_T, LANES), lambda bi, p, i: (bi, i, sec_q + p)),
                  pl.BlockSpec((1, s, LANES), lambda bi, p, i: (bi, 0, sec_k + p)),
                  pl.BlockSpec((1, s, LANES), lambda bi, p, i: (bi, 0, sec_v + p)),
                  pl.BlockSpec((SB_T, SB_T), lambda bi, p, i: (0, 0))],
        out_specs=pl.BlockSpec((1, SB_G * SB_T, LANES), lambda bi, p, i: (bi, i, p)),
        out_shape=jax.ShapeDtypeStruct((b, s, D_HALF), F32),
        scratch_shapes=[pltpu.VMEM((SB_G, 2 * SB_T, LANES), F32), pltpu.VMEM((SB_G, 2 * SB_T, 1), F32)],
        compiler_params=_cparams(("parallel", "parallel", "arbitrary")),
        name="stickbreaking",
    )(qkv3, qkv3, qkv3, _suffix_matrix())
    return o.reshape(b * s, D_HALF)


def _token_major(ref, d, st_ref):
    if d == 1:
        return ref[...]
    width = ref.shape[1] // d
    for r in range(d):
        for ch in range(width // LANES):
            col = r * width + ch * LANES
            st_ref[ch, pl.ds(r, MIX_TM // d, stride=d), :] = ref[:, col:col + LANES]
    return jnp.concatenate([st_ref[ch] for ch in range(width // LANES)], axis=1)


def _mix_kernel(o1_ref, o2_ref, o3_ref, l1_ref, l2_ref, l3_ref, ob_ref, x_ref,
                ga_ref, gb_ref, e_ref, wo_ref, gm_ref, wr_ref, br_ref, tri_ref,
                x1_ref, h2_ref, meta_ref, gate_ref, cnt_ref, carry_ref, *scratch):
    dils = [d for _w, d in DIL_PATTERNS]
    o_st = dict(zip(STRIDED_DILATIONS, scratch[:len(STRIDED_DILATIONS)]))
    l_st = dict(zip(STRIDED_DILATIONS, scratch[len(STRIDED_DILATIONS):]))
    o1, o2, o3 = [_token_major(r, d, o_st.get(d)) for r, d in zip((o1_ref, o2_ref, o3_ref), dils)]
    l1, l2, l3 = [_token_major(r, d, l_st.get(d)) for r, d in zip((l1_ref, l2_ref, l3_ref), dils)]
    m = jnp.maximum(jnp.maximum(l1, l2), l3)
    e1, e2, e3 = jnp.exp(l1 - m), jnp.exp(l2 - m), jnp.exp(l3 - m)
    inv = 1.0 / (e1 + e2 + e3)
    expand = e_ref[...]
    oa = (_mm_split(e1 * inv, expand) * o1
          + _mm_split(e2 * inv, expand) * o2
          + _mm_split(e3 * inv, expand) * o3)
    oa = (_rms(oa) * ga_ref[...]).astype(BF16)
    ob = (_rms(ob_ref[...]) * gb_ref[...]).astype(BF16)
    mix = _mm(oa, wo_ref[:D_HALF, :]) + _mm(ob, wo_ref[D_HALF:, :])
    x1 = x_ref[...] + mix
    x1_ref[...] = x1
    _route(x1, gm_ref, wr_ref, br_ref, tri_ref, h2_ref, meta_ref, gate_ref, cnt_ref, carry_ref)


def _head_expand():
    lane = jnp.arange(LANES, dtype=jnp.int32)[:, None]
    col = jnp.arange(D_HALF, dtype=jnp.int32)[None, :]
    return jnp.where(col // HEAD_DIM == lane, 1.0, 0.0).astype(BF16)


def _mix_route(os_, lses, ob, x2, g_out_dil, g_out_sb, w_o, g_moe, w_router, b_router):
    n = x2.shape[0]
    wr = jnp.zeros((D_MODEL, LANES), F32).at[:, :N_EXPERTS].set(w_router.astype(F32))
    wr_hi = wr.astype(BF16)
    wr_lo = (wr - wr_hi.astype(F32)).astype(BF16)
    br = jnp.zeros((1, LANES), F32).at[0, :N_EXPERTS].set(b_router.astype(F32))
    r = jnp.arange(MIX_TM, dtype=jnp.int32)
    tri = jnp.where(r[None, :] < r[:, None], 1.0, 0.0).astype(BF16)
    row = lambda w: pl.BlockSpec((MIX_TM, w), lambda i: (i, 0))
    srow = lambda a: pl.BlockSpec((MIX_TM * a.shape[0] // n, a.shape[1]), lambda i: (i, 0))
    const = lambda shape: pl.BlockSpec(shape, lambda i: (0, 0))
    return pl.pallas_call(
        _mix_kernel,
        grid=(n // MIX_TM,),
        in_specs=[srow(a) for a in os_] + [srow(a) for a in lses] + [row(D_HALF), row(D_MODEL),
                  const((1, D_HALF)), const((1, D_HALF)), const((LANES, D_HALF)), const((D_MODEL, D_MODEL)),
                  const((1, D_MODEL)), const((D_MODEL, 2 * LANES)), const((1, LANES)), const((MIX_TM, MIX_TM))],
        out_specs=[row(D_MODEL), row(D_HALF), row(LANES), row(LANES), const((8, LANES))],
        out_shape=[jax.ShapeDtypeStruct((n, D_MODEL), F32),
                   jax.ShapeDtypeStruct((n, D_HALF), jnp.int32),
                   jax.ShapeDtypeStruct((n, LANES), jnp.int32),
                   jax.ShapeDtypeStruct((n, LANES), F32),
                   jax.ShapeDtypeStruct((8, LANES), F32)],
        scratch_shapes=[pltpu.VMEM((8, LANES), F32)]
                       + [pltpu.VMEM((D_HALF // LANES, MIX_TM, LANES), F32) for _ in STRIDED_DILATIONS]
                       + [pltpu.VMEM((1, MIX_TM, LANES), F32) for _ in STRIDED_DILATIONS],
        compiler_params=_cparams(("arbitrary",)),
        name="mix_route",
    )(*os_, *lses, ob, x2, g_out_dil.reshape(1, D_HALF), g_out_sb.reshape(1, D_HALF),
      _head_expand(), w_o.astype(BF16), g_moe.reshape(1, D_MODEL), jnp.concatenate([wr_hi, wr_lo], axis=1), br, tri)


def _route(x1, g_ref, wr_ref, br_ref, tri_ref, h2_ref, meta_ref, gate_ref, cnt_ref, carry_ref):
    tm = x1.shape[0]

    @pl.when(pl.program_id(0) == 0)
    def _():
        carry_ref[...] = jnp.zeros_like(carry_ref)

    h2 = _rms(x1) * g_ref[...]
    h2_ref[...] = _pack_bf16_halves(h2)
    parts = _mm(jnp.concatenate(_split_bf16(h2), axis=0), wr_ref[...])
    top, bot = parts[:tm], parts[tm:]
    logits = (top[:, :LANES] + br_ref[...]) + ((top[:, LANES:] + bot[:, :LANES]) + bot[:, LANES:])
    lane = lax.broadcasted_iota(jnp.int32, (tm, LANES), 1)
    lane_f = lane.astype(F32)
    logits = jnp.where(lane < N_EXPERTS, logits, -jnp.inf)

    onehots, vals, ids = [], [], []
    for _k in range(TOP_K):
        m = jnp.max(logits, axis=-1, keepdims=True)
        idx = jnp.min(jnp.where(logits == m, lane_f, float(LANES)), axis=-1, keepdims=True)
        oh = lane_f == idx
        logits = jnp.where(oh, -jnp.inf, logits)
        onehots.append(oh)
        vals.append(m)
        ids.append(idx)

    es = [jnp.exp(v - vals[0]) for v in vals]
    inv = 1.0 / (es[0] + es[1] + es[2] + es[3])

    sel = jnp.zeros((tm, LANES), F32)
    for oh in onehots:
        sel = jnp.where(oh, 1.0, sel)
    before = _mm(tri_ref[...], sel.astype(BF16)) + carry_ref[0:1, :]
    meta = jnp.zeros((tm, LANES), F32)
    gate = jnp.zeros((tm, LANES), F32)
    for k in range(TOP_K):
        rank = jnp.sum(jnp.where(onehots[k], before, 0.0), axis=-1, keepdims=True)
        meta = jnp.where(lane == k, ids[k], meta)
        meta = jnp.where(lane == TOP_K + k, rank, meta)
        gate = jnp.where(lane == k, es[k] * inv, gate)
    meta_ref[...] = meta.astype(jnp.int32)
    gate_ref[...] = gate
    total = carry_ref[...] + jnp.sum(sel, axis=0, keepdims=True)
    carry_ref[...] = total
    cnt_ref[...] = total


def _dispatch(h2, dest, n_slots):
    n = h2.shape[0]
    per_w = n // SC_WORKERS
    nch = per_w // SC_SCATTER_ROWS
    nslab = h2.shape[1] // SC_SCATTER_COLS
    nj = nch * nslab
    idx = dest.T.reshape(TOP_K, SC_WORKERS, nch, SC_SCATTER_ROWS).transpose(1, 2, 0, 3)
    idx = idx.reshape(SC_WORKERS * nch * TOP_K, SC_SCATTER_ROWS)
    mesh = plsc.VectorSubcoreMesh(core_axis_name="core", subcore_axis_name="subcore")

    @functools.partial(
        pl.kernel, mesh=mesh,
        out_type=jax.ShapeDtypeStruct((n_slots, h2.shape[1]), h2.dtype),
        scratch_types=[pltpu.VMEM((nch * TOP_K, SC_SCATTER_ROWS), jnp.int32),
                       pltpu.VMEM((2, SC_SCATTER_ROWS, SC_SCATTER_COLS), h2.dtype),
                       pltpu.SemaphoreType.DMA((2,)), pltpu.SemaphoreType.DMA((2,))])
    def scatter_kernel(h2_hbm, idx_hbm, xs_hbm, idx_v, rows_v, sem_l, sem_s):
        wid = lax.axis_index("subcore") * SC_CORES + lax.axis_index("core")
        t0 = wid * per_w
        pltpu.sync_copy(idx_hbm.at[pl.ds(wid * (nch * TOP_K), nch * TOP_K)], idx_v)

        def load(j, b):
            rows = pl.ds(t0 + (j // nslab) * SC_SCATTER_ROWS, SC_SCATTER_ROWS)
            cols = pl.ds((j % nslab) * SC_SCATTER_COLS, SC_SCATTER_COLS)
            return pltpu.make_async_copy(h2_hbm.at[rows, cols], rows_v.at[b], sem_l.at[b])

        def scatter(j, k, b):
            cols = pl.ds((j % nslab) * SC_SCATTER_COLS, SC_SCATTER_COLS)
            return pltpu.make_async_copy(rows_v.at[b], xs_hbm.at[idx_v.at[(j // nslab) * TOP_K + k], cols], sem_s.at[b])

        load(0, 0).start()

        @pl.loop(0, nj, step=2)
        def _(j0):
            for b in (0, 1):
                j = j0 + b

                @pl.when(j >= 1)
                def _():
                    for k in range(TOP_K):
                        scatter(j - 1, k, 1 - b).wait()

                @pl.when(j + 1 < nj)
                def _():
                    load(j + 1, 1 - b).start()

                load(j, b).wait()
                for k in range(TOP_K):
                    scatter(j, k, b).start()

        for k in range(TOP_K):
            scatter(nj - 1, k, 1).wait()

    return scatter_kernel(h2, idx)


def _expert_kernel(be_ref, nu_ref, nv_ref, x_ref, wg_ref, bg_ref, wu_ref, bu_ref, wd_ref, bd_ref, y_ref):
    del be_ref
    i = pl.program_id(0)

    @pl.when(i < nu_ref[0])
    def _():
        row = lax.broadcasted_iota(jnp.int32, (EXPERT_TM, 1), 0)
        xb = _unpack_bf16_halves(jnp.where(row < nv_ref[i], x_ref[...], jnp.int32(0))).astype(BF16)
        glu = _mm(xb, wg_ref[0].astype(BF16)) + bg_ref[0]
        lin = _mm(xb, wu_ref[0].astype(BF16)) + bu_ref[0]
        glu = jnp.minimum(glu, SWIGLU_LIMIT)
        lin = jnp.clip(lin, -SWIGLU_LIMIT, SWIGLU_LIMIT)
        act = (0.5 * glu) * (1.0 + jnp.tanh((0.5 * SWIGLU_ALPHA) * glu)) * (lin + 1.0)
        y = _mm(act.astype(BF16), wd_ref[0].astype(BF16)) + bd_ref[0]
        y_ref[...] = _pack_bf16_halves(y)


def _experts(xs, block_expert, n_used, n_valid, w_gate, b_gate, w_up, b_up, w_down, b_down):
    n_slots = xs.shape[0]
    n_blocks = n_slots // EXPERT_TM
    rows = lambda i, be, nu, nv: (jnp.minimum(i, nu[0] - 1), 0)
    wspec = pl.BlockSpec((1, D_MODEL, D_MODEL), lambda i, be, nu, nv: (be[i], 0, 0))
    bspec = pl.BlockSpec((1, 1, D_MODEL), lambda i, be, nu, nv: (be[i], 0, 0))
    grid_spec = pltpu.PrefetchScalarGridSpec(
        num_scalar_prefetch=3,
        grid=(n_blocks,),
        in_specs=[pl.BlockSpec((EXPERT_TM, D_HALF), rows), wspec, bspec, wspec, bspec, wspec, bspec],
        out_specs=pl.BlockSpec((EXPERT_TM, D_HALF), rows),
    )
    b3 = lambda a: a.astype(F32).reshape(N_EXPERTS, 1, D_MODEL)
    return pl.pallas_call(
        _expert_kernel,
        grid_spec=grid_spec,
        out_shape=jax.ShapeDtypeStruct((n_slots, D_HALF), jnp.int32),
        compiler_params=_cparams(("arbitrary",)),
        name="experts",
    )(block_expert, n_used, n_valid, xs, w_gate, b3(b_gate), w_up, b3(b_up), w_down, b3(b_down))


def _sc_row_gather(table, idx):
    n_idx = idx.shape[0]
    width = table.shape[1]
    per_w = n_idx // SC_WORKERS
    nch = per_w // SC_CHUNK
    mesh = plsc.VectorSubcoreMesh(core_axis_name="core", subcore_axis_name="subcore")

    @functools.partial(
        pl.kernel, mesh=mesh,
        out_type=jax.ShapeDtypeStruct((n_idx, width), table.dtype),
        scratch_types=[pltpu.VMEM((per_w,), jnp.int32), pltpu.VMEM((2, SC_CHUNK, width), table.dtype),
                       pltpu.SemaphoreType.DMA((2,)), pltpu.SemaphoreType.DMA((2,))])
    def gather_kernel(table_hbm, idx_hbm, out_hbm, idx_v, rows_v, sem_g, sem_p):
        wid = lax.axis_index("subcore") * SC_CORES + lax.axis_index("core")
        base = wid * per_w
        pltpu.sync_copy(idx_hbm.at[pl.ds(base, per_w)], idx_v)

        def gather(c, b):
            return pltpu.make_async_copy(table_hbm.at[idx_v.at[pl.ds(c * SC_CHUNK, SC_CHUNK)]], rows_v.at[b], sem_g.at[b])

        def put(c, b):
            return pltpu.make_async_copy(rows_v.at[b], out_hbm.at[pl.ds(base + c * SC_CHUNK, SC_CHUNK)], sem_p.at[b])

        gather(0, 0).start()

        @pl.loop(0, nch, step=2)
        def _(c):
            for b in (0, 1):
                cc = c + b

                @pl.when(cc >= 1)
                def _():
                    put(cc - 1, 1 - b).wait()

                @pl.when(cc + 1 < nch)
                def _():
                    gather(cc + 1, 1 - b).start()

                gather(cc, b).wait()
                put(cc, b).start()

        put(nch - 1, 1).wait()

    return gather_kernel(table, idx)


def _combine_kernel(yt_ref, gate_ref, x1_ref, g_ref, o_ref):
    gate = gate_ref[...]
    x = x1_ref[...]
    for k in range(TOP_K):
        x = x + gate[:, k:k + 1] * _unpack_bf16_halves(yt_ref[k])
    o_ref[...] = _rms(x) * g_ref[...]


def _combine(yt, gates, x1, g_final):
    n = x1.shape[0]
    yt = yt.reshape(TOP_K, n, D_HALF)
    row = lambda w: pl.BlockSpec((COMBINE_R, w), lambda i: (i, 0))
    return pl.pallas_call(
        _combine_kernel,
        grid=(n // COMBINE_R,),
        in_specs=[pl.BlockSpec((TOP_K, COMBINE_R, D_HALF), lambda i: (0, i, 0)),
                  row(LANES), row(D_MODEL),
                  pl.BlockSpec((1, D_MODEL), lambda i: (0, 0))],
        out_specs=row(D_MODEL),
        out_shape=jax.ShapeDtypeStruct((n, D_MODEL), F32),
        compiler_params=_cparams(("parallel",)),
        name="combine",
    )(yt, gates, x1, g_final.reshape(1, D_MODEL))


def _slot_layout(meta, counts_f, n):
    eid = meta[:, :TOP_K]
    rank = meta[:, TOP_K:2 * TOP_K]
    counts = counts_f[0, :N_EXPERTS].astype(jnp.int32)
    padded = (counts + EXPERT_TM - 1) // EXPERT_TM * EXPERT_TM
    pend = jnp.cumsum(padded)
    pstart = pend - padded
    experts = jnp.arange(N_EXPERTS, dtype=jnp.int32)
    dest = rank + jnp.sum(jnp.where(eid[..., None] == experts, pstart, 0), axis=-1)
    n_slots = -(-(n * TOP_K + N_EXPERTS * (EXPERT_TM - 1)) // EXPERT_TM) * EXPERT_TM
    block_start = jnp.arange(n_slots // EXPERT_TM, dtype=jnp.int32) * EXPERT_TM
    block_expert = jnp.minimum(jnp.sum(pend[None, :] <= block_start[:, None], axis=-1), N_EXPERTS - 1)
    last_valid = jnp.sum(jnp.where(block_expert[:, None] == experts, pstart + counts, 0), axis=-1)
    n_valid = jnp.clip(last_valid - block_start, 0, EXPERT_TM).astype(jnp.int32)
    n_used = (pend[-1:] // EXPERT_TM).astype(jnp.int32)
    return dest.astype(jnp.int32), block_expert.astype(jnp.int32), n_used, n_valid, n_slots


def kernel(x, g_attn, w_qkv, rel_bias, g_out_dil, g_out_sb, w_o, g_moe, w_router, b_router,
           w_gate, b_gate, w_up, b_up, w_down, b_down, g_final):
    b, s, d = x.shape
    n = b * s
    x2 = x.reshape(n, d)
    qkv, *strided = _qkv(x2, g_attn, w_qkv)
    sources = dict(zip(STRIDED_DILATIONS, strided))
    dil = [_dilated(sources.get(dilation, qkv), b, s, rel_bias, dilation) for _window, dilation in DIL_PATTERNS]
    ob = _stickbreaking(qkv.reshape(b, s, 3 * d))
    x1, h2, meta, gates, counts = _mix_route([o for o, _ in dil], [l for _, l in dil], ob, x2,
                                             g_out_dil, g_out_sb, w_o, g_moe, w_router, b_router)
    dest, block_expert, n_used, n_valid, n_slots = _slot_layout(meta, counts, n)
    xs = _dispatch(h2, dest, n_slots)
    ys = _experts(xs, block_expert, n_used, n_valid, w_gate, b_gate, w_up, b_up, w_down, b_down)
    yt = _sc_row_gather(ys, dest.T.reshape(-1))
    return _combine(yt, gates, x1, g_final).reshape(b, s, d)
```

```python
import functools
import math

import jax
import jax.numpy as jnp
from jax import lax
from jax.experimental import pallas as pl
from jax.experimental.pallas import tpu as pltpu
from jax.experimental.pallas import tpu_sc as plsc

F32 = jnp.float32
BF16 = jnp.bfloat16

D_MODEL = 1024
HEAD_DIM = 64
D_HALF = 512
N_HEADS = 8
LANES = 128
N_PAIRS = D_HALF // LANES
DIL_PATTERNS = ((128, 1), (512, 4), (2048, 16))
BAND = 128
REL_BUCKETS = 32
REL_MAX_DISTANCE = 2048
N_EXPERTS = 32
TOP_K = 4
SWIGLU_ALPHA = 1.702
SWIGLU_LIMIT = 7.0
RMS_EPS = 1e-6
NEG_INF = -1e30

QKV_TM = 512
DIL_G = 8
A_COLS = 3 * D_HALF
STRIDED_DILATIONS = tuple(d for _w, d in DIL_PATTERNS if d > 1)
LOG2E = math.log2(math.e)
SB_T = 256
SB_G = 8
SB_EXP_ZERO = 104.0
MIX_TM = 512
EXPERT_TM = 512
COMBINE_R = 256
SC_CORES = 2
SC_WORKERS = SC_CORES * 16
SC_CHUNK = 64
SC_SCATTER_ROWS = 128
SC_SCATTER_COLS = 256
VMEM_LIMIT = 56 * 1024 * 1024


def _cparams(sem):
    return pltpu.CompilerParams(dimension_semantics=sem, vmem_limit_bytes=VMEM_LIMIT)


def _mm(a, b):
    return jnp.dot(a, b, preferred_element_type=F32)


def _mm_nt(a, b):
    return lax.dot_general(a, b, (((1,), (1,)), ((), ())), preferred_element_type=F32)


def _split_bf16(a):
    hi = a.astype(BF16)
    lo = (a - hi.astype(F32)).astype(BF16)
    return hi, lo


def _mm_split(a, b_bf16):
    hi, lo = _split_bf16(a)
    return _mm(hi, b_bf16) + _mm(lo, b_bf16)


def _pack_bf16_halves(x):
    c = x.shape[1] // 2
    return pltpu.pack_elementwise([x[:, :c], x[:, c:]], packed_dtype=BF16)


def _unpack_bf16_halves(p):
    return jnp.concatenate([pltpu.unpack_elementwise(p, index=i, packed_dtype=BF16, unpacked_dtype=F32)
                            for i in (0, 1)], axis=1)


def _rms(x):
    return x * lax.rsqrt(jnp.mean(x * x, axis=-1, keepdims=True) + RMS_EPS)


def _half_masks(dtype):
    lane = lax.broadcasted_iota(jnp.int32, (1, LANES), 1)
    lo = jnp.where(lane < HEAD_DIM, 1.0, 0.0).astype(dtype)
    hi = jnp.where(lane >= HEAD_DIM, 1.0, 0.0).astype(dtype)
    return lo, hi


def _qkv_kernel(x_ref, g_ref, w_ref, o_ref, *rest):
    stage_ref = rest[-1]
    h = (_rms(x_ref[...]) * g_ref[...]).astype(BF16)
    for c in range(3 * D_MODEL // D_HALF):
        y = _mm(h, w_ref[:, c * D_HALF:(c + 1) * D_HALF])
        if c == 0:
            y = y * (1.0 / math.sqrt(HEAD_DIM))
        if c == 3:
            y = y * (LOG2E / math.sqrt(HEAD_DIM))
        o_ref[:, c * D_HALF:(c + 1) * D_HALF] = y.astype(BF16)
        if c < 3:
            for p in range(N_PAIRS):
                stage_ref[c * N_PAIRS + p] = y[:, p * LANES:(p + 1) * LANES]
    for od_ref, d in zip(rest[:-1], STRIDED_DILATIONS):
        for r in range(d):
            for ch in range(A_COLS // LANES):
                col = r * A_COLS + ch * LANES
                od_ref[:, col:col + LANES] = stage_ref[ch, pl.ds(r, QKV_TM // d, stride=d), :].astype(BF16)


def _qkv(x2, g_attn, w_qkv):
    n = x2.shape[0]
    strided = [(n // d, d * A_COLS) for d in STRIDED_DILATIONS]
    return pl.pallas_call(
        _qkv_kernel,
        grid=(n // QKV_TM,),
        in_specs=[pl.BlockSpec((QKV_TM, D_MODEL), lambda i: (i, 0)),
                  pl.BlockSpec((1, D_MODEL), lambda i: (0, 0)),
                  pl.BlockSpec((D_MODEL, 3 * D_MODEL), lambda i: (0, 0))],
        out_specs=[pl.BlockSpec((QKV_TM, 3 * D_MODEL), lambda i: (i, 0))]
                  + [pl.BlockSpec((QKV_TM // d, d * A_COLS), lambda i: (i, 0)) for d in STRIDED_DILATIONS],
        out_shape=[jax.ShapeDtypeStruct((n, 3 * D_MODEL), BF16)]
                  + [jax.ShapeDtypeStruct(shape, BF16) for shape in strided],
        scratch_shapes=[pltpu.VMEM((A_COLS // LANES, QKV_TM, LANES), F32)],
        compiler_params=_cparams(("parallel",)),
        name="qkv",
    )(x2, g_attn.reshape(1, D_MODEL), w_qkv.astype(BF16))


def _t5_bucket(dist):
    max_exact = REL_BUCKETS // 2
    d_f = jnp.maximum(dist, 1).astype(jnp.float32)
    large = max_exact + (jnp.log(d_f / max_exact)
                         / math.log(REL_MAX_DISTANCE / max_exact)
                         * (REL_BUCKETS - max_exact)).astype(jnp.int32)
    large = jnp.minimum(large, REL_BUCKETS - 1)
    return jnp.where(dist < max_exact, dist, large)


def _band_bias(rel_bias, dilation):
    qi = jnp.arange(BAND, dtype=jnp.int32)[:, None]
    kj = jnp.arange(2 * BAND, dtype=jnp.int32)[None, :]
    rel = qi + BAND - kj
    ok = (rel >= 0) & (rel <= BAND)
    bucket = _t5_bucket(jnp.maximum(rel, 0) * dilation)
    table = rel_bias.astype(F32)
    bias = jnp.zeros((N_HEADS, BAND, 2 * BAND), F32)
    for bk in range(REL_BUCKETS):
        bias = jnp.where(bucket[None] == bk, table[bk][:, None, None], bias)
    return jnp.where(ok[None], bias, NEG_INF)


def _dilated_kernel(q_ref, kp_ref, kc_ref, vp_ref, vc_ref, bias_ref, o_ref, lse_ref, *, n_blocks):
    n = pl.program_id(2)
    kj = lax.broadcasted_iota(jnp.int32, (BAND, 2 * BAND), 1)
    has_prev = jnp.logical_or(n > 0, kj >= BAND)
    lane = lax.broadcasted_iota(jnp.int32, (BAND, LANES), 1)
    m_lo, m_hi = _half_masks(BF16)
    for g in range(n_blocks):
        rows = slice(g * BAND, (g + 1) * BAND)
        prev = slice((g - 1) * BAND, g * BAND)
        lse_tile = jnp.zeros((BAND, LANES), F32)
        for p in range(N_PAIRS):
            cols = slice(p * LANES, (p + 1) * LANES)
            q2 = q_ref[0, rows, cols]
            k_prev = kp_ref[0, :, cols] if g == 0 else kc_ref[0, prev, cols]
            v_prev = vp_ref[0, :, cols] if g == 0 else vc_ref[0, prev, cols]
            kcat = jnp.concatenate([k_prev, kc_ref[0, rows, cols]], axis=0)
            vcat = jnp.concatenate([v_prev, vc_ref[0, rows, cols]], axis=0)
            outs = []
            for s, msk in enumerate((m_lo, m_hi)):
                h = 2 * p + s
                logits = _mm_nt(q2 * msk, kcat) + bias_ref[h]
                if g == 0:
                    logits = jnp.where(has_prev, logits, NEG_INF)
                m = jnp.max(logits, axis=-1, keepdims=True)
                pr = jnp.exp(logits - m)
                den = jnp.sum(pr, axis=-1, keepdims=True)
                outs.append(_mm(pr.astype(BF16), vcat))
                lse_tile = jnp.where(lane == h, m, lse_tile)
                lse_tile = jnp.where(lane == N_HEADS + h, den, lse_tile)
            o_ref[0, rows, cols] = jnp.where(lane < HEAD_DIM, outs[0], outs[1])
        lse_ref[0, rows, :] = lse_tile


def _dilated(src, b, s, rel_bias, dilation):
    d = dilation
    l = s // d
    g_blocks = min(DIL_G, l // BAND)
    nstep = l // (g_blocks * BAND)
    view = src.reshape(b, l, src.shape[1])
    n_sec = src.shape[1] // d // D_HALF

    def sec(k, prev):
        if prev:
            return pl.BlockSpec((1, BAND, D_HALF),
                                lambda bi, r, n: (bi, jnp.maximum(n * g_blocks - 1, 0), r * n_sec + k))
        return pl.BlockSpec((1, g_blocks * BAND, D_HALF), lambda bi, r, n: (bi, n, r * n_sec + k))

    o, lse = pl.pallas_call(
        functools.partial(_dilated_kernel, n_blocks=g_blocks),
        grid=(b, d, nstep),
        in_specs=[sec(0, False), sec(1, True), sec(1, False), sec(2, True), sec(2, False),
                  pl.BlockSpec((N_HEADS, BAND, 2 * BAND), lambda bi, r, n: (0, 0, 0))],
        out_specs=[pl.BlockSpec((1, g_blocks * BAND, D_HALF), lambda bi, r, n: (bi, n, r)),
                   pl.BlockSpec((1, g_blocks * BAND, LANES), lambda bi, r, n: (bi, n, r))],
        out_shape=[jax.ShapeDtypeStruct((b, l, d * D_HALF), F32),
                   jax.ShapeDtypeStruct((b, l, d * LANES), F32)],
        compiler_params=_cparams(("parallel", "parallel", "arbitrary")),
        name=f"dilated_d{d}",
    )(view, view, view, view, view, _band_bias(rel_bias, d))
    return o.reshape(b * l, d * D_HALF), lse.reshape(b * l, d * LANES)


def _sb_kernel(q_ref, k_ref, v_ref, u_ref, o_ref, acc_ref, carry_ref):
    i = pl.program_id(2)
    m_lo, m_hi = _half_masks(BF16)
    row = lax.broadcasted_iota(jnp.int32, (2 * SB_T, SB_T), 0) & (SB_T - 1)
    col = lax.broadcasted_iota(jnp.int32, (2 * SB_T, SB_T), 1)
    causal = col < row
    lane = lax.broadcasted_iota(jnp.int32, (SB_T, LANES), 1)
    u = u_ref[...]

    def tile(qs, j, mask, carry):
        start = pl.multiple_of(j * SB_T, SB_T)
        kj = k_ref[0, pl.ds(start, SB_T), :]
        vj = v_ref[0, pl.ds(start, SB_T), :]
        z = _mm_nt(qs, kj)
        neg_abs = lax.bitcast_convert_type(
            lax.bitcast_convert_type(z, jnp.uint32) | jnp.uint32(0x80000000), F32)
        sp = jnp.log2(1.0 + jnp.exp2(neg_abs))
        log_b = jnp.minimum(z, 0.0) - sp
        l1m = log_b - z
        if mask is not None:
            l1m = jnp.where(mask, l1m, 0.0)
        cum = _mm(l1m.astype(BF16), u)
        total = cum[:, 0:1] + l1m[:, 0:1]
        a = jnp.exp2(log_b + cum + carry)
        if mask is not None:
            a = jnp.where(mask, a, 0.0)
        return _mm(a.astype(BF16), vj), total

    def more(state):
        j, top = state
        return jnp.logical_and(j >= 0, top > -SB_EXP_ZERO * LOG2E)

    tops = []
    for g in range(SB_G):
        blk = i * SB_G + g
        q2 = q_ref[0, g * SB_T:(g + 1) * SB_T, :]
        qs = jnp.concatenate([q2 * m_lo, q2 * m_hi], axis=0)
        pv_d, tot_d = tile(qs, blk, causal, 0.0)
        if g == 0:
            pv_p, tot_p = tile(qs, jnp.maximum(blk - 1, 0), blk > 0, tot_d)
        else:
            pv_p, tot_p = tile(qs, blk - 1, None, tot_d)
        acc_ref[g] = pv_d + pv_p
        carry0 = tot_d + tot_p
        carry_ref[g] = carry0
        tops.append(jnp.max(carry0))

    for g in range(SB_G):
        def body(state, g=g):
            j, _ = state
            q2 = q_ref[0, g * SB_T:(g + 1) * SB_T, :]
            qs = jnp.concatenate([q2 * m_lo, q2 * m_hi], axis=0)
            pv, tot = tile(qs, j, None, carry_ref[g])
            acc_ref[g] += pv
            carry = carry_ref[g] + tot
            carry_ref[g] = carry
            return j - 1, jnp.max(carry)

        lax.while_loop(more, body, (i * SB_G + g - 2, tops[g]))
        o_ref[0, g * SB_T:(g + 1) * SB_T, :] = jnp.where(lane < HEAD_DIM, acc_ref[g, :SB_T], acc_ref[g, SB_T:])


def _suffix_matrix():
    sp = jnp.arange(SB_T, dtype=jnp.int32)[:, None]
    sc = jnp.arange(SB_T, dtype=jnp.int32)[None, :]
    return jnp.where(sp > sc, 1.0, 0.0).astype(BF16)


def _stickbreaking(qkv3):
    b, s, _ = qkv3.shape
    sec_q, sec_k, sec_v = 3 * N_PAIRS, 4 * N_PAIRS, 5 * N_PAIRS
    o = pl.pallas_call(
        _sb_kernel,
        grid=(b, N_PAIRS, s // (SB_G * SB_T)),
        in_specs=[pl.BlockSpec((1, SB_G * SB_T, LANES), lambda bi, p, i: (bi, i, sec_q + p)),
                  pl.BlockSpec((1, s, LANES), lambda bi, p, i: (bi, 0, sec_k + p)),
                  pl.BlockSpec((1, s, LANES), lambda bi, p, i: (bi, 0, sec_v + p)),
                  pl.BlockSpec((SB_T, SB_T), lambda bi, p, i: (0, 0))],
        out_specs=pl.BlockSpec((1, SB_G * SB_T, LANES), lambda bi, p, i: (bi, i, p)),
        out_shape=jax.ShapeDtypeStruct((b, s, D_HALF), F32),
        scratch_shapes=[pltpu.VMEM((SB_G, 2 * SB_T, LANES), F32), pltpu.VMEM((SB_G, 2 * SB_T, 1), F32)],
        compiler_params=_cparams(("parallel", "parallel", "arbitrary")),
        name="stickbreaking",
    )(qkv3, qkv3, qkv3, _suffix_matrix())
    return o.reshape(b * s, D_HALF)


def _token_major(ref, d, st_ref):
    if d == 1:
        return ref[...]
    width = ref.shape[1] // d
    for r in range(d):
        for ch in range(width // LANES):
            col = r * width + ch * LANES
            st_ref[ch, pl.ds(r, MIX_TM // d, stride=d), :] = ref[:, col:col + LANES]
    return jnp.concatenate([st_ref[ch] for ch in range(width // LANES)], axis=1)


def _mix_kernel(o1_ref, o2_ref, o3_ref, l1_ref, l2_ref, l3_ref, ob_ref, x_ref,
                ga_ref, gb_ref, e_ref, wo_ref, gm_ref, wr_ref, br_ref, tri_ref,
                x1_ref, h2_ref, meta_ref, gate_ref, cnt_ref, carry_ref, *scratch):
    dils = [d for _w, d in DIL_PATTERNS]
    o_st = dict(zip(STRIDED_DILATIONS, scratch[:len(STRIDED_DILATIONS)]))
    l_st = dict(zip(STRIDED_DILATIONS, scratch[len(STRIDED_DILATIONS):]))
    o1, o2, o3 = [_token_major(r, d, o_st.get(d)) for r, d in zip((o1_ref, o2_ref, o3_ref), dils)]
    l1, l2, l3 = [_token_major(r, d, l_st.get(d)) for r, d in zip((l1_ref, l2_ref, l3_ref), dils)]
    lane = lax.broadcasted_iota(jnp.int32, l1.shape, 1)
    m = jnp.maximum(jnp.maximum(l1, l2), l3)
    e1, e2, e3 = jnp.exp(l1 - m), jnp.exp(l2 - m), jnp.exp(l3 - m)
    s1, s2, s3 = [pltpu.roll(l, LANES - N_HEADS, 1) for l in (l1, l2, l3)]
    inv = 1.0 / (e1 * s1 + e2 * s2 + e3 * s3)
    w1, w2, w3 = [jnp.where(lane < N_HEADS, e * inv, 0.0) for e in (e1, e2, e3)]
    expand = e_ref[...]
    oa = (_mm_split(w1, expand) * o1
          + _mm_split(w2, expand) * o2
          + _mm_split(w3, expand) * o3)
    oa = (_rms(oa) * ga_ref[...]).astype(BF16)
    ob = (_rms(ob_ref[...]) * gb_ref[...]).astype(BF16)
    mix = _mm(oa, wo_ref[:D_HALF, :]) + _mm(ob, wo_ref[D_HALF:, :])
    x1 = x_ref[...] + mix
    x1_ref[...] = x1
    _route(x1, gm_ref, wr_ref, br_ref, tri_ref, h2_ref, meta_ref, gate_ref, cnt_ref, carry_ref)


def _head_expand():
    lane = jnp.arange(LANES, dtype=jnp.int32)[:, None]
    col = jnp.arange(D_HALF, dtype=jnp.int32)[None, :]
    return jnp.where(col // HEAD_DIM == lane, 1.0, 0.0).astype(BF16)


def _mix_route(os_, lses, ob, x2, g_out_dil, g_out_sb, w_o, g_moe, w_router, b_router):
    n = x2.shape[0]
    wr = jnp.zeros((D_MODEL, LANES), F32).at[:, :N_EXPERTS].set(w_router.astype(F32))
    wr_hi = wr.astype(BF16)
    wr_lo = (wr - wr_hi.astype(F32)).astype(BF16)
    br = jnp.zeros((1, LANES), F32).at[0, :N_EXPERTS].set(b_router.astype(F32))
    r = jnp.arange(MIX_TM, dtype=jnp.int32)
    tri = jnp.where(r[None, :] < r[:, None], 1.0, 0.0).astype(BF16)
    row = lambda w: pl.BlockSpec((MIX_TM, w), lambda i: (i, 0))
    srow = lambda a: pl.BlockSpec((MIX_TM * a.shape[0] // n, a.shape[1]), lambda i: (i, 0))
    const = lambda shape: pl.BlockSpec(shape, lambda i: (0, 0))
    return pl.pallas_call(
        _mix_kernel,
        grid=(n // MIX_TM,),
        in_specs=[srow(a) for a in os_] + [srow(a) for a in lses] + [row(D_HALF), row(D_MODEL),
                  const((1, D_HALF)), const((1, D_HALF)), const((LANES, D_HALF)), const((D_MODEL, D_MODEL)),
                  const((1, D_MODEL)), const((D_MODEL, 2 * LANES)), const((1, LANES)), const((MIX_TM, MIX_TM))],
        out_specs=[row(D_MODEL), row(D_HALF), row(LANES), row(LANES), const((8, LANES))],
        out_shape=[jax.ShapeDtypeStruct((n, D_MODEL), F32),
                   jax.ShapeDtypeStruct((n, D_HALF), jnp.int32),
                   jax.ShapeDtypeStruct((n, LANES), jnp.int32),
                   jax.ShapeDtypeStruct((n, LANES), F32),
                   jax.ShapeDtypeStruct((8, LANES), F32)],
        scratch_shapes=[pltpu.VMEM((8, LANES), F32)]
                       + [pltpu.VMEM((D_HALF // LANES, MIX_TM, LANES), F32) for _ in STRIDED_DILATIONS]
                       + [pltpu.VMEM((1, MIX_TM, LANES), F32) for _ in STRIDED_DILATIONS],
        compiler_params=_cparams(("arbitrary",)),
        name="mix_route",
    )(*os_, *lses, ob, x2, g_out_dil.reshape(1, D_HALF), g_out_sb.reshape(1, D_HALF),
      _head_expand(), w_o.astype(BF16), g_moe.reshape(1, D_MODEL), jnp.concatenate([wr_hi, wr_lo], axis=1), br, tri)


def _route(x1, g_ref, wr_ref, br_ref, tri_ref, h2_ref, meta_ref, gate_ref, cnt_ref, carry_ref):
    tm = x1.shape[0]

    @pl.when(pl.program_id(0) == 0)
    def _():
        carry_ref[...] = jnp.zeros_like(carry_ref)

    h2 = _rms(x1) * g_ref[...]
    h2_ref[...] = _pack_bf16_halves(h2)
    parts = _mm(jnp.concatenate(_split_bf16(h2), axis=0), wr_ref[...])
    top, bot = parts[:tm], parts[tm:]
    logits = (top[:, :LANES] + br_ref[...]) + ((top[:, LANES:] + bot[:, :LANES]) + bot[:, LANES:])
    lane = lax.broadcasted_iota(jnp.int32, (tm, LANES), 1)
    lane_f = lane.astype(F32)
    logits = jnp.where(lane < N_EXPERTS, logits, -jnp.inf)

    onehots, vals, ids = [], [], []
    for _k in range(TOP_K):
        m = jnp.max(logits, axis=-1, keepdims=True)
        idx = jnp.min(jnp.where(logits == m, lane_f, float(LANES)), axis=-1, keepdims=True)
        oh = lane_f == idx
        logits = jnp.where(oh, -jnp.inf, logits)
        onehots.append(oh)
        vals.append(m)
        ids.append(idx)

    es = [jnp.exp(v - vals[0]) for v in vals]
    inv = 1.0 / (es[0] + es[1] + es[2] + es[3])

    sel = jnp.zeros((tm, LANES), F32)
    for oh in onehots:
        sel = jnp.where(oh, 1.0, sel)
    before = _mm(tri_ref[...], sel.astype(BF16)) + carry_ref[0:1, :]
    meta = jnp.zeros((tm, LANES), F32)
    gate = jnp.zeros((tm, LANES), F32)
    for k in range(TOP_K):
        rank = jnp.sum(jnp.where(onehots[k], before, 0.0), axis=-1, keepdims=True)
        meta = jnp.where(lane == k, ids[k], meta)
        meta = jnp.where(lane == TOP_K + k, rank, meta)
        gate = jnp.where(lane == k, es[k] * inv, gate)
    meta_ref[...] = meta.astype(jnp.int32)
    gate_ref[...] = gate
    total = carry_ref[...] + jnp.sum(sel, axis=0, keepdims=True)
    carry_ref[...] = total
    cnt_ref[...] = total


def _dispatch(h2, dest, n_slots):
    n = h2.shape[0]
    per_w = n // SC_WORKERS
    nch = per_w // SC_SCATTER_ROWS
    nslab = h2.shape[1] // SC_SCATTER_COLS
    nj = nch * nslab
    idx = dest.T.reshape(TOP_K, SC_WORKERS, nch, SC_SCATTER_ROWS).transpose(1, 2, 0, 3)
    idx = idx.reshape(SC_WORKERS * nch * TOP_K, SC_SCATTER_ROWS)
    mesh = plsc.VectorSubcoreMesh(core_axis_name="core", subcore_axis_name="subcore")

    @functools.partial(
        pl.kernel, mesh=mesh,
        out_type=jax.ShapeDtypeStruct((n_slots, h2.shape[1]), h2.dtype),
        scratch_types=[pltpu.VMEM((nch * TOP_K, SC_SCATTER_ROWS), jnp.int32),
                       pltpu.VMEM((2, SC_SCATTER_ROWS, SC_SCATTER_COLS), h2.dtype),
                       pltpu.SemaphoreType.DMA((2,)), pltpu.SemaphoreType.DMA((2,))])
    def scatter_kernel(h2_hbm, idx_hbm, xs_hbm, idx_v, rows_v, sem_l, sem_s):
        wid = lax.axis_index("subcore") * SC_CORES + lax.axis_index("core")
        t0 = wid * per_w
        pltpu.sync_copy(idx_hbm.at[pl.ds(wid * (nch * TOP_K), nch * TOP_K)], idx_v)

        def load(j, b):
            rows = pl.ds(t0 + (j // nslab) * SC_SCATTER_ROWS, SC_SCATTER_ROWS)
            cols = pl.ds((j % nslab) * SC_SCATTER_COLS, SC_SCATTER_COLS)
            return pltpu.make_async_copy(h2_hbm.at[rows, cols], rows_v.at[b], sem_l.at[b])

        def scatter(j, k, b):
            cols = pl.ds((j % nslab) * SC_SCATTER_COLS, SC_SCATTER_COLS)
            return pltpu.make_async_copy(rows_v.at[b], xs_hbm.at[idx_v.at[(j // nslab) * TOP_K + k], cols], sem_s.at[b])

        load(0, 0).start()

        @pl.loop(0, nj, step=2)
        def _(j0):
            for b in (0, 1):
                j = j0 + b

                @pl.when(j >= 1)
                def _():
                    for k in range(TOP_K):
                        scatter(j - 1, k, 1 - b).wait()

                @pl.when(j + 1 < nj)
                def _():
                    load(j + 1, 1 - b).start()

                load(j, b).wait()
                for k in range(TOP_K):
                    scatter(j, k, b).start()

        for k in range(TOP_K):
            scatter(nj - 1, k, 1).wait()

    return scatter_kernel(h2, idx)


def _expert_kernel(be_ref, nu_ref, nv_ref, x_ref, wg_ref, bg_ref, wu_ref, bu_ref, wd_ref, bd_ref, y_ref):
    del be_ref
    i = pl.program_id(0)

    @pl.when(i < nu_ref[0])
    def _():
        row = lax.broadcasted_iota(jnp.int32, (EXPERT_TM, 1), 0)
        xb = _unpack_bf16_halves(jnp.where(row < nv_ref[i], x_ref[...], jnp.int32(0))).astype(BF16)
        glu = _mm(xb, wg_ref[0].astype(BF16)) + bg_ref[0]
        lin = _mm(xb, wu_ref[0].astype(BF16)) + bu_ref[0]
        glu = jnp.minimum(glu, SWIGLU_LIMIT)
        lin = jnp.clip(lin, -SWIGLU_LIMIT, SWIGLU_LIMIT)
        act = (0.5 * glu) * (1.0 + jnp.tanh((0.5 * SWIGLU_ALPHA) * glu)) * (lin + 1.0)
        y = _mm(act.astype(BF16), wd_ref[0].astype(BF16)) + bd_ref[0]
        y_ref[...] = _pack_bf16_halves(y)


def _experts(xs, block_expert, n_used, n_valid, w_gate, b_gate, w_up, b_up, w_down, b_down):
    n_slots = xs.shape[0]
    n_blocks = n_slots // EXPERT_TM
    rows = lambda i, be, nu, nv: (jnp.minimum(i, nu[0] - 1), 0)
    wspec = pl.BlockSpec((1, D_MODEL, D_MODEL), lambda i, be, nu, nv: (be[i], 0, 0))
    bspec = pl.BlockSpec((1, 1, D_MODEL), lambda i, be, nu, nv: (be[i], 0, 0))
    grid_spec = pltpu.PrefetchScalarGridSpec(
        num_scalar_prefetch=3,
        grid=(n_blocks,),
        in_specs=[pl.BlockSpec((EXPERT_TM, D_HALF), rows), wspec, bspec, wspec, bspec, wspec, bspec],
        out_specs=pl.BlockSpec((EXPERT_TM, D_HALF), rows),
    )
    b3 = lambda a: a.astype(F32).reshape(N_EXPERTS, 1, D_MODEL)
    return pl.pallas_call(
        _expert_kernel,
        grid_spec=grid_spec,
        out_shape=jax.ShapeDtypeStruct((n_slots, D_HALF), jnp.int32),
        compiler_params=_cparams(("arbitrary",)),
        name="experts",
    )(block_expert, n_used, n_valid, xs, w_gate, b3(b_gate), w_up, b3(b_up), w_down, b3(b_down))


def _sc_row_gather(table, idx):
    n_idx = idx.shape[0]
    width = table.shape[1]
    per_w = n_idx // SC_WORKERS
    nch = per_w // SC_CHUNK
    mesh = plsc.VectorSubcoreMesh(core_axis_name="core", subcore_axis_name="subcore")

    @functools.partial(
        pl.kernel, mesh=mesh,
        out_type=jax.ShapeDtypeStruct((n_idx, width), table.dtype),
        scratch_types=[pltpu.VMEM((per_w,), jnp.int32), pltpu.VMEM((2, SC_CHUNK, width), table.dtype),
                       pltpu.SemaphoreType.DMA((2,)), pltpu.SemaphoreType.DMA((2,))])
    def gather_kernel(table_hbm, idx_hbm, out_hbm, idx_v, rows_v, sem_g, sem_p):
        wid = lax.axis_index("subcore") * SC_CORES + lax.axis_index("core")
        base = wid * per_w
        pltpu.sync_copy(idx_hbm.at[pl.ds(base, per_w)], idx_v)

        def gather(c, b):
            return pltpu.make_async_copy(table_hbm.at[idx_v.at[pl.ds(c * SC_CHUNK, SC_CHUNK)]], rows_v.at[b], sem_g.at[b])

        def put(c, b):
            return pltpu.make_async_copy(rows_v.at[b], out_hbm.at[pl.ds(base + c * SC_CHUNK, SC_CHUNK)], sem_p.at[b])

        gather(0, 0).start()

        @pl.loop(0, nch, step=2)
        def _(c):
            for b in (0, 1):
                cc = c + b

                @pl.when(cc >= 1)
                def _():
                    put(cc - 1, 1 - b).wait()

                @pl.when(cc + 1 < nch)
                def _():
                    gather(cc + 1, 1 - b).start()

                gather(cc, b).wait()
                put(cc, b).start()

        put(nch - 1, 1).wait()

    return gather_kernel(table, idx)


def _combine_kernel(yt_ref, gate_ref, x1_ref, g_ref, o_ref):
    gate = gate_ref[...]
    x = x1_ref[...]
    for k in range(TOP_K):
        x = x + gate[:, k:k + 1] * _unpack_bf16_halves(yt_ref[k])
    o_ref[...] = _rms(x) * g_ref[...]


def _combine(yt, gates, x1, g_final):
    n = x1.shape[0]
    yt = yt.reshape(TOP_K, n, D_HALF)
    row = lambda w: pl.BlockSpec((COMBINE_R, w), lambda i: (i, 0))
    return pl.pallas_call(
        _combine_kernel,
        grid=(n // COMBINE_R,),
        in_specs=[pl.BlockSpec((TOP_K, COMBINE_R, D_HALF), lambda i: (0, i, 0)),
                  row(LANES), row(D_MODEL),
                  pl.BlockSpec((1, D_MODEL), lambda i: (0, 0))],
        out_specs=row(D_MODEL),
        out_shape=jax.ShapeDtypeStruct((n, D_MODEL), F32),
        compiler_params=_cparams(("parallel",)),
        name="combine",
    )(yt, gates, x1, g_final.reshape(1, D_MODEL))


def _slot_layout(meta, counts_f, n):
    eid = meta[:, :TOP_K]
    rank = meta[:, TOP_K:2 * TOP_K]
    counts = counts_f[0, :N_EXPERTS].astype(jnp.int32)
    padded = (counts + EXPERT_TM - 1) // EXPERT_TM * EXPERT_TM
    pend = jnp.cumsum(padded)
    pstart = pend - padded
    experts = jnp.arange(N_EXPERTS, dtype=jnp.int32)
    dest = rank + jnp.sum(jnp.where(eid[..., None] == experts, pstart, 0), axis=-1)
    n_slots = -(-(n * TOP_K + N_EXPERTS * (EXPERT_TM - 1)) // EXPERT_TM) * EXPERT_TM
    block_start = jnp.arange(n_slots // EXPERT_TM, dtype=jnp.int32) * EXPERT_TM
    block_expert = jnp.minimum(jnp.sum(pend[None, :] <= block_start[:, None], axis=-1), N_EXPERTS - 1)
    last_valid = jnp.sum(jnp.where(block_expert[:, None] == experts, pstart + counts, 0), axis=-1)
    n_valid = jnp.clip(last_valid - block_start, 0, EXPERT_TM).astype(jnp.int32)
    n_used = (pend[-1:] // EXPERT_TM).astype(jnp.int32)
    return dest.astype(jnp.int32), block_expert.astype(jnp.int32), n_used, n_valid, n_slots


def kernel(x, g_attn, w_qkv, rel_bias, g_out_dil, g_out_sb, w_o, g_moe, w_router, b_router,
           w_gate, b_gate, w_up, b_up, w_down, b_down, g_final):
    b, s, d = x.shape
    n = b * s
    x2 = x.reshape(n, d)
    qkv, *strided = _qkv(x2, g_attn, w_qkv)
    sources = dict(zip(STRIDED_DILATIONS, strided))
    dil = [_dilated(sources.get(dilation, qkv), b, s, rel_bias, dilation) for _window, dilation in DIL_PATTERNS]
    ob = _stickbreaking(qkv.reshape(b, s, 3 * d))
    x1, h2, meta, gates, counts = _mix_route([o for o, _ in dil], [l for _, l in dil], ob, x2,
                                             g_out_dil, g_out_sb, w_o, g_moe, w_router, b_router)
    dest, block_expert, n_used, n_valid, n_slots = _slot_layout(meta, counts, n)
    xs = _dispatch(h2, dest, n_slots)
    ys = _experts(xs, block_expert, n_used, n_valid, w_gate, b_gate, w_up, b_up, w_down, b_down)
    yt = _sc_row_gather(ys, dest.T.reshape(-1))
    return _combine(yt, gates, x1, g_final).reshape(b, s, d)
```

```python
import functools
import math

import jax
import jax.numpy as jnp
from jax import lax
from jax.experimental import pallas as pl
from jax.experimental.pallas import tpu as pltpu
from jax.experimental.pallas import tpu_sc as plsc

F32 = jnp.float32
BF16 = jnp.bfloat16

D_MODEL = 1024
HEAD_DIM = 64
D_HALF = 512
N_HEADS = 8
LANES = 128
N_PAIRS = D_HALF // LANES
DIL_PATTERNS = ((128, 1), (512, 4), (2048, 16))
BAND = 128
REL_BUCKETS = 32
REL_MAX_DISTANCE = 2048
N_EXPERTS = 32
TOP_K = 4
SWIGLU_ALPHA = 1.702
SWIGLU_LIMIT = 7.0
RMS_EPS = 1e-6
NEG_INF = -1e30

QKV_TM = 512
DIL_G = 8
A_COLS = 3 * D_HALF
STRIDED_DILATIONS = tuple(d for _w, d in DIL_PATTERNS if d > 1)
LOG2E = math.log2(math.e)
SB_T = 256
SB_G = 8
SB_EXP_ZERO = 104.0
MIX_TM = 512
EXPERT_TM = 1024
COMBINE_R = 512
SC_CORES = 2
SC_WORKERS = SC_CORES * 16
SC_CHUNK = 64
SC_SCATTER_ROWS = 128
SC_SCATTER_COLS = 256
VMEM_LIMIT = 56 * 1024 * 1024


def _cparams(sem):
    return pltpu.CompilerParams(dimension_semantics=sem, vmem_limit_bytes=VMEM_LIMIT)


def _mm(a, b):
    return jnp.dot(a, b, preferred_element_type=F32)


def _mm_nt(a, b):
    return lax.dot_general(a, b, (((1,), (1,)), ((), ())), preferred_element_type=F32)


def _split_bf16(a):
    hi = a.astype(BF16)
    lo = (a - hi.astype(F32)).astype(BF16)
    return hi, lo


def _mm_split(a, b_bf16):
    hi, lo = _split_bf16(a)
    return _mm(hi, b_bf16) + _mm(lo, b_bf16)


def _pack_bf16_halves(x):
    c = x.shape[1] // 2
    return pltpu.pack_elementwise([x[:, :c], x[:, c:]], packed_dtype=BF16)


def _unpack_bf16_halves(p):
    return jnp.concatenate([pltpu.unpack_elementwise(p, index=i, packed_dtype=BF16, unpacked_dtype=F32)
                            for i in (0, 1)], axis=1)


def _rms(x):
    return x * lax.rsqrt(jnp.mean(x * x, axis=-1, keepdims=True) + RMS_EPS)


def _half_masks(dtype):
    lane = lax.broadcasted_iota(jnp.int32, (1, LANES), 1)
    lo = jnp.where(lane < HEAD_DIM, 1.0, 0.0).astype(dtype)
    hi = jnp.where(lane >= HEAD_DIM, 1.0, 0.0).astype(dtype)
    return lo, hi


def _qkv_kernel(x_ref, g_ref, w_ref, o_ref, *rest):
    stage_ref = rest[-1]
    h = (_rms(x_ref[...]) * g_ref[...]).astype(BF16)
    for c in range(3 * D_MODEL // D_HALF):
        y = _mm(h, w_ref[:, c * D_HALF:(c + 1) * D_HALF])
        if c == 0:
            y = y * (1.0 / math.sqrt(HEAD_DIM))
        if c == 3:
            y = y * (LOG2E / math.sqrt(HEAD_DIM))
        o_ref[:, c * D_HALF:(c + 1) * D_HALF] = y.astype(BF16)
        if c < 3:
            for p in range(N_PAIRS):
                stage_ref[c * N_PAIRS + p] = y[:, p * LANES:(p + 1) * LANES]
    for od_ref, d in zip(rest[:-1], STRIDED_DILATIONS):
        for r in range(d):
            for ch in range(A_COLS // LANES):
                col = r * A_COLS + ch * LANES
                od_ref[:, col:col + LANES] = stage_ref[ch, pl.ds(r, QKV_TM // d, stride=d), :].astype(BF16)


def _qkv(x2, g_attn, w_qkv):
    n = x2.shape[0]
    strided = [(n // d, d * A_COLS) for d in STRIDED_DILATIONS]
    return pl.pallas_call(
        _qkv_kernel,
        grid=(n // QKV_TM,),
        in_specs=[pl.BlockSpec((QKV_TM, D_MODEL), lambda i: (i, 0)),
                  pl.BlockSpec((1, D_MODEL), lambda i: (0, 0)),
                  pl.BlockSpec((D_MODEL, 3 * D_MODEL), lambda i: (0, 0))],
        out_specs=[pl.BlockSpec((QKV_TM, 3 * D_MODEL), lambda i: (i, 0))]
                  + [pl.BlockSpec((QKV_TM // d, d * A_COLS), lambda i: (i, 0)) for d in STRIDED_DILATIONS],
        out_shape=[jax.ShapeDtypeStruct((n, 3 * D_MODEL), BF16)]
                  + [jax.ShapeDtypeStruct(shape, BF16) for shape in strided],
        scratch_shapes=[pltpu.VMEM((A_COLS // LANES, QKV_TM, LANES), F32)],
        compiler_params=_cparams(("parallel",)),
        name="qkv",
    )(x2, g_attn.reshape(1, D_MODEL), w_qkv.astype(BF16))


def _t5_bucket(dist):
    max_exact = REL_BUCKETS // 2
    d_f = jnp.maximum(dist, 1).astype(jnp.float32)
    large = max_exact + (jnp.log(d_f / max_exact)
                         / math.log(REL_MAX_DISTANCE / max_exact)
                         * (REL_BUCKETS - max_exact)).astype(jnp.int32)
    large = jnp.minimum(large, REL_BUCKETS - 1)
    return jnp.where(dist < max_exact, dist, large)


def _band_bias(rel_bias, dilation):
    qi = jnp.arange(BAND, dtype=jnp.int32)[:, None]
    kj = jnp.arange(2 * BAND, dtype=jnp.int32)[None, :]
    rel = qi + BAND - kj
    ok = (rel >= 0) & (rel <= BAND)
    bucket = _t5_bucket(jnp.maximum(rel, 0) * dilation)
    table = rel_bias.astype(F32)
    bias = jnp.zeros((N_HEADS, BAND, 2 * BAND), F32)
    for bk in range(REL_BUCKETS):
        bias = jnp.where(bucket[None] == bk, table[bk][:, None, None], bias)
    return jnp.where(ok[None], bias, NEG_INF)


def _dilated_kernel(q_ref, kp_ref, kc_ref, vp_ref, vc_ref, bias_ref, o_ref, lse_ref, *, n_blocks):
    n = pl.program_id(2)
    kj = lax.broadcasted_iota(jnp.int32, (BAND, 2 * BAND), 1)
    has_prev = jnp.logical_or(n > 0, kj >= BAND)
    lane = lax.broadcasted_iota(jnp.int32, (BAND, LANES), 1)
    m_lo, m_hi = _half_masks(BF16)
    for g in range(n_blocks):
        rows = slice(g * BAND, (g + 1) * BAND)
        prev = slice((g - 1) * BAND, g * BAND)
        lse_tile = jnp.zeros((BAND, LANES), F32)
        for p in range(N_PAIRS):
            cols = slice(p * LANES, (p + 1) * LANES)
            q2 = q_ref[0, rows, cols]
            k_prev = kp_ref[0, :, cols] if g == 0 else kc_ref[0, prev, cols]
            v_prev = vp_ref[0, :, cols] if g == 0 else vc_ref[0, prev, cols]
            kcat = jnp.concatenate([k_prev, kc_ref[0, rows, cols]], axis=0)
            vcat = jnp.concatenate([v_prev, vc_ref[0, rows, cols]], axis=0)
            outs = []
            for s, msk in enumerate((m_lo, m_hi)):
                h = 2 * p + s
                logits = _mm_nt(q2 * msk, kcat) + bias_ref[h]
                if g == 0:
                    logits = jnp.where(has_prev, logits, NEG_INF)
                m = jnp.max(logits, axis=-1, keepdims=True)
                pr = jnp.exp(logits - m)
                den = jnp.sum(pr, axis=-1, keepdims=True)
                outs.append(_mm(pr.astype(BF16), vcat))
                lse_tile = jnp.where(lane == h, m, lse_tile)
                lse_tile = jnp.where(lane == N_HEADS + h, den, lse_tile)
            o_ref[0, rows, cols] = jnp.where(lane < HEAD_DIM, outs[0], outs[1])
        lse_ref[0, rows, :] = lse_tile


def _dilated(src, b, s, rel_bias, dilation):
    d = dilation
    l = s // d
    g_blocks = min(DIL_G, l // BAND)
    nstep = l // (g_blocks * BAND)
    view = src.reshape(b, l, src.shape[1])
    n_sec = src.shape[1] // d // D_HALF

    def sec(k, prev):
        if prev:
            return pl.BlockSpec((1, BAND, D_HALF),
                                lambda bi, r, n: (bi, jnp.maximum(n * g_blocks - 1, 0), r * n_sec + k))
        return pl.BlockSpec((1, g_blocks * BAND, D_HALF), lambda bi, r, n: (bi, n, r * n_sec + k))

    o, lse = pl.pallas_call(
        functools.partial(_dilated_kernel, n_blocks=g_blocks),
        grid=(b, d, nstep),
        in_specs=[sec(0, False), sec(1, True), sec(1, False), sec(2, True), sec(2, False),
                  pl.BlockSpec((N_HEADS, BAND, 2 * BAND), lambda bi, r, n: (0, 0, 0))],
        out_specs=[pl.BlockSpec((1, g_blocks * BAND, D_HALF), lambda bi, r, n: (bi, n, r)),
                   pl.BlockSpec((1, g_blocks * BAND, LANES), lambda bi, r, n: (bi, n, r))],
        out_shape=[jax.ShapeDtypeStruct((b, l, d * D_HALF), F32),
                   jax.ShapeDtypeStruct((b, l, d * LANES), F32)],
        compiler_params=_cparams(("parallel", "parallel", "arbitrary")),
        name=f"dilated_d{d}",
    )(view, view, view, view, view, _band_bias(rel_bias, d))
    return o.reshape(b * l, d * D_HALF), lse.reshape(b * l, d * LANES)


def _sb_kernel(q_ref, k_ref, v_ref, u_ref, o_ref, acc_ref, carry_ref):
    i = pl.program_id(2)
    m_lo, m_hi = _half_masks(BF16)
    row = lax.broadcasted_iota(jnp.int32, (2 * SB_T, SB_T), 0) & (SB_T - 1)
    col = lax.broadcasted_iota(jnp.int32, (2 * SB_T, SB_T), 1)
    causal = col < row
    lane = lax.broadcasted_iota(jnp.int32, (SB_T, LANES), 1)
    u = u_ref[...]

    def tile(qs, j, mask, carry):
        start = pl.multiple_of(j * SB_T, SB_T)
        kj = k_ref[0, pl.ds(start, SB_T), :]
        vj = v_ref[0, pl.ds(start, SB_T), :]
        z = _mm_nt(qs, kj)
        neg_abs = lax.bitcast_convert_type(
            lax.bitcast_convert_type(z, jnp.uint32) | jnp.uint32(0x80000000), F32)
        sp = jnp.log2(1.0 + jnp.exp2(neg_abs))
        log_b = jnp.minimum(z, 0.0) - sp
        l1m = log_b - z
        if mask is not None:
            l1m = jnp.where(mask, l1m, 0.0)
        cum = _mm(l1m.astype(BF16), u)
        total = cum[:, 0:1] + l1m[:, 0:1]
        a = jnp.exp2(log_b + cum + carry)
        if mask is not None:
            a = jnp.where(mask, a, 0.0)
        return _mm(a.astype(BF16), vj), total

    def more(state):
        j, top = state
        return jnp.logical_and(j >= 0, top > -SB_EXP_ZERO * LOG2E)

    tops = []
    for g in range(SB_G):
        blk = i * SB_G + g
        q2 = q_ref[0, g * SB_T:(g + 1) * SB_T, :]
        qs = jnp.concatenate([q2 * m_lo, q2 * m_hi], axis=0)
        pv_d, tot_d = tile(qs, blk, causal, 0.0)
        if g == 0:
            pv_p, tot_p = tile(qs, jnp.maximum(blk - 1, 0), blk > 0, tot_d)
        else:
            pv_p, tot_p = tile(qs, blk - 1, None, tot_d)
        acc_ref[g] = pv_d + pv_p
        carry0 = tot_d + tot_p
        carry_ref[g] = carry0
        tops.append(jnp.max(carry0))

    for g in range(SB_G):
        def body(state, g=g):
            j, _ = state
            q2 = q_ref[0, g * SB_T:(g + 1) * SB_T, :]
            qs = jnp.concatenate([q2 * m_lo, q2 * m_hi], axis=0)
            pv, tot = tile(qs, j, None, carry_ref[g])
            acc_ref[g] += pv
            carry = carry_ref[g] + tot
            carry_ref[g] = carry
            return j - 1, jnp.max(carry)

        lax.while_loop(more, body, (i * SB_G + g - 2, tops[g]))
        o_ref[0, g * SB_T:(g + 1) * SB_T, :] = jnp.where(lane < HEAD_DIM, acc_ref[g, :SB_T], acc_ref[g, SB_T:])


def _suffix_matrix():
    sp = jnp.arange(SB_T, dtype=jnp.int32)[:, None]
    sc = jnp.arange(SB_T, dtype=jnp.int32)[None, :]
    return jnp.where(sp > sc, 1.0, 0.0).astype(BF16)


def _stickbreaking(qkv3):
    b, s, _ = qkv3.shape
    sec_q, sec_k, sec_v = 3 * N_PAIRS, 4 * N_PAIRS, 5 * N_PAIRS
    o = pl.pallas_call(
        _sb_kernel,
        grid=(b, N_PAIRS, s // (SB_G * SB_T)),
        in_specs=[pl.BlockSpec((1, SB_G * SB_T, LANES), lambda bi, p, i: (bi, i, sec_q + p)),
                  pl.BlockSpec((1, s, LANES), lambda bi, p, i: (bi, 0, sec_k + p)),
                  pl.BlockSpec((1, s, LANES), lambda bi, p, i: (bi, 0, sec_v + p)),
                  pl.BlockSpec((SB_T, SB_T), lambda bi, p, i: (0, 0))],
        out_specs=pl.BlockSpec((1, SB_G * SB_T, LANES), lambda bi, p, i: (bi, i, p)),
        out_shape=jax.ShapeDtypeStruct((b, s, D_HALF), F32),
        scratch_shapes=[pltpu.VMEM((SB_G, 2 * SB_T, LANES), F32), pltpu.VMEM((SB_G, 2 * SB_T, 1), F32)],
        compiler_params=_cparams(("parallel", "parallel", "arbitrary")),
        name="stickbreaking",
    )(qkv3, qkv3, qkv3, _suffix_matrix())
    return o.reshape(b * s, D_HALF)


def _token_major(ref, d, st_ref):
    if d == 1:
        return ref[...]
    width = ref.shape[1] // d
    for r in range(d):
        for ch in range(width // LANES):
            col = r * width + ch * LANES
            st_ref[ch, pl.ds(r, MIX_TM // d, stride=d), :] = ref[:, col:col + LANES]
    return jnp.concatenate([st_ref[ch] for ch in range(width // LANES)], axis=1)


def _mix_kernel(o1_ref, o2_ref, o3_ref, l1_ref, l2_ref, l3_ref, ob_ref, x_ref,
                ga_ref, gb_ref, e_ref, wo_ref, gm_ref, wr_ref, br_ref, tri_ref,
                x1_ref, h2_ref, meta_ref, gate_ref, cnt_ref, carry_ref, *scratch):
    dils = [d for _w, d in DIL_PATTERNS]
    o_st = dict(zip(STRIDED_DILATIONS, scratch[:len(STRIDED_DILATIONS)]))
    l_st = dict(zip(STRIDED_DILATIONS, scratch[len(STRIDED_DILATIONS):]))
    o1, o2, o3 = [_token_major(r, d, o_st.get(d)) for r, d in zip((o1_ref, o2_ref, o3_ref), dils)]
    l1, l2, l3 = [_token_major(r, d, l_st.get(d)) for r, d in zip((l1_ref, l2_ref, l3_ref), dils)]
    lane = lax.broadcasted_iota(jnp.int32, l1.shape, 1)
    m = jnp.maximum(jnp.maximum(l1, l2), l3)
    e1, e2, e3 = jnp.exp(l1 - m), jnp.exp(l2 - m), jnp.exp(l3 - m)
    s1, s2, s3 = [pltpu.roll(l, LANES - N_HEADS, 1) for l in (l1, l2, l3)]
    inv = 1.0 / (e1 * s1 + e2 * s2 + e3 * s3)
    w1, w2, w3 = [jnp.where(lane < N_HEADS, e * inv, 0.0) for e in (e1, e2, e3)]
    expand = e_ref[...]
    oa = (_mm_split(w1, expand) * o1
          + _mm_split(w2, expand) * o2
          + _mm_split(w3, expand) * o3)
    oa = (_rms(oa) * ga_ref[...]).astype(BF16)
    ob = (_rms(ob_ref[...]) * gb_ref[...]).astype(BF16)
    mix = _mm(oa, wo_ref[:D_HALF, :]) + _mm(ob, wo_ref[D_HALF:, :])
    x1 = x_ref[...] + mix
    x1_ref[...] = x1
    _route(x1, gm_ref, wr_ref, br_ref, tri_ref, h2_ref, meta_ref, gate_ref, cnt_ref, carry_ref)


def _head_expand():
    lane = jnp.arange(LANES, dtype=jnp.int32)[:, None]
    col = jnp.arange(D_HALF, dtype=jnp.int32)[None, :]
    return jnp.where(col // HEAD_DIM == lane, 1.0, 0.0).astype(BF16)


def _mix_route(os_, lses, ob, x2, g_out_dil, g_out_sb, w_o, g_moe, w_router, b_router):
    n = x2.shape[0]
    wr = jnp.zeros((D_MODEL, LANES), F32).at[:, :N_EXPERTS].set(w_router.astype(F32))
    wr_hi = wr.astype(BF16)
    wr_lo = (wr - wr_hi.astype(F32)).astype(BF16)
    br = jnp.zeros((1, LANES), F32).at[0, :N_EXPERTS].set(b_router.astype(F32))
    r = jnp.arange(MIX_TM, dtype=jnp.int32)
    tri = jnp.where(r[None, :] < r[:, None], 1.0, 0.0).astype(BF16)
    row = lambda w: pl.BlockSpec((MIX_TM, w), lambda i: (i, 0))
    srow = lambda a: pl.BlockSpec((MIX_TM * a.shape[0] // n, a.shape[1]), lambda i: (i, 0))
    const = lambda shape: pl.BlockSpec(shape, lambda i: (0, 0))
    return pl.pallas_call(
        _mix_kernel,
        grid=(n // MIX_TM,),
        in_specs=[srow(a) for a in os_] + [srow(a) for a in lses] + [row(D_HALF), row(D_MODEL),
                  const((1, D_HALF)), const((1, D_HALF)), const((LANES, D_HALF)), const((D_MODEL, D_MODEL)),
                  const((1, D_MODEL)), const((D_MODEL, 2 * LANES)), const((1, LANES)), const((MIX_TM, MIX_TM))],
        out_specs=[row(D_MODEL), row(D_HALF), row(LANES), row(LANES), const((8, LANES))],
        out_shape=[jax.ShapeDtypeStruct((n, D_MODEL), F32),
                   jax.ShapeDtypeStruct((n, D_HALF), jnp.int32),
                   jax.ShapeDtypeStruct((n, LANES), jnp.int32),
                   jax.ShapeDtypeStruct((n, LANES), F32),
                   jax.ShapeDtypeStruct((8, LANES), F32)],
        scratch_shapes=[pltpu.VMEM((8, LANES), F32)]
                       + [pltpu.VMEM((D_HALF // LANES, MIX_TM, LANES), F32) for _ in STRIDED_DILATIONS]
                       + [pltpu.VMEM((1, MIX_TM, LANES), F32) for _ in STRIDED_DILATIONS],
        compiler_params=_cparams(("arbitrary",)),
        name="mix_route",
    )(*os_, *lses, ob, x2, g_out_dil.reshape(1, D_HALF), g_out_sb.reshape(1, D_HALF),
      _head_expand(), w_o.astype(BF16), g_moe.reshape(1, D_MODEL), jnp.concatenate([wr_hi, wr_lo], axis=1), br, tri)


def _route(x1, g_ref, wr_ref, br_ref, tri_ref, h2_ref, meta_ref, gate_ref, cnt_ref, carry_ref):
    tm = x1.shape[0]

    @pl.when(pl.program_id(0) == 0)
    def _():
        carry_ref[...] = jnp.zeros_like(carry_ref)

    h2 = _rms(x1) * g_ref[...]
    h2_ref[...] = _pack_bf16_halves(h2)
    parts = _mm(jnp.concatenate(_split_bf16(h2), axis=0), wr_ref[...])
    top, bot = parts[:tm], parts[tm:]
    logits = (top[:, :LANES] + br_ref[...]) + ((top[:, LANES:] + bot[:, :LANES]) + bot[:, LANES:])
    lane = lax.broadcasted_iota(jnp.int32, (tm, LANES), 1)
    lane_f = lane.astype(F32)
    logits = jnp.where(lane < N_EXPERTS, logits, -jnp.inf)

    onehots, vals, ids = [], [], []
    for _k in range(TOP_K):
        m = jnp.max(logits, axis=-1, keepdims=True)
        idx = jnp.min(jnp.where(logits == m, lane_f, float(LANES)), axis=-1, keepdims=True)
        oh = lane_f == idx
        logits = jnp.where(oh, -jnp.inf, logits)
        onehots.append(oh)
        vals.append(m)
        ids.append(idx)

    es = [jnp.exp(v - vals[0]) for v in vals]
    inv = 1.0 / (es[0] + es[1] + es[2] + es[3])

    sel = jnp.zeros((tm, LANES), F32)
    for oh in onehots:
        sel = jnp.where(oh, 1.0, sel)
    before = _mm(tri_ref[...], sel.astype(BF16)) + carry_ref[0:1, :]
    meta = jnp.zeros((tm, LANES), F32)
    gate = jnp.zeros((tm, LANES), F32)
    for k in range(TOP_K):
        rank = jnp.sum(jnp.where(onehots[k], before, 0.0), axis=-1, keepdims=True)
        meta = jnp.where(lane == k, ids[k], meta)
        meta = jnp.where(lane == TOP_K + k, rank, meta)
        gate = jnp.where(lane == k, es[k] * inv, gate)
    meta_ref[...] = meta.astype(jnp.int32)
    gate_ref[...] = gate
    total = carry_ref[...] + jnp.sum(sel, axis=0, keepdims=True)
    carry_ref[...] = total
    cnt_ref[...] = total


def _dispatch(h2, dest, n_slots):
    n = h2.shape[0]
    per_w = n // SC_WORKERS
    nch = per_w // SC_SCATTER_ROWS
    nslab = h2.shape[1] // SC_SCATTER_COLS
    nj = nch * nslab
    idx = dest.T.reshape(TOP_K, SC_WORKERS, nch, SC_SCATTER_ROWS).transpose(1, 2, 0, 3)
    idx = idx.reshape(SC_WORKERS * nch * TOP_K, SC_SCATTER_ROWS)
    mesh = plsc.VectorSubcoreMesh(core_axis_name="core", subcore_axis_name="subcore")

    @functools.partial(
        pl.kernel, mesh=mesh,
        out_type=jax.ShapeDtypeStruct((n_slots, h2.shape[1]), h2.dtype),
        scratch_types=[pltpu.VMEM((nch * TOP_K, SC_SCATTER_ROWS), jnp.int32),
                       pltpu.VMEM((2, SC_SCATTER_ROWS, SC_SCATTER_COLS), h2.dtype),
                       pltpu.SemaphoreType.DMA((2,)), pltpu.SemaphoreType.DMA((2,))])
    def scatter_kernel(h2_hbm, idx_hbm, xs_hbm, idx_v, rows_v, sem_l, sem_s):
        wid = lax.axis_index("subcore") * SC_CORES + lax.axis_index("core")
        t0 = wid * per_w
        pltpu.sync_copy(idx_hbm.at[pl.ds(wid * (nch * TOP_K), nch * TOP_K)], idx_v)

        def load(j, b):
            rows = pl.ds(t0 + (j // nslab) * SC_SCATTER_ROWS, SC_SCATTER_ROWS)
            cols = pl.ds((j % nslab) * SC_SCATTER_COLS, SC_SCATTER_COLS)
            return pltpu.make_async_copy(h2_hbm.at[rows, cols], rows_v.at[b], sem_l.at[b])

        def scatter(j, k, b):
            cols = pl.ds((j % nslab) * SC_SCATTER_COLS, SC_SCATTER_COLS)
            return pltpu.make_async_copy(rows_v.at[b], xs_hbm.at[idx_v.at[(j // nslab) * TOP_K + k], cols], sem_s.at[b])

        load(0, 0).start()

        @pl.loop(0, nj, step=2)
        def _(j0):
            for b in (0, 1):
                j = j0 + b

                @pl.when(j >= 1)
                def _():
                    for k in range(TOP_K):
                        scatter(j - 1, k, 1 - b).wait()

                @pl.when(j + 1 < nj)
                def _():
                    load(j + 1, 1 - b).start()

                load(j, b).wait()
                for k in range(TOP_K):
                    scatter(j, k, b).start()

        for k in range(TOP_K):
            scatter(nj - 1, k, 1).wait()

    return scatter_kernel(h2, idx)


def _expert_kernel(be_ref, nu_ref, nv_ref, x_ref, wg_ref, bg_ref, wu_ref, bu_ref, wd_ref, bd_ref, y_ref):
    del be_ref
    i = pl.program_id(0)

    @pl.when(i < nu_ref[0])
    def _():
        row = lax.broadcasted_iota(jnp.int32, (EXPERT_TM, 1), 0)
        xb = _unpack_bf16_halves(jnp.where(row < nv_ref[i], x_ref[...], jnp.int32(0))).astype(BF16)
        glu = _mm(xb, wg_ref[0].astype(BF16)) + bg_ref[0]
        lin = _mm(xb, wu_ref[0].astype(BF16)) + bu_ref[0]
        glu = jnp.minimum(glu, SWIGLU_LIMIT)
        lin = jnp.clip(lin, -SWIGLU_LIMIT, SWIGLU_LIMIT)
        act = (0.5 * glu) * (1.0 + jnp.tanh((0.5 * SWIGLU_ALPHA) * glu)) * (lin + 1.0)
        y = _mm(act.astype(BF16), wd_ref[0].astype(BF16)) + bd_ref[0]
        y_ref[...] = _pack_bf16_halves(y)


def _experts(xs, block_expert, n_used, n_valid, w_gate, b_gate, w_up, b_up, w_down, b_down):
    n_slots = xs.shape[0]
    n_blocks = n_slots // EXPERT_TM
    rows = lambda i, be, nu, nv: (jnp.minimum(i, nu[0] - 1), 0)
    wspec = pl.BlockSpec((1, D_MODEL, D_MODEL), lambda i, be, nu, nv: (be[i], 0, 0))
    bspec = pl.BlockSpec((1, 1, D_MODEL), lambda i, be, nu, nv: (be[i], 0, 0))
    grid_spec = pltpu.PrefetchScalarGridSpec(
        num_scalar_prefetch=3,
        grid=(n_blocks,),
        in_specs=[pl.BlockSpec((EXPERT_TM, D_HALF), rows), wspec, bspec, wspec, bspec, wspec, bspec],
        out_specs=pl.BlockSpec((EXPERT_TM, D_HALF), rows),
    )
    b3 = lambda a: a.astype(F32).reshape(N_EXPERTS, 1, D_MODEL)
    return pl.pallas_call(
        _expert_kernel,
        grid_spec=grid_spec,
        out_shape=jax.ShapeDtypeStruct((n_slots, D_HALF), jnp.int32),
        compiler_params=_cparams(("arbitrary",)),
        name="experts",
    )(block_expert, n_used, n_valid, xs, w_gate, b3(b_gate), w_up, b3(b_up), w_down, b3(b_down))


def _sc_row_gather(table, idx):
    n_idx = idx.shape[0]
    width = table.shape[1]
    per_w = n_idx // SC_WORKERS
    nch = per_w // SC_CHUNK
    mesh = plsc.VectorSubcoreMesh(core_axis_name="core", subcore_axis_name="subcore")

    @functools.partial(
        pl.kernel, mesh=mesh,
        out_type=jax.ShapeDtypeStruct((n_idx, width), table.dtype),
        scratch_types=[pltpu.VMEM((per_w,), jnp.int32), pltpu.VMEM((2, SC_CHUNK, width), table.dtype),
                       pltpu.SemaphoreType.DMA((2,)), pltpu.SemaphoreType.DMA((2,))])
    def gather_kernel(table_hbm, idx_hbm, out_hbm, idx_v, rows_v, sem_g, sem_p):
        wid = lax.axis_index("subcore") * SC_CORES + lax.axis_index("core")
        base = wid * per_w
        pltpu.sync_copy(idx_hbm.at[pl.ds(base, per_w)], idx_v)

        def gather(c, b):
            return pltpu.make_async_copy(table_hbm.at[idx_v.at[pl.ds(c * SC_CHUNK, SC_CHUNK)]], rows_v.at[b], sem_g.at[b])

        def put(c, b):
            return pltpu.make_async_copy(rows_v.at[b], out_hbm.at[pl.ds(base + c * SC_CHUNK, SC_CHUNK)], sem_p.at[b])

        gather(0, 0).start()

        @pl.loop(0, nch, step=2)
        def _(c):
            for b in (0, 1):
                cc = c + b

                @pl.when(cc >= 1)
                def _():
                    put(cc - 1, 1 - b).wait()

                @pl.when(cc + 1 < nch)
                def _():
                    gather(cc + 1, 1 - b).start()

                gather(cc, b).wait()
                put(cc, b).start()

        put(nch - 1, 1).wait()

    return gather_kernel(table, idx)


def _combine_kernel(yt_ref, gate_ref, x1_ref, g_ref, o_ref):
    gate = gate_ref[...]
    x = x1_ref[...]
    for k in range(TOP_K):
        x = x + gate[:, k:k + 1] * _unpack_bf16_halves(yt_ref[k])
    o_ref[...] = _rms(x) * g_ref[...]


def _combine(yt, gates, x1, g_final):
    n = x1.shape[0]
    yt = yt.reshape(TOP_K, n, D_HALF)
    row = lambda w: pl.BlockSpec((COMBINE_R, w), lambda i: (i, 0))
    return pl.pallas_call(
        _combine_kernel,
        grid=(n // COMBINE_R,),
        in_specs=[pl.BlockSpec((TOP_K, COMBINE_R, D_HALF), lambda i: (0, i, 0)),
                  row(LANES), row(D_MODEL),
                  pl.BlockSpec((1, D_MODEL), lambda i: (0, 0))],
        out_specs=row(D_MODEL),
        out_shape=jax.ShapeDtypeStruct((n, D_MODEL), F32),
        compiler_params=_cparams(("parallel",)),
        name="combine",
    )(yt, gates, x1, g_final.reshape(1, D_MODEL))


def _slot_layout(meta, counts_f, n):
    eid = meta[:, :TOP_K]
    rank = meta[:, TOP_K:2 * TOP_K]
    counts = counts_f[0, :N_EXPERTS].astype(jnp.int32)
    padded = (counts + EXPERT_TM - 1) // EXPERT_TM * EXPERT_TM
    pend = jnp.cumsum(padded)
    pstart = pend - padded
    experts = jnp.arange(N_EXPERTS, dtype=jnp.int32)
    dest = rank + jnp.sum(jnp.where(eid[..., None] == experts, pstart, 0), axis=-1)
    n_slots = -(-(n * TOP_K + N_EXPERTS * (EXPERT_TM - 1)) // EXPERT_TM) * EXPERT_TM
    block_start = jnp.arange(n_slots // EXPERT_TM, dtype=jnp.int32) * EXPERT_TM
    block_expert = jnp.minimum(jnp.sum(pend[None, :] <= block_start[:, None], axis=-1), N_EXPERTS - 1)
    last_valid = jnp.sum(jnp.where(block_expert[:, None] == experts, pstart + counts, 0), axis=-1)
    n_valid = jnp.clip(last_valid - block_start, 0, EXPERT_TM).astype(jnp.int32)
    n_used = (pend[-1:] // EXPERT_TM).astype(jnp.int32)
    return dest.astype(jnp.int32), block_expert.astype(jnp.int32), n_used, n_valid, n_slots


def kernel(x, g_attn, w_qkv, rel_bias, g_out_dil, g_out_sb, w_o, g_moe, w_router, b_router,
           w_gate, b_gate, w_up, b_up, w_down, b_down, g_final):
    b, s, d = x.shape
    n = b * s
    x2 = x.reshape(n, d)
    qkv, *strided = _qkv(x2, g_attn, w_qkv)
    sources = dict(zip(STRIDED_DILATIONS, strided))
    dil = [_dilated(sources.get(dilation, qkv), b, s, rel_bias, dilation) for _window, dilation in DIL_PATTERNS]
    ob = _stickbreaking(qkv.reshape(b, s, 3 * d))
    x1, h2, meta, gates, counts = _mix_route([o for o, _ in dil], [l for _, l in dil], ob, x2,
                                             g_out_dil, g_out_sb, w_o, g_moe, w_router, b_router)
    dest, block_expert, n_used, n_valid, n_slots = _slot_layout(meta, counts, n)
    xs = _dispatch(h2, dest, n_slots)
    ys = _experts(xs, block_expert, n_used, n_valid, w_gate, b_gate, w_up, b_up, w_down, b_down)
    yt = _sc_row_gather(ys, dest.T.reshape(-1))
    return _combine(yt, gates, x1, g_final).reshape(b, s, d)
```

```python
import functools
import math

import jax
import jax.numpy as jnp
from jax import lax
from jax.experimental import pallas as pl
from jax.experimental.pallas import tpu as pltpu
from jax.experimental.pallas import tpu_sc as plsc

F32 = jnp.float32
BF16 = jnp.bfloat16

D_MODEL = 1024
HEAD_DIM = 64
D_HALF = 512
N_HEADS = 8
LANES = 128
N_PAIRS = D_HALF // LANES
DIL_PATTERNS = ((128, 1), (512, 4), (2048, 16))
BAND = 128
REL_BUCKETS = 32
REL_MAX_DISTANCE = 2048
N_EXPERTS = 32
TOP_K = 4
SWIGLU_ALPHA = 1.702
SWIGLU_LIMIT = 7.0
RMS_EPS = 1e-6
NEG_INF = -1e30

QKV_TM = 512
DIL_G = 8
A_COLS = 3 * D_HALF
STRIDED_DILATIONS = tuple(d for _w, d in DIL_PATTERNS if d > 1)
LOG2E = math.log2(math.e)
SB_T = 256
SB_G = 8
SB_EXP_ZERO = 104.0
MIX_TM = 512
EXPERT_TM = 1024
COMBINE_R = 512
SC_CORES = 2
SC_WORKERS = SC_CORES * 16
SC_CHUNK = 64
SC_SCATTER_ROWS = 128
SC_SCATTER_COLS = 256
VMEM_LIMIT = 56 * 1024 * 1024


def _cparams(sem):
    return pltpu.CompilerParams(dimension_semantics=sem, vmem_limit_bytes=VMEM_LIMIT)


def _mm(a, b):
    return jnp.dot(a, b, preferred_element_type=F32)


def _mm_nt(a, b):
    return lax.dot_general(a, b, (((1,), (1,)), ((), ())), preferred_element_type=F32)


def _split_bf16(a):
    hi = a.astype(BF16)
    lo = (a - hi.astype(F32)).astype(BF16)
    return hi, lo


def _mm_split(a, b_bf16):
    hi, lo = _split_bf16(a)
    return _mm(hi, b_bf16) + _mm(lo, b_bf16)


def _pack_bf16_halves(x):
    c = x.shape[1] // 2
    return pltpu.pack_elementwise([x[:, :c], x[:, c:]], packed_dtype=BF16)


def _unpack_bf16_halves(p):
    return jnp.concatenate([pltpu.unpack_elementwise(p, index=i, packed_dtype=BF16, unpacked_dtype=F32)
                            for i in (0, 1)], axis=1)


def _rms(x):
    return x * lax.rsqrt(jnp.mean(x * x, axis=-1, keepdims=True) + RMS_EPS)


def _half_masks(dtype):
    lane = lax.broadcasted_iota(jnp.int32, (1, LANES), 1)
    lo = jnp.where(lane < HEAD_DIM, 1.0, 0.0).astype(dtype)
    hi = jnp.where(lane >= HEAD_DIM, 1.0, 0.0).astype(dtype)
    return lo, hi


def _qkv_kernel(x_ref, g_ref, w_ref, o_ref, *rest):
    stage_ref = rest[-1]
    h = (_rms(x_ref[...]) * g_ref[...]).astype(BF16)
    for c in range(3 * D_MODEL // D_HALF):
        y = _mm(h, w_ref[:, c * D_HALF:(c + 1) * D_HALF])
        if c == 0:
            y = y * (1.0 / math.sqrt(HEAD_DIM))
        if c == 3:
            y = y * (LOG2E / math.sqrt(HEAD_DIM))
        o_ref[:, c * D_HALF:(c + 1) * D_HALF] = y.astype(BF16)
        if c < 3:
            for p in range(N_PAIRS):
                stage_ref[c * N_PAIRS + p] = y[:, p * LANES:(p + 1) * LANES]
    for od_ref, d in zip(rest[:-1], STRIDED_DILATIONS):
        for r in range(d):
            for ch in range(A_COLS // LANES):
                col = r * A_COLS + ch * LANES
                od_ref[:, col:col + LANES] = stage_ref[ch, pl.ds(r, QKV_TM // d, stride=d), :].astype(BF16)


def _qkv(x2, g_attn, w_qkv):
    n = x2.shape[0]
    strided = [(n // d, d * A_COLS) for d in STRIDED_DILATIONS]
    return pl.pallas_call(
        _qkv_kernel,
        grid=(n // QKV_TM,),
        in_specs=[pl.BlockSpec((QKV_TM, D_MODEL), lambda i: (i, 0)),
                  pl.BlockSpec((1, D_MODEL), lambda i: (0, 0)),
                  pl.BlockSpec((D_MODEL, 3 * D_MODEL), lambda i: (0, 0))],
        out_specs=[pl.BlockSpec((QKV_TM, 3 * D_MODEL), lambda i: (i, 0))]
                  + [pl.BlockSpec((QKV_TM // d, d * A_COLS), lambda i: (i, 0)) for d in STRIDED_DILATIONS],
        out_shape=[jax.ShapeDtypeStruct((n, 3 * D_MODEL), BF16)]
                  + [jax.ShapeDtypeStruct(shape, BF16) for shape in strided],
        scratch_shapes=[pltpu.VMEM((A_COLS // LANES, QKV_TM, LANES), F32)],
        compiler_params=_cparams(("parallel",)),
        name="qkv",
    )(x2, g_attn.reshape(1, D_MODEL), w_qkv.astype(BF16))


def _t5_bucket(dist):
    max_exact = REL_BUCKETS // 2
    d_f = jnp.maximum(dist, 1).astype(jnp.float32)
    large = max_exact + (jnp.log(d_f / max_exact)
                         / math.log(REL_MAX_DISTANCE / max_exact)
                         * (REL_BUCKETS - max_exact)).astype(jnp.int32)
    large = jnp.minimum(large, REL_BUCKETS - 1)
    return jnp.where(dist < max_exact, dist, large)


def _band_bias(rel_bias, dilation):
    qi = jnp.arange(BAND, dtype=jnp.int32)[:, None]
    kj = jnp.arange(2 * BAND, dtype=jnp.int32)[None, :]
    rel = qi + BAND - kj
    ok = (rel >= 0) & (rel <= BAND)
    bucket = _t5_bucket(jnp.maximum(rel, 0) * dilation)
    table = rel_bias.astype(F32)
    bias = jnp.zeros((N_HEADS, BAND, 2 * BAND), F32)
    for bk in range(REL_BUCKETS):
        bias = jnp.where(bucket[None] == bk, table[bk][:, None, None], bias)
    return jnp.where(ok[None], bias, NEG_INF)


def _dilated_kernel(q_ref, kp_ref, kc_ref, vp_ref, vc_ref, bias_ref, o_ref, lse_ref, *, n_blocks):
    n = pl.program_id(2)
    kj = lax.broadcasted_iota(jnp.int32, (BAND, 2 * BAND), 1)
    has_prev = jnp.logical_or(n > 0, kj >= BAND)
    lane = lax.broadcasted_iota(jnp.int32, (BAND, LANES), 1)
    m_lo, m_hi = _half_masks(BF16)
    for g in range(n_blocks):
        rows = slice(g * BAND, (g + 1) * BAND)
        prev = slice((g - 1) * BAND, g * BAND)
        lse_tile = jnp.zeros((BAND, LANES), F32)
        for p in range(N_PAIRS):
            cols = slice(p * LANES, (p + 1) * LANES)
            q2 = q_ref[0, rows, cols]
            k_prev = kp_ref[0, :, cols] if g == 0 else kc_ref[0, prev, cols]
            v_prev = vp_ref[0, :, cols] if g == 0 else vc_ref[0, prev, cols]
            kcat = jnp.concatenate([k_prev, kc_ref[0, rows, cols]], axis=0)
            vcat = jnp.concatenate([v_prev, vc_ref[0, rows, cols]], axis=0)
            outs = []
            for s, msk in enumerate((m_lo, m_hi)):
                h = 2 * p + s
                logits = _mm_nt(q2 * msk, kcat) + bias_ref[h]
                if g == 0:
                    logits = jnp.where(has_prev, logits, NEG_INF)
                m = jnp.max(logits, axis=-1, keepdims=True)
                pr = jnp.exp(logits - m)
                den = jnp.sum(pr, axis=-1, keepdims=True)
                outs.append(_mm(pr.astype(BF16), vcat))
                lse_tile = jnp.where(lane == h, m, lse_tile)
                lse_tile = jnp.where(lane == N_HEADS + h, den, lse_tile)
            o_ref[0, rows, cols] = jnp.where(lane < HEAD_DIM, outs[0], outs[1])
        lse_ref[0, rows, :] = lse_tile


def _dilated(src, b, s, rel_bias, dilation):
    d = dilation
    l = s // d
    g_blocks = min(DIL_G, l // BAND)
    nstep = l // (g_blocks * BAND)
    view = src.reshape(b, l, src.shape[1])
    n_sec = src.shape[1] // d // D_HALF

    def sec(k, prev):
        if prev:
            return pl.BlockSpec((1, BAND, D_HALF),
                                lambda bi, r, n: (bi, jnp.maximum(n * g_blocks - 1, 0), r * n_sec + k))
        return pl.BlockSpec((1, g_blocks * BAND, D_HALF), lambda bi, r, n: (bi, n, r * n_sec + k))

    o, lse = pl.pallas_call(
        functools.partial(_dilated_kernel, n_blocks=g_blocks),
        grid=(b, d, nstep),
        in_specs=[sec(0, False), sec(1, True), sec(1, False), sec(2, True), sec(2, False),
                  pl.BlockSpec((N_HEADS, BAND, 2 * BAND), lambda bi, r, n: (0, 0, 0))],
        out_specs=[pl.BlockSpec((1, g_blocks * BAND, D_HALF), lambda bi, r, n: (bi, n, r)),
                   pl.BlockSpec((1, g_blocks * BAND, LANES), lambda bi, r, n: (bi, n, r))],
        out_shape=[jax.ShapeDtypeStruct((b, l, d * D_HALF), F32),
                   jax.ShapeDtypeStruct((b, l, d * LANES), F32)],
        compiler_params=_cparams(("parallel", "parallel", "arbitrary")),
        name=f"dilated_d{d}",
    )(view, view, view, view, view, _band_bias(rel_bias, d))
    return o.reshape(b * l, d * D_HALF), lse.reshape(b * l, d * LANES)


def _sb_kernel(q_ref, k_ref, v_ref, u_ref, o_ref, acc_ref, carry_ref):
    i = pl.program_id(2)
    m_lo, m_hi = _half_masks(BF16)
    row = lax.broadcasted_iota(jnp.int32, (2 * SB_T, SB_T), 0) & (SB_T - 1)
    col = lax.broadcasted_iota(jnp.int32, (2 * SB_T, SB_T), 1)
    causal = col < row
    lane = lax.broadcasted_iota(jnp.int32, (SB_T, LANES), 1)
    u = u_ref[...]

    def tile(qs, j, mask, carry):
        start = pl.multiple_of(j * SB_T, SB_T)
        kj = k_ref[0, pl.ds(start, SB_T), :]
        vj = v_ref[0, pl.ds(start, SB_T), :]
        z = _mm_nt(qs, kj)
        neg_abs = lax.bitcast_convert_type(
            lax.bitcast_convert_type(z, jnp.uint32) | jnp.uint32(0x80000000), F32)
        sp = jnp.log2(1.0 + jnp.exp2(neg_abs))
        log_b = jnp.minimum(z, 0.0) - sp
        l1m = log_b - z
        if mask is not None:
            l1m = jnp.where(mask, l1m, 0.0)
        cum = _mm(l1m.astype(BF16), u)
        total = cum[:, 0:1] + l1m[:, 0:1]
        a = jnp.exp2(log_b + cum + carry)
        if mask is not None:
            a = jnp.where(mask, a, 0.0)
        return _mm(a.astype(BF16), vj), total

    def more(state):
        j, top = state
        return jnp.logical_and(j >= 0, top > -SB_EXP_ZERO * LOG2E)

    tops = []
    for g in range(SB_G):
        blk = i * SB_G + g
        q2 = q_ref[0, g * SB_T:(g + 1) * SB_T, :]
        qs = jnp.concatenate([q2 * m_lo, q2 * m_hi], axis=0)
        pv_d, tot_d = tile(qs, blk, causal, 0.0)
        if g == 0:
            pv_p, tot_p = tile(qs, jnp.maximum(blk - 1, 0), blk > 0, tot_d)
        else:
            pv_p, tot_p = tile(qs, blk - 1, None, tot_d)
        acc_ref[g] = pv_d + pv_p
        carry0 = tot_d + tot_p
        carry_ref[g] = carry0
        tops.append(jnp.max(carry0))

    for g in range(SB_G):
        def body(state, g=g):
            j, _ = state
            q2 = q_ref[0, g * SB_T:(g + 1) * SB_T, :]
            qs = jnp.concatenate([q2 * m_lo, q2 * m_hi], axis=0)
            pv, tot = tile(qs, j, None, carry_ref[g])
            acc_ref[g] += pv
            carry = carry_ref[g] + tot
            carry_ref[g] = carry
            return j - 1, jnp.max(carry)

        lax.while_loop(more, body, (i * SB_G + g - 2, tops[g]))
        o_ref[0, g * SB_T:(g + 1) * SB_T, :] = jnp.where(lane < HEAD_DIM, acc_ref[g, :SB_T], acc_ref[g, SB_T:])


def _suffix_matrix():
    sp = jnp.arange(SB_T, dtype=jnp.int32)[:, None]
    sc = jnp.arange(SB_T, dtype=jnp.int32)[None, :]
    return jnp.where(sp > sc, 1.0, 0.0).astype(BF16)


def _stickbreaking(qkv3):
    b, s, _ = qkv3.shape
    sec_q, sec_k, sec_v = 3 * N_PAIRS, 4 * N_PAIRS, 5 * N_PAIRS
    o = pl.pallas_call(
        _sb_kernel,
        grid=(b, N_PAIRS, s // (SB_G * SB_T)),
        in_specs=[pl.BlockSpec((1, SB_G * SB_T, LANES), lambda bi, p, i: (bi, i, sec_q + p)),
                  pl.BlockSpec((1, s, LANES), lambda bi, p, i: (bi, 0, sec_k + p)),
                  pl.BlockSpec((1, s, LANES), lambda bi, p, i: (bi, 0, sec_v + p)),
                  pl.BlockSpec((SB_T, SB_T), lambda bi, p, i: (0, 0))],
        out_specs=pl.BlockSpec((1, SB_G * SB_T, LANES), lambda bi, p, i: (bi, i, p)),
        out_shape=jax.ShapeDtypeStruct((b, s, D_HALF), F32),
        scratch_shapes=[pltpu.VMEM((SB_G, 2 * SB_T, LANES), F32), pltpu.VMEM((SB_G, 2 * SB_T, 1), F32)],
        compiler_params=_cparams(("parallel", "parallel", "arbitrary")),
        name="stickbreaking",
    )(qkv3, qkv3, qkv3, _suffix_matrix())
    return o.reshape(b * s, D_HALF)


def _token_major(ref, d, st_ref):
    if d == 1:
        return ref[...]
    width = ref.shape[1] // d
    for r in range(d):
        for ch in range(width // LANES):
            col = r * width + ch * LANES
            st_ref[ch, pl.ds(r, MIX_TM // d, stride=d), :] = ref[:, col:col + LANES]
    return jnp.concatenate([st_ref[ch] for ch in range(width // LANES)], axis=1)


def _mix_kernel(o1_ref, o2_ref, o3_ref, l1_ref, l2_ref, l3_ref, ob_ref, x_ref,
                ga_ref, gb_ref, e_ref, wo_ref, gm_ref, wr_ref, br_ref, tri_ref,
                x1_ref, h2_ref, meta_ref, gate_ref, cnt_ref, carry_ref, *scratch):
    dils = [d for _w, d in DIL_PATTERNS]
    o_st = dict(zip(STRIDED_DILATIONS, scratch[:len(STRIDED_DILATIONS)]))
    l_st = dict(zip(STRIDED_DILATIONS, scratch[len(STRIDED_DILATIONS):]))
    o1, o2, o3 = [_token_major(r, d, o_st.get(d)) for r, d in zip((o1_ref, o2_ref, o3_ref), dils)]
    l1, l2, l3 = [_token_major(r, d, l_st.get(d)) for r, d in zip((l1_ref, l2_ref, l3_ref), dils)]
    lane = lax.broadcasted_iota(jnp.int32, l1.shape, 1)
    m = jnp.maximum(jnp.maximum(l1, l2), l3)
    e1, e2, e3 = jnp.exp(l1 - m), jnp.exp(l2 - m), jnp.exp(l3 - m)
    s1, s2, s3 = [pltpu.roll(l, LANES - N_HEADS, 1) for l in (l1, l2, l3)]
    inv = 1.0 / (e1 * s1 + e2 * s2 + e3 * s3)
    w1, w2, w3 = [jnp.where(lane < N_HEADS, e * inv, 0.0) for e in (e1, e2, e3)]
    expand = e_ref[...]
    oa = (_mm_split(w1, expand) * o1
          + _mm_split(w2, expand) * o2
          + _mm_split(w3, expand) * o3)
    oa = (_rms(oa) * ga_ref[...]).astype(BF16)
    ob = (_rms(ob_ref[...]) * gb_ref[...]).astype(BF16)
    mix = _mm(oa, wo_ref[:D_HALF, :]) + _mm(ob, wo_ref[D_HALF:, :])
    x1 = x_ref[...] + mix
    x1_ref[...] = x1
    _route(x1, gm_ref, wr_ref, br_ref, tri_ref, h2_ref, meta_ref, gate_ref, cnt_ref, carry_ref)


def _head_expand():
    lane = jnp.arange(LANES, dtype=jnp.int32)[:, None]
    col = jnp.arange(D_HALF, dtype=jnp.int32)[None, :]
    return jnp.where(col // HEAD_DIM == lane, 1.0, 0.0).astype(BF16)


def _mix_route(os_, lses, ob, x2, g_out_dil, g_out_sb, w_o, g_moe, w_router, b_router):
    n = x2.shape[0]
    wr = w_router.astype(F32).T
    wr_hi = wr.astype(BF16)
    wr_lo = (wr - wr_hi.astype(F32)).astype(BF16)
    br = jnp.broadcast_to(b_router.astype(F32)[:, None], (N_EXPERTS, LANES))
    r = jnp.arange(MIX_TM, dtype=jnp.int32)
    tri = jnp.where(r[:, None] < r[None, :], 1.0, 0.0).astype(BF16)
    col = lambda rows: pl.BlockSpec((rows, MIX_TM), lambda i: (0, i))
    row = lambda w: pl.BlockSpec((MIX_TM, w), lambda i: (i, 0))
    srow = lambda a: pl.BlockSpec((MIX_TM * a.shape[0] // n, a.shape[1]), lambda i: (i, 0))
    const = lambda shape: pl.BlockSpec(shape, lambda i: (0, 0))
    return pl.pallas_call(
        _mix_kernel,
        grid=(n // MIX_TM,),
        in_specs=[srow(a) for a in os_] + [srow(a) for a in lses] + [row(D_HALF), row(D_MODEL),
                  const((1, D_HALF)), const((1, D_HALF)), const((LANES, D_HALF)), const((D_MODEL, D_MODEL)),
                  const((1, D_MODEL)), const((2 * N_EXPERTS, D_MODEL)), const((N_EXPERTS, LANES)), const((MIX_TM, MIX_TM))],
        out_specs=[row(D_MODEL), row(D_HALF), col(2 * TOP_K), col(2 * TOP_K), const((N_EXPERTS, LANES))],
        out_shape=[jax.ShapeDtypeStruct((n, D_MODEL), F32),
                   jax.ShapeDtypeStruct((n, D_HALF), jnp.int32),
                   jax.ShapeDtypeStruct((2 * TOP_K, n), jnp.int32),
                   jax.ShapeDtypeStruct((2 * TOP_K, n), F32),
                   jax.ShapeDtypeStruct((N_EXPERTS, LANES), F32)],
        scratch_shapes=[pltpu.VMEM((N_EXPERTS, LANES), F32)]
                       + [pltpu.VMEM((D_HALF // LANES, MIX_TM, LANES), F32) for _ in STRIDED_DILATIONS]
                       + [pltpu.VMEM((1, MIX_TM, LANES), F32) for _ in STRIDED_DILATIONS],
        compiler_params=_cparams(("arbitrary",)),
        name="mix_route",
    )(*os_, *lses, ob, x2, g_out_dil.reshape(1, D_HALF), g_out_sb.reshape(1, D_HALF),
      _head_expand(), w_o.astype(BF16), g_moe.reshape(1, D_MODEL), jnp.concatenate([wr_hi, wr_lo], axis=0), br, tri)


def _route(x1, g_ref, wr_ref, br_ref, tri_ref, h2_ref, meta_ref, gate_ref, cnt_ref, carry_ref):
    tm = x1.shape[0]

    @pl.when(pl.program_id(0) == 0)
    def _():
        carry_ref[...] = jnp.zeros_like(carry_ref)

    h2 = _rms(x1) * g_ref[...]
    h2_ref[...] = _pack_bf16_halves(h2)
    parts = _mm_nt(wr_ref[...], jnp.concatenate(_split_bf16(h2), axis=0))
    hi, lo = parts[:N_EXPERTS], parts[N_EXPERTS:]
    logits = (hi[:, :tm] + br_ref[:, 0:1]) + ((hi[:, tm:] + lo[:, :tm]) + lo[:, tm:])
    expert = lax.broadcasted_iota(jnp.int32, (N_EXPERTS, tm), 0).astype(F32)

    onehots, vals, ids = [], [], []
    for _k in range(TOP_K):
        m = jnp.max(logits, axis=0, keepdims=True)
        idx = jnp.min(jnp.where(logits == m, expert, float(N_EXPERTS)), axis=0, keepdims=True)
        oh = expert == idx
        logits = jnp.where(oh, -jnp.inf, logits)
        onehots.append(oh)
        vals.append(m)
        ids.append(idx)

    es = [jnp.exp(v - vals[0]) for v in vals]
    inv = 1.0 / (es[0] + es[1] + es[2] + es[3])

    sel = jnp.zeros((N_EXPERTS, tm), F32)
    for oh in onehots:
        sel = jnp.where(oh, 1.0, sel)
    before = _mm(sel.astype(BF16), tri_ref[...]) + carry_ref[:, 0:1]
    ranks = [jnp.sum(jnp.where(oh, before, 0.0), axis=0, keepdims=True) for oh in onehots]
    zero = jnp.zeros((1, tm), F32)
    meta_ref[...] = jnp.concatenate(ids + ranks, axis=0).astype(jnp.int32)
    gate_ref[...] = jnp.concatenate([e * inv for e in es] + [zero] * TOP_K, axis=0)
    total = carry_ref[...] + jnp.sum(sel, axis=1, keepdims=True)
    carry_ref[...] = total
    cnt_ref[...] = total


def _dispatch(h2, dest, n_slots):
    n = h2.shape[0]
    per_w = n // SC_WORKERS
    nch = per_w // SC_SCATTER_ROWS
    nslab = h2.shape[1] // SC_SCATTER_COLS
    nj = nch * nslab
    idx = dest.reshape(TOP_K, SC_WORKERS, nch, SC_SCATTER_ROWS).transpose(1, 2, 0, 3)
    idx = idx.reshape(SC_WORKERS * nch * TOP_K, SC_SCATTER_ROWS)
    mesh = plsc.VectorSubcoreMesh(core_axis_name="core", subcore_axis_name="subcore")

    @functools.partial(
        pl.kernel, mesh=mesh,
        out_type=jax.ShapeDtypeStruct((n_slots, h2.shape[1]), h2.dtype),
        scratch_types=[pltpu.VMEM((nch * TOP_K, SC_SCATTER_ROWS), jnp.int32),
                       pltpu.VMEM((2, SC_SCATTER_ROWS, SC_SCATTER_COLS), h2.dtype),
                       pltpu.SemaphoreType.DMA((2,)), pltpu.SemaphoreType.DMA((2,))])
    def scatter_kernel(h2_hbm, idx_hbm, xs_hbm, idx_v, rows_v, sem_l, sem_s):
        wid = lax.axis_index("subcore") * SC_CORES + lax.axis_index("core")
        t0 = wid * per_w
        pltpu.sync_copy(idx_hbm.at[pl.ds(wid * (nch * TOP_K), nch * TOP_K)], idx_v)

        def load(j, b):
            rows = pl.ds(t0 + (j // nslab) * SC_SCATTER_ROWS, SC_SCATTER_ROWS)
            cols = pl.ds((j % nslab) * SC_SCATTER_COLS, SC_SCATTER_COLS)
            return pltpu.make_async_copy(h2_hbm.at[rows, cols], rows_v.at[b], sem_l.at[b])

        def scatter(j, k, b):
            cols = pl.ds((j % nslab) * SC_SCATTER_COLS, SC_SCATTER_COLS)
            return pltpu.make_async_copy(rows_v.at[b], xs_hbm.at[idx_v.at[(j // nslab) * TOP_K + k], cols], sem_s.at[b])

        load(0, 0).start()

        @pl.loop(0, nj, step=2)
        def _(j0):
            for b in (0, 1):
                j = j0 + b

                @pl.when(j >= 1)
                def _():
                    for k in range(TOP_K):
                        scatter(j - 1, k, 1 - b).wait()

                @pl.when(j + 1 < nj)
                def _():
                    load(j + 1, 1 - b).start()

                load(j, b).wait()
                for k in range(TOP_K):
                    scatter(j, k, b).start()

        for k in range(TOP_K):
            scatter(nj - 1, k, 1).wait()

    return scatter_kernel(h2, idx)


def _expert_kernel(be_ref, nu_ref, nv_ref, x_ref, wg_ref, bg_ref, wu_ref, bu_ref, wd_ref, bd_ref, y_ref):
    del be_ref
    i = pl.program_id(0)

    @pl.when(i < nu_ref[0])
    def _():
        row = lax.broadcasted_iota(jnp.int32, (EXPERT_TM, 1), 0)
        xb = _unpack_bf16_halves(jnp.where(row < nv_ref[i], x_ref[...], jnp.int32(0))).astype(BF16)
        glu = _mm(xb, wg_ref[0].astype(BF16)) + bg_ref[0]
        lin = _mm(xb, wu_ref[0].astype(BF16)) + bu_ref[0]
        glu = jnp.minimum(glu, SWIGLU_LIMIT)
        lin = jnp.clip(lin, -SWIGLU_LIMIT, SWIGLU_LIMIT)
        act = (0.5 * glu) * (1.0 + jnp.tanh((0.5 * SWIGLU_ALPHA) * glu)) * (lin + 1.0)
        y = _mm(act.astype(BF16), wd_ref[0].astype(BF16)) + bd_ref[0]
        y_ref[...] = _pack_bf16_halves(y)


def _experts(xs, block_expert, n_used, n_valid, w_gate, b_gate, w_up, b_up, w_down, b_down):
    n_slots = xs.shape[0]
    n_blocks = n_slots // EXPERT_TM
    rows = lambda i, be, nu, nv: (jnp.minimum(i, nu[0] - 1), 0)
    wspec = pl.BlockSpec((1, D_MODEL, D_MODEL), lambda i, be, nu, nv: (be[i], 0, 0))
    bspec = pl.BlockSpec((1, 1, D_MODEL), lambda i, be, nu, nv: (be[i], 0, 0))
    grid_spec = pltpu.PrefetchScalarGridSpec(
        num_scalar_prefetch=3,
        grid=(n_blocks,),
        in_specs=[pl.BlockSpec((EXPERT_TM, D_HALF), rows), wspec, bspec, wspec, bspec, wspec, bspec],
        out_specs=pl.BlockSpec((EXPERT_TM, D_HALF), rows),
    )
    b3 = lambda a: a.astype(F32).reshape(N_EXPERTS, 1, D_MODEL)
    return pl.pallas_call(
        _expert_kernel,
        grid_spec=grid_spec,
        out_shape=jax.ShapeDtypeStruct((n_slots, D_HALF), jnp.int32),
        compiler_params=_cparams(("arbitrary",)),
        name="experts",
    )(block_expert, n_used, n_valid, xs, w_gate, b3(b_gate), w_up, b3(b_up), w_down, b3(b_down))


def _sc_row_gather(table, idx):
    n_idx = idx.shape[0]
    width = table.shape[1]
    per_w = n_idx // SC_WORKERS
    nch = per_w // SC_CHUNK
    mesh = plsc.VectorSubcoreMesh(core_axis_name="core", subcore_axis_name="subcore")

    @functools.partial(
        pl.kernel, mesh=mesh,
        out_type=jax.ShapeDtypeStruct((n_idx, width), table.dtype),
        scratch_types=[pltpu.VMEM((per_w,), jnp.int32), pltpu.VMEM((2, SC_CHUNK, width), table.dtype),
                       pltpu.SemaphoreType.DMA((2,)), pltpu.SemaphoreType.DMA((2,))])
    def gather_kernel(table_hbm, idx_hbm, out_hbm, idx_v, rows_v, sem_g, sem_p):
        wid = lax.axis_index("subcore") * SC_CORES + lax.axis_index("core")
        base = wid * per_w
        pltpu.sync_copy(idx_hbm.at[pl.ds(base, per_w)], idx_v)

        def gather(c, b):
            return pltpu.make_async_copy(table_hbm.at[idx_v.at[pl.ds(c * SC_CHUNK, SC_CHUNK)]], rows_v.at[b], sem_g.at[b])

        def put(c, b):
            return pltpu.make_async_copy(rows_v.at[b], out_hbm.at[pl.ds(base + c * SC_CHUNK, SC_CHUNK)], sem_p.at[b])

        gather(0, 0).start()

        @pl.loop(0, nch, step=2)
        def _(c):
            for b in (0, 1):
                cc = c + b

                @pl.when(cc >= 1)
                def _():
                    put(cc - 1, 1 - b).wait()

                @pl.when(cc + 1 < nch)
                def _():
                    gather(cc + 1, 1 - b).start()

                gather(cc, b).wait()
                put(cc, b).start()

        put(nch - 1, 1).wait()

    return gather_kernel(table, idx)


def _combine_kernel(yt_ref, gate_ref, x1_ref, g_ref, o_ref):
    gate = gate_ref[...].T
    x = x1_ref[...]
    for k in range(TOP_K):
        x = x + gate[:, k:k + 1] * _unpack_bf16_halves(yt_ref[k])
    o_ref[...] = _rms(x) * g_ref[...]


def _combine(yt, gates, x1, g_final):
    n = x1.shape[0]
    yt = yt.reshape(TOP_K, n, D_HALF)
    row = lambda w: pl.BlockSpec((COMBINE_R, w), lambda i: (i, 0))
    return pl.pallas_call(
        _combine_kernel,
        grid=(n // COMBINE_R,),
        in_specs=[pl.BlockSpec((TOP_K, COMBINE_R, D_HALF), lambda i: (0, i, 0)),
                  pl.BlockSpec((2 * TOP_K, COMBINE_R), lambda i: (0, i)), row(D_MODEL),
                  pl.BlockSpec((1, D_MODEL), lambda i: (0, 0))],
        out_specs=row(D_MODEL),
        out_shape=jax.ShapeDtypeStruct((n, D_MODEL), F32),
        compiler_params=_cparams(("parallel",)),
        name="combine",
    )(yt, gates, x1, g_final.reshape(1, D_MODEL))


def _slot_layout(meta, counts_f, n):
    eid = meta[:TOP_K]
    rank = meta[TOP_K:]
    counts = counts_f[:, 0].astype(jnp.int32)
    padded = (counts + EXPERT_TM - 1) // EXPERT_TM * EXPERT_TM
    pend = jnp.cumsum(padded)
    pstart = pend - padded
    experts = jnp.arange(N_EXPERTS, dtype=jnp.int32)
    dest = rank + jnp.sum(jnp.where(eid[..., None] == experts, pstart, 0), axis=-1)
    n_slots = -(-(n * TOP_K + N_EXPERTS * (EXPERT_TM - 1)) // EXPERT_TM) * EXPERT_TM
    block_start = jnp.arange(n_slots // EXPERT_TM, dtype=jnp.int32) * EXPERT_TM
    block_expert = jnp.minimum(jnp.sum(pend[None, :] <= block_start[:, None], axis=-1), N_EXPERTS - 1)
    last_valid = jnp.sum(jnp.where(block_expert[:, None] == experts, pstart + counts, 0), axis=-1)
    n_valid = jnp.clip(last_valid - block_start, 0, EXPERT_TM).astype(jnp.int32)
    n_used = (pend[-1:] // EXPERT_TM).astype(jnp.int32)
    return dest.astype(jnp.int32), block_expert.astype(jnp.int32), n_used, n_valid, n_slots


def kernel(x, g_attn, w_qkv, rel_bias, g_out_dil, g_out_sb, w_o, g_moe, w_router, b_router,
           w_gate, b_gate, w_up, b_up, w_down, b_down, g_final):
    b, s, d = x.shape
    n = b * s
    x2 = x.reshape(n, d)
    qkv, *strided = _qkv(x2, g_attn, w_qkv)
    sources = dict(zip(STRIDED_DILATIONS, strided))
    dil = [_dilated(sources.get(dilation, qkv), b, s, rel_bias, dilation) for _window, dilation in DIL_PATTERNS]
    ob = _stickbreaking(qkv.reshape(b, s, 3 * d))
    x1, h2, meta, gates, counts = _mix_route([o for o, _ in dil], [l for _, l in dil], ob, x2,
                                             g_out_dil, g_out_sb, w_o, g_moe, w_router, b_router)
    dest, block_expert, n_used, n_valid, n_slots = _slot_layout(meta, counts, n)
    xs = _dispatch(h2, dest, n_slots)
    ys = _experts(xs, block_expert, n_used, n_valid, w_gate, b_gate, w_up, b_up, w_down, b_down)
    yt = _sc_row_gather(ys, dest.reshape(-1))
    return _combine(yt, gates, x1, g_final).reshape(b, s, d)
```

```python
import functools
import math

import jax
import jax.numpy as jnp
from jax import lax
from jax.experimental import pallas as pl
from jax.experimental.pallas import tpu as pltpu
from jax.experimental.pallas import tpu_sc as plsc

F32 = jnp.float32
BF16 = jnp.bfloat16

D_MODEL = 1024
HEAD_DIM = 64
D_HALF = 512
N_HEADS = 8
LANES = 128
N_PAIRS = D_HALF // LANES
DIL_PATTERNS = ((128, 1), (512, 4), (2048, 16))
BAND = 128
REL_BUCKETS = 32
REL_MAX_DISTANCE = 2048
N_EXPERTS = 32
TOP_K = 4
SWIGLU_ALPHA = 1.702
SWIGLU_LIMIT = 7.0
RMS_EPS = 1e-6
NEG_INF = -1e30

QKV_TM = 512
DIL_G = 8
A_COLS = 3 * D_HALF
STRIDED_DILATIONS = tuple(d for _w, d in DIL_PATTERNS if d > 1)
LOG2E = math.log2(math.e)
SB_T = 256
SB_G = 8
SB_EXP_ZERO = 104.0
MIX_TM = 512
EXPERT_TM = 1024
COMBINE_R = 512
SC_CORES = 2
SC_SUBCORES = 16
SC_WORKERS = SC_CORES * SC_SUBCORES
SC_CHUNK = 64
SC_SCATTER_ROWS = 128
SC_SCATTER_COLS = 256
VMEM_LIMIT = 56 * 1024 * 1024


def _cparams(sem):
    return pltpu.CompilerParams(dimension_semantics=sem, vmem_limit_bytes=VMEM_LIMIT)


def _mm(a, b):
    return jnp.dot(a, b, preferred_element_type=F32)


def _mm_nt(a, b):
    return lax.dot_general(a, b, (((1,), (1,)), ((), ())), preferred_element_type=F32)


def _split_bf16(a):
    hi = a.astype(BF16)
    lo = (a - hi.astype(F32)).astype(BF16)
    return hi, lo


def _mm_split(a, b_bf16):
    hi, lo = _split_bf16(a)
    return _mm(hi, b_bf16) + _mm(lo, b_bf16)


def _pack_bf16_halves(x):
    c = x.shape[1] // 2
    return pltpu.pack_elementwise([x[:, :c], x[:, c:]], packed_dtype=BF16)


def _unpack_bf16_halves(p):
    return jnp.concatenate([pltpu.unpack_elementwise(p, index=i, packed_dtype=BF16, unpacked_dtype=F32)
                            for i in (0, 1)], axis=1)


def _rms(x):
    return x * lax.rsqrt(jnp.mean(x * x, axis=-1, keepdims=True) + RMS_EPS)


def _half_masks(dtype):
    lane = lax.broadcasted_iota(jnp.int32, (1, LANES), 1)
    lo = jnp.where(lane < HEAD_DIM, 1.0, 0.0).astype(dtype)
    hi = jnp.where(lane >= HEAD_DIM, 1.0, 0.0).astype(dtype)
    return lo, hi


def _qkv_kernel(x_ref, g_ref, w_ref, o_ref, *rest):
    stage_ref = rest[-1]
    h = (_rms(x_ref[...]) * g_ref[...]).astype(BF16)
    for c in range(3 * D_MODEL // D_HALF):
        y = _mm(h, w_ref[:, c * D_HALF:(c + 1) * D_HALF])
        if c == 0:
            y = y * (1.0 / math.sqrt(HEAD_DIM))
        if c == 3:
            y = y * (LOG2E / math.sqrt(HEAD_DIM))
        o_ref[:, c * D_HALF:(c + 1) * D_HALF] = y.astype(BF16)
        if c < 3:
            for p in range(N_PAIRS):
                stage_ref[c * N_PAIRS + p] = y[:, p * LANES:(p + 1) * LANES]
    for od_ref, d in zip(rest[:-1], STRIDED_DILATIONS):
        for r in range(d):
            for ch in range(A_COLS // LANES):
                col = r * A_COLS + ch * LANES
                od_ref[:, col:col + LANES] = stage_ref[ch, pl.ds(r, QKV_TM // d, stride=d), :].astype(BF16)


def _qkv(x2, g_attn, w_qkv):
    n = x2.shape[0]
    strided = [(n // d, d * A_COLS) for d in STRIDED_DILATIONS]
    return pl.pallas_call(
        _qkv_kernel,
        grid=(n // QKV_TM,),
        in_specs=[pl.BlockSpec((QKV_TM, D_MODEL), lambda i: (i, 0)),
                  pl.BlockSpec((1, D_MODEL), lambda i: (0, 0)),
                  pl.BlockSpec((D_MODEL, 3 * D_MODEL), lambda i: (0, 0))],
        out_specs=[pl.BlockSpec((QKV_TM, 3 * D_MODEL), lambda i: (i, 0))]
                  + [pl.BlockSpec((QKV_TM // d, d * A_COLS), lambda i: (i, 0)) for d in STRIDED_DILATIONS],
        out_shape=[jax.ShapeDtypeStruct((n, 3 * D_MODEL), BF16)]
                  + [jax.ShapeDtypeStruct(shape, BF16) for shape in strided],
        scratch_shapes=[pltpu.VMEM((A_COLS // LANES, QKV_TM, LANES), F32)],
        compiler_params=_cparams(("parallel",)),
        name="qkv",
    )(x2, g_attn.reshape(1, D_MODEL), w_qkv.astype(BF16))


def _t5_bucket(dist):
    max_exact = REL_BUCKETS // 2
    d_f = jnp.maximum(dist, 1).astype(jnp.float32)
    large = max_exact + (jnp.log(d_f / max_exact)
                         / math.log(REL_MAX_DISTANCE / max_exact)
                         * (REL_BUCKETS - max_exact)).astype(jnp.int32)
    large = jnp.minimum(large, REL_BUCKETS - 1)
    return jnp.where(dist < max_exact, dist, large)


def _band_bias(rel_bias, dilation):
    qi = jnp.arange(BAND, dtype=jnp.int32)[:, None]
    kj = jnp.arange(2 * BAND, dtype=jnp.int32)[None, :]
    rel = qi + BAND - kj
    ok = (rel >= 0) & (rel <= BAND)
    bucket = _t5_bucket(jnp.maximum(rel, 0) * dilation)
    table = rel_bias.astype(F32)
    bias = jnp.zeros((N_HEADS, BAND, 2 * BAND), F32)
    for bk in range(REL_BUCKETS):
        bias = jnp.where(bucket[None] == bk, table[bk][:, None, None], bias)
    return jnp.where(ok[None], bias, NEG_INF)


def _dilated_kernel(q_ref, kp_ref, kc_ref, vp_ref, vc_ref, bias_ref, o_ref, lse_ref, *, n_blocks):
    n = pl.program_id(2)
    kj = lax.broadcasted_iota(jnp.int32, (BAND, 2 * BAND), 1)
    has_prev = jnp.logical_or(n > 0, kj >= BAND)
    lane = lax.broadcasted_iota(jnp.int32, (BAND, LANES), 1)
    m_lo, m_hi = _half_masks(BF16)
    for g in range(n_blocks):
        rows = slice(g * BAND, (g + 1) * BAND)
        prev = slice((g - 1) * BAND, g * BAND)
        lse_tile = jnp.zeros((BAND, LANES), F32)
        for p in range(N_PAIRS):
            cols = slice(p * LANES, (p + 1) * LANES)
            q2 = q_ref[0, rows, cols]
            k_prev = kp_ref[0, :, cols] if g == 0 else kc_ref[0, prev, cols]
            v_prev = vp_ref[0, :, cols] if g == 0 else vc_ref[0, prev, cols]
            kcat = jnp.concatenate([k_prev, kc_ref[0, rows, cols]], axis=0)
            vcat = jnp.concatenate([v_prev, vc_ref[0, rows, cols]], axis=0)
            outs = []
            for s, msk in enumerate((m_lo, m_hi)):
                h = 2 * p + s
                logits = _mm_nt(q2 * msk, kcat) + bias_ref[h]
                if g == 0:
                    logits = jnp.where(has_prev, logits, NEG_INF)
                m = jnp.max(logits, axis=-1, keepdims=True)
                pr = jnp.exp(logits - m)
                den = jnp.sum(pr, axis=-1, keepdims=True)
                outs.append(_mm(pr.astype(BF16), vcat))
                lse_tile = jnp.where(lane == h, m, lse_tile)
                lse_tile = jnp.where(lane == N_HEADS + h, den, lse_tile)
            o_ref[0, rows, cols] = jnp.where(lane < HEAD_DIM, outs[0], outs[1])
        lse_ref[0, rows, :] = lse_tile


def _dilated(src, b, s, rel_bias, dilation):
    d = dilation
    l = s // d
    g_blocks = min(DIL_G, l // BAND)
    nstep = l // (g_blocks * BAND)
    view = src.reshape(b, l, src.shape[1])
    n_sec = src.shape[1] // d // D_HALF

    def sec(k, prev):
        if prev:
            return pl.BlockSpec((1, BAND, D_HALF),
                                lambda bi, r, n: (bi, jnp.maximum(n * g_blocks - 1, 0), r * n_sec + k))
        return pl.BlockSpec((1, g_blocks * BAND, D_HALF), lambda bi, r, n: (bi, n, r * n_sec + k))

    o, lse = pl.pallas_call(
        functools.partial(_dilated_kernel, n_blocks=g_blocks),
        grid=(b, d, nstep),
        in_specs=[sec(0, False), sec(1, True), sec(1, False), sec(2, True), sec(2, False),
                  pl.BlockSpec((N_HEADS, BAND, 2 * BAND), lambda bi, r, n: (0, 0, 0))],
        out_specs=[pl.BlockSpec((1, g_blocks * BAND, D_HALF), lambda bi, r, n: (bi, n, r)),
                   pl.BlockSpec((1, g_blocks * BAND, LANES), lambda bi, r, n: (bi, n, r))],
        out_shape=[jax.ShapeDtypeStruct((b, l, d * D_HALF), F32),
                   jax.ShapeDtypeStruct((b, l, d * LANES), F32)],
        compiler_params=_cparams(("parallel", "parallel", "arbitrary")),
        name=f"dilated_d{d}",
    )(view, view, view, view, view, _band_bias(rel_bias, d))
    return o.reshape(b * l, d * D_HALF), lse.reshape(b * l, d * LANES)


def _sb_kernel(q_ref, k_ref, v_ref, u_ref, o_ref, acc_ref, carry_ref):
    i = pl.program_id(2)
    m_lo, m_hi = _half_masks(BF16)
    row = lax.broadcasted_iota(jnp.int32, (2 * SB_T, SB_T), 0) & (SB_T - 1)
    col = lax.broadcasted_iota(jnp.int32, (2 * SB_T, SB_T), 1)
    causal = col < row
    lane = lax.broadcasted_iota(jnp.int32, (SB_T, LANES), 1)
    u = u_ref[...]

    def tile(qs, j, mask, carry):
        start = pl.multiple_of(j * SB_T, SB_T)
        kj = k_ref[0, pl.ds(start, SB_T), :]
        vj = v_ref[0, pl.ds(start, SB_T), :]
        z = _mm_nt(qs, kj)
        neg_abs = lax.bitcast_convert_type(
            lax.bitcast_convert_type(z, jnp.uint32) | jnp.uint32(0x80000000), F32)
        sp = jnp.log2(1.0 + jnp.exp2(neg_abs))
        log_b = jnp.minimum(z, 0.0) - sp
        l1m = log_b - z
        if mask is not None:
            l1m = jnp.where(mask, l1m, 0.0)
        cum = _mm(l1m.astype(BF16), u)
        total = cum[:, 0:1] + l1m[:, 0:1]
        a = jnp.exp2(log_b + cum + carry)
        if mask is not None:
            a = jnp.where(mask, a, 0.0)
        return _mm(a.astype(BF16), vj), total

    def more(state):
        j, top = state
        return jnp.logical_and(j >= 0, top > -SB_EXP_ZERO * LOG2E)

    tops = []
    for g in range(SB_G):
        blk = i * SB_G + g
        q2 = q_ref[0, g * SB_T:(g + 1) * SB_T, :]
        qs = jnp.concatenate([q2 * m_lo, q2 * m_hi], axis=0)
        pv_d, tot_d = tile(qs, blk, causal, 0.0)
        if g == 0:
            pv_p, tot_p = tile(qs, jnp.maximum(blk - 1, 0), blk > 0, tot_d)
        else:
            pv_p, tot_p = tile(qs, blk - 1, None, tot_d)
        acc_ref[g] = pv_d + pv_p
        carry0 = tot_d + tot_p
        carry_ref[g] = carry0
        tops.append(jnp.max(carry0))

    for g in range(SB_G):
        def body(state, g=g):
            j, _ = state
            q2 = q_ref[0, g * SB_T:(g + 1) * SB_T, :]
            qs = jnp.concatenate([q2 * m_lo, q2 * m_hi], axis=0)
            pv, tot = tile(qs, j, None, carry_ref[g])
            acc_ref[g] += pv
            carry = carry_ref[g] + tot
            carry_ref[g] = carry
            return j - 1, jnp.max(carry)

        lax.while_loop(more, body, (i * SB_G + g - 2, tops[g]))
        o_ref[0, g * SB_T:(g + 1) * SB_T, :] = jnp.where(lane < HEAD_DIM, acc_ref[g, :SB_T], acc_ref[g, SB_T:])


def _suffix_matrix():
    sp = jnp.arange(SB_T, dtype=jnp.int32)[:, None]
    sc = jnp.arange(SB_T, dtype=jnp.int32)[None, :]
    return jnp.where(sp > sc, 1.0, 0.0).astype(BF16)


def _stickbreaking(qkv3):
    b, s, _ = qkv3.shape
    sec_q, sec_k, sec_v = 3 * N_PAIRS, 4 * N_PAIRS, 5 * N_PAIRS
    o = pl.pallas_call(
        _sb_kernel,
        grid=(b, N_PAIRS, s // (SB_G * SB_T)),
        in_specs=[pl.BlockSpec((1, SB_G * SB_T, LANES), lambda bi, p, i: (bi, i, sec_q + p)),
                  pl.BlockSpec((1, s, LANES), lambda bi, p, i: (bi, 0, sec_k + p)),
                  pl.BlockSpec((1, s, LANES), lambda bi, p, i: (bi, 0, sec_v + p)),
                  pl.BlockSpec((SB_T, SB_T), lambda bi, p, i: (0, 0))],
        out_specs=pl.BlockSpec((1, SB_G * SB_T, LANES), lambda bi, p, i: (bi, i, p)),
        out_shape=jax.ShapeDtypeStruct((b, s, D_HALF), F32),
        scratch_shapes=[pltpu.VMEM((SB_G, 2 * SB_T, LANES), F32), pltpu.VMEM((SB_G, 2 * SB_T, 1), F32)],
        compiler_params=_cparams(("parallel", "parallel", "arbitrary")),
        name="stickbreaking",
    )(qkv3, qkv3, qkv3, _suffix_matrix())
    return o.reshape(b * s, D_HALF)


def _token_major(ref, d, st_ref):
    if d == 1:
        return ref[...]
    width = ref.shape[1] // d
    for r in range(d):
        for ch in range(width // LANES):
            col = r * width + ch * LANES
            st_ref[ch, pl.ds(r, MIX_TM // d, stride=d), :] = ref[:, col:col + LANES]
    return jnp.concatenate([st_ref[ch] for ch in range(width // LANES)], axis=1)


def _mix_kernel(o1_ref, o2_ref, o3_ref, l1_ref, l2_ref, l3_ref, ob_ref, x_ref,
                ga_ref, gb_ref, e_ref, wo_ref, gm_ref, wr_ref, br_ref, tri_ref,
                x1_ref, h2_ref, meta_ref, gate_ref, cnt_ref, carry_ref, *scratch):
    dils = [d for _w, d in DIL_PATTERNS]
    o_st = dict(zip(STRIDED_DILATIONS, scratch[:len(STRIDED_DILATIONS)]))
    l_st = dict(zip(STRIDED_DILATIONS, scratch[len(STRIDED_DILATIONS):]))
    o1, o2, o3 = [_token_major(r, d, o_st.get(d)) for r, d in zip((o1_ref, o2_ref, o3_ref), dils)]
    l1, l2, l3 = [_token_major(r, d, l_st.get(d)) for r, d in zip((l1_ref, l2_ref, l3_ref), dils)]
    lane = lax.broadcasted_iota(jnp.int32, l1.shape, 1)
    m = jnp.maximum(jnp.maximum(l1, l2), l3)
    e1, e2, e3 = jnp.exp(l1 - m), jnp.exp(l2 - m), jnp.exp(l3 - m)
    s1, s2, s3 = [pltpu.roll(l, LANES - N_HEADS, 1) for l in (l1, l2, l3)]
    inv = 1.0 / (e1 * s1 + e2 * s2 + e3 * s3)
    w1, w2, w3 = [jnp.where(lane < N_HEADS, e * inv, 0.0) for e in (e1, e2, e3)]
    expand = e_ref[...]
    oa = (_mm_split(w1, expand) * o1
          + _mm_split(w2, expand) * o2
          + _mm_split(w3, expand) * o3)
    oa = (_rms(oa) * ga_ref[...]).astype(BF16)
    ob = (_rms(ob_ref[...]) * gb_ref[...]).astype(BF16)
    mix = _mm(oa, wo_ref[:D_HALF, :]) + _mm(ob, wo_ref[D_HALF:, :])
    x1 = x_ref[...] + mix
    x1_ref[...] = x1
    _route(x1, gm_ref, wr_ref, br_ref, tri_ref, h2_ref, meta_ref, gate_ref, cnt_ref, carry_ref)


def _head_expand():
    lane = jnp.arange(LANES, dtype=jnp.int32)[:, None]
    col = jnp.arange(D_HALF, dtype=jnp.int32)[None, :]
    return jnp.where(col // HEAD_DIM == lane, 1.0, 0.0).astype(BF16)


def _mix_route(os_, lses, ob, x2, g_out_dil, g_out_sb, w_o, g_moe, w_router, b_router):
    n = x2.shape[0]
    wr = w_router.astype(F32).T
    wr_hi = wr.astype(BF16)
    wr_lo = (wr - wr_hi.astype(F32)).astype(BF16)
    br = jnp.broadcast_to(b_router.astype(F32)[:, None], (N_EXPERTS, LANES))
    r = jnp.arange(MIX_TM, dtype=jnp.int32)
    tri = jnp.where(r[:, None] < r[None, :], 1.0, 0.0).astype(BF16)
    col = lambda rows: pl.BlockSpec((rows, MIX_TM), lambda i: (0, i))
    row = lambda w: pl.BlockSpec((MIX_TM, w), lambda i: (i, 0))
    srow = lambda a: pl.BlockSpec((MIX_TM * a.shape[0] // n, a.shape[1]), lambda i: (i, 0))
    const = lambda shape: pl.BlockSpec(shape, lambda i: (0, 0))
    return pl.pallas_call(
        _mix_kernel,
        grid=(n // MIX_TM,),
        in_specs=[srow(a) for a in os_] + [srow(a) for a in lses] + [row(D_HALF), row(D_MODEL),
                  const((1, D_HALF)), const((1, D_HALF)), const((LANES, D_HALF)), const((D_MODEL, D_MODEL)),
                  const((1, D_MODEL)), const((2 * N_EXPERTS, D_MODEL)), const((N_EXPERTS, LANES)), const((MIX_TM, MIX_TM))],
        out_specs=[row(D_MODEL), row(D_HALF), col(2 * TOP_K), col(2 * TOP_K), const((N_EXPERTS, LANES))],
        out_shape=[jax.ShapeDtypeStruct((n, D_MODEL), F32),
                   jax.ShapeDtypeStruct((n, D_HALF), jnp.int32),
                   jax.ShapeDtypeStruct((2 * TOP_K, n), jnp.int32),
                   jax.ShapeDtypeStruct((2 * TOP_K, n), F32),
                   jax.ShapeDtypeStruct((N_EXPERTS, LANES), F32)],
        scratch_shapes=[pltpu.VMEM((N_EXPERTS, LANES), F32)]
                       + [pltpu.VMEM((D_HALF // LANES, MIX_TM, LANES), F32) for _ in STRIDED_DILATIONS]
                       + [pltpu.VMEM((1, MIX_TM, LANES), F32) for _ in STRIDED_DILATIONS],
        compiler_params=_cparams(("arbitrary",)),
        name="mix_route",
    )(*os_, *lses, ob, x2, g_out_dil.reshape(1, D_HALF), g_out_sb.reshape(1, D_HALF),
      _head_expand(), w_o.astype(BF16), g_moe.reshape(1, D_MODEL), jnp.concatenate([wr_hi, wr_lo], axis=0), br, tri)


def _route(x1, g_ref, wr_ref, br_ref, tri_ref, h2_ref, meta_ref, gate_ref, cnt_ref, carry_ref):
    tm = x1.shape[0]

    @pl.when(pl.program_id(0) == 0)
    def _():
        carry_ref[...] = jnp.zeros_like(carry_ref)

    h2 = _rms(x1) * g_ref[...]
    h2_ref[...] = _pack_bf16_halves(h2)
    parts = _mm_nt(wr_ref[...], jnp.concatenate(_split_bf16(h2), axis=0))
    hi, lo = parts[:N_EXPERTS], parts[N_EXPERTS:]
    logits = (hi[:, :tm] + br_ref[:, 0:1]) + ((hi[:, tm:] + lo[:, :tm]) + lo[:, tm:])
    expert = lax.broadcasted_iota(jnp.int32, (N_EXPERTS, tm), 0).astype(F32)

    onehots, vals, ids = [], [], []
    for _k in range(TOP_K):
        m = jnp.max(logits, axis=0, keepdims=True)
        idx = jnp.min(jnp.where(logits == m, expert, float(N_EXPERTS)), axis=0, keepdims=True)
        oh = expert == idx
        logits = jnp.where(oh, -jnp.inf, logits)
        onehots.append(oh)
        vals.append(m)
        ids.append(idx)

    es = [jnp.exp(v - vals[0]) for v in vals]
    inv = 1.0 / (es[0] + es[1] + es[2] + es[3])

    sel = jnp.zeros((N_EXPERTS, tm), F32)
    for oh in onehots:
        sel = jnp.where(oh, 1.0, sel)
    before = _mm(sel.astype(BF16), tri_ref[...]) + carry_ref[:, 0:1]
    ranks = [jnp.sum(jnp.where(oh, before, 0.0), axis=0, keepdims=True) for oh in onehots]
    zero = jnp.zeros((1, tm), F32)
    meta_ref[...] = jnp.concatenate(ids + ranks, axis=0).astype(jnp.int32)
    gate_ref[...] = jnp.concatenate([e * inv for e in es] + [zero] * TOP_K, axis=0)
    total = carry_ref[...] + jnp.sum(sel, axis=1, keepdims=True)
    carry_ref[...] = total
    cnt_ref[...] = total


def _dispatch(h2, dest, n_slots):
    n = h2.shape[0]
    per_w = n // SC_WORKERS
    nch = per_w // SC_SCATTER_ROWS
    nslab = h2.shape[1] // SC_SCATTER_COLS
    nj = nch * nslab
    assert n == SC_WORKERS * nch * SC_SCATTER_ROWS and h2.shape[1] == nslab * SC_SCATTER_COLS and nj % 2 == 0
    idx = dest.reshape(TOP_K, SC_WORKERS, nch, SC_SCATTER_ROWS).transpose(1, 2, 0, 3)
    idx = idx.reshape(SC_WORKERS * nch * TOP_K, SC_SCATTER_ROWS)
    mesh = plsc.VectorSubcoreMesh(core_axis_name="core", subcore_axis_name="subcore")

    @functools.partial(
        pl.kernel, mesh=mesh,
        out_type=jax.ShapeDtypeStruct((n_slots, h2.shape[1]), h2.dtype),
        scratch_types=[pltpu.VMEM((nch * TOP_K, SC_SCATTER_ROWS), jnp.int32),
                       pltpu.VMEM((2, SC_SCATTER_ROWS, SC_SCATTER_COLS), h2.dtype),
                       pltpu.SemaphoreType.DMA((2,)), pltpu.SemaphoreType.DMA((2,))])
    def scatter_kernel(h2_hbm, idx_hbm, xs_hbm, idx_v, rows_v, sem_l, sem_s):
        wid = lax.axis_index("subcore") * SC_CORES + lax.axis_index("core")
        t0 = wid * per_w
        pltpu.sync_copy(idx_hbm.at[pl.ds(wid * (nch * TOP_K), nch * TOP_K)], idx_v)

        def load(j, b):
            rows = pl.ds(t0 + (j // nslab) * SC_SCATTER_ROWS, SC_SCATTER_ROWS)
            cols = pl.ds((j % nslab) * SC_SCATTER_COLS, SC_SCATTER_COLS)
            return pltpu.make_async_copy(h2_hbm.at[rows, cols], rows_v.at[b], sem_l.at[b])

        def scatter(j, k, b):
            cols = pl.ds((j % nslab) * SC_SCATTER_COLS, SC_SCATTER_COLS)
            return pltpu.make_async_copy(rows_v.at[b], xs_hbm.at[idx_v.at[(j // nslab) * TOP_K + k], cols], sem_s.at[b])

        load(0, 0).start()

        @pl.loop(0, nj, step=2)
        def _(j0):
            for b in (0, 1):
                j = j0 + b

                @pl.when(j >= 1)
                def _():
                    for k in range(TOP_K):
                        scatter(j - 1, k, 1 - b).wait()

                @pl.when(j + 1 < nj)
                def _():
                    load(j + 1, 1 - b).start()

                load(j, b).wait()
                for k in range(TOP_K):
                    scatter(j, k, b).start()

        for k in range(TOP_K):
            scatter(nj - 1, k, 1).wait()

    return scatter_kernel(h2, idx)


def _expert_kernel(be_ref, nu_ref, nv_ref, x_ref, wg_ref, bg_ref, wu_ref, bu_ref, wd_ref, bd_ref, y_ref):
    del be_ref
    i = pl.program_id(0)

    @pl.when(i < nu_ref[0])
    def _():
        row = lax.broadcasted_iota(jnp.int32, (EXPERT_TM, 1), 0)
        xb = _unpack_bf16_halves(jnp.where(row < nv_ref[i], x_ref[...], jnp.int32(0))).astype(BF16)
        glu = _mm(xb, wg_ref[0].astype(BF16)) + bg_ref[0]
        lin = _mm(xb, wu_ref[0].astype(BF16)) + bu_ref[0]
        glu = jnp.minimum(glu, SWIGLU_LIMIT)
        lin = jnp.clip(lin, -SWIGLU_LIMIT, SWIGLU_LIMIT)
        act = (0.5 * glu) * (1.0 + jnp.tanh((0.5 * SWIGLU_ALPHA) * glu)) * (lin + 1.0)
        y = _mm(act.astype(BF16), wd_ref[0].astype(BF16)) + bd_ref[0]
        y_ref[...] = _pack_bf16_halves(y)


def _experts(xs, block_expert, n_used, n_valid, w_gate, b_gate, w_up, b_up, w_down, b_down):
    n_slots = xs.shape[0]
    n_blocks = n_slots // EXPERT_TM
    rows = lambda i, be, nu, nv: (jnp.minimum(i, nu[0] - 1), 0)
    wspec = pl.BlockSpec((1, D_MODEL, D_MODEL), lambda i, be, nu, nv: (be[i], 0, 0))
    bspec = pl.BlockSpec((1, 1, D_MODEL), lambda i, be, nu, nv: (be[i], 0, 0))
    grid_spec = pltpu.PrefetchScalarGridSpec(
        num_scalar_prefetch=3,
        grid=(n_blocks,),
        in_specs=[pl.BlockSpec((EXPERT_TM, D_HALF), rows), wspec, bspec, wspec, bspec, wspec, bspec],
        out_specs=pl.BlockSpec((EXPERT_TM, D_HALF), rows),
    )
    b3 = lambda a: a.astype(F32).reshape(N_EXPERTS, 1, D_MODEL)
    return pl.pallas_call(
        _expert_kernel,
        grid_spec=grid_spec,
        out_shape=jax.ShapeDtypeStruct((n_slots, D_HALF), jnp.int32),
        compiler_params=_cparams(("arbitrary",)),
        name="experts",
    )(block_expert, n_used, n_valid, xs, w_gate, b3(b_gate), w_up, b3(b_up), w_down, b3(b_down))


def _sc_row_gather(table, idx):
    n_idx = idx.shape[0]
    width = table.shape[1]
    per_w = n_idx // SC_WORKERS
    nch = per_w // SC_CHUNK
    assert n_idx == SC_WORKERS * nch * SC_CHUNK and nch % 2 == 0
    mesh = plsc.VectorSubcoreMesh(core_axis_name="core", subcore_axis_name="subcore")

    @functools.partial(
        pl.kernel, mesh=mesh,
        out_type=jax.ShapeDtypeStruct((n_idx, width), table.dtype),
        scratch_types=[pltpu.VMEM((per_w,), jnp.int32), pltpu.VMEM((2, SC_CHUNK, width), table.dtype),
                       pltpu.SemaphoreType.DMA((2,)), pltpu.SemaphoreType.DMA((2,))])
    def gather_kernel(table_hbm, idx_hbm, out_hbm, idx_v, rows_v, sem_g, sem_p):
        wid = lax.axis_index("subcore") * SC_CORES + lax.axis_index("core")
        base = wid * per_w
        pltpu.sync_copy(idx_hbm.at[pl.ds(base, per_w)], idx_v)

        def gather(c, b):
            return pltpu.make_async_copy(table_hbm.at[idx_v.at[pl.ds(c * SC_CHUNK, SC_CHUNK)]], rows_v.at[b], sem_g.at[b])

        def put(c, b):
            return pltpu.make_async_copy(rows_v.at[b], out_hbm.at[pl.ds(base + c * SC_CHUNK, SC_CHUNK)], sem_p.at[b])

        gather(0, 0).start()

        @pl.loop(0, nch, step=2)
        def _(c):
            for b in (0, 1):
                cc = c + b

                @pl.when(cc >= 1)
                def _():
                    put(cc - 1, 1 - b).wait()

                @pl.when(cc + 1 < nch)
                def _():
                    gather(cc + 1, 1 - b).start()

                gather(cc, b).wait()
                put(cc, b).start()

        put(nch - 1, 1).wait()

    return gather_kernel(table, idx)


def _combine_kernel(yt_ref, gate_ref, x1_ref, g_ref, o_ref):
    gate = gate_ref[...].T
    x = x1_ref[...]
    for k in range(TOP_K):
        x = x + gate[:, k:k + 1] * _unpack_bf16_halves(yt_ref[k])
    o_ref[...] = _rms(x) * g_ref[...]


def _combine(yt, gates, x1, g_final):
    n = x1.shape[0]
    yt = yt.reshape(TOP_K, n, D_HALF)
    row = lambda w: pl.BlockSpec((COMBINE_R, w), lambda i: (i, 0))
    return pl.pallas_call(
        _combine_kernel,
        grid=(n // COMBINE_R,),
        in_specs=[pl.BlockSpec((TOP_K, COMBINE_R, D_HALF), lambda i: (0, i, 0)),
                  pl.BlockSpec((2 * TOP_K, COMBINE_R), lambda i: (0, i)), row(D_MODEL),
                  pl.BlockSpec((1, D_MODEL), lambda i: (0, 0))],
        out_specs=row(D_MODEL),
        out_shape=jax.ShapeDtypeStruct((n, D_MODEL), F32),
        compiler_params=_cparams(("parallel",)),
        name="combine",
    )(yt, gates, x1, g_final.reshape(1, D_MODEL))


def _slot_layout(meta, counts_f, n):
    eid = meta[:TOP_K]
    rank = meta[TOP_K:]
    counts = counts_f[:, 0].astype(jnp.int32)
    padded = (counts + EXPERT_TM - 1) // EXPERT_TM * EXPERT_TM
    pend = jnp.cumsum(padded)
    pstart = pend - padded
    experts = jnp.arange(N_EXPERTS, dtype=jnp.int32)
    dest = rank + jnp.sum(jnp.where(eid[..., None] == experts, pstart, 0), axis=-1)
    n_slots = -(-(n * TOP_K + N_EXPERTS * (EXPERT_TM - 1)) // EXPERT_TM) * EXPERT_TM
    block_start = jnp.arange(n_slots // EXPERT_TM, dtype=jnp.int32) * EXPERT_TM
    block_expert = jnp.minimum(jnp.sum(pend[None, :] <= block_start[:, None], axis=-1), N_EXPERTS - 1)
    last_valid = jnp.sum(jnp.where(block_expert[:, None] == experts, pstart + counts, 0), axis=-1)
    n_valid = jnp.clip(last_valid - block_start, 0, EXPERT_TM).astype(jnp.int32)
    n_used = (pend[-1:] // EXPERT_TM).astype(jnp.int32)
    return dest.astype(jnp.int32), block_expert.astype(jnp.int32), n_used, n_valid, n_slots


def kernel(x, g_attn, w_qkv, rel_bias, g_out_dil, g_out_sb, w_o, g_moe, w_router, b_router,
           w_gate, b_gate, w_up, b_up, w_down, b_down, g_final):
    b, s, d = x.shape
    n = b * s
    assert d == D_MODEL and w_qkv.shape == (D_MODEL, 3 * D_MODEL) and w_gate.shape == (N_EXPERTS, D_MODEL, D_MODEL)
    assert s % (SB_G * SB_T) == 0 and all(s % (dil * BAND) == 0 for _w, dil in DIL_PATTERNS)
    assert n % QKV_TM == 0 and n % MIX_TM == 0 and n % COMBINE_R == 0
    x2 = x.reshape(n, d)
    qkv, *strided = _qkv(x2, g_attn, w_qkv)
    sources = dict(zip(STRIDED_DILATIONS, strided))
    dil = [_dilated(sources.get(dilation, qkv), b, s, rel_bias, dilation) for _window, dilation in DIL_PATTERNS]
    ob = _stickbreaking(qkv.reshape(b, s, 3 * d))
    x1, h2, meta, gates, counts = _mix_route([o for o, _ in dil], [l for _, l in dil], ob, x2,
                                             g_out_dil, g_out_sb, w_o, g_moe, w_router, b_router)
    dest, block_expert, n_used, n_valid, n_slots = _slot_layout(meta, counts, n)
    xs = _dispatch(h2, dest, n_slots)
    ys = _experts(xs, block_expert, n_used, n_valid, w_gate, b_gate, w_up, b_up, w_down, b_down)
    yt = _sc_row_gather(ys, dest.reshape(-1))
    return _combine(yt, gates, x1, g_final).reshape(b, s, d)
```

```python
import functools
import math

import jax
import jax.numpy as jnp
from jax import lax
from jax.experimental import pallas as pl
from jax.experimental.pallas import tpu as pltpu
from jax.experimental.pallas import tpu_sc as plsc

F32 = jnp.float32
BF16 = jnp.bfloat16

D_MODEL = 1024
HEAD_DIM = 64
D_HALF = 512
N_HEADS = 8
LANES = 128
N_PAIRS = D_HALF // LANES
DIL_PATTERNS = ((128, 1), (512, 4), (2048, 16))
BAND = 128
REL_BUCKETS = 32
REL_MAX_DISTANCE = 2048
N_EXPERTS = 32
TOP_K = 4
SWIGLU_ALPHA = 1.702
SWIGLU_LIMIT = 7.0
RMS_EPS = 1e-6
NEG_INF = -1e30

QKV_TM = 512
DIL_G = 8
A_COLS = 3 * D_HALF
STRIDED_DILATIONS = tuple(d for _w, d in DIL_PATTERNS if d > 1)
LOG2E = math.log2(math.e)
SB_T = 256
SB_G = 8
SB_EXP_ZERO = 104.0
MIX_TM = 512
EXPERT_TM = 1024
COMBINE_R = 512
SC_CORES = 2
SC_SUBCORES = 16
SC_WORKERS = SC_CORES * SC_SUBCORES
SC_CHUNK = 64
SC_SCATTER_ROWS = 128
SC_SCATTER_COLS = 256
VMEM_LIMIT = 56 * 1024 * 1024


def _cparams(sem):
    return pltpu.CompilerParams(dimension_semantics=sem, vmem_limit_bytes=VMEM_LIMIT)


def _mm(a, b):
    return jnp.dot(a, b, preferred_element_type=F32)


def _mm_nt(a, b):
    return lax.dot_general(a, b, (((1,), (1,)), ((), ())), preferred_element_type=F32)


def _split_bf16(a):
    hi = a.astype(BF16)
    lo = (a - hi.astype(F32)).astype(BF16)
    return hi, lo


def _mm_split(a, b_bf16):
    hi, lo = _split_bf16(a)
    return _mm(hi, b_bf16) + _mm(lo, b_bf16)


def _pack_bf16_halves(x):
    c = x.shape[1] // 2
    return pltpu.pack_elementwise([x[:, :c], x[:, c:]], packed_dtype=BF16)


def _unpack_bf16_halves(p):
    return jnp.concatenate([pltpu.unpack_elementwise(p, index=i, packed_dtype=BF16, unpacked_dtype=F32)
                            for i in (0, 1)], axis=1)


def _rms(x):
    return x * lax.rsqrt(jnp.mean(x * x, axis=-1, keepdims=True) + RMS_EPS)


def _half_masks(dtype):
    lane = lax.broadcasted_iota(jnp.int32, (1, LANES), 1)
    lo = jnp.where(lane < HEAD_DIM, 1.0, 0.0).astype(dtype)
    hi = jnp.where(lane >= HEAD_DIM, 1.0, 0.0).astype(dtype)
    return lo, hi


def _qkv_kernel(x_ref, g_ref, w_ref, o_ref, *rest):
    stage_ref = rest[-1]
    h = (_rms(x_ref[...]) * g_ref[...]).astype(BF16)
    for c in range(3 * D_MODEL // D_HALF):
        y = _mm(h, w_ref[:, c * D_HALF:(c + 1) * D_HALF])
        if c == 0:
            y = y * (1.0 / math.sqrt(HEAD_DIM))
        if c == 3:
            y = y * (LOG2E / math.sqrt(HEAD_DIM))
        o_ref[:, c * D_HALF:(c + 1) * D_HALF] = y.astype(BF16)
        if c < 3:
            for p in range(N_PAIRS):
                stage_ref[c * N_PAIRS + p] = y[:, p * LANES:(p + 1) * LANES]
    for od_ref, d in zip(rest[:-1], STRIDED_DILATIONS):
        for r in range(d):
            for ch in range(A_COLS // LANES):
                col = r * A_COLS + ch * LANES
                od_ref[:, col:col + LANES] = stage_ref[ch, pl.ds(r, QKV_TM // d, stride=d), :].astype(BF16)


def _qkv(x2, g_attn, w_qkv):
    n = x2.shape[0]
    strided = [(n // d, d * A_COLS) for d in STRIDED_DILATIONS]
    return pl.pallas_call(
        _qkv_kernel,
        grid=(n // QKV_TM,),
        in_specs=[pl.BlockSpec((QKV_TM, D_MODEL), lambda i: (i, 0)),
                  pl.BlockSpec((1, D_MODEL), lambda i: (0, 0)),
                  pl.BlockSpec((D_MODEL, 3 * D_MODEL), lambda i: (0, 0))],
        out_specs=[pl.BlockSpec((QKV_TM, 3 * D_MODEL), lambda i: (i, 0))]
                  + [pl.BlockSpec((QKV_TM // d, d * A_COLS), lambda i: (i, 0)) for d in STRIDED_DILATIONS],
        out_shape=[jax.ShapeDtypeStruct((n, 3 * D_MODEL), BF16)]
                  + [jax.ShapeDtypeStruct(shape, BF16) for shape in strided],
        scratch_shapes=[pltpu.VMEM((A_COLS // LANES, QKV_TM, LANES), F32)],
        compiler_params=_cparams(("parallel",)),
        name="qkv",
    )(x2, g_attn.reshape(1, D_MODEL), w_qkv.astype(BF16))


def _t5_bucket(dist):
    max_exact = REL_BUCKETS // 2
    d_f = jnp.maximum(dist, 1).astype(jnp.float32)
    large = max_exact + (jnp.log(d_f / max_exact)
                         / math.log(REL_MAX_DISTANCE / max_exact)
                         * (REL_BUCKETS - max_exact)).astype(jnp.int32)
    large = jnp.minimum(large, REL_BUCKETS - 1)
    return jnp.where(dist < max_exact, dist, large)


def _band_bias(rel_bias, dilation):
    qi = jnp.arange(BAND, dtype=jnp.int32)[:, None]
    kj = jnp.arange(2 * BAND, dtype=jnp.int32)[None, :]
    rel = qi + BAND - kj
    ok = (rel >= 0) & (rel <= BAND)
    bucket = _t5_bucket(jnp.maximum(rel, 0) * dilation)
    table = rel_bias.astype(F32)
    bias = jnp.zeros((N_HEADS, BAND, 2 * BAND), F32)
    for bk in range(REL_BUCKETS):
        bias = jnp.where(bucket[None] == bk, table[bk][:, None, None], bias)
    return jnp.where(ok[None], bias, NEG_INF)


def _dilated_kernel(cur_ref, prev_ref, bias_ref, o_ref, lse_ref, *, n_blocks, n_res):
    n = pl.program_id(2)
    kj = lax.broadcasted_iota(jnp.int32, (BAND, 2 * BAND), 1)
    has_prev = jnp.logical_or(n > 0, kj >= BAND)
    lane = lax.broadcasted_iota(jnp.int32, (BAND, LANES), 1)
    m_lo, m_hi = _half_masks(BF16)
    for j in range(n_res):
        for g in range(n_blocks):
            rows = slice(g * BAND, (g + 1) * BAND)
            prev = slice((g - 1) * BAND, g * BAND)
            lse_tile = jnp.zeros((BAND, LANES), F32)
            for p in range(N_PAIRS):
                qc, kc, vc = [slice(j * A_COLS + sec * D_HALF + p * LANES, j * A_COLS + sec * D_HALF + (p + 1) * LANES)
                              for sec in range(3)]
                q2 = cur_ref[0, rows, qc]
                k_prev = prev_ref[0, :, kc] if g == 0 else cur_ref[0, prev, kc]
                v_prev = prev_ref[0, :, vc] if g == 0 else cur_ref[0, prev, vc]
                kcat = jnp.concatenate([k_prev, cur_ref[0, rows, kc]], axis=0)
                vcat = jnp.concatenate([v_prev, cur_ref[0, rows, vc]], axis=0)
                outs = []
                for s, msk in enumerate((m_lo, m_hi)):
                    h = 2 * p + s
                    logits = _mm_nt(q2 * msk, kcat) + bias_ref[h]
                    if g == 0:
                        logits = jnp.where(has_prev, logits, NEG_INF)
                    m = jnp.max(logits, axis=-1, keepdims=True)
                    pr = jnp.exp(logits - m)
                    den = jnp.sum(pr, axis=-1, keepdims=True)
                    outs.append(_mm(pr.astype(BF16), vcat))
                    lse_tile = jnp.where(lane == h, m, lse_tile)
                    lse_tile = jnp.where(lane == N_HEADS + h, den, lse_tile)
                oc = slice(j * D_HALF + p * LANES, j * D_HALF + (p + 1) * LANES)
                o_ref[0, rows, oc] = jnp.where(lane < HEAD_DIM, outs[0], outs[1])
            lse_ref[0, rows, j * LANES:(j + 1) * LANES] = lse_tile


def _dilated(src, b, s, rel_bias, dilation):
    d = dilation
    l = s // d
    g_blocks = min(DIL_G, l // BAND)
    nstep = l // (g_blocks * BAND)
    width = src.shape[1] // d
    n_res = min(d, DIL_G // g_blocks) if width == A_COLS else 1
    view = src.reshape(b, l, src.shape[1])
    cols = lambda r: r * (width // A_COLS) if n_res == 1 else r
    o, lse = pl.pallas_call(
        functools.partial(_dilated_kernel, n_blocks=g_blocks, n_res=n_res),
        grid=(b, d // n_res, nstep),
        in_specs=[pl.BlockSpec((1, g_blocks * BAND, n_res * A_COLS), lambda bi, r, n: (bi, n, cols(r))),
                  pl.BlockSpec((1, BAND, n_res * A_COLS),
                               lambda bi, r, n: (bi, jnp.maximum(n * g_blocks - 1, 0), cols(r))),
                  pl.BlockSpec((N_HEADS, BAND, 2 * BAND), lambda bi, r, n: (0, 0, 0))],
        out_specs=[pl.BlockSpec((1, g_blocks * BAND, n_res * D_HALF), lambda bi, r, n: (bi, n, r)),
                   pl.BlockSpec((1, g_blocks * BAND, n_res * LANES), lambda bi, r, n: (bi, n, r))],
        out_shape=[jax.ShapeDtypeStruct((b, l, d * D_HALF), F32),
                   jax.ShapeDtypeStruct((b, l, d * LANES), F32)],
        compiler_params=_cparams(("parallel", "parallel", "arbitrary")),
        name=f"dilated_d{d}",
    )(view, view, _band_bias(rel_bias, d))
    return o.reshape(b * l, d * D_HALF), lse.reshape(b * l, d * LANES)


def _sb_kernel(q_ref, k_ref, v_ref, u_ref, o_ref, acc_ref, carry_ref):
    i = pl.program_id(2)
    m_lo, m_hi = _half_masks(BF16)
    row = lax.broadcasted_iota(jnp.int32, (2 * SB_T, SB_T), 0) & (SB_T - 1)
    col = lax.broadcasted_iota(jnp.int32, (2 * SB_T, SB_T), 1)
    causal = col < row
    lane = lax.broadcasted_iota(jnp.int32, (SB_T, LANES), 1)
    u = u_ref[...]

    def tile(qs, j, mask, carry):
        start = pl.multiple_of(j * SB_T, SB_T)
        kj = k_ref[0, pl.ds(start, SB_T), :]
        vj = v_ref[0, pl.ds(start, SB_T), :]
        z = _mm_nt(qs, kj)
        neg_abs = lax.bitcast_convert_type(
            lax.bitcast_convert_type(z, jnp.uint32) | jnp.uint32(0x80000000), F32)
        sp = jnp.log2(1.0 + jnp.exp2(neg_abs))
        log_b = jnp.minimum(z, 0.0) - sp
        l1m = log_b - z
        if mask is not None:
            l1m = jnp.where(mask, l1m, 0.0)
        cum = _mm(l1m.astype(BF16), u)
        total = cum[:, 0:1] + l1m[:, 0:1]
        a = jnp.exp2(log_b + cum + carry)
        if mask is not None:
            a = jnp.where(mask, a, 0.0)
        return _mm(a.astype(BF16), vj), total

    def more(state):
        j, top = state
        return jnp.logical_and(j >= 0, top > -SB_EXP_ZERO * LOG2E)

    tops = []
    for g in range(SB_G):
        blk = i * SB_G + g
        q2 = q_ref[0, g * SB_T:(g + 1) * SB_T, :]
        qs = jnp.concatenate([q2 * m_lo, q2 * m_hi], axis=0)
        pv_d, tot_d = tile(qs, blk, causal, 0.0)
        if g == 0:
            pv_p, tot_p = tile(qs, jnp.maximum(blk - 1, 0), blk > 0, tot_d)
        else:
            pv_p, tot_p = tile(qs, blk - 1, None, tot_d)
        acc_ref[g] = pv_d + pv_p
        carry0 = tot_d + tot_p
        carry_ref[g] = carry0
        tops.append(jnp.max(carry0))

    for g in range(SB_G):
        def body(state, g=g):
            j, _ = state
            q2 = q_ref[0, g * SB_T:(g + 1) * SB_T, :]
            qs = jnp.concatenate([q2 * m_lo, q2 * m_hi], axis=0)
            pv, tot = tile(qs, j, None, carry_ref[g])
            acc_ref[g] += pv
            carry = carry_ref[g] + tot
            carry_ref[g] = carry
            return j - 1, jnp.max(carry)

        lax.while_loop(more, body, (i * SB_G + g - 2, tops[g]))
        o_ref[0, g * SB_T:(g + 1) * SB_T, :] = jnp.where(lane < HEAD_DIM, acc_ref[g, :SB_T], acc_ref[g, SB_T:])


def _suffix_matrix():
    sp = jnp.arange(SB_T, dtype=jnp.int32)[:, None]
    sc = jnp.arange(SB_T, dtype=jnp.int32)[None, :]
    return jnp.where(sp > sc, 1.0, 0.0).astype(BF16)


def _stickbreaking(qkv3):
    b, s, _ = qkv3.shape
    sec_q, sec_k, sec_v = 3 * N_PAIRS, 4 * N_PAIRS, 5 * N_PAIRS
    o = pl.pallas_call(
        _sb_kernel,
        grid=(b, N_PAIRS, s // (SB_G * SB_T)),
        in_specs=[pl.BlockSpec((1, SB_G * SB_T, LANES), lambda bi, p, i: (bi, i, sec_q + p)),
                  pl.BlockSpec((1, s, LANES), lambda bi, p, i: (bi, 0, sec_k + p)),
                  pl.BlockSpec((1, s, LANES), lambda bi, p, i: (bi, 0, sec_v + p)),
                  pl.BlockSpec((SB_T, SB_T), lambda bi, p, i: (0, 0))],
        out_specs=pl.BlockSpec((1, SB_G * SB_T, LANES), lambda bi, p, i: (bi, i, p)),
        out_shape=jax.ShapeDtypeStruct((b, s, D_HALF), F32),
        scratch_shapes=[pltpu.VMEM((SB_G, 2 * SB_T, LANES), F32), pltpu.VMEM((SB_G, 2 * SB_T, 1), F32)],
        compiler_params=_cparams(("parallel", "parallel", "arbitrary")),
        name="stickbreaking",
    )(qkv3, qkv3, qkv3, _suffix_matrix())
    return o.reshape(b * s, D_HALF)


def _token_major(ref, d, st_ref):
    if d == 1:
        return ref[...]
    width = ref.shape[1] // d
    for r in range(d):
        for ch in range(width // LANES):
            col = r * width + ch * LANES
            st_ref[ch, pl.ds(r, MIX_TM // d, stride=d), :] = ref[:, col:col + LANES]
    return jnp.concatenate([st_ref[ch] for ch in range(width // LANES)], axis=1)


def _mix_kernel(o1_ref, o2_ref, o3_ref, l1_ref, l2_ref, l3_ref, ob_ref, x_ref,
                ga_ref, gb_ref, e_ref, wo_ref, gm_ref, wr_ref, br_ref, tri_ref,
                x1_ref, h2_ref, meta_ref, gate_ref, cnt_ref, carry_ref, *scratch):
    dils = [d for _w, d in DIL_PATTERNS]
    o_st = dict(zip(STRIDED_DILATIONS, scratch[:len(STRIDED_DILATIONS)]))
    l_st = dict(zip(STRIDED_DILATIONS, scratch[len(STRIDED_DILATIONS):]))
    o1, o2, o3 = [_token_major(r, d, o_st.get(d)) for r, d in zip((o1_ref, o2_ref, o3_ref), dils)]
    l1, l2, l3 = [_token_major(r, d, l_st.get(d)) for r, d in zip((l1_ref, l2_ref, l3_ref), dils)]
    lane = lax.broadcasted_iota(jnp.int32, l1.shape, 1)
    m = jnp.maximum(jnp.maximum(l1, l2), l3)
    e1, e2, e3 = jnp.exp(l1 - m), jnp.exp(l2 - m), jnp.exp(l3 - m)
    s1, s2, s3 = [pltpu.roll(l, LANES - N_HEADS, 1) for l in (l1, l2, l3)]
    inv = 1.0 / (e1 * s1 + e2 * s2 + e3 * s3)
    w1, w2, w3 = [jnp.where(lane < N_HEADS, e * inv, 0.0) for e in (e1, e2, e3)]
    expand = e_ref[...]
    oa = (_mm_split(w1, expand) * o1
          + _mm_split(w2, expand) * o2
          + _mm_split(w3, expand) * o3)
    oa = (_rms(oa) * ga_ref[...]).astype(BF16)
    ob = (_rms(ob_ref[...]) * gb_ref[...]).astype(BF16)
    mix = _mm(oa, wo_ref[:D_HALF, :]) + _mm(ob, wo_ref[D_HALF:, :])
    x1 = x_ref[...] + mix
    x1_ref[...] = x1
    _route(x1, gm_ref, wr_ref, br_ref, tri_ref, h2_ref, meta_ref, gate_ref, cnt_ref, carry_ref)


def _head_expand():
    lane = jnp.arange(LANES, dtype=jnp.int32)[:, None]
    col = jnp.arange(D_HALF, dtype=jnp.int32)[None, :]
    return jnp.where(col // HEAD_DIM == lane, 1.0, 0.0).astype(BF16)


def _mix_route(os_, lses, ob, x2, g_out_dil, g_out_sb, w_o, g_moe, w_router, b_router):
    n = x2.shape[0]
    wr = w_router.astype(F32).T
    wr_hi = wr.astype(BF16)
    wr_lo = (wr - wr_hi.astype(F32)).astype(BF16)
    br = jnp.broadcast_to(b_router.astype(F32)[:, None], (N_EXPERTS, LANES))
    r = jnp.arange(MIX_TM, dtype=jnp.int32)
    tri = jnp.where(r[:, None] < r[None, :], 1.0, 0.0).astype(BF16)
    col = lambda rows: pl.BlockSpec((rows, MIX_TM), lambda i: (0, i))
    row = lambda w: pl.BlockSpec((MIX_TM, w), lambda i: (i, 0))
    srow = lambda a: pl.BlockSpec((MIX_TM * a.shape[0] // n, a.shape[1]), lambda i: (i, 0))
    const = lambda shape: pl.BlockSpec(shape, lambda i: (0, 0))
    return pl.pallas_call(
        _mix_kernel,
        grid=(n // MIX_TM,),
        in_specs=[srow(a) for a in os_] + [srow(a) for a in lses] + [row(D_HALF), row(D_MODEL),
                  const((1, D_HALF)), const((1, D_HALF)), const((LANES, D_HALF)), const((D_MODEL, D_MODEL)),
                  const((1, D_MODEL)), const((2 * N_EXPERTS, D_MODEL)), const((N_EXPERTS, LANES)), const((MIX_TM, MIX_TM))],
        out_specs=[row(D_MODEL), row(D_HALF), col(2 * TOP_K), col(2 * TOP_K), const((N_EXPERTS, LANES))],
        out_shape=[jax.ShapeDtypeStruct((n, D_MODEL), F32),
                   jax.ShapeDtypeStruct((n, D_HALF), jnp.int32),
                   jax.ShapeDtypeStruct((2 * TOP_K, n), jnp.int32),
                   jax.ShapeDtypeStruct((2 * TOP_K, n), F32),
                   jax.ShapeDtypeStruct((N_EXPERTS, LANES), F32)],
        scratch_shapes=[pltpu.VMEM((N_EXPERTS, LANES), F32)]
                       + [pltpu.VMEM((D_HALF // LANES, MIX_TM, LANES), F32) for _ in STRIDED_DILATIONS]
                       + [pltpu.VMEM((1, MIX_TM, LANES), F32) for _ in STRIDED_DILATIONS],
        compiler_params=_cparams(("arbitrary",)),
        name="mix_route",
    )(*os_, *lses, ob, x2, g_out_dil.reshape(1, D_HALF), g_out_sb.reshape(1, D_HALF),
      _head_expand(), w_o.astype(BF16), g_moe.reshape(1, D_MODEL), jnp.concatenate([wr_hi, wr_lo], axis=0), br, tri)


def _route(x1, g_ref, wr_ref, br_ref, tri_ref, h2_ref, meta_ref, gate_ref, cnt_ref, carry_ref):
    tm = x1.shape[0]

    @pl.when(pl.program_id(0) == 0)
    def _():
        carry_ref[...] = jnp.zeros_like(carry_ref)

    h2 = _rms(x1) * g_ref[...]
    h2_ref[...] = _pack_bf16_halves(h2)
    parts = _mm_nt(wr_ref[...], jnp.concatenate(_split_bf16(h2), axis=0))
    hi, lo = parts[:N_EXPERTS], parts[N_EXPERTS:]
    logits = (hi[:, :tm] + br_ref[:, 0:1]) + ((hi[:, tm:] + lo[:, :tm]) + lo[:, tm:])
    expert = lax.broadcasted_iota(jnp.int32, (N_EXPERTS, tm), 0).astype(F32)

    onehots, vals, ids = [], [], []
    for _k in range(TOP_K):
        m = jnp.max(logits, axis=0, keepdims=True)
        idx = jnp.min(jnp.where(logits == m, expert, float(N_EXPERTS)), axis=0, keepdims=True)
        oh = expert == idx
        logits = jnp.where(oh, -jnp.inf, logits)
        onehots.append(oh)
        vals.append(m)
        ids.append(idx)

    es = [jnp.exp(v - vals[0]) for v in vals]
    inv = 1.0 / (es[0] + es[1] + es[2] + es[3])

    sel = jnp.zeros((N_EXPERTS, tm), F32)
    for oh in onehots:
        sel = jnp.where(oh, 1.0, sel)
    before = _mm(sel.astype(BF16), tri_ref[...]) + carry_ref[:, 0:1]
    ranks = [jnp.sum(jnp.where(oh, before, 0.0), axis=0, keepdims=True) for oh in onehots]
    zero = jnp.zeros((1, tm), F32)
    meta_ref[...] = jnp.concatenate(ids + ranks, axis=0).astype(jnp.int32)
    gate_ref[...] = jnp.concatenate([e * inv for e in es] + [zero] * TOP_K, axis=0)
    total = carry_ref[...] + jnp.sum(sel, axis=1, keepdims=True)
    carry_ref[...] = total
    cnt_ref[...] = total


def _dispatch(h2, dest, n_slots):
    n = h2.shape[0]
    per_w = n // SC_WORKERS
    nch = per_w // SC_SCATTER_ROWS
    nslab = h2.shape[1] // SC_SCATTER_COLS
    nj = nch * nslab
    assert n == SC_WORKERS * nch * SC_SCATTER_ROWS and h2.shape[1] == nslab * SC_SCATTER_COLS and nj % 2 == 0
    idx = dest.reshape(TOP_K, SC_WORKERS, nch, SC_SCATTER_ROWS).transpose(1, 2, 0, 3)
    idx = idx.reshape(SC_WORKERS * nch * TOP_K, SC_SCATTER_ROWS)
    mesh = plsc.VectorSubcoreMesh(core_axis_name="core", subcore_axis_name="subcore")

    @functools.partial(
        pl.kernel, mesh=mesh,
        out_type=jax.ShapeDtypeStruct((n_slots, h2.shape[1]), h2.dtype),
        scratch_types=[pltpu.VMEM((nch * TOP_K, SC_SCATTER_ROWS), jnp.int32),
                       pltpu.VMEM((2, SC_SCATTER_ROWS, SC_SCATTER_COLS), h2.dtype),
                       pltpu.SemaphoreType.DMA((2,)), pltpu.SemaphoreType.DMA((2,))])
    def scatter_kernel(h2_hbm, idx_hbm, xs_hbm, idx_v, rows_v, sem_l, sem_s):
        wid = lax.axis_index("subcore") * SC_CORES + lax.axis_index("core")
        t0 = wid * per_w
        pltpu.sync_copy(idx_hbm.at[pl.ds(wid * (nch * TOP_K), nch * TOP_K)], idx_v)

        def load(j, b):
            rows = pl.ds(t0 + (j // nslab) * SC_SCATTER_ROWS, SC_SCATTER_ROWS)
            cols = pl.ds((j % nslab) * SC_SCATTER_COLS, SC_SCATTER_COLS)
            return pltpu.make_async_copy(h2_hbm.at[rows, cols], rows_v.at[b], sem_l.at[b])

        def scatter(j, k, b):
            cols = pl.ds((j % nslab) * SC_SCATTER_COLS, SC_SCATTER_COLS)
            return pltpu.make_async_copy(rows_v.at[b], xs_hbm.at[idx_v.at[(j // nslab) * TOP_K + k], cols], sem_s.at[b])

        load(0, 0).start()

        @pl.loop(0, nj, step=2)
        def _(j0):
            for b in (0, 1):
                j = j0 + b

                @pl.when(j >= 1)
                def _():
                    for k in range(TOP_K):
                        scatter(j - 1, k, 1 - b).wait()

                @pl.when(j + 1 < nj)
                def _():
                    load(j + 1, 1 - b).start()

                load(j, b).wait()
                for k in range(TOP_K):
                    scatter(j, k, b).start()

        for k in range(TOP_K):
            scatter(nj - 1, k, 1).wait()

    return scatter_kernel(h2, idx)


def _expert_kernel(be_ref, nu_ref, nv_ref, x_ref, wg_ref, bg_ref, wu_ref, bu_ref, wd_ref, bd_ref, y_ref):
    del be_ref
    i = pl.program_id(0)

    @pl.when(i < nu_ref[0])
    def _():
        row = lax.broadcasted_iota(jnp.int32, (EXPERT_TM, 1), 0)
        xb = _unpack_bf16_halves(jnp.where(row < nv_ref[i], x_ref[...], jnp.int32(0))).astype(BF16)
        glu = _mm(xb, wg_ref[0].astype(BF16)) + bg_ref[0]
        lin = _mm(xb, wu_ref[0].astype(BF16)) + bu_ref[0]
        glu = jnp.minimum(glu, SWIGLU_LIMIT)
        lin = jnp.clip(lin, -SWIGLU_LIMIT, SWIGLU_LIMIT)
        act = (0.5 * glu) * (1.0 + jnp.tanh((0.5 * SWIGLU_ALPHA) * glu)) * (lin + 1.0)
        y = _mm(act.astype(BF16), wd_ref[0].astype(BF16)) + bd_ref[0]
        y_ref[...] = _pack_bf16_halves(y)


def _experts(xs, block_expert, n_used, n_valid, w_gate, b_gate, w_up, b_up, w_down, b_down):
    n_slots = xs.shape[0]
    n_blocks = n_slots // EXPERT_TM
    rows = lambda i, be, nu, nv: (jnp.minimum(i, nu[0] - 1), 0)
    wspec = pl.BlockSpec((1, D_MODEL, D_MODEL), lambda i, be, nu, nv: (be[i], 0, 0))
    bspec = pl.BlockSpec((1, 1, D_MODEL), lambda i, be, nu, nv: (be[i], 0, 0))
    grid_spec = pltpu.PrefetchScalarGridSpec(
        num_scalar_prefetch=3,
        grid=(n_blocks,),
        in_specs=[pl.BlockSpec((EXPERT_TM, D_HALF), rows), wspec, bspec, wspec, bspec, wspec, bspec],
        out_specs=pl.BlockSpec((EXPERT_TM, D_HALF), rows),
    )
    b3 = lambda a: a.astype(F32).reshape(N_EXPERTS, 1, D_MODEL)
    return pl.pallas_call(
        _expert_kernel,
        grid_spec=grid_spec,
        out_shape=jax.ShapeDtypeStruct((n_slots, D_HALF), jnp.int32),
        compiler_params=_cparams(("arbitrary",)),
        name="experts",
    )(block_expert, n_used, n_valid, xs, w_gate, b3(b_gate), w_up, b3(b_up), w_down, b3(b_down))


def _sc_row_gather(table, idx):
    n_idx = idx.shape[0]
    width = table.shape[1]
    per_w = n_idx // SC_WORKERS
    nch = per_w // SC_CHUNK
    assert n_idx == SC_WORKERS * nch * SC_CHUNK and nch % 2 == 0
    mesh = plsc.VectorSubcoreMesh(core_axis_name="core", subcore_axis_name="subcore")

    @functools.partial(
        pl.kernel, mesh=mesh,
        out_type=jax.ShapeDtypeStruct((n_idx, width), table.dtype),
        scratch_types=[pltpu.VMEM((per_w,), jnp.int32), pltpu.VMEM((2, SC_CHUNK, width), table.dtype),
                       pltpu.SemaphoreType.DMA((2,)), pltpu.SemaphoreType.DMA((2,))])
    def gather_kernel(table_hbm, idx_hbm, out_hbm, idx_v, rows_v, sem_g, sem_p):
        wid = lax.axis_index("subcore") * SC_CORES + lax.axis_index("core")
        base = wid * per_w
        pltpu.sync_copy(idx_hbm.at[pl.ds(base, per_w)], idx_v)

        def gather(c, b):
            return pltpu.make_async_copy(table_hbm.at[idx_v.at[pl.ds(c * SC_CHUNK, SC_CHUNK)]], rows_v.at[b], sem_g.at[b])

        def put(c, b):
            return pltpu.make_async_copy(rows_v.at[b], out_hbm.at[pl.ds(base + c * SC_CHUNK, SC_CHUNK)], sem_p.at[b])

        gather(0, 0).start()

        @pl.loop(0, nch, step=2)
        def _(c):
            for b in (0, 1):
                cc = c + b

                @pl.when(cc >= 1)
                def _():
                    put(cc - 1, 1 - b).wait()

                @pl.when(cc + 1 < nch)
                def _():
                    gather(cc + 1, 1 - b).start()

                gather(cc, b).wait()
                put(cc, b).start()

        put(nch - 1, 1).wait()

    return gather_kernel(table, idx)


def _combine_kernel(yt_ref, gate_ref, x1_ref, g_ref, o_ref):
    gate = gate_ref[...].T
    x = x1_ref[...]
    for k in range(TOP_K):
        x = x + gate[:, k:k + 1] * _unpack_bf16_halves(yt_ref[k])
    o_ref[...] = _rms(x) * g_ref[...]


def _combine(yt, gates, x1, g_final):
    n = x1.shape[0]
    yt = yt.reshape(TOP_K, n, D_HALF)
    row = lambda w: pl.BlockSpec((COMBINE_R, w), lambda i: (i, 0))
    return pl.pallas_call(
        _combine_kernel,
        grid=(n // COMBINE_R,),
        in_specs=[pl.BlockSpec((TOP_K, COMBINE_R, D_HALF), lambda i: (0, i, 0)),
                  pl.BlockSpec((2 * TOP_K, COMBINE_R), lambda i: (0, i)), row(D_MODEL),
                  pl.BlockSpec((1, D_MODEL), lambda i: (0, 0))],
        out_specs=row(D_MODEL),
        out_shape=jax.ShapeDtypeStruct((n, D_MODEL), F32),
        compiler_params=_cparams(("parallel",)),
        name="combine",
    )(yt, gates, x1, g_final.reshape(1, D_MODEL))


def _slot_layout(meta, counts_f, n):
    eid = meta[:TOP_K]
    rank = meta[TOP_K:]
    counts = counts_f[:, 0].astype(jnp.int32)
    padded = (counts + EXPERT_TM - 1) // EXPERT_TM * EXPERT_TM
    pend = jnp.cumsum(padded)
    pstart = pend - padded
    experts = jnp.arange(N_EXPERTS, dtype=jnp.int32)
    dest = rank + jnp.sum(jnp.where(eid[..., None] == experts, pstart, 0), axis=-1)
    n_slots = -(-(n * TOP_K + N_EXPERTS * (EXPERT_TM - 1)) // EXPERT_TM) * EXPERT_TM
    block_start = jnp.arange(n_slots // EXPERT_TM, dtype=jnp.int32) * EXPERT_TM
    block_expert = jnp.minimum(jnp.sum(pend[None, :] <= block_start[:, None], axis=-1), N_EXPERTS - 1)
    last_valid = jnp.sum(jnp.where(block_expert[:, None] == experts, pstart + counts, 0), axis=-1)
    n_valid = jnp.clip(last_valid - block_start, 0, EXPERT_TM).astype(jnp.int32)
    n_used = (pend[-1:] // EXPERT_TM).astype(jnp.int32)
    return dest.astype(jnp.int32), block_expert.astype(jnp.int32), n_used, n_valid, n_slots


def kernel(x, g_attn, w_qkv, rel_bias, g_out_dil, g_out_sb, w_o, g_moe, w_router, b_router,
           w_gate, b_gate, w_up, b_up, w_down, b_down, g_final):
    b, s, d = x.shape
    n = b * s
    assert d == D_MODEL and w_qkv.shape == (D_MODEL, 3 * D_MODEL) and w_gate.shape == (N_EXPERTS, D_MODEL, D_MODEL)
    assert s % (SB_G * SB_T) == 0 and all(s % (dil * BAND) == 0 for _w, dil in DIL_PATTERNS)
    assert n % QKV_TM == 0 and n % MIX_TM == 0 and n % COMBINE_R == 0
    x2 = x.reshape(n, d)
    qkv, *strided = _qkv(x2, g_attn, w_qkv)
    sources = dict(zip(STRIDED_DILATIONS, strided))
    dil = [_dilated(sources.get(dilation, qkv), b, s, rel_bias, dilation) for _window, dilation in DIL_PATTERNS]
    ob = _stickbreaking(qkv.reshape(b, s, 3 * d))
    x1, h2, meta, gates, counts = _mix_route([o for o, _ in dil], [l for _, l in dil], ob, x2,
                                             g_out_dil, g_out_sb, w_o, g_moe, w_router, b_router)
    dest, block_expert, n_used, n_valid, n_slots = _slot_layout(meta, counts, n)
    xs = _dispatch(h2, dest, n_slots)
    ys = _experts(xs, block_expert, n_used, n_valid, w_gate, b_gate, w_up, b_up, w_down, b_down)
    yt = _sc_row_gather(ys, dest.reshape(-1))
    return _combine(yt, gates, x1, g_final).reshape(b, s, d)
```

```python
import functools
import math

import jax
import jax.numpy as jnp
from jax import lax
from jax.experimental import pallas as pl
from jax.experimental.pallas import tpu as pltpu
from jax.experimental.pallas import tpu_sc as plsc

F32 = jnp.float32
BF16 = jnp.bfloat16

D_MODEL = 1024
HEAD_DIM = 64
D_HALF = 512
N_HEADS = 8
LANES = 128
N_PAIRS = D_HALF // LANES
DIL_PATTERNS = ((128, 1), (512, 4), (2048, 16))
BAND = 128
REL_BUCKETS = 32
REL_MAX_DISTANCE = 2048
N_EXPERTS = 32
TOP_K = 4
SWIGLU_ALPHA = 1.702
SWIGLU_LIMIT = 7.0
RMS_EPS = 1e-6
NEG_INF = -1e30

QKV_TM = 1024
DIL_G = 8
A_COLS = 3 * D_HALF
STRIDED_DILATIONS = tuple(d for _w, d in DIL_PATTERNS if d > 1)
LOG2E = math.log2(math.e)
SB_T = 256
SB_G = 16
SB_EXP_ZERO = 104.0
MIX_TM = 512
EXPERT_TM = 1024
COMBINE_R = 512
SC_CORES = 2
SC_SUBCORES = 16
SC_WORKERS = SC_CORES * SC_SUBCORES
SC_CHUNK = 64
SC_SCATTER_ROWS = 128
SC_SCATTER_COLS = 256
VMEM_LIMIT = 56 * 1024 * 1024


def _cparams(sem):
    return pltpu.CompilerParams(dimension_semantics=sem, vmem_limit_bytes=VMEM_LIMIT)


def _mm(a, b):
    return jnp.dot(a, b, preferred_element_type=F32)


def _mm_nt(a, b):
    return lax.dot_general(a, b, (((1,), (1,)), ((), ())), preferred_element_type=F32)


def _split_bf16(a):
    hi = a.astype(BF16)
    lo = (a - hi.astype(F32)).astype(BF16)
    return hi, lo


def _mm_split(a, b_bf16):
    hi, lo = _split_bf16(a)
    return _mm(hi, b_bf16) + _mm(lo, b_bf16)


def _pack_bf16_halves(x):
    c = x.shape[1] // 2
    return pltpu.pack_elementwise([x[:, :c], x[:, c:]], packed_dtype=BF16)


def _unpack_bf16_halves(p):
    return jnp.concatenate([pltpu.unpack_elementwise(p, index=i, packed_dtype=BF16, unpacked_dtype=F32)
                            for i in (0, 1)], axis=1)


def _rms(x):
    return x * lax.rsqrt(jnp.mean(x * x, axis=-1, keepdims=True) + RMS_EPS)


def _half_masks(dtype):
    lane = lax.broadcasted_iota(jnp.int32, (1, LANES), 1)
    lo = jnp.where(lane < HEAD_DIM, 1.0, 0.0).astype(dtype)
    hi = jnp.where(lane >= HEAD_DIM, 1.0, 0.0).astype(dtype)
    return lo, hi


def _qkv_kernel(x_ref, g_ref, w_ref, o_ref, *rest):
    stage_ref = rest[-1]
    h = (_rms(x_ref[...]) * g_ref[...]).astype(BF16)
    for c in range(3 * D_MODEL // D_HALF):
        y = _mm(h, w_ref[:, c * D_HALF:(c + 1) * D_HALF])
        if c == 0:
            y = y * (1.0 / math.sqrt(HEAD_DIM))
        if c == 3:
            y = y * (LOG2E / math.sqrt(HEAD_DIM))
        o_ref[:, c * D_HALF:(c + 1) * D_HALF] = y.astype(BF16)
        if c < 3:
            for p in range(N_PAIRS):
                stage_ref[c * N_PAIRS + p] = y[:, p * LANES:(p + 1) * LANES]
    for od_ref, d in zip(rest[:-1], STRIDED_DILATIONS):
        for r in range(d):
            for ch in range(A_COLS // LANES):
                col = r * A_COLS + ch * LANES
                od_ref[:, col:col + LANES] = stage_ref[ch, pl.ds(r, QKV_TM // d, stride=d), :].astype(BF16)


def _qkv(x2, g_attn, w_qkv):
    n = x2.shape[0]
    strided = [(n // d, d * A_COLS) for d in STRIDED_DILATIONS]
    return pl.pallas_call(
        _qkv_kernel,
        grid=(n // QKV_TM,),
        in_specs=[pl.BlockSpec((QKV_TM, D_MODEL), lambda i: (i, 0)),
                  pl.BlockSpec((1, D_MODEL), lambda i: (0, 0)),
                  pl.BlockSpec((D_MODEL, 3 * D_MODEL), lambda i: (0, 0))],
        out_specs=[pl.BlockSpec((QKV_TM, 3 * D_MODEL), lambda i: (i, 0))]
                  + [pl.BlockSpec((QKV_TM // d, d * A_COLS), lambda i: (i, 0)) for d in STRIDED_DILATIONS],
        out_shape=[jax.ShapeDtypeStruct((n, 3 * D_MODEL), BF16)]
                  + [jax.ShapeDtypeStruct(shape, BF16) for shape in strided],
        scratch_shapes=[pltpu.VMEM((A_COLS // LANES, QKV_TM, LANES), F32)],
        compiler_params=_cparams(("parallel",)),
        name="qkv",
    )(x2, g_attn.reshape(1, D_MODEL), w_qkv.astype(BF16))


def _t5_bucket(dist):
    max_exact = REL_BUCKETS // 2
    d_f = jnp.maximum(dist, 1).astype(jnp.float32)
    large = max_exact + (jnp.log(d_f / max_exact)
                         / math.log(REL_MAX_DISTANCE / max_exact)
                         * (REL_BUCKETS - max_exact)).astype(jnp.int32)
    large = jnp.minimum(large, REL_BUCKETS - 1)
    return jnp.where(dist < max_exact, dist, large)


def _band_bias(rel_bias, dilation):
    qi = jnp.arange(BAND, dtype=jnp.int32)[:, None]
    kj = jnp.arange(2 * BAND, dtype=jnp.int32)[None, :]
    rel = qi + BAND - kj
    ok = (rel >= 0) & (rel <= BAND)
    bucket = _t5_bucket(jnp.maximum(rel, 0) * dilation)
    table = rel_bias.astype(F32)
    bias = jnp.zeros((N_HEADS, BAND, 2 * BAND), F32)
    for bk in range(REL_BUCKETS):
        bias = jnp.where(bucket[None] == bk, table[bk][:, None, None], bias)
    return jnp.where(ok[None], bias, NEG_INF)


def _dilated_kernel(cur_ref, prev_ref, bias_ref, o_ref, lse_ref, *, n_blocks, n_res):
    n = pl.program_id(2)
    kj = lax.broadcasted_iota(jnp.int32, (BAND, 2 * BAND), 1)
    has_prev = jnp.logical_or(n > 0, kj >= BAND)
    lane = lax.broadcasted_iota(jnp.int32, (BAND, LANES), 1)
    m_lo, m_hi = _half_masks(BF16)
    for j in range(n_res):
        for g in range(n_blocks):
            rows = slice(g * BAND, (g + 1) * BAND)
            prev = slice((g - 1) * BAND, g * BAND)
            lse_tile = jnp.zeros((BAND, LANES), F32)
            for p in range(N_PAIRS):
                qc, kc, vc = [slice(j * A_COLS + sec * D_HALF + p * LANES, j * A_COLS + sec * D_HALF + (p + 1) * LANES)
                              for sec in range(3)]
                q2 = cur_ref[0, rows, qc]
                k_prev = prev_ref[0, :, kc] if g == 0 else cur_ref[0, prev, kc]
                v_prev = prev_ref[0, :, vc] if g == 0 else cur_ref[0, prev, vc]
                kcat = jnp.concatenate([k_prev, cur_ref[0, rows, kc]], axis=0)
                vcat = jnp.concatenate([v_prev, cur_ref[0, rows, vc]], axis=0)
                outs = []
                for s, msk in enumerate((m_lo, m_hi)):
                    h = 2 * p + s
                    logits = _mm_nt(q2 * msk, kcat) + bias_ref[h]
                    if g == 0:
                        logits = jnp.where(has_prev, logits, NEG_INF)
                    m = jnp.max(logits, axis=-1, keepdims=True)
                    pr = jnp.exp(logits - m)
                    den = jnp.sum(pr, axis=-1, keepdims=True)
                    outs.append(_mm(pr.astype(BF16), vcat))
                    lse_tile = jnp.where(lane == h, m, lse_tile)
                    lse_tile = jnp.where(lane == N_HEADS + h, den, lse_tile)
                oc = slice(j * D_HALF + p * LANES, j * D_HALF + (p + 1) * LANES)
                o_ref[0, rows, oc] = jnp.where(lane < HEAD_DIM, outs[0], outs[1])
            lse_ref[0, rows, j * LANES:(j + 1) * LANES] = lse_tile


def _dilated(src, b, s, rel_bias, dilation):
    d = dilation
    l = s // d
    g_blocks = min(DIL_G, l // BAND)
    nstep = l // (g_blocks * BAND)
    width = src.shape[1] // d
    n_res = min(d, DIL_G // g_blocks) if width == A_COLS else 1
    view = src.reshape(b, l, src.shape[1])
    cols = lambda r: r * (width // A_COLS) if n_res == 1 else r
    o, lse = pl.pallas_call(
        functools.partial(_dilated_kernel, n_blocks=g_blocks, n_res=n_res),
        grid=(b, d // n_res, nstep),
        in_specs=[pl.BlockSpec((1, g_blocks * BAND, n_res * A_COLS), lambda bi, r, n: (bi, n, cols(r))),
                  pl.BlockSpec((1, BAND, n_res * A_COLS),
                               lambda bi, r, n: (bi, jnp.maximum(n * g_blocks - 1, 0), cols(r))),
                  pl.BlockSpec((N_HEADS, BAND, 2 * BAND), lambda bi, r, n: (0, 0, 0))],
        out_specs=[pl.BlockSpec((1, g_blocks * BAND, n_res * D_HALF), lambda bi, r, n: (bi, n, r)),
                   pl.BlockSpec((1, g_blocks * BAND, n_res * LANES), lambda bi, r, n: (bi, n, r))],
        out_shape=[jax.ShapeDtypeStruct((b, l, d * D_HALF), F32),
                   jax.ShapeDtypeStruct((b, l, d * LANES), F32)],
        compiler_params=_cparams(("parallel", "parallel", "arbitrary")),
        name=f"dilated_d{d}",
    )(view, view, _band_bias(rel_bias, d))
    return o.reshape(b * l, d * D_HALF), lse.reshape(b * l, d * LANES)


def _sb_kernel(q_ref, k_ref, v_ref, u_ref, o_ref, acc_ref, carry_ref):
    i = pl.program_id(2)
    m_lo, m_hi = _half_masks(BF16)
    row = lax.broadcasted_iota(jnp.int32, (2 * SB_T, SB_T), 0) & (SB_T - 1)
    col = lax.broadcasted_iota(jnp.int32, (2 * SB_T, SB_T), 1)
    causal = col < row
    lane = lax.broadcasted_iota(jnp.int32, (SB_T, LANES), 1)
    u = u_ref[...]

    def tile(qs, j, mask, carry):
        start = pl.multiple_of(j * SB_T, SB_T)
        kj = k_ref[0, pl.ds(start, SB_T), :]
        vj = v_ref[0, pl.ds(start, SB_T), :]
        z = _mm_nt(qs, kj)
        neg_abs = lax.bitcast_convert_type(
            lax.bitcast_convert_type(z, jnp.uint32) | jnp.uint32(0x80000000), F32)
        sp = jnp.log2(1.0 + jnp.exp2(neg_abs))
        log_b = jnp.minimum(z, 0.0) - sp
        l1m = log_b - z
        if mask is not None:
            l1m = jnp.where(mask, l1m, 0.0)
        cum = _mm(l1m.astype(BF16), u)
        total = cum[:, 0:1] + l1m[:, 0:1]
        a = jnp.exp2(log_b + cum + carry)
        if mask is not None:
            a = jnp.where(mask, a, 0.0)
        return _mm(a.astype(BF16), vj), total

    def more(state):
        j, top = state
        return jnp.logical_and(j >= 0, top > -SB_EXP_ZERO * LOG2E)

    tops = []
    for g in range(SB_G):
        blk = i * SB_G + g
        q2 = q_ref[0, g * SB_T:(g + 1) * SB_T, :]
        qs = jnp.concatenate([q2 * m_lo, q2 * m_hi], axis=0)
        pv_d, tot_d = tile(qs, blk, causal, 0.0)
        if g == 0:
            pv_p, tot_p = tile(qs, jnp.maximum(blk - 1, 0), blk > 0, tot_d)
        else:
            pv_p, tot_p = tile(qs, blk - 1, None, tot_d)
        acc_ref[g] = pv_d + pv_p
        carry0 = tot_d + tot_p
        carry_ref[g] = carry0
        tops.append(jnp.max(carry0))

    for g in range(SB_G):
        def body(state, g=g):
            j, _ = state
            q2 = q_ref[0, g * SB_T:(g + 1) * SB_T, :]
            qs = jnp.concatenate([q2 * m_lo, q2 * m_hi], axis=0)
            pv, tot = tile(qs, j, None, carry_ref[g])
            acc_ref[g] += pv
            carry = carry_ref[g] + tot
            carry_ref[g] = carry
            return j - 1, jnp.max(carry)

        lax.while_loop(more, body, (i * SB_G + g - 2, tops[g]))
        o_ref[0, g * SB_T:(g + 1) * SB_T, :] = jnp.where(lane < HEAD_DIM, acc_ref[g, :SB_T], acc_ref[g, SB_T:])


def _suffix_matrix():
    sp = jnp.arange(SB_T, dtype=jnp.int32)[:, None]
    sc = jnp.arange(SB_T, dtype=jnp.int32)[None, :]
    return jnp.where(sp > sc, 1.0, 0.0).astype(BF16)


def _stickbreaking(qkv3):
    b, s, _ = qkv3.shape
    sec_q, sec_k, sec_v = 3 * N_PAIRS, 4 * N_PAIRS, 5 * N_PAIRS
    o = pl.pallas_call(
        _sb_kernel,
        grid=(b, N_PAIRS, s // (SB_G * SB_T)),
        in_specs=[pl.BlockSpec((1, SB_G * SB_T, LANES), lambda bi, p, i: (bi, i, sec_q + p)),
                  pl.BlockSpec((1, s, LANES), lambda bi, p, i: (bi, 0, sec_k + p)),
                  pl.BlockSpec((1, s, LANES), lambda bi, p, i: (bi, 0, sec_v + p)),
                  pl.BlockSpec((SB_T, SB_T), lambda bi, p, i: (0, 0))],
        out_specs=pl.BlockSpec((1, SB_G * SB_T, LANES), lambda bi, p, i: (bi, i, p)),
        out_shape=jax.ShapeDtypeStruct((b, s, D_HALF), F32),
        scratch_shapes=[pltpu.VMEM((SB_G, 2 * SB_T, LANES), F32), pltpu.VMEM((SB_G, 2 * SB_T, 1), F32)],
        compiler_params=_cparams(("parallel", "parallel", "arbitrary")),
        name="stickbreaking",
    )(qkv3, qkv3, qkv3, _suffix_matrix())
    return o.reshape(b * s, D_HALF)


def _token_major(ref, d, st_ref):
    if d == 1:
        return ref[...]
    width = ref.shape[1] // d
    for r in range(d):
        for ch in range(width // LANES):
            col = r * width + ch * LANES
            st_ref[ch, pl.ds(r, MIX_TM // d, stride=d), :] = ref[:, col:col + LANES]
    return jnp.concatenate([st_ref[ch] for ch in range(width // LANES)], axis=1)


def _mix_kernel(o1_ref, o2_ref, o3_ref, l1_ref, l2_ref, l3_ref, ob_ref, x_ref,
                ga_ref, gb_ref, e_ref, wo_ref, gm_ref, wr_ref, br_ref, tri_ref,
                x1_ref, h2_ref, meta_ref, gate_ref, cnt_ref, carry_ref, *scratch):
    dils = [d for _w, d in DIL_PATTERNS]
    o_st = dict(zip(STRIDED_DILATIONS, scratch[:len(STRIDED_DILATIONS)]))
    l_st = dict(zip(STRIDED_DILATIONS, scratch[len(STRIDED_DILATIONS):]))
    o1, o2, o3 = [_token_major(r, d, o_st.get(d)) for r, d in zip((o1_ref, o2_ref, o3_ref), dils)]
    l1, l2, l3 = [_token_major(r, d, l_st.get(d)) for r, d in zip((l1_ref, l2_ref, l3_ref), dils)]
    lane = lax.broadcasted_iota(jnp.int32, l1.shape, 1)
    m = jnp.maximum(jnp.maximum(l1, l2), l3)
    e1, e2, e3 = jnp.exp(l1 - m), jnp.exp(l2 - m), jnp.exp(l3 - m)
    s1, s2, s3 = [pltpu.roll(l, LANES - N_HEADS, 1) for l in (l1, l2, l3)]
    inv = 1.0 / (e1 * s1 + e2 * s2 + e3 * s3)
    w1, w2, w3 = [jnp.where(lane < N_HEADS, e * inv, 0.0) for e in (e1, e2, e3)]
    expand = e_ref[...]
    oa = (_mm_split(w1, expand) * o1
          + _mm_split(w2, expand) * o2
          + _mm_split(w3, expand) * o3)
    oa = (_rms(oa) * ga_ref[...]).astype(BF16)
    ob = (_rms(ob_ref[...]) * gb_ref[...]).astype(BF16)
    mix = _mm(oa, wo_ref[:D_HALF, :]) + _mm(ob, wo_ref[D_HALF:, :])
    x1 = x_ref[...] + mix
    x1_ref[...] = x1
    _route(x1, gm_ref, wr_ref, br_ref, tri_ref, h2_ref, meta_ref, gate_ref, cnt_ref, carry_ref)


def _head_expand():
    lane = jnp.arange(LANES, dtype=jnp.int32)[:, None]
    col = jnp.arange(D_HALF, dtype=jnp.int32)[None, :]
    return jnp.where(col // HEAD_DIM == lane, 1.0, 0.0).astype(BF16)


def _mix_route(os_, lses, ob, x2, g_out_dil, g_out_sb, w_o, g_moe, w_router, b_router):
    n = x2.shape[0]
    wr = w_router.astype(F32).T
    wr_hi = wr.astype(BF16)
    wr_lo = (wr - wr_hi.astype(F32)).astype(BF16)
    br = jnp.broadcast_to(b_router.astype(F32)[:, None], (N_EXPERTS, LANES))
    r = jnp.arange(MIX_TM, dtype=jnp.int32)
    tri = jnp.where(r[:, None] < r[None, :], 1.0, 0.0).astype(BF16)
    col = lambda rows: pl.BlockSpec((rows, MIX_TM), lambda i: (0, i))
    row = lambda w: pl.BlockSpec((MIX_TM, w), lambda i: (i, 0))
    srow = lambda a: pl.BlockSpec((MIX_TM * a.shape[0] // n, a.shape[1]), lambda i: (i, 0))
    const = lambda shape: pl.BlockSpec(shape, lambda i: (0, 0))
    return pl.pallas_call(
        _mix_kernel,
        grid=(n // MIX_TM,),
        in_specs=[srow(a) for a in os_] + [srow(a) for a in lses] + [row(D_HALF), row(D_MODEL),
                  const((1, D_HALF)), const((1, D_HALF)), const((LANES, D_HALF)), const((D_MODEL, D_MODEL)),
                  const((1, D_MODEL)), const((2 * N_EXPERTS, D_MODEL)), const((N_EXPERTS, LANES)), const((MIX_TM, MIX_TM))],
        out_specs=[row(D_MODEL), row(D_HALF), col(2 * TOP_K), col(2 * TOP_K), const((N_EXPERTS, LANES))],
        out_shape=[jax.ShapeDtypeStruct((n, D_MODEL), F32),
                   jax.ShapeDtypeStruct((n, D_HALF), jnp.int32),
                   jax.ShapeDtypeStruct((2 * TOP_K, n), jnp.int32),
                   jax.ShapeDtypeStruct((2 * TOP_K, n), F32),
                   jax.ShapeDtypeStruct((N_EXPERTS, LANES), F32)],
        scratch_shapes=[pltpu.VMEM((N_EXPERTS, LANES), F32)]
                       + [pltpu.VMEM((D_HALF // LANES, MIX_TM, LANES), F32) for _ in STRIDED_DILATIONS]
                       + [pltpu.VMEM((1, MIX_TM, LANES), F32) for _ in STRIDED_DILATIONS],
        compiler_params=_cparams(("arbitrary",)),
        name="mix_route",
    )(*os_, *lses, ob, x2, g_out_dil.reshape(1, D_HALF), g_out_sb.reshape(1, D_HALF),
      _head_expand(), w_o.astype(BF16), g_moe.reshape(1, D_MODEL), jnp.concatenate([wr_hi, wr_lo], axis=0), br, tri)


def _route(x1, g_ref, wr_ref, br_ref, tri_ref, h2_ref, meta_ref, gate_ref, cnt_ref, carry_ref):
    tm = x1.shape[0]

    @pl.when(pl.program_id(0) == 0)
    def _():
        carry_ref[...] = jnp.zeros_like(carry_ref)

    h2 = _rms(x1) * g_ref[...]
    h2_ref[...] = _pack_bf16_halves(h2)
    parts = _mm_nt(wr_ref[...], jnp.concatenate(_split_bf16(h2), axis=0))
    hi, lo = parts[:N_EXPERTS], parts[N_EXPERTS:]
    logits = (hi[:, :tm] + br_ref[:, 0:1]) + ((hi[:, tm:] + lo[:, :tm]) + lo[:, tm:])
    expert = lax.broadcasted_iota(jnp.int32, (N_EXPERTS, tm), 0).astype(F32)

    onehots, vals, ids = [], [], []
    for _k in range(TOP_K):
        m = jnp.max(logits, axis=0, keepdims=True)
        idx = jnp.min(jnp.where(logits == m, expert, float(N_EXPERTS)), axis=0, keepdims=True)
        oh = expert == idx
        logits = jnp.where(oh, -jnp.inf, logits)
        onehots.append(oh)
        vals.append(m)
        ids.append(idx)

    es = [jnp.exp(v - vals[0]) for v in vals]
    inv = 1.0 / (es[0] + es[1] + es[2] + es[3])

    sel = jnp.zeros((N_EXPERTS, tm), F32)
    for oh in onehots:
        sel = jnp.where(oh, 1.0, sel)
    before = _mm(sel.astype(BF16), tri_ref[...]) + carry_ref[:, 0:1]
    ranks = [jnp.sum(jnp.where(oh, before, 0.0), axis=0, keepdims=True) for oh in onehots]
    zero = jnp.zeros((1, tm), F32)
    meta_ref[...] = jnp.concatenate(ids + ranks, axis=0).astype(jnp.int32)
    gate_ref[...] = jnp.concatenate([e * inv for e in es] + [zero] * TOP_K, axis=0)
    total = carry_ref[...] + jnp.sum(sel, axis=1, keepdims=True)
    carry_ref[...] = total
    cnt_ref[...] = total


def _dispatch(h2, dest, n_slots):
    n = h2.shape[0]
    per_w = n // SC_WORKERS
    nch = per_w // SC_SCATTER_ROWS
    nslab = h2.shape[1] // SC_SCATTER_COLS
    nj = nch * nslab
    assert n == SC_WORKERS * nch * SC_SCATTER_ROWS and h2.shape[1] == nslab * SC_SCATTER_COLS and nj % 2 == 0
    idx = dest.reshape(TOP_K, SC_WORKERS, nch, SC_SCATTER_ROWS).transpose(1, 2, 0, 3)
    idx = idx.reshape(SC_WORKERS * nch * TOP_K, SC_SCATTER_ROWS)
    mesh = plsc.VectorSubcoreMesh(core_axis_name="core", subcore_axis_name="subcore")

    @functools.partial(
        pl.kernel, mesh=mesh,
        out_type=jax.ShapeDtypeStruct((n_slots, h2.shape[1]), h2.dtype),
        scratch_types=[pltpu.VMEM((nch * TOP_K, SC_SCATTER_ROWS), jnp.int32),
                       pltpu.VMEM((2, SC_SCATTER_ROWS, SC_SCATTER_COLS), h2.dtype),
                       pltpu.SemaphoreType.DMA((2,)), pltpu.SemaphoreType.DMA((2,))])
    def scatter_kernel(h2_hbm, idx_hbm, xs_hbm, idx_v, rows_v, sem_l, sem_s):
        wid = lax.axis_index("subcore") * SC_CORES + lax.axis_index("core")
        t0 = wid * per_w
        pltpu.sync_copy(idx_hbm.at[pl.ds(wid * (nch * TOP_K), nch * TOP_K)], idx_v)

        def load(j, b):
            rows = pl.ds(t0 + (j // nslab) * SC_SCATTER_ROWS, SC_SCATTER_ROWS)
            cols = pl.ds((j % nslab) * SC_SCATTER_COLS, SC_SCATTER_COLS)
            return pltpu.make_async_copy(h2_hbm.at[rows, cols], rows_v.at[b], sem_l.at[b])

        def scatter(j, k, b):
            cols = pl.ds((j % nslab) * SC_SCATTER_COLS, SC_SCATTER_COLS)
            return pltpu.make_async_copy(rows_v.at[b], xs_hbm.at[idx_v.at[(j // nslab) * TOP_K + k], cols], sem_s.at[b])

        load(0, 0).start()

        @pl.loop(0, nj, step=2)
        def _(j0):
            for b in (0, 1):
                j = j0 + b

                @pl.when(j >= 1)
                def _():
                    for k in range(TOP_K):
                        scatter(j - 1, k, 1 - b).wait()

                @pl.when(j + 1 < nj)
                def _():
                    load(j + 1, 1 - b).start()

                load(j, b).wait()
                for k in range(TOP_K):
                    scatter(j, k, b).start()

        for k in range(TOP_K):
            scatter(nj - 1, k, 1).wait()

    return scatter_kernel(h2, idx)


def _expert_kernel(be_ref, nu_ref, nv_ref, x_ref, wg_ref, bg_ref, wu_ref, bu_ref, wd_ref, bd_ref, y_ref):
    del be_ref
    i = pl.program_id(0)

    @pl.when(i < nu_ref[0])
    def _():
        row = lax.broadcasted_iota(jnp.int32, (EXPERT_TM, 1), 0)
        xb = _unpack_bf16_halves(jnp.where(row < nv_ref[i], x_ref[...], jnp.int32(0))).astype(BF16)
        glu = _mm(xb, wg_ref[0].astype(BF16)) + bg_ref[0]
        lin = _mm(xb, wu_ref[0].astype(BF16)) + bu_ref[0]
        glu = jnp.minimum(glu, SWIGLU_LIMIT)
        lin = jnp.clip(lin, -SWIGLU_LIMIT, SWIGLU_LIMIT)
        act = (0.5 * glu) * (1.0 + jnp.tanh((0.5 * SWIGLU_ALPHA) * glu)) * (lin + 1.0)
        y = _mm(act.astype(BF16), wd_ref[0].astype(BF16)) + bd_ref[0]
        y_ref[...] = _pack_bf16_halves(y)


def _experts(xs, block_expert, n_used, n_valid, w_gate, b_gate, w_up, b_up, w_down, b_down):
    n_slots = xs.shape[0]
    n_blocks = n_slots // EXPERT_TM
    rows = lambda i, be, nu, nv: (jnp.minimum(i, nu[0] - 1), 0)
    wspec = pl.BlockSpec((1, D_MODEL, D_MODEL), lambda i, be, nu, nv: (be[i], 0, 0))
    bspec = pl.BlockSpec((1, 1, D_MODEL), lambda i, be, nu, nv: (be[i], 0, 0))
    grid_spec = pltpu.PrefetchScalarGridSpec(
        num_scalar_prefetch=3,
        grid=(n_blocks,),
        in_specs=[pl.BlockSpec((EXPERT_TM, D_HALF), rows), wspec, bspec, wspec, bspec, wspec, bspec],
        out_specs=pl.BlockSpec((EXPERT_TM, D_HALF), rows),
    )
    b3 = lambda a: a.astype(F32).reshape(N_EXPERTS, 1, D_MODEL)
    return pl.pallas_call(
        _expert_kernel,
        grid_spec=grid_spec,
        out_shape=jax.ShapeDtypeStruct((n_slots, D_HALF), jnp.int32),
        compiler_params=_cparams(("arbitrary",)),
        name="experts",
    )(block_expert, n_used, n_valid, xs, w_gate, b3(b_gate), w_up, b3(b_up), w_down, b3(b_down))


def _sc_row_gather(table, idx):
    n_idx = idx.shape[0]
    width = table.shape[1]
    per_w = n_idx // SC_WORKERS
    nch = per_w // SC_CHUNK
    assert n_idx == SC_WORKERS * nch * SC_CHUNK and nch % 2 == 0
    mesh = plsc.VectorSubcoreMesh(core_axis_name="core", subcore_axis_name="subcore")

    @functools.partial(
        pl.kernel, mesh=mesh,
        out_type=jax.ShapeDtypeStruct((n_idx, width), table.dtype),
        scratch_types=[pltpu.VMEM((per_w,), jnp.int32), pltpu.VMEM((2, SC_CHUNK, width), table.dtype),
                       pltpu.SemaphoreType.DMA((2,)), pltpu.SemaphoreType.DMA((2,))])
    def gather_kernel(table_hbm, idx_hbm, out_hbm, idx_v, rows_v, sem_g, sem_p):
        wid = lax.axis_index("subcore") * SC_CORES + lax.axis_index("core")
        base = wid * per_w
        pltpu.sync_copy(idx_hbm.at[pl.ds(base, per_w)], idx_v)

        def gather(c, b):
            return pltpu.make_async_copy(table_hbm.at[idx_v.at[pl.ds(c * SC_CHUNK, SC_CHUNK)]], rows_v.at[b], sem_g.at[b])

        def put(c, b):
            return pltpu.make_async_copy(rows_v.at[b], out_hbm.at[pl.ds(base + c * SC_CHUNK, SC_CHUNK)], sem_p.at[b])

        gather(0, 0).start()

        @pl.loop(0, nch, step=2)
        def _(c):
            for b in (0, 1):
                cc = c + b

                @pl.when(cc >= 1)
                def _():
                    put(cc - 1, 1 - b).wait()

                @pl.when(cc + 1 < nch)
                def _():
                    gather(cc + 1, 1 - b).start()

                gather(cc, b).wait()
                put(cc, b).start()

        put(nch - 1, 1).wait()

    return gather_kernel(table, idx)


def _combine_kernel(yt_ref, gate_ref, x1_ref, g_ref, o_ref):
    gate = gate_ref[...].T
    x = x1_ref[...]
    for k in range(TOP_K):
        x = x + gate[:, k:k + 1] * _unpack_bf16_halves(yt_ref[k])
    o_ref[...] = _rms(x) * g_ref[...]


def _combine(yt, gates, x1, g_final):
    n = x1.shape[0]
    yt = yt.reshape(TOP_K, n, D_HALF)
    row = lambda w: pl.BlockSpec((COMBINE_R, w), lambda i: (i, 0))
    return pl.pallas_call(
        _combine_kernel,
        grid=(n // COMBINE_R,),
        in_specs=[pl.BlockSpec((TOP_K, COMBINE_R, D_HALF), lambda i: (0, i, 0)),
                  pl.BlockSpec((2 * TOP_K, COMBINE_R), lambda i: (0, i)), row(D_MODEL),
                  pl.BlockSpec((1, D_MODEL), lambda i: (0, 0))],
        out_specs=row(D_MODEL),
        out_shape=jax.ShapeDtypeStruct((n, D_MODEL), F32),
        compiler_params=_cparams(("parallel",)),
        name="combine",
    )(yt, gates, x1, g_final.reshape(1, D_MODEL))


def _slot_layout(meta, counts_f, n):
    eid = meta[:TOP_K]
    rank = meta[TOP_K:]
    counts = counts_f[:, 0].astype(jnp.int32)
    padded = (counts + EXPERT_TM - 1) // EXPERT_TM * EXPERT_TM
    pend = jnp.cumsum(padded)
    pstart = pend - padded
    experts = jnp.arange(N_EXPERTS, dtype=jnp.int32)
    dest = rank + jnp.sum(jnp.where(eid[..., None] == experts, pstart, 0), axis=-1)
    n_slots = -(-(n * TOP_K + N_EXPERTS * (EXPERT_TM - 1)) // EXPERT_TM) * EXPERT_TM
    block_start = jnp.arange(n_slots // EXPERT_TM, dtype=jnp.int32) * EXPERT_TM
    block_expert = jnp.minimum(jnp.sum(pend[None, :] <= block_start[:, None], axis=-1), N_EXPERTS - 1)
    last_valid = jnp.sum(jnp.where(block_expert[:, None] == experts, pstart + counts, 0), axis=-1)
    n_valid = jnp.clip(last_valid - block_start, 0, EXPERT_TM).astype(jnp.int32)
    n_used = (pend[-1:] // EXPERT_TM).astype(jnp.int32)
    return dest.astype(jnp.int32), block_expert.astype(jnp.int32), n_used, n_valid, n_slots


def kernel(x, g_attn, w_qkv, rel_bias, g_out_dil, g_out_sb, w_o, g_moe, w_router, b_router,
           w_gate, b_gate, w_up, b_up, w_down, b_down, g_final):
    b, s, d = x.shape
    n = b * s
    assert d == D_MODEL and w_qkv.shape == (D_MODEL, 3 * D_MODEL) and w_gate.shape == (N_EXPERTS, D_MODEL, D_MODEL)
    assert s % (SB_G * SB_T) == 0 and all(s % (dil * BAND) == 0 for _w, dil in DIL_PATTERNS)
    assert n % QKV_TM == 0 and n % MIX_TM == 0 and n % COMBINE_R == 0
    x2 = x.reshape(n, d)
    qkv, *strided = _qkv(x2, g_attn, w_qkv)
    sources = dict(zip(STRIDED_DILATIONS, strided))
    dil = [_dilated(sources.get(dilation, qkv), b, s, rel_bias, dilation) for _window, dilation in DIL_PATTERNS]
    ob = _stickbreaking(qkv.reshape(b, s, 3 * d))
    x1, h2, meta, gates, counts = _mix_route([o for o, _ in dil], [l for _, l in dil], ob, x2,
                                             g_out_dil, g_out_sb, w_o, g_moe, w_router, b_router)
    dest, block_expert, n_used, n_valid, n_slots = _slot_layout(meta, counts, n)
    xs = _dispatch(h2, dest, n_slots)
    ys = _experts(xs, block_expert, n_used, n_valid, w_gate, b_gate, w_up, b_up, w_down, b_down)
    yt = _sc_row_gather(ys, dest.reshape(-1))
    return _combine(yt, gates, x1, g_final).reshape(b, s, d)
```

```python
import functools
import math

import jax
import jax.numpy as jnp
from jax import lax
from jax.experimental import pallas as pl
from jax.experimental.pallas import tpu as pltpu
from jax.experimental.pallas import tpu_sc as plsc

F32 = jnp.float32
BF16 = jnp.bfloat16

D_MODEL = 1024
HEAD_DIM = 64
D_HALF = 512
N_HEADS = 8
LANES = 128
N_PAIRS = D_HALF // LANES
DIL_PATTERNS = ((128, 1), (512, 4), (2048, 16))
BAND = 128
REL_BUCKETS = 32
REL_MAX_DISTANCE = 2048
N_EXPERTS = 32
TOP_K = 4
SWIGLU_ALPHA = 1.702
SWIGLU_LIMIT = 7.0
RMS_EPS = 1e-6
NEG_INF = -1e30

QKV_TM = 1024
DIL_G = 16
A_COLS = 3 * D_HALF
STRIDED_DILATIONS = tuple(d for _w, d in DIL_PATTERNS if d > 1)
LOG2E = math.log2(math.e)
SB_T = 256
SB_G = 16
SB_EXP_ZERO = 104.0
MIX_TM = 512
EXPERT_TM = 1024
COMBINE_R = 512
SC_CORES = 2
SC_SUBCORES = 16
SC_WORKERS = SC_CORES * SC_SUBCORES
SC_CHUNK = 64
SC_SCATTER_ROWS = 128
SC_SCATTER_COLS = 256
VMEM_LIMIT = 56 * 1024 * 1024


def _cparams(sem):
    return pltpu.CompilerParams(dimension_semantics=sem, vmem_limit_bytes=VMEM_LIMIT)


def _mm(a, b):
    return jnp.dot(a, b, preferred_element_type=F32)


def _mm_nt(a, b):
    return lax.dot_general(a, b, (((1,), (1,)), ((), ())), preferred_element_type=F32)


def _split_bf16(a):
    hi = a.astype(BF16)
    lo = (a - hi.astype(F32)).astype(BF16)
    return hi, lo


def _mm_split(a, b_bf16):
    hi, lo = _split_bf16(a)
    return _mm(hi, b_bf16) + _mm(lo, b_bf16)


def _pack_bf16_halves(x):
    c = x.shape[1] // 2
    return pltpu.pack_elementwise([x[:, :c], x[:, c:]], packed_dtype=BF16)


def _unpack_bf16_halves(p):
    return jnp.concatenate([pltpu.unpack_elementwise(p, index=i, packed_dtype=BF16, unpacked_dtype=F32)
                            for i in (0, 1)], axis=1)


def _rms(x):
    return x * lax.rsqrt(jnp.mean(x * x, axis=-1, keepdims=True) + RMS_EPS)


def _half_masks(dtype):
    lane = lax.broadcasted_iota(jnp.int32, (1, LANES), 1)
    lo = jnp.where(lane < HEAD_DIM, 1.0, 0.0).astype(dtype)
    hi = jnp.where(lane >= HEAD_DIM, 1.0, 0.0).astype(dtype)
    return lo, hi


def _qkv_kernel(x_ref, g_ref, w_ref, o_ref, *rest):
    stage_ref = rest[-1]
    h = (_rms(x_ref[...]) * g_ref[...]).astype(BF16)
    for c in range(3 * D_MODEL // D_HALF):
        y = _mm(h, w_ref[:, c * D_HALF:(c + 1) * D_HALF])
        if c == 0:
            y = y * (1.0 / math.sqrt(HEAD_DIM))
        if c == 3:
            y = y * (LOG2E / math.sqrt(HEAD_DIM))
        o_ref[:, c * D_HALF:(c + 1) * D_HALF] = y.astype(BF16)
        if c < 3:
            for p in range(N_PAIRS):
                stage_ref[c * N_PAIRS + p] = y[:, p * LANES:(p + 1) * LANES]
    for od_ref, d in zip(rest[:-1], STRIDED_DILATIONS):
        for r in range(d):
            for ch in range(A_COLS // LANES):
                col = r * A_COLS + ch * LANES
                od_ref[:, col:col + LANES] = stage_ref[ch, pl.ds(r, QKV_TM // d, stride=d), :].astype(BF16)


def _qkv(x2, g_attn, w_qkv):
    n = x2.shape[0]
    strided = [(n // d, d * A_COLS) for d in STRIDED_DILATIONS]
    return pl.pallas_call(
        _qkv_kernel,
        grid=(n // QKV_TM,),
        in_specs=[pl.BlockSpec((QKV_TM, D_MODEL), lambda i: (i, 0)),
                  pl.BlockSpec((1, D_MODEL), lambda i: (0, 0)),
                  pl.BlockSpec((D_MODEL, 3 * D_MODEL), lambda i: (0, 0))],
        out_specs=[pl.BlockSpec((QKV_TM, 3 * D_MODEL), lambda i: (i, 0))]
                  + [pl.BlockSpec((QKV_TM // d, d * A_COLS), lambda i: (i, 0)) for d in STRIDED_DILATIONS],
        out_shape=[jax.ShapeDtypeStruct((n, 3 * D_MODEL), BF16)]
                  + [jax.ShapeDtypeStruct(shape, BF16) for shape in strided],
        scratch_shapes=[pltpu.VMEM((A_COLS // LANES, QKV_TM, LANES), F32)],
        compiler_params=_cparams(("parallel",)),
        name="qkv",
    )(x2, g_attn.reshape(1, D_MODEL), w_qkv.astype(BF16))


def _t5_bucket(dist):
    max_exact = REL_BUCKETS // 2
    d_f = jnp.maximum(dist, 1).astype(jnp.float32)
    large = max_exact + (jnp.log(d_f / max_exact)
                         / math.log(REL_MAX_DISTANCE / max_exact)
                         * (REL_BUCKETS - max_exact)).astype(jnp.int32)
    large = jnp.minimum(large, REL_BUCKETS - 1)
    return jnp.where(dist < max_exact, dist, large)


def _band_bias(rel_bias, dilation):
    qi = jnp.arange(BAND, dtype=jnp.int32)[:, None]
    kj = jnp.arange(2 * BAND, dtype=jnp.int32)[None, :]
    rel = qi + BAND - kj
    ok = (rel >= 0) & (rel <= BAND)
    bucket = _t5_bucket(jnp.maximum(rel, 0) * dilation)
    table = rel_bias.astype(F32)
    bias = jnp.zeros((N_HEADS, BAND, 2 * BAND), F32)
    for bk in range(REL_BUCKETS):
        bias = jnp.where(bucket[None] == bk, table[bk][:, None, None], bias)
    return jnp.where(ok[None], bias, NEG_INF)


def _dilated_kernel(cur_ref, prev_ref, bias_ref, o_ref, lse_ref, *, n_blocks, n_res):
    n = pl.program_id(2)
    kj = lax.broadcasted_iota(jnp.int32, (BAND, 2 * BAND), 1)
    has_prev = jnp.logical_or(n > 0, kj >= BAND)
    lane = lax.broadcasted_iota(jnp.int32, (BAND, LANES), 1)
    m_lo, m_hi = _half_masks(BF16)
    for j in range(n_res):
        for g in range(n_blocks):
            rows = slice(g * BAND, (g + 1) * BAND)
            prev = slice((g - 1) * BAND, g * BAND)
            lse_tile = jnp.zeros((BAND, LANES), F32)
            for p in range(N_PAIRS):
                qc, kc, vc = [slice(j * A_COLS + sec * D_HALF + p * LANES, j * A_COLS + sec * D_HALF + (p + 1) * LANES)
                              for sec in range(3)]
                q2 = cur_ref[0, rows, qc]
                k_prev = prev_ref[0, :, kc] if g == 0 else cur_ref[0, prev, kc]
                v_prev = prev_ref[0, :, vc] if g == 0 else cur_ref[0, prev, vc]
                kcat = jnp.concatenate([k_prev, cur_ref[0, rows, kc]], axis=0)
                vcat = jnp.concatenate([v_prev, cur_ref[0, rows, vc]], axis=0)
                outs = []
                for s, msk in enumerate((m_lo, m_hi)):
                    h = 2 * p + s
                    logits = _mm_nt(q2 * msk, kcat) + bias_ref[h]
                    if g == 0:
                        logits = jnp.where(has_prev, logits, NEG_INF)
                    m = jnp.max(logits, axis=-1, keepdims=True)
                    pr = jnp.exp(logits - m)
                    den = jnp.sum(pr, axis=-1, keepdims=True)
                    outs.append(_mm(pr.astype(BF16), vcat))
                    lse_tile = jnp.where(lane == h, m, lse_tile)
                    lse_tile = jnp.where(lane == N_HEADS + h, den, lse_tile)
                oc = slice(j * D_HALF + p * LANES, j * D_HALF + (p + 1) * LANES)
                o_ref[0, rows, oc] = jnp.where(lane < HEAD_DIM, outs[0], outs[1])
            lse_ref[0, rows, j * LANES:(j + 1) * LANES] = lse_tile


def _dilated(src, b, s, rel_bias, dilation):
    d = dilation
    l = s // d
    g_blocks = min(DIL_G, l // BAND)
    nstep = l // (g_blocks * BAND)
    width = src.shape[1] // d
    n_res = min(d, DIL_G // g_blocks) if width == A_COLS else 1
    view = src.reshape(b, l, src.shape[1])
    cols = lambda r: r * (width // A_COLS) if n_res == 1 else r
    o, lse = pl.pallas_call(
        functools.partial(_dilated_kernel, n_blocks=g_blocks, n_res=n_res),
        grid=(b, d // n_res, nstep),
        in_specs=[pl.BlockSpec((1, g_blocks * BAND, n_res * A_COLS), lambda bi, r, n: (bi, n, cols(r))),
                  pl.BlockSpec((1, BAND, n_res * A_COLS),
                               lambda bi, r, n: (bi, jnp.maximum(n * g_blocks - 1, 0), cols(r))),
                  pl.BlockSpec((N_HEADS, BAND, 2 * BAND), lambda bi, r, n: (0, 0, 0))],
        out_specs=[pl.BlockSpec((1, g_blocks * BAND, n_res * D_HALF), lambda bi, r, n: (bi, n, r)),
                   pl.BlockSpec((1, g_blocks * BAND, n_res * LANES), lambda bi, r, n: (bi, n, r))],
        out_shape=[jax.ShapeDtypeStruct((b, l, d * D_HALF), F32),
                   jax.ShapeDtypeStruct((b, l, d * LANES), F32)],
        compiler_params=_cparams(("parallel", "parallel", "arbitrary")),
        name=f"dilated_d{d}",
    )(view, view, _band_bias(rel_bias, d))
    return o.reshape(b * l, d * D_HALF), lse.reshape(b * l, d * LANES)


def _sb_kernel(q_ref, k_ref, v_ref, u_ref, o_ref, acc_ref, carry_ref):
    i = pl.program_id(2)
    m_lo, m_hi = _half_masks(BF16)
    row = lax.broadcasted_iota(jnp.int32, (2 * SB_T, SB_T), 0) & (SB_T - 1)
    col = lax.broadcasted_iota(jnp.int32, (2 * SB_T, SB_T), 1)
    causal = col < row
    lane = lax.broadcasted_iota(jnp.int32, (SB_T, LANES), 1)
    u = u_ref[...]

    def tile(qs, j, mask, carry):
        start = pl.multiple_of(j * SB_T, SB_T)
        kj = k_ref[0, pl.ds(start, SB_T), :]
        vj = v_ref[0, pl.ds(start, SB_T), :]
        z = _mm_nt(qs, kj)
        neg_abs = lax.bitcast_convert_type(
            lax.bitcast_convert_type(z, jnp.uint32) | jnp.uint32(0x80000000), F32)
        sp = jnp.log2(1.0 + jnp.exp2(neg_abs))
        log_b = jnp.minimum(z, 0.0) - sp
        l1m = log_b - z
        if mask is not None:
            l1m = jnp.where(mask, l1m, 0.0)
        cum = _mm(l1m.astype(BF16), u)
        total = cum[:, 0:1] + l1m[:, 0:1]
        a = jnp.exp2(log_b + cum + carry)
        if mask is not None:
            a = jnp.where(mask, a, 0.0)
        return _mm(a.astype(BF16), vj), total

    def more(state):
        j, top = state
        return jnp.logical_and(j >= 0, top > -SB_EXP_ZERO * LOG2E)

    tops = []
    for g in range(SB_G):
        blk = i * SB_G + g
        q2 = q_ref[0, g * SB_T:(g + 1) * SB_T, :]
        qs = jnp.concatenate([q2 * m_lo, q2 * m_hi], axis=0)
        pv_d, tot_d = tile(qs, blk, causal, 0.0)
        if g == 0:
            pv_p, tot_p = tile(qs, jnp.maximum(blk - 1, 0), blk > 0, tot_d)
        else:
            pv_p, tot_p = tile(qs, blk - 1, None, tot_d)
        acc_ref[g] = pv_d + pv_p
        carry0 = tot_d + tot_p
        carry_ref[g] = carry0
        tops.append(jnp.max(carry0))

    for g in range(SB_G):
        def body(state, g=g):
            j, _ = state
            q2 = q_ref[0, g * SB_T:(g + 1) * SB_T, :]
            qs = jnp.concatenate([q2 * m_lo, q2 * m_hi], axis=0)
            pv, tot = tile(qs, j, None, carry_ref[g])
            acc_ref[g] += pv
            carry = carry_ref[g] + tot
            carry_ref[g] = carry
            return j - 1, jnp.max(carry)

        lax.while_loop(more, body, (i * SB_G + g - 2, tops[g]))
        o_ref[0, g * SB_T:(g + 1) * SB_T, :] = jnp.where(lane < HEAD_DIM, acc_ref[g, :SB_T], acc_ref[g, SB_T:])


def _suffix_matrix():
    sp = jnp.arange(SB_T, dtype=jnp.int32)[:, None]
    sc = jnp.arange(SB_T, dtype=jnp.int32)[None, :]
    return jnp.where(sp > sc, 1.0, 0.0).astype(BF16)


def _stickbreaking(qkv3):
    b, s, _ = qkv3.shape
    sec_q, sec_k, sec_v = 3 * N_PAIRS, 4 * N_PAIRS, 5 * N_PAIRS
    o = pl.pallas_call(
        _sb_kernel,
        grid=(b, N_PAIRS, s // (SB_G * SB_T)),
        in_specs=[pl.BlockSpec((1, SB_G * SB_T, LANES), lambda bi, p, i: (bi, i, sec_q + p)),
                  pl.BlockSpec((1, s, LANES), lambda bi, p, i: (bi, 0, sec_k + p)),
                  pl.BlockSpec((1, s, LANES), lambda bi, p, i: (bi, 0, sec_v + p)),
                  pl.BlockSpec((SB_T, SB_T), lambda bi, p, i: (0, 0))],
        out_specs=pl.BlockSpec((1, SB_G * SB_T, LANES), lambda bi, p, i: (bi, i, p)),
        out_shape=jax.ShapeDtypeStruct((b, s, D_HALF), F32),
        scratch_shapes=[pltpu.VMEM((SB_G, 2 * SB_T, LANES), F32), pltpu.VMEM((SB_G, 2 * SB_T, 1), F32)],
        compiler_params=_cparams(("parallel", "parallel", "arbitrary")),
        name="stickbreaking",
    )(qkv3, qkv3, qkv3, _suffix_matrix())
    return o.reshape(b * s, D_HALF)


def _token_major(ref, d, st_ref):
    if d == 1:
        return ref[...]
    width = ref.shape[1] // d
    for r in range(d):
        for ch in range(width // LANES):
            col = r * width + ch * LANES
            st_ref[ch, pl.ds(r, MIX_TM // d, stride=d), :] = ref[:, col:col + LANES]
    return jnp.concatenate([st_ref[ch] for ch in range(width // LANES)], axis=1)


def _mix_kernel(o1_ref, o2_ref, o3_ref, l1_ref, l2_ref, l3_ref, ob_ref, x_ref,
                ga_ref, gb_ref, e_ref, wo_ref, gm_ref, wr_ref, br_ref, tri_ref,
                x1_ref, h2_ref, meta_ref, gate_ref, cnt_ref, carry_ref, *scratch):
    dils = [d for _w, d in DIL_PATTERNS]
    o_st = dict(zip(STRIDED_DILATIONS, scratch[:len(STRIDED_DILATIONS)]))
    l_st = dict(zip(STRIDED_DILATIONS, scratch[len(STRIDED_DILATIONS):]))
    o1, o2, o3 = [_token_major(r, d, o_st.get(d)) for r, d in zip((o1_ref, o2_ref, o3_ref), dils)]
    l1, l2, l3 = [_token_major(r, d, l_st.get(d)) for r, d in zip((l1_ref, l2_ref, l3_ref), dils)]
    lane = lax.broadcasted_iota(jnp.int32, l1.shape, 1)
    m = jnp.maximum(jnp.maximum(l1, l2), l3)
    e1, e2, e3 = jnp.exp(l1 - m), jnp.exp(l2 - m), jnp.exp(l3 - m)
    s1, s2, s3 = [pltpu.roll(l, LANES - N_HEADS, 1) for l in (l1, l2, l3)]
    inv = 1.0 / (e1 * s1 + e2 * s2 + e3 * s3)
    w1, w2, w3 = [jnp.where(lane < N_HEADS, e * inv, 0.0) for e in (e1, e2, e3)]
    expand = e_ref[...]
    oa = (_mm_split(w1, expand) * o1
          + _mm_split(w2, expand) * o2
          + _mm_split(w3, expand) * o3)
    oa = (_rms(oa) * ga_ref[...]).astype(BF16)
    ob = (_rms(ob_ref[...]) * gb_ref[...]).astype(BF16)
    mix = _mm(oa, wo_ref[:D_HALF, :]) + _mm(ob, wo_ref[D_HALF:, :])
    x1 = x_ref[...] + mix
    x1_ref[...] = x1
    _route(x1, gm_ref, wr_ref, br_ref, tri_ref, h2_ref, meta_ref, gate_ref, cnt_ref, carry_ref)


def _head_expand():
    lane = jnp.arange(LANES, dtype=jnp.int32)[:, None]
    col = jnp.arange(D_HALF, dtype=jnp.int32)[None, :]
    return jnp.where(col // HEAD_DIM == lane, 1.0, 0.0).astype(BF16)


def _mix_route(os_, lses, ob, x2, g_out_dil, g_out_sb, w_o, g_moe, w_router, b_router):
    n = x2.shape[0]
    wr = w_router.astype(F32).T
    wr_hi = wr.astype(BF16)
    wr_lo = (wr - wr_hi.astype(F32)).astype(BF16)
    br = jnp.broadcast_to(b_router.astype(F32)[:, None], (N_EXPERTS, LANES))
    r = jnp.arange(MIX_TM, dtype=jnp.int32)
    tri = jnp.where(r[:, None] < r[None, :], 1.0, 0.0).astype(BF16)
    col = lambda rows: pl.BlockSpec((rows, MIX_TM), lambda i: (0, i))
    row = lambda w: pl.BlockSpec((MIX_TM, w), lambda i: (i, 0))
    srow = lambda a: pl.BlockSpec((MIX_TM * a.shape[0] // n, a.shape[1]), lambda i: (i, 0))
    const = lambda shape: pl.BlockSpec(shape, lambda i: (0, 0))
    return pl.pallas_call(
        _mix_kernel,
        grid=(n // MIX_TM,),
        in_specs=[srow(a) for a in os_] + [srow(a) for a in lses] + [row(D_HALF), row(D_MODEL),
                  const((1, D_HALF)), const((1, D_HALF)), const((LANES, D_HALF)), const((D_MODEL, D_MODEL)),
                  const((1, D_MODEL)), const((2 * N_EXPERTS, D_MODEL)), const((N_EXPERTS, LANES)), const((MIX_TM, MIX_TM))],
        out_specs=[row(D_MODEL), row(D_HALF), col(2 * TOP_K), col(2 * TOP_K), const((N_EXPERTS, LANES))],
        out_shape=[jax.ShapeDtypeStruct((n, D_MODEL), F32),
                   jax.ShapeDtypeStruct((n, D_HALF), jnp.int32),
                   jax.ShapeDtypeStruct((2 * TOP_K, n), jnp.int32),
                   jax.ShapeDtypeStruct((2 * TOP_K, n), F32),
                   jax.ShapeDtypeStruct((N_EXPERTS, LANES), F32)],
        scratch_shapes=[pltpu.VMEM((N_EXPERTS, LANES), F32)]
                       + [pltpu.VMEM((D_HALF // LANES, MIX_TM, LANES), F32) for _ in STRIDED_DILATIONS]
                       + [pltpu.VMEM((1, MIX_TM, LANES), F32) for _ in STRIDED_DILATIONS],
        compiler_params=_cparams(("arbitrary",)),
        name="mix_route",
    )(*os_, *lses, ob, x2, g_out_dil.reshape(1, D_HALF), g_out_sb.reshape(1, D_HALF),
      _head_expand(), w_o.astype(BF16), g_moe.reshape(1, D_MODEL), jnp.concatenate([wr_hi, wr_lo], axis=0), br, tri)


def _route(x1, g_ref, wr_ref, br_ref, tri_ref, h2_ref, meta_ref, gate_ref, cnt_ref, carry_ref):
    tm = x1.shape[0]

    @pl.when(pl.program_id(0) == 0)
    def _():
        carry_ref[...] = jnp.zeros_like(carry_ref)

    h2 = _rms(x1) * g_ref[...]
    h2_ref[...] = _pack_bf16_halves(h2)
    parts = _mm_nt(wr_ref[...], jnp.concatenate(_split_bf16(h2), axis=0))
    hi, lo = parts[:N_EXPERTS], parts[N_EXPERTS:]
    logits = (hi[:, :tm] + br_ref[:, 0:1]) + ((hi[:, tm:] + lo[:, :tm]) + lo[:, tm:])
    expert = lax.broadcasted_iota(jnp.int32, (N_EXPERTS, tm), 0).astype(F32)

    onehots, vals, ids = [], [], []
    for _k in range(TOP_K):
        m = jnp.max(logits, axis=0, keepdims=True)
        idx = jnp.min(jnp.where(logits == m, expert, float(N_EXPERTS)), axis=0, keepdims=True)
        oh = expert == idx
        logits = jnp.where(oh, -jnp.inf, logits)
        onehots.append(oh)
        vals.append(m)
        ids.append(idx)

    es = [jnp.exp(v - vals[0]) for v in vals]
    inv = 1.0 / (es[0] + es[1] + es[2] + es[3])

    sel = jnp.zeros((N_EXPERTS, tm), F32)
    for oh in onehots:
        sel = jnp.where(oh, 1.0, sel)
    before = _mm(sel.astype(BF16), tri_ref[...]) + carry_ref[:, 0:1]
    ranks = [jnp.sum(jnp.where(oh, before, 0.0), axis=0, keepdims=True) for oh in onehots]
    zero = jnp.zeros((1, tm), F32)
    meta_ref[...] = jnp.concatenate(ids + ranks, axis=0).astype(jnp.int32)
    gate_ref[...] = jnp.concatenate([e * inv for e in es] + [zero] * TOP_K, axis=0)
    total = carry_ref[...] + jnp.sum(sel, axis=1, keepdims=True)
    carry_ref[...] = total
    cnt_ref[...] = total


def _dispatch(h2, dest, n_slots):
    n = h2.shape[0]
    per_w = n // SC_WORKERS
    nch = per_w // SC_SCATTER_ROWS
    nslab = h2.shape[1] // SC_SCATTER_COLS
    nj = nch * nslab
    assert n == SC_WORKERS * nch * SC_SCATTER_ROWS and h2.shape[1] == nslab * SC_SCATTER_COLS and nj % 2 == 0
    idx = dest.reshape(TOP_K, SC_WORKERS, nch, SC_SCATTER_ROWS).transpose(1, 2, 0, 3)
    idx = idx.reshape(SC_WORKERS * nch * TOP_K, SC_SCATTER_ROWS)
    mesh = plsc.VectorSubcoreMesh(core_axis_name="core", subcore_axis_name="subcore")

    @functools.partial(
        pl.kernel, mesh=mesh,
        out_type=jax.ShapeDtypeStruct((n_slots, h2.shape[1]), h2.dtype),
        scratch_types=[pltpu.VMEM((nch * TOP_K, SC_SCATTER_ROWS), jnp.int32),
                       pltpu.VMEM((2, SC_SCATTER_ROWS, SC_SCATTER_COLS), h2.dtype),
                       pltpu.SemaphoreType.DMA((2,)), pltpu.SemaphoreType.DMA((2,))])
    def scatter_kernel(h2_hbm, idx_hbm, xs_hbm, idx_v, rows_v, sem_l, sem_s):
        wid = lax.axis_index("subcore") * SC_CORES + lax.axis_index("core")
        t0 = wid * per_w
        pltpu.sync_copy(idx_hbm.at[pl.ds(wid * (nch * TOP_K), nch * TOP_K)], idx_v)

        def load(j, b):
            rows = pl.ds(t0 + (j // nslab) * SC_SCATTER_ROWS, SC_SCATTER_ROWS)
            cols = pl.ds((j % nslab) * SC_SCATTER_COLS, SC_SCATTER_COLS)
            return pltpu.make_async_copy(h2_hbm.at[rows, cols], rows_v.at[b], sem_l.at[b])

        def scatter(j, k, b):
            cols = pl.ds((j % nslab) * SC_SCATTER_COLS, SC_SCATTER_COLS)
            return pltpu.make_async_copy(rows_v.at[b], xs_hbm.at[idx_v.at[(j // nslab) * TOP_K + k], cols], sem_s.at[b])

        load(0, 0).start()

        @pl.loop(0, nj, step=2)
        def _(j0):
            for b in (0, 1):
                j = j0 + b

                @pl.when(j >= 1)
                def _():
                    for k in range(TOP_K):
                        scatter(j - 1, k, 1 - b).wait()

                @pl.when(j + 1 < nj)
                def _():
                    load(j + 1, 1 - b).start()

                load(j, b).wait()
                for k in range(TOP_K):
                    scatter(j, k, b).start()

        for k in range(TOP_K):
            scatter(nj - 1, k, 1).wait()

    return scatter_kernel(h2, idx)


def _expert_kernel(be_ref, nu_ref, nv_ref, x_ref, wg_ref, bg_ref, wu_ref, bu_ref, wd_ref, bd_ref, y_ref):
    del be_ref
    i = pl.program_id(0)

    @pl.when(i < nu_ref[0])
    def _():
        row = lax.broadcasted_iota(jnp.int32, (EXPERT_TM, 1), 0)
        xb = _unpack_bf16_halves(jnp.where(row < nv_ref[i], x_ref[...], jnp.int32(0))).astype(BF16)
        glu = _mm(xb, wg_ref[0].astype(BF16)) + bg_ref[0]
        lin = _mm(xb, wu_ref[0].astype(BF16)) + bu_ref[0]
        glu = jnp.minimum(glu, SWIGLU_LIMIT)
        lin = jnp.clip(lin, -SWIGLU_LIMIT, SWIGLU_LIMIT)
        act = (0.5 * glu) * (1.0 + jnp.tanh((0.5 * SWIGLU_ALPHA) * glu)) * (lin + 1.0)
        y = _mm(act.astype(BF16), wd_ref[0].astype(BF16)) + bd_ref[0]
        y_ref[...] = _pack_bf16_halves(y)


def _experts(xs, block_expert, n_used, n_valid, w_gate, b_gate, w_up, b_up, w_down, b_down):
    n_slots = xs.shape[0]
    n_blocks = n_slots // EXPERT_TM
    rows = lambda i, be, nu, nv: (jnp.minimum(i, nu[0] - 1), 0)
    wspec = pl.BlockSpec((1, D_MODEL, D_MODEL), lambda i, be, nu, nv: (be[i], 0, 0))
    bspec = pl.BlockSpec((1, 1, D_MODEL), lambda i, be, nu, nv: (be[i], 0, 0))
    grid_spec = pltpu.PrefetchScalarGridSpec(
        num_scalar_prefetch=3,
        grid=(n_blocks,),
        in_specs=[pl.BlockSpec((EXPERT_TM, D_HALF), rows), wspec, bspec, wspec, bspec, wspec, bspec],
        out_specs=pl.BlockSpec((EXPERT_TM, D_HALF), rows),
    )
    b3 = lambda a: a.astype(F32).reshape(N_EXPERTS, 1, D_MODEL)
    return pl.pallas_call(
        _expert_kernel,
        grid_spec=grid_spec,
        out_shape=jax.ShapeDtypeStruct((n_slots, D_HALF), jnp.int32),
        compiler_params=_cparams(("arbitrary",)),
        name="experts",
    )(block_expert, n_used, n_valid, xs, w_gate, b3(b_gate), w_up, b3(b_up), w_down, b3(b_down))


def _sc_row_gather(table, idx):
    n_idx = idx.shape[0]
    width = table.shape[1]
    per_w = n_idx // SC_WORKERS
    nch = per_w // SC_CHUNK
    assert n_idx == SC_WORKERS * nch * SC_CHUNK and nch % 2 == 0
    mesh = plsc.VectorSubcoreMesh(core_axis_name="core", subcore_axis_name="subcore")

    @functools.partial(
        pl.kernel, mesh=mesh,
        out_type=jax.ShapeDtypeStruct((n_idx, width), table.dtype),
        scratch_types=[pltpu.VMEM((per_w,), jnp.int32), pltpu.VMEM((2, SC_CHUNK, width), table.dtype),
                       pltpu.SemaphoreType.DMA((2,)), pltpu.SemaphoreType.DMA((2,))])
    def gather_kernel(table_hbm, idx_hbm, out_hbm, idx_v, rows_v, sem_g, sem_p):
        wid = lax.axis_index("subcore") * SC_CORES + lax.axis_index("core")
        base = wid * per_w
        pltpu.sync_copy(idx_hbm.at[pl.ds(base, per_w)], idx_v)

        def gather(c, b):
            return pltpu.make_async_copy(table_hbm.at[idx_v.at[pl.ds(c * SC_CHUNK, SC_CHUNK)]], rows_v.at[b], sem_g.at[b])

        def put(c, b):
            return pltpu.make_async_copy(rows_v.at[b], out_hbm.at[pl.ds(base + c * SC_CHUNK, SC_CHUNK)], sem_p.at[b])

        gather(0, 0).start()

        @pl.loop(0, nch, step=2)
        def _(c):
            for b in (0, 1):
                cc = c + b

                @pl.when(cc >= 1)
                def _():
                    put(cc - 1, 1 - b).wait()

                @pl.when(cc + 1 < nch)
                def _():
                    gather(cc + 1, 1 - b).start()

                gather(cc, b).wait()
                put(cc, b).start()

        put(nch - 1, 1).wait()

    return gather_kernel(table, idx)


def _combine_kernel(yt_ref, gate_ref, x1_ref, g_ref, o_ref):
    gate = gate_ref[...].T
    x = x1_ref[...]
    for k in range(TOP_K):
        x = x + gate[:, k:k + 1] * _unpack_bf16_halves(yt_ref[k])
    o_ref[...] = _rms(x) * g_ref[...]


def _combine(yt, gates, x1, g_final):
    n = x1.shape[0]
    yt = yt.reshape(TOP_K, n, D_HALF)
    row = lambda w: pl.BlockSpec((COMBINE_R, w), lambda i: (i, 0))
    return pl.pallas_call(
        _combine_kernel,
        grid=(n // COMBINE_R,),
        in_specs=[pl.BlockSpec((TOP_K, COMBINE_R, D_HALF), lambda i: (0, i, 0)),
                  pl.BlockSpec((2 * TOP_K, COMBINE_R), lambda i: (0, i)), row(D_MODEL),
                  pl.BlockSpec((1, D_MODEL), lambda i: (0, 0))],
        out_specs=row(D_MODEL),
        out_shape=jax.ShapeDtypeStruct((n, D_MODEL), F32),
        compiler_params=_cparams(("parallel",)),
        name="combine",
    )(yt, gates, x1, g_final.reshape(1, D_MODEL))


def _slot_layout(meta, counts_f, n):
    eid = meta[:TOP_K]
    rank = meta[TOP_K:]
    counts = counts_f[:, 0].astype(jnp.int32)
    padded = (counts + EXPERT_TM - 1) // EXPERT_TM * EXPERT_TM
    pend = jnp.cumsum(padded)
    pstart = pend - padded
    experts = jnp.arange(N_EXPERTS, dtype=jnp.int32)
    dest = rank + jnp.sum(jnp.where(eid[..., None] == experts, pstart, 0), axis=-1)
    n_slots = -(-(n * TOP_K + N_EXPERTS * (EXPERT_TM - 1)) // EXPERT_TM) * EXPERT_TM
    block_start = jnp.arange(n_slots // EXPERT_TM, dtype=jnp.int32) * EXPERT_TM
    block_expert = jnp.minimum(jnp.sum(pend[None, :] <= block_start[:, None], axis=-1), N_EXPERTS - 1)
    last_valid = jnp.sum(jnp.where(block_expert[:, None] == experts, pstart + counts, 0), axis=-1)
    n_valid = jnp.clip(last_valid - block_start, 0, EXPERT_TM).astype(jnp.int32)
    n_used = (pend[-1:] // EXPERT_TM).astype(jnp.int32)
    return dest.astype(jnp.int32), block_expert.astype(jnp.int32), n_used, n_valid, n_slots


def kernel(x, g_attn, w_qkv, rel_bias, g_out_dil, g_out_sb, w_o, g_moe, w_router, b_router,
           w_gate, b_gate, w_up, b_up, w_down, b_down, g_final):
    b, s, d = x.shape
    n = b * s
    assert d == D_MODEL and w_qkv.shape == (D_MODEL, 3 * D_MODEL) and w_gate.shape == (N_EXPERTS, D_MODEL, D_MODEL)
    assert s % (SB_G * SB_T) == 0 and all(s % (dil * BAND) == 0 for _w, dil in DIL_PATTERNS)
    assert n % QKV_TM == 0 and n % MIX_TM == 0 and n % COMBINE_R == 0
    x2 = x.reshape(n, d)
    qkv, *strided = _qkv(x2, g_attn, w_qkv)
    sources = dict(zip(STRIDED_DILATIONS, strided))
    dil = [_dilated(sources.get(dilation, qkv), b, s, rel_bias, dilation) for _window, dilation in DIL_PATTERNS]
    ob = _stickbreaking(qkv.reshape(b, s, 3 * d))
    x1, h2, meta, gates, counts = _mix_route([o for o, _ in dil], [l for _, l in dil], ob, x2,
                                             g_out_dil, g_out_sb, w_o, g_moe, w_router, b_router)
    dest, block_expert, n_used, n_valid, n_slots = _slot_layout(meta, counts, n)
    xs = _dispatch(h2, dest, n_slots)
    ys = _experts(xs, block_expert, n_used, n_valid, w_gate, b_gate, w_up, b_up, w_down, b_down)
    yt = _sc_row_gather(ys, dest.reshape(-1))
    return _combine(yt, gates, x1, g_final).reshape(b, s, d)
```
